```python
import math
import jax
import jax.numpy as jnp
from jax import lax
import numpy as np

D_MODEL = 1024
BATCH = 4
SEQ = 4096
DEPTH = 2
DEC_BATCH = 128
DEC_SEQ = 4
PAST_LEN = 2048
PAGE_SIZE = 128

S5_WIDTH = D_MODEL // 2
S5_GROUP = 16
S5_GROUPS = S5_WIDTH // S5_GROUP
S5_STATE = 64
FOX_HEADS = 8
FOX_HD = 64
FOX_WIDTH = FOX_HEADS * FOX_HD
Q_BLOCK = 128
ML_HEADS = 4
ML_HD = 128
ML_WIDTH = ML_HEADS * ML_HD
ML_CHUNK = 64
N_MEM = 256
MEM_HEADS = 4
MEM_HD = 128
MEM_WIDTH = MEM_HEADS * MEM_HD
D_FF = 4 * D_MODEL
N_BRANCH = 3
EPS = 1e-6
SPLITS = (S5_WIDTH, FOX_WIDTH, FOX_WIDTH, FOX_WIDTH, FOX_HEADS, ML_WIDTH, ML_WIDTH, ML_WIDTH, ML_HEADS, ML_HEADS, ML_WIDTH, N_BRANCH * D_MODEL)
D_IN = sum(SPLITS)

kernel_name = 'hybrid_s5_fox_mlstm_decode_step'


def rmsnorm(x, g):
    xf = x.astype(jnp.float32)
    y = xf * lax.rsqrt(jnp.mean(xf * xf, axis=-1, keepdims=True) + EPS)
    return (y * g.astype(jnp.float32)).astype(x.dtype)


def split_columns(a):
    idx = [int(i) for i in np.cumsum(SPLITS)[:-1]]
    return jnp.split(a, idx, axis=-1)


def _linear_combine(e1, e2):
    a1, b1 = e1
    a2, b2 = e2
    return a2 * a1, a2 * b1 + b2


def s5_mixer(u, h0_re, h0_im, a_re, a_im, log_step, b_re, b_im, c_re, c_im, d_skip, w_glu, b_glu):
    f32 = jnp.float32
    bsz, t, _ = u.shape
    lam = lax.complex(a_re.astype(f32), a_im.astype(f32))
    dt = jnp.exp(log_step.astype(f32))[:, None]
    lam_bar = jnp.exp(lam * dt)
    b_c = lax.complex(b_re.astype(f32), b_im.astype(f32))
    b_bar = ((lam_bar - 1.0) / lam)[..., None] * b_c
    uf = u.astype(f32)
    ug = uf.reshape(bsz, t, S5_GROUPS, S5_GROUP).astype(jnp.complex64)
    bu = jnp.einsum('btgc,gpc->btgp', ug, b_bar)
    h0 = lax.complex(h0_re.astype(f32), h0_im.astype(f32))
    bu = bu.at[:, 0].add(lam_bar[None] * h0)
    lam_seq = jnp.broadcast_to(lam_bar, bu.shape)
    _, h = lax.associative_scan(_linear_combine, (lam_seq, bu), axis=1)
    c_c = lax.complex(c_re.astype(f32), c_im.astype(f32))
    y = jnp.real(jnp.einsum('btgp,gcp->btgc', h, c_c)).reshape(bsz, t, S5_WIDTH)
    y = jax.nn.gelu(y + d_skip.astype(f32) * uf)
    y = y * jax.nn.sigmoid(y @ w_glu.astype(f32) + b_glu.astype(f32))
    h_last = h[:, -1]
    return y.astype(u.dtype), jnp.real(h_last), jnp.imag(h_last)


def fox_attend(q, k, v, f_q, f_k, q_pos, k_pos):
    bsz, tq, nh, hd = q.shape
    qb = Q_BLOCK if tq % Q_BLOCK == 0 else tq
    nb = tq // qb
    scale = hd ** -0.5
    f_kT = f_k.transpose(0, 2, 1)[:, :, None, :]

    def block(xs):
        qi, fqi, pi = xs
        s = jnp.einsum('bqhd,bkhd->bhqk', qi, k).astype(jnp.float32) * scale
        s = s + fqi.transpose(0, 2, 1)[..., None] - f_kT
        s = jnp.where((pi[:, None] >= k_pos[None, :])[None, None], s, -jnp.inf)
        p = jax.nn.softmax(s, axis=-1).astype(v.dtype)
        return jnp.einsum('bhqk,bkhd->bqhd', p, v)

    qs = q.reshape(bsz, nb, qb, nh, hd).swapaxes(0, 1)
    fqs = f_q.reshape(bsz, nb, qb, nh).swapaxes(0, 1)
    ps = q_pos.reshape(nb, qb)
    o = lax.map(block, (qs, fqs, ps))
    return o.swapaxes(0, 1).reshape(bsz, tq, nh, hd)


def mlstm_chunkwise(q, k, v, ig, lf, c0, n0, m0):
    f32 = jnp.float32
    bsz, t, nh, dh = q.shape
    L = ML_CHUNK if t % ML_CHUNK == 0 else t
    nc = t // L

    def chunks(a):
        return a.astype(f32).reshape(bsz, nc, L, *a.shape[2:]).swapaxes(0, 1)

    causal = jnp.tril(jnp.ones((L, L), dtype=bool))

    def step(carry, xs):
        c, n, m = carry
        qc, kc, vc, ic, fc = xs
        bT = jnp.cumsum(fc, axis=1).transpose(0, 2, 1)
        iT = ic.transpose(0, 2, 1)
        log_d = bT[..., :, None] - bT[..., None, :] + iT[..., None, :]
        log_d = jnp.where(causal, log_d, -jnp.inf)
        log_inter = bT + m[..., None]
        m_t = jnp.maximum(log_inter, jnp.max(log_d, axis=-1))
        d_w = jnp.exp(log_d - m_t[..., None])
        inter_w = jnp.exp(log_inter - m_t)
        s = jnp.einsum('blhd,bshd->bhls', qc, kc) * d_w
        num = jnp.einsum('bhls,bshe->bhle', s, vc) + inter_w[..., None] * jnp.einsum('bhed,blhd->bhle', c, qc)
        den = jnp.sum(s, axis=-1) + inter_w * jnp.einsum('bhd,blhd->bhl', n, qc)
        h = num / jnp.maximum(jnp.abs(den), jnp.exp(-m_t))[..., None]
        w_end = d_w[:, :, -1, :]
        a_end = inter_w[:, :, -1]
        c_new = a_end[..., None, None] * c + jnp.einsum('bhs,bshe,bshd->bhed', w_end, vc, kc)
        n_new = a_end[..., None] * n + jnp.einsum('bhs,bshd->bhd', w_end, kc)
        return (c_new, n_new, m_t[:, :, -1]), h.transpose(0, 2, 1, 3)

    xs = (chunks(q), chunks(k), chunks(v), chunks(ig), chunks(lf))
    (c, n, m), hs = lax.scan(step, (c0.astype(f32), n0.astype(f32), m0.astype(f32)), xs)
    h = hs.swapaxes(0, 1).reshape(bsz, t, nh, dh)
    return h, c, n, m


def memory_kv(mem, g_mem, w_mk, w_mv, g_k):
    bsz, nm, _ = mem.shape
    mn = rmsnorm(mem, g_mem)
    k = rmsnorm((mn @ w_mk).reshape(bsz, nm, MEM_HEADS, MEM_HD), g_k)
    v = (mn @ w_mv).reshape(bsz, nm, MEM_HEADS, MEM_HD)
    return k, v


def gather_pages(pool, page_table):
    g = pool[page_table]
    return g.reshape(g.shape[0], g.shape[1] * g.shape[2], *g.shape[3:])


def hybrid_layer(x, mem_k, mem_v, fox_past, s5_h0_re, s5_h0_im, ml_c0, ml_n0, ml_m0, pos0, W):
    f32 = jnp.float32
    bsz, t, _ = x.shape
    h = rmsnorm(x, W['g_mix'])
    (u_s5, fq, fk, fv, ff, mq, mk, mv, mi, mf, mo, gates) = split_columns(h @ W['w_in'])
    y_s5, s5_re, s5_im = s5_mixer(u_s5, s5_h0_re, s5_h0_im, W['s5_a_re'], W['s5_a_im'], W['s5_log_step'],
                                  W['s5_b_re'], W['s5_b_im'], W['s5_c_re'], W['s5_c_im'], W['s5_d'],
                                  W['s5_w_glu'], W['s5_b_glu'])
    q = rmsnorm(fq.reshape(bsz, t, FOX_HEADS, FOX_HD), W['fox_gq'])
    k = rmsnorm(fk.reshape(bsz, t, FOX_HEADS, FOX_HD), W['fox_gk'])
    v = fv.reshape(bsz, t, FOX_HEADS, FOX_HD)
    logf = jax.nn.log_sigmoid((ff + W['fox_bf']).astype(f32))
    q_pos = pos0 + jnp.arange(t)
    if fox_past is None:
        f_q = jnp.cumsum(logf, axis=1)
        k_all, v_all, f_k = k, v, f_q
    else:
        k_past, v_past, logf_past = fox_past
        f_past = jnp.cumsum(logf_past.astype(f32), axis=1)
        f_q = f_past[:, -1:] + jnp.cumsum(logf, axis=1)
        k_all = jnp.concatenate([k_past.astype(k.dtype), k], axis=1)
        v_all = jnp.concatenate([v_past.astype(v.dtype), v], axis=1)
        f_k = jnp.concatenate([f_past, f_q], axis=1)
    k_pos = jnp.arange(k_all.shape[1])
    y_fox = fox_attend(q, k_all, v_all, f_q, f_k, q_pos, k_pos).reshape(bsz, t, FOX_WIDTH)
    mq_h = mq.reshape(bsz, t, ML_HEADS, ML_HD)
    mk_h = mk.reshape(bsz, t, ML_HEADS, ML_HD) * (ML_HD ** -0.5)
    mv_h = mv.reshape(bsz, t, ML_HEADS, ML_HD)
    i_pre = (mi + W['ml_bi']).astype(f32)
    log_fg = jax.nn.log_sigmoid((mf + W['ml_bf']).astype(f32))
    h_ml, c_new, n_new, m_new = mlstm_chunkwise(mq_h, mk_h, mv_h, i_pre, log_fg, ml_c0, ml_n0, ml_m0)
    y_ml = (rmsnorm(h_ml, W['ml_gn']).reshape(bsz, t, ML_WIDTH) * jax.nn.sigmoid(mo.astype(f32))).astype(x.dtype)
    g = jax.nn.sigmoid(gates.astype(f32)).reshape(bsz, t, N_BRANCH, D_MODEL)
    merged = (g[:, :, 0] * (y_s5 @ W['w_br_s5']) + g[:, :, 1] * (y_fox @ W['w_br_fox'])
              + g[:, :, 2] * (y_ml @ W['w_br_ml'])).astype(x.dtype)
    x = x + merged @ W['w_out']
    hc = rmsnorm(x, W['g_cross'])
    qc = rmsnorm((hc @ W['w_cq']).reshape(bsz, t, MEM_HEADS, MEM_HD), W['cross_gq'])
    s = jnp.einsum('bthd,bmhd->bhtm', qc, mem_k.astype(qc.dtype)).astype(f32) * (MEM_HD ** -0.5)
    p = jax.nn.softmax(s, axis=-1).astype(x.dtype)
    oc = jnp.einsum('bhtm,bmhd->bthd', p, mem_v.astype(x.dtype)).reshape(bsz, t, MEM_WIDTH)
    x = x + oc @ W['w_co']
    hm = rmsnorm(x, W['g_mlp'])
    x = x + jnp.square(jax.nn.relu(hm @ W['w_up'])) @ W['w_down']
    return x, (k, v, logf, s5_re, s5_im, c_new, n_new, m_new)


def setup_inputs(seed: int = 0) -> dict:
    key = jax.random.key(seed)
    ks = jax.random.split(key, 48)
    f32 = jnp.float32
    n_pages = PAST_LEN // PAGE_SIZE
    n_used = DEC_BATCH * n_pages
    n_pool = n_used + n_used // 4

    def nrm(i, shape, scale=1.0):
        return jax.random.normal(ks[i], shape, f32) * scale

    def gain(i, shape):
        return 1.0 + 0.02 * jax.random.normal(ks[i], shape, f32)

    page_table = jax.random.permutation(ks[6], n_pool)[:n_used].reshape(DEC_BATCH, n_pages).astype(jnp.int32)
    a_im = jnp.pi * jnp.arange(S5_STATE, dtype=f32)[None, None, :] + 0.01 * nrm(17, (DEPTH, S5_GROUPS, S5_STATE))
    return {
        'x_prompt': nrm(0, (BATCH, SEQ, D_MODEL)),
        'x_sample': nrm(1, (DEC_BATCH, DEC_SEQ, D_MODEL)),
        'mem_prompt': nrm(2, (BATCH, N_MEM, D_MODEL)),
        'cache_fox_k': nrm(3, (DEPTH, n_pool, PAGE_SIZE, FOX_HEADS, FOX_HD)),
        'cache_fox_v': nrm(4, (DEPTH, n_pool, PAGE_SIZE, FOX_HEADS, FOX_HD)),
        'cache_fox_logf': jax.nn.log_sigmoid(3.0 + 0.5 * nrm(5, (DEPTH, n_pool, PAGE_SIZE, FOX_HEADS))),
        'page_table': page_table,
        'state_s5_re': nrm(7, (DEPTH, DEC_BATCH, S5_GROUPS, S5_STATE), 0.3),
        'state_s5_im': nrm(8, (DEPTH, DEC_BATCH, S5_GROUPS, S5_STATE), 0.3),
        'state_mlstm_C': nrm(9, (DEPTH, DEC_BATCH, ML_HEADS, ML_HD, ML_HD), 0.5),
        'state_mlstm_n': nrm(10, (DEPTH, DEC_BATCH, ML_HEADS, ML_HD), 0.5),
        'state_mlstm_m': nrm(11, (DEPTH, DEC_BATCH, ML_HEADS)),
        'cache_mem_k': nrm(12, (DEPTH, DEC_BATCH, N_MEM, MEM_HEADS, MEM_HD)),
        'cache_mem_v': nrm(13, (DEPTH, DEC_BATCH, N_MEM, MEM_HEADS, MEM_HD)),
        'g_mix': gain(14, (DEPTH, D_MODEL)),
        'w_in': nrm(15, (DEPTH, D_MODEL, D_IN), D_MODEL ** -0.5),
        's5_a_re': -0.5 * jnp.exp(0.05 * nrm(16, (DEPTH, S5_GROUPS, S5_STATE))),
        's5_a_im': a_im,
        's5_log_step': jax.random.uniform(ks[18], (DEPTH, S5_GROUPS), f32, math.log(1e-3), math.log(1e-1)),
        's5_b_re': nrm(19, (DEPTH, S5_GROUPS, S5_STATE, S5_GROUP), (2.0 * S5_GROUP) ** -0.5),
        's5_b_im': nrm(20, (DEPTH, S5_GROUPS, S5_STATE, S5_GROUP), (2.0 * S5_GROUP) ** -0.5),
        's5_c_re': nrm(21, (DEPTH, S5_GROUPS, S5_GROUP, S5_STATE), (2.0 / S5_STATE) ** 0.5),
        's5_c_im': nrm(22, (DEPTH, S5_GROUPS, S5_GROUP, S5_STATE), (2.0 / S5_STATE) ** 0.5),
        's5_d': nrm(23, (DEPTH, S5_WIDTH)),
        's5_w_glu': nrm(24, (DEPTH, S5_WIDTH, S5_WIDTH), S5_WIDTH ** -0.5),
        's5_b_glu': nrm(25, (DEPTH, S5_WIDTH), 0.02),
        'fox_gq': gain(26, (DEPTH, FOX_HD)),
        'fox_gk': gain(27, (DEPTH, FOX_HD)),
        'fox_bf': 3.0 + 0.5 * nrm(28, (DEPTH, FOX_HEADS)),
        'ml_bi': nrm(29, (DEPTH, ML_HEADS), 0.1),
        'ml_bf': jnp.linspace(3.0, 6.0, ML_HEADS, dtype=f32)[None, :] + nrm(30, (DEPTH, ML_HEADS), 0.1),
        'ml_gn': gain(31, (DEPTH, ML_HD)),
        'w_br_s5': nrm(32, (DEPTH, S5_WIDTH, D_MODEL), S5_WIDTH ** -0.5),
        'w_br_fox': nrm(33, (DEPTH, FOX_WIDTH, D_MODEL), FOX_WIDTH ** -0.5),
        'w_br_ml': nrm(34, (DEPTH, ML_WIDTH, D_MODEL), ML_WIDTH ** -0.5),
        'w_out': nrm(35, (DEPTH, D_MODEL, D_MODEL), D_MODEL ** -0.5),
        'g_cross': gain(36, (DEPTH, D_MODEL)),
        'w_cq': nrm(37, (DEPTH, D_MODEL, MEM_WIDTH), D_MODEL ** -0.5),
        'cross_gq': gain(38, (DEPTH, MEM_HD)),
        'g_mem': gain(39, (DEPTH, D_MODEL)),
        'w_mk': nrm(40, (DEPTH, D_MODEL, MEM_WIDTH), D_MODEL ** -0.5),
        'w_mv': nrm(41, (DEPTH, D_MODEL, MEM_WIDTH), D_MODEL ** -0.5),
        'cross_gk': gain(42, (DEPTH, MEM_HD)),
        'w_co': nrm(43, (DEPTH, MEM_WIDTH, D_MODEL), MEM_WIDTH ** -0.5),
        'g_mlp': gain(44, (DEPTH, D_MODEL)),
        'w_up': nrm(45, (DEPTH, D_MODEL, D_FF), D_MODEL ** -0.5),
        'w_down': nrm(46, (DEPTH, D_FF, D_MODEL), D_FF ** -0.5),
    }


def reference(x_prompt, x_sample, mem_prompt, cache_fox_k, cache_fox_v, cache_fox_logf, page_table,
              state_s5_re, state_s5_im, state_mlstm_C, state_mlstm_n, state_mlstm_m, cache_mem_k, cache_mem_v,
              g_mix, w_in, s5_a_re, s5_a_im, s5_log_step, s5_b_re, s5_b_im, s5_c_re, s5_c_im, s5_d,
              s5_w_glu, s5_b_glu, fox_gq, fox_gk, fox_bf, ml_bi, ml_bf, ml_gn, w_br_s5, w_br_fox, w_br_ml,
              w_out, g_cross, w_cq, cross_gq, g_mem, w_mk, w_mv, cross_gk, w_co, g_mlp, w_up, w_down):
    f32 = jnp.float32
    past_len = page_table.shape[1] * cache_fox_k.shape[2]
    bp = x_prompt.shape[0]
    yp, ys = x_prompt, x_sample
    st_p, st_s = [], []
    for l in range(DEPTH):
        W = {
            'g_mix': g_mix[l], 'w_in': w_in[l],
            's5_a_re': s5_a_re[l], 's5_a_im': s5_a_im[l], 's5_log_step': s5_log_step[l],
            's5_b_re': s5_b_re[l], 's5_b_im': s5_b_im[l], 's5_c_re': s5_c_re[l], 's5_c_im': s5_c_im[l],
            's5_d': s5_d[l], 's5_w_glu': s5_w_glu[l], 's5_b_glu': s5_b_glu[l],
            'fox_gq': fox_gq[l], 'fox_gk': fox_gk[l], 'fox_bf': fox_bf[l],
            'ml_bi': ml_bi[l], 'ml_bf': ml_bf[l], 'ml_gn': ml_gn[l],
            'w_br_s5': w_br_s5[l], 'w_br_fox': w_br_fox[l], 'w_br_ml': w_br_ml[l], 'w_out': w_out[l],
            'g_cross': g_cross[l], 'w_cq': w_cq[l], 'cross_gq': cross_gq[l], 'w_co': w_co[l],
            'g_mlp': g_mlp[l], 'w_up': w_up[l], 'w_down': w_down[l],
        }
        mk_p, mv_p = memory_kv(mem_prompt, g_mem[l], w_mk[l], w_mv[l], cross_gk[l])
        z_s5 = jnp.zeros((bp, S5_GROUPS, S5_STATE), f32)
        yp, sp = hybrid_layer(yp, mk_p, mv_p, None, z_s5, z_s5,
                              jnp.zeros((bp, ML_HEADS, ML_HD, ML_HD), f32),
                              jnp.zeros((bp, ML_HEADS, ML_HD), f32),
                              jnp.zeros((bp, ML_HEADS), f32), 0, W)
        st_p.append(sp + (mk_p, mv_p))
        past = (gather_pages(cache_fox_k[l], page_table), gather_pages(cache_fox_v[l], page_table),
                gather_pages(cache_fox_logf[l], page_table))
        ys, ss = hybrid_layer(ys, cache_mem_k[l], cache_mem_v[l], past, state_s5_re[l], state_s5_im[l],
                              state_mlstm_C[l], state_mlstm_n[l], state_mlstm_m[l], past_len, W)
        st_s.append(ss)
    (p_fox_k, p_fox_v, p_fox_logf, p_s5_re, p_s5_im, p_ml_C, p_ml_n, p_ml_m, p_mem_k, p_mem_v) = [jnp.stack(a) for a in zip(*st_p)]
    (s_fox_k, s_fox_v, s_fox_logf, s_s5_re, s_s5_im, s_ml_C, s_ml_n, s_ml_m) = [jnp.stack(a) for a in zip(*st_s)]
    return (yp, ys, p_fox_k, p_fox_v, p_fox_logf, p_s5_re, p_s5_im, p_ml_C, p_ml_n, p_ml_m, p_mem_k, p_mem_v,
            s_fox_k, s_fox_v, s_fox_logf, s_s5_re, s_s5_im, s_ml_C, s_ml_n, s_ml_m)
```

```python
import functools
import math

import jax
import jax.numpy as jnp
import numpy as np
from jax import lax
from jax.experimental import pallas as pl
from jax.experimental.pallas import tpu as pltpu

F32 = jnp.float32
BF16 = jnp.bfloat16

LANES = 128
SUBLANES = 8
MIN_BF16_ROWS = 16
VMEM_LIMIT_BYTES = 48 * 1024 * 1024

D_MODEL = 1024
S5_WIDTH = 512
S5_GROUP = 16
S5_GROUPS = 32
S5_STATE = 64
S5_LANES = S5_GROUPS * S5_STATE
FOX_HEADS = 8
FOX_HD = 64
FOX_WIDTH = 512
ML_HEADS = 4
ML_HD = 128
ML_WIDTH = 512
MEM_HEADS = 4
MEM_HD = 128
MEM_WIDTH = 512
D_FF = 4096
EPS = 1e-6
SPLITS = (S5_WIDTH, FOX_WIDTH, FOX_WIDTH, FOX_WIDTH, FOX_HEADS, ML_WIDTH, ML_WIDTH, ML_WIDTH,
          ML_HEADS, ML_HEADS, ML_WIDTH, 3 * D_MODEL)

COL_S5, COL_FQ, COL_FK, COL_FV, COL_MQ, COL_MK, COL_MV, COL_MO = range(8)
COL_GATES = 8
COL_SMALL = 14
D_PACK = 15 * 512
LANE_FOXF = 0
LANE_MLI = 8
LANE_MLF = 12
SEQ_TILE = 256
SAMPLE_T = 8


def _cparams(*sem):
    return pltpu.CompilerParams(dimension_semantics=sem, vmem_limit_bytes=VMEM_LIMIT_BYTES)


def _dot(a, b):
    return jnp.dot(a, b, preferred_element_type=F32)


def _dot_nt(a, b):
    return lax.dot_general(a, b, (((1,), (1,)), ((), ())), preferred_element_type=F32)


def _dot_tn(a, b):
    return lax.dot_general(a, b, (((0,), (0,)), ((), ())), preferred_element_type=F32)


def _split3(x):
    hi = x.astype(BF16)
    r1 = x - hi.astype(F32)
    mid = r1.astype(BF16)
    lo = (r1 - mid.astype(F32)).astype(BF16)
    return hi, mid, lo


def _dot_exact_rhs(x, ones_rhs):
    hi, mid, lo = _split3(x)
    return _dot(hi, ones_rhs) + _dot(mid, ones_rhs) + _dot(lo, ones_rhs)


def _dot_exact_lhs(ones_lhs, x):
    hi, mid, lo = _split3(x)
    return _dot(ones_lhs, hi) + _dot(ones_lhs, mid) + _dot(ones_lhs, lo)


def _pad_rows(x, n, fill=0.0):
    if x.shape[0] >= n:
        return x
    return jnp.concatenate([x, jnp.full((n - x.shape[0], x.shape[1]), fill, x.dtype)], axis=0)


def _log_sigmoid(a):
    return jnp.minimum(a, 0.0) - jnp.log1p(jnp.exp(-jnp.abs(a)))


def _sigmoid(a):
    return 1.0 / (1.0 + jnp.exp(-a))


def _gelu_tanh(x):
    c = math.sqrt(2.0 / math.pi)
    return 0.5 * x * (1.0 + jnp.tanh(c * (x + 0.044715 * (x * x * x))))


def _rms(x, g):
    ms = jnp.mean(x * x, axis=-1, keepdims=True)
    return x * lax.rsqrt(ms + EPS) * g


def _norm_matmul_kernel(x_ref, g_ref, w_ref, *rest, head_norm):
    if head_norm:
        hg_ref, o_ref, hn_ref = rest
    else:
        o_ref, hn_ref = rest

    @pl.when(pl.program_id(1) == 0)
    def _():
        hn_ref[...] = _rms(x_ref[...], g_ref[...]).astype(BF16)

    y = _dot(hn_ref[...], w_ref[...])
    if head_norm:
        tn = y.shape[1]
        for s in range(tn // LANES):
            sl = slice(s * LANES, (s + 1) * LANES)
            o_ref[:, sl] = _rms(y[:, sl], hg_ref[...]).astype(o_ref.dtype)
    else:
        o_ref[...] = y.astype(o_ref.dtype)


def norm_matmul(x, g, w, head_gain=None, out_dtype=F32, tm=512, tn=512):
    m, k = x.shape
    n = w.shape[1]
    tm = min(tm, m)
    tn = min(tn, n)
    head_norm = head_gain is not None
    in_specs = [pl.BlockSpec((tm, k), lambda i, j: (i, 0)),
                pl.BlockSpec((1, k), lambda i, j: (0, 0)),
                pl.BlockSpec((k, tn), lambda i, j: (0, j))]
    args = [x, g.reshape(1, k), w]
    if head_norm:
        in_specs.append(pl.BlockSpec((1, LANES), lambda i, j: (0, 0)))
        args.append(head_gain.reshape(1, LANES))
    return pl.pallas_call(
        functools.partial(_norm_matmul_kernel, head_norm=head_norm),
        out_shape=jax.ShapeDtypeStruct((m, n), out_dtype),
        grid=(m // tm, n // tn),
        in_specs=in_specs,
        out_specs=pl.BlockSpec((tm, tn), lambda i, j: (i, j)),
        scratch_shapes=[pltpu.VMEM((tm, k), BF16)],
        compiler_params=_cparams("parallel", "arbitrary"),
        name="norm_matmul",
    )(*args)


def _prep_kernel(fq_ref, fk_ref, sm_ref, gq_ref, gk_ref, bias_ref, gmat_ref, tril_ref,
                 qx_ref, kn_ref, g_ref, cs_ref, grow_ref, csrow_ref, carry_ref, *, t_valid):
    c = pl.program_id(1)
    tc = sm_ref.shape[0]
    tp = tril_ref.shape[0]

    @pl.when(c == 0)
    def _():
        carry_ref[...] = jnp.zeros_like(carry_ref)

    gmat = gmat_ref[...]

    def head_rms(x, gain):
        x2 = _pad_rows(x * x, MIN_BF16_ROWS)
        hi = x2.astype(BF16)
        lo = (x2 - hi.astype(F32)).astype(BF16)
        ss = ((_dot(hi, gmat) + _dot(lo, gmat)) * (1.0 / FOX_HD))[0:tc]
        return x * lax.rsqrt(ss + EPS) * gain

    qn = head_rms(fq_ref[...], gq_ref[...]) * (FOX_HD ** -0.5)
    kn_ref[...] = head_rms(fk_ref[...], gk_ref[...])
    lane1 = lax.broadcasted_iota(jnp.int32, (1, LANES), 1)
    for h in range(FOX_HEADS):
        pair = qn[:, (h // 2) * LANES:(h // 2 + 1) * LANES]
        keep = (lane1 >= FOX_HD) if (h % 2) else (lane1 < FOX_HD)
        qx_ref[:, h * LANES:(h + 1) * LANES] = jnp.where(keep, pair, 0.0).astype(qx_ref.dtype)

    a = _pad_rows(sm_ref[...], tp) + bias_ref[...]
    lane = lax.broadcasted_iota(jnp.int32, a.shape, 1)
    row = lax.broadcasted_iota(jnp.int32, a.shape, 0) + c * tc
    is_i = (lane >= LANE_MLI) & (lane < LANE_MLF)
    used = lane < LANE_MLF + ML_HEADS
    valid = row < t_valid
    g = jnp.where(is_i, a, _log_sigmoid(a))
    g = jnp.where(used, g, 0.0)
    g = jnp.where(valid, g, jnp.where(is_i, -jnp.inf, 0.0))
    gc = jnp.where(is_i, 0.0, g)
    cs = _dot_exact_lhs(tril_ref[...], gc)
    carry = carry_ref[...]
    csg = cs + jnp.where(lane < FOX_HEADS, carry, 0.0)
    carry_ref[...] = carry + cs[tc - 1:tc, :]
    g_ref[...] = g[0:tc]
    cs_ref[...] = csg[0:tc]
    grow_ref[...] = g.T
    csrow_ref[...] = csg.T


def gate_prep(z, gq, gk, bias_row, nb, t, tc, t_valid, qx_dtype):
    m = nb * t
    nc = t // tc
    gmat = jnp.asarray(np.kron(np.eye(FOX_HEADS), np.ones((FOX_HD, FOX_HD))), BF16)
    tp = max(tc, LANES)
    tril = jnp.asarray(np.tril(np.ones((tp, tp))), BF16)
    row_map = lambda b, c: (b * nc + c, 0)
    const = lambda b, c: (0, 0)
    return pl.pallas_call(
        functools.partial(_prep_kernel, t_valid=t_valid),
        out_shape=(jax.ShapeDtypeStruct((m, FOX_HEADS * LANES), qx_dtype),
                   jax.ShapeDtypeStruct((m, FOX_WIDTH), F32),
                   jax.ShapeDtypeStruct((m, LANES), F32),
                   jax.ShapeDtypeStruct((m, LANES), F32),
                   jax.ShapeDtypeStruct((nb * LANES, nc * tp), F32),
                   jax.ShapeDtypeStruct((nb * LANES, nc * tp), F32)),
        grid=(nb, nc),
        in_specs=[pl.BlockSpec((tc, 512), lambda b, c: (b * nc + c, COL_FQ)),
                  pl.BlockSpec((tc, 512), lambda b, c: (b * nc + c, COL_FK)),
                  pl.BlockSpec((tc, LANES), lambda b, c: (b * nc + c, COL_SMALL * 4)),
                  pl.BlockSpec((1, 512), const), pl.BlockSpec((1, 512), const),
                  pl.BlockSpec((1, LANES), const),
                  pl.BlockSpec((512, 512), const), pl.BlockSpec((tp, tp), const)],
        out_specs=(pl.BlockSpec((tc, FOX_HEADS * LANES), row_map),
                   pl.BlockSpec((tc, FOX_WIDTH), row_map),
                   pl.BlockSpec((tc, LANES), row_map),
                   pl.BlockSpec((tc, LANES), row_map),
                   pl.BlockSpec((LANES, tp), lambda b, c: (b, c)),
                   pl.BlockSpec((LANES, tp), lambda b, c: (b, c))),
        scratch_shapes=[pltpu.VMEM((1, LANES), F32)],
        compiler_params=_cparams("parallel", "arbitrary"),
        name="gate_prep",
    )(z, z, z, gq, gk, bias_row, gmat, tril)


S5_SCAN_LANES = 256


def _s5_kernel(u_ref, h0r_ref, h0i_ref, bre_ref, bim_ref, pw_ref, cre_ref, cim_ref, d_ref, wglu_ref, bglu_ref,
               y_ref, hlr_ref, hli_ref, hr_ref, hi_ref, cr_ref, ci_ref, *, t_last):
    c = pl.program_id(1)
    nc = pl.num_programs(1)
    tc = u_ref.shape[0]

    @pl.when(c == 0)
    def _():
        cr_ref[...] = jnp.broadcast_to(h0r_ref[0], cr_ref.shape)
        ci_ref[...] = jnp.broadcast_to(h0i_ref[0], ci_ref.shape)

    u = u_ref[...]
    ub = _pad_rows(u, MIN_BF16_ROWS).astype(BF16)
    hr_ref[...] = _dot(ub, bre_ref[...])[0:tc]
    hi_ref[...] = _dot(ub, bim_ref[...])[0:tc]

    for lc in range(S5_LANES // S5_SCAN_LANES):
        ls = slice(lc * S5_SCAN_LANES, (lc + 1) * S5_SCAN_LANES)
        pw = [pw_ref[i, :, ls] for i in range(8)]

        def body(r, carry):
            car_r, car_i = carry
            rows = pl.ds(pl.multiple_of(r * SUBLANES, SUBLANES), SUBLANES)
            xr = hr_ref[rows, ls]
            xi = hi_ref[rows, ls]
            for j, lag in enumerate((1, 2, 4)):
                pr, pi = pw[2 * j], pw[2 * j + 1]
                sr = pltpu.roll(xr, lag, 0)
                si = pltpu.roll(xi, lag, 0)
                xr, xi = xr + pr * sr - pi * si, xi + pr * si + pi * sr
            pr, pi = pw[6], pw[7]
            xr, xi = xr + pr * car_r - pi * car_i, xi + pr * car_i + pi * car_r
            hr_ref[rows, ls] = xr
            hi_ref[rows, ls] = xi
            return (jnp.broadcast_to(xr[SUBLANES - 1:SUBLANES, :], xr.shape),
                    jnp.broadcast_to(xi[SUBLANES - 1:SUBLANES, :], xi.shape))

        car = lax.fori_loop(0, tc // SUBLANES, body, (cr_ref[:, ls], ci_ref[:, ls]))
        cr_ref[:, ls] = car[0]
        ci_ref[:, ls] = car[1]

    @pl.when(c == nc - 1)
    def _():
        tl = t_last % tc
        hlr_ref[0] = hr_ref[tl:tl + 1, :]
        hli_ref[0] = hi_ref[tl:tl + 1, :]

    y = (_dot(_pad_rows(hr_ref[...], MIN_BF16_ROWS).astype(BF16), cre_ref[...])
         + _dot(_pad_rows(hi_ref[...], MIN_BF16_ROWS).astype(BF16), cim_ref[...]))[0:tc]
    y = _gelu_tanh(y + d_ref[...] * u)
    gate = _dot(_pad_rows(y, MIN_BF16_ROWS).astype(BF16), wglu_ref[...])[0:tc]
    y_ref[...] = y * _sigmoid(gate + bglu_ref[...])


def s5_mixer(z, h0r, h0i, sp, nb, t, tc, t_last):
    nc = t // tc
    const = lambda b, c: (0, 0)
    state_spec = pl.BlockSpec((1, 1, S5_LANES), lambda b, c: (b, 0, 0))
    return pl.pallas_call(
        functools.partial(_s5_kernel, t_last=t_last),
        out_shape=(jax.ShapeDtypeStruct((nb * t, S5_WIDTH), F32),
                   jax.ShapeDtypeStruct((nb, 1, S5_LANES), F32),
                   jax.ShapeDtypeStruct((nb, 1, S5_LANES), F32)),
        grid=(nb, nc),
        in_specs=[pl.BlockSpec((tc, 512), lambda b, c: (b * nc + c, COL_S5)),
                  state_spec, state_spec,
                  pl.BlockSpec((S5_WIDTH, S5_LANES), const), pl.BlockSpec((S5_WIDTH, S5_LANES), const),
                  pl.BlockSpec((8, SUBLANES, S5_LANES), lambda b, c: (0, 0, 0)),
                  pl.BlockSpec((S5_LANES, S5_WIDTH), const), pl.BlockSpec((S5_LANES, S5_WIDTH), const),
                  pl.BlockSpec((1, S5_WIDTH), const),
                  pl.BlockSpec((S5_WIDTH, S5_WIDTH), const), pl.BlockSpec((1, S5_WIDTH), const)],
        out_specs=(pl.BlockSpec((tc, S5_WIDTH), lambda b, c: (b * nc + c, 0)), state_spec, state_spec),
        scratch_shapes=[pltpu.VMEM((tc, S5_LANES), F32), pltpu.VMEM((tc, S5_LANES), F32),
                        pltpu.VMEM((SUBLANES, S5_LANES), F32), pltpu.VMEM((SUBLANES, S5_LANES), F32)],
        compiler_params=_cparams("parallel", "arbitrary"),
        name="s5_mixer",
    )(z, h0r.reshape(nb, 1, S5_LANES), h0i.reshape(nb, 1, S5_LANES), sp["bre"], sp["bim"], sp["pw"],
      sp["cre"], sp["cim"], sp["d"], sp["wglu"], sp["bglu"])


def s5_params(a_re, a_im, log_step, b_re, b_im, c_re, c_im, d_skip, w_glu, b_glu):
    dt = jnp.exp(log_step)[:, None]
    mag = jnp.exp(a_re * dt)
    lr = mag * jnp.cos(a_im * dt)
    li = mag * jnp.sin(a_im * dt)
    den = a_re * a_re + a_im * a_im
    xr, xi = lr - 1.0, li
    fr = (xr * a_re + xi * a_im) / den
    fi = (xi * a_re - xr * a_im) / den
    bbr = fr[..., None] * b_re - fi[..., None] * b_im
    bbi = fr[..., None] * b_im + fi[..., None] * b_re
    eye = jnp.eye(S5_GROUPS, dtype=F32)

    def in_mat(b):
        return jnp.einsum('gpc,gh->gchp', b, eye).reshape(S5_WIDTH, S5_LANES).astype(BF16)

    def out_mat(cm):
        return jnp.einsum('gcp,gh->gphc', cm, eye).reshape(S5_LANES, S5_WIDTH).astype(BF16)

    pr, pi = [lr], [li]
    for _ in range(7):
        pr.append(pr[-1] * lr - pi[-1] * li)
        pi.append(pr[-2] * li + pi[-1] * lr)
    flat = lambda a: a.reshape(1, S5_LANES)
    trow = jnp.arange(SUBLANES)[:, None]
    tabs = []
    for lag in (1, 2, 4):
        msk = (trow >= lag).astype(F32)
        tabs.append(msk * flat(pr[lag - 1]))
        tabs.append(msk * flat(pi[lag - 1]))
    tabs.append(jnp.concatenate([flat(p) for p in pr], axis=0))
    tabs.append(jnp.concatenate([flat(p) for p in pi], axis=0))
    return dict(bre=in_mat(bbr), bim=in_mat(bbi), pw=jnp.stack(tabs), cre=out_mat(c_re), cim=out_mat(-c_im),
                d=d_skip.reshape(1, S5_WIDTH), wglu=w_glu.astype(BF16), bglu=b_glu.reshape(1, S5_WIDTH))


def _fox_flash_kernel(qx_ref, k_ref, v_ref, fq_ref, fk_ref, o_ref, m_ref, l_ref, acc_ref):
    i = pl.program_id(1)
    j = pl.program_id(2)
    tq = qx_ref.shape[0]
    tk = k_ref.shape[0]

    @pl.when(j == 0)
    def _():
        m_ref[...] = jnp.full_like(m_ref, -jnp.inf)
        l_ref[...] = jnp.zeros_like(l_ref)
        acc_ref[...] = jnp.zeros_like(acc_ref)

    def step(masked):
        fq = fq_ref[...]
        fk = fk_ref[...]
        if masked:
            keep = (lax.broadcasted_iota(jnp.int32, (tq, tk), 0) + i * tq >=
                    lax.broadcasted_iota(jnp.int32, (tq, tk), 1) + j * tk)
        for h in range(FOX_HEADS):
            ps = slice((h // 2) * LANES, (h // 2 + 1) * LANES)
            kp = k_ref[:, ps].astype(BF16)
            vp = v_ref[:, ps].astype(BF16)
            s = _dot_nt(qx_ref[:, h * LANES:(h + 1) * LANES], kp)
            s = s + (fq[:, LANE_FOXF + h:LANE_FOXF + h + 1] - fk[LANE_FOXF + h:LANE_FOXF + h + 1, :])
            if masked:
                s = jnp.where(keep, s, -jnp.inf)
            m_prev = m_ref[h]
            m_new = jnp.maximum(m_prev, jnp.max(s, axis=-1, keepdims=True))
            alpha = jnp.exp(m_prev - m_new)
            p = jnp.exp(s - m_new)
            l_ref[h] = alpha * l_ref[h] + jnp.sum(p, axis=-1, keepdims=True)
            acc_ref[h] = alpha * acc_ref[h] + _dot(p.astype(BF16), vp)
            m_ref[h] = m_new

    q_lo = i * tq
    q_hi = i * tq + tq - 1
    k_lo = j * tk
    k_hi = j * tk + tk - 1

    @pl.when(k_hi <= q_lo)
    def _():
        step(False)

    @pl.when((k_lo <= q_hi) & (k_hi > q_lo))
    def _():
        step(True)

    @pl.when(j == pl.num_programs(2) - 1)
    def _():
        lane = lax.broadcasted_iota(jnp.int32, (1, LANES), 1)
        for p in range(FOX_HEADS // 2):
            lo = acc_ref[2 * p] / l_ref[2 * p]
            hi = acc_ref[2 * p + 1] / l_ref[2 * p + 1]
            o_ref[:, p * LANES:(p + 1) * LANES] = jnp.where(lane < FOX_HD, lo, hi)


def fox_flash(qx, kn, z, cs, csrow, nb, t, tq, tk):
    nq, nk = t // tq, t // tk

    def kv_idx(i, j):
        return jnp.minimum(j, (i * tq + tq - 1) // tk)

    return pl.pallas_call(
        _fox_flash_kernel,
        out_shape=jax.ShapeDtypeStruct((nb * t, FOX_WIDTH), F32),
        grid=(nb, nq, nk),
        in_specs=[pl.BlockSpec((tq, FOX_HEADS * LANES), lambda b, i, j: (b * nq + i, 0)),
                  pl.BlockSpec((tk, FOX_WIDTH), lambda b, i, j: (b * nk + kv_idx(i, j), 0)),
                  pl.BlockSpec((tk, FOX_WIDTH), lambda b, i, j: (b * nk + kv_idx(i, j), COL_FV)),
                  pl.BlockSpec((tq, LANES), lambda b, i, j: (b * nq + i, 0)),
                  pl.BlockSpec((SUBLANES, tk), lambda b, i, j: (b * (LANES // SUBLANES), kv_idx(i, j)))],
        out_specs=pl.BlockSpec((tq, FOX_WIDTH), lambda b, i, j: (b * nq + i, 0)),
        scratch_shapes=[pltpu.VMEM((FOX_HEADS, tq, 1), F32), pltpu.VMEM((FOX_HEADS, tq, 1), F32),
                        pltpu.VMEM((FOX_HEADS, tq, LANES), F32)],
        compiler_params=_cparams("parallel", "parallel", "arbitrary"),
        name="fox_flash",
    )(qx, kn, z, cs, csrow)


FOX_PAGES_PER_STEP = 4


def _fox_decode_kernel(pt_ref, qx_ref, kn_ref, vn_ref, csrow_ref, *rest, n_pages):
    npg = FOX_PAGES_PER_STEP
    k_refs = rest[0:npg]
    v_refs = rest[npg:2 * npg]
    f_refs = rest[2 * npg:3 * npg]
    triu_ref = rest[3 * npg]
    o_ref = rest[3 * npg + 1]
    qb_ref, m_ref, l_ref, acc_ref, fc_ref, kpad_ref, vpad_ref = rest[3 * npg + 2:]
    j = pl.program_id(1)
    page = kpad_ref.shape[0]
    nrow = FOX_HEADS * SAMPLE_T

    @pl.when(j == 0)
    def _():
        qb_ref[...] = jnp.zeros_like(qb_ref)
        for h in range(FOX_HEADS):
            ps = slice((h // 2) * LANES, (h // 2 + 1) * LANES)
            qb_ref[h * SAMPLE_T:(h + 1) * SAMPLE_T, ps] = qx_ref[:, h * LANES:(h + 1) * LANES]
        m_ref[...] = jnp.full_like(m_ref, -jnp.inf)
        l_ref[...] = jnp.zeros_like(l_ref)
        acc_ref[...] = jnp.zeros_like(acc_ref)
        fc_ref[...] = jnp.zeros_like(fc_ref)

    qb = qb_ref[...].astype(BF16)

    def update(s, vs):
        m_prev = m_ref[...]
        m_new = jnp.maximum(m_prev, jnp.max(s, axis=-1, keepdims=True))
        alpha = jnp.exp(m_prev - m_new)
        p = jnp.exp(s - m_new)
        l_ref[...] = alpha * l_ref[...] + jnp.sum(p, axis=-1, keepdims=True)
        pv = _dot(p[:, 0:page].astype(BF16), vs[0])
        for i in range(1, len(vs)):
            pv = pv + _dot(p[:, i * page:(i + 1) * page].astype(BF16), vs[i])
        acc_ref[...] = alpha * acc_ref[...] + pv
        m_ref[...] = m_new

    def head_rows(x):
        return jnp.concatenate([jnp.broadcast_to(x[h:h + 1, :], (SAMPLE_T, x.shape[1])) for h in range(FOX_HEADS)],
                               axis=0)

    ss, vs = [], []
    for i in range(npg):
        cum = _dot_exact_rhs(_pad_rows(f_refs[i][0], MIN_BF16_ROWS), triu_ref[...])[0:FOX_HEADS] + fc_ref[...]
        fc_ref[...] = jnp.broadcast_to(cum[:, page - 1:page], fc_ref.shape)
        ss.append(_dot_nt(qb, k_refs[i][0].astype(BF16)) - head_rows(cum))
        vs.append(v_refs[i][0].astype(BF16))
    update(jnp.concatenate(ss, axis=1), vs)

    @pl.when(j == n_pages // npg - 1)
    def _():
        kpad_ref[...] = jnp.zeros_like(kpad_ref)
        vpad_ref[...] = jnp.zeros_like(vpad_ref)
        kpad_ref[0:SAMPLE_T, :] = kn_ref[...]
        vpad_ref[0:SAMPLE_T, :] = vn_ref[...]
        bias = head_rows(csrow_ref[...] + fc_ref[...])
        s = _dot_nt(qb, kpad_ref[...].astype(BF16)) - bias
        tq_idx = lax.broadcasted_iota(jnp.int32, (nrow, page), 0) % SAMPLE_T
        tk_idx = lax.broadcasted_iota(jnp.int32, (nrow, page), 1)
        s = jnp.where(tk_idx <= tq_idx, s, -jnp.inf)
        update(s, [vpad_ref[...].astype(BF16)])
        lane = lax.broadcasted_iota(jnp.int32, (1, FOX_WIDTH), 1)
        out = jnp.zeros((SAMPLE_T, FOX_WIDTH), F32)
        for h in range(FOX_HEADS):
            rows = slice(h * SAMPLE_T, (h + 1) * SAMPLE_T)
            oh = acc_ref[rows, :] / l_ref[rows, :]
            out = out + jnp.where((lane >= h * FOX_HD) & (lane < (h + 1) * FOX_HD), oh, 0.0)
        o_ref[...] = out


def fox_decode(page_table, qx, kn, z, csrow, pool_k, pool_v, pool_f_rows):
    nb, n_pages = page_table.shape
    page = pool_k.shape[1]
    assert page == LANES and n_pages % FOX_PAGES_PER_STEP == 0
    npg = FOX_PAGES_PER_STEP
    nrow = FOX_HEADS * SAMPLE_T
    pt = page_table.reshape(-1)
    triu = jnp.asarray(np.triu(np.ones((page, page))), BF16)

    def pg(i):
        return lambda b, j, pt: (pt[b * n_pages + j * npg + i], 0, 0)

    row = lambda b, j, pt: (b, 0)
    in_specs = [pl.BlockSpec((SAMPLE_T, FOX_HEADS * LANES), row),
                pl.BlockSpec((SAMPLE_T, FOX_WIDTH), row),
                pl.BlockSpec((SAMPLE_T, FOX_WIDTH), lambda b, j, pt: (b, COL_FV)),
                pl.BlockSpec((FOX_HEADS, page), lambda b, j, pt: (b * (LANES // FOX_HEADS), 0))]
    in_specs += [pl.BlockSpec((1, page, FOX_WIDTH), pg(i)) for i in range(npg)]
    in_specs += [pl.BlockSpec((1, page, FOX_WIDTH), pg(i)) for i in range(npg)]
    in_specs += [pl.BlockSpec((1, FOX_HEADS, page), pg(i)) for i in range(npg)]
    in_specs += [pl.BlockSpec((page, page), lambda b, j, pt: (0, 0))]
    return pl.pallas_call(
        functools.partial(_fox_decode_kernel, n_pages=n_pages),
        out_shape=jax.ShapeDtypeStruct((nb * SAMPLE_T, FOX_WIDTH), F32),
        grid_spec=pltpu.PrefetchScalarGridSpec(
            num_scalar_prefetch=1,
            grid=(nb, n_pages // npg),
            in_specs=in_specs,
            out_specs=pl.BlockSpec((SAMPLE_T, FOX_WIDTH), row),
            scratch_shapes=[pltpu.VMEM((nrow, FOX_WIDTH), F32), pltpu.VMEM((nrow, 1), F32),
                            pltpu.VMEM((nrow, 1), F32), pltpu.VMEM((nrow, FOX_WIDTH), F32),
                            pltpu.VMEM((FOX_HEADS, page), F32),
                            pltpu.VMEM((page, FOX_WIDTH), F32), pltpu.VMEM((page, FOX_WIDTH), F32)]),
        compiler_params=_cparams("parallel", "arbitrary"),
        name="fox_decode",
    )(pt, qx, kn, z, csrow, *([pool_k] * npg), *([pool_v] * npg), *([pool_f_rows] * npg), triu)


def _mlstm_kernel(q_ref, k_ref, v_ref, o_ref, g_ref, cs_ref, grow_ref, csrow_ref, c0_ref, n0_ref, m0_ref, gn_ref,
                  y_ref, cout_ref, nout_ref, mout_ref, c_ref, n_ref, m_ref):
    c = pl.program_id(1)
    nc = pl.num_programs(1)
    L = q_ref.shape[0]

    @pl.when(c == 0)
    def _():
        c_ref[...] = c0_ref[0]
        n_ref[...] = n0_ref[0]
        m_ref[...] = m0_ref[0]

    Lp = grow_ref.shape[1]
    g = g_ref[...]
    cs = cs_ref[...]
    grow = grow_ref[...]
    csrow = csrow_ref[...]
    causal = (lax.broadcasted_iota(jnp.int32, (Lp, Lp), 0) >= lax.broadcasted_iota(jnp.int32, (Lp, Lp), 1))
    for h in range(ML_HEADS):
        hs = slice(h * ML_HD, (h + 1) * ML_HD)
        qh = _pad_rows(q_ref[:, hs], Lp)
        kh = _pad_rows(k_ref[:, hs], Lp) * (ML_HD ** -0.5)
        vh = _pad_rows(v_ref[:, hs], Lp)
        qb, kb, vb = qh.astype(BF16), kh.astype(BF16), vh.astype(BF16)
        bcol = _pad_rows(cs[:, LANE_MLF + h:LANE_MLF + h + 1], Lp)
        icol = _pad_rows(g[:, LANE_MLI + h:LANE_MLI + h + 1], Lp, -jnp.inf)
        brow = csrow[LANE_MLF + h:LANE_MLF + h + 1, :]
        irow = grow[LANE_MLI + h:LANE_MLI + h + 1, :]
        m_prev = m_ref[0:1, h:h + 1]
        log_d = jnp.where(causal, bcol - brow + irow, -jnp.inf)
        log_inter = bcol + m_prev
        m_t = jnp.maximum(log_inter, jnp.max(log_d, axis=-1, keepdims=True))
        d_w = jnp.exp(log_d - m_t)
        inter_w = jnp.exp(log_inter - m_t)
        s = _dot_nt(qb, kb) * d_w
        ch = c_ref[h]
        n_row = n_ref[h:h + 1, :]
        num = _dot(s.astype(BF16), vb) + inter_w * _dot_nt(qb, ch.astype(BF16))
        den = jnp.sum(s, axis=-1, keepdims=True) + inter_w * jnp.sum(qh * n_row, axis=-1, keepdims=True)
        hh = num / jnp.maximum(jnp.abs(den), jnp.exp(-m_t))
        y_ref[:, hs] = _rms(hh[0:L], gn_ref[...]) * _sigmoid(o_ref[:, hs])
        m_end = m_t[L - 1:L, :]
        a_end = inter_w[L - 1:L, :]
        w_col = jnp.exp(bcol[L - 1:L, :] - bcol + icol - m_end)
        c_ref[h] = a_end * ch + _dot_tn((vh * w_col).astype(BF16), kb)
        n_ref[h:h + 1, :] = a_end * n_row + jnp.sum(kh * w_col, axis=0, keepdims=True)
        m_ref[0:1, h:h + 1] = m_end

    @pl.when(c == nc - 1)
    def _():
        cout_ref[0] = c_ref[...]
        nout_ref[0] = n_ref[...]
        mout_ref[0] = m_ref[...]


def mlstm(z, g, cs, grow, csrow, c0, n0, m0, gn, nb, t, L):
    nc = t // L
    rows = lambda b, c: (b * nc + c, 0)
    rr = 2 * SUBLANES
    m0p = jnp.zeros((nb, 1, LANES), F32).at[:, 0, :ML_HEADS].set(m0)
    outs = pl.pallas_call(
        _mlstm_kernel,
        out_shape=(jax.ShapeDtypeStruct((nb * t, ML_WIDTH), F32),
                   jax.ShapeDtypeStruct((nb, ML_HEADS, ML_HD, ML_HD), F32),
                   jax.ShapeDtypeStruct((nb, ML_HEADS, ML_HD), F32),
                   jax.ShapeDtypeStruct((nb, 1, LANES), F32)),
        grid=(nb, nc),
        in_specs=[pl.BlockSpec((L, 512), lambda b, c: (b * nc + c, COL_MQ)),
                  pl.BlockSpec((L, 512), lambda b, c: (b * nc + c, COL_MK)),
                  pl.BlockSpec((L, 512), lambda b, c: (b * nc + c, COL_MV)),
                  pl.BlockSpec((L, 512), lambda b, c: (b * nc + c, COL_MO)),
                  pl.BlockSpec((L, LANES), rows), pl.BlockSpec((L, LANES), rows),
                  pl.BlockSpec((rr, max(L, LANES)), lambda b, c: (b * (LANES // rr), c)),
                  pl.BlockSpec((rr, max(L, LANES)), lambda b, c: (b * (LANES // rr), c)),
                  pl.BlockSpec((1, ML_HEADS, ML_HD, ML_HD), lambda b, c: (b, 0, 0, 0)),
                  pl.BlockSpec((1, ML_HEADS, ML_HD), lambda b, c: (b, 0, 0)),
                  pl.BlockSpec((1, 1, LANES), lambda b, c: (b, 0, 0)),
                  pl.BlockSpec((1, ML_HD), lambda b, c: (0, 0))],
        out_specs=(pl.BlockSpec((L, ML_WIDTH), rows),
                   pl.BlockSpec((1, ML_HEADS, ML_HD, ML_HD), lambda b, c: (b, 0, 0, 0)),
                   pl.BlockSpec((1, ML_HEADS, ML_HD), lambda b, c: (b, 0, 0)),
                   pl.BlockSpec((1, 1, LANES), lambda b, c: (b, 0, 0))),
        scratch_shapes=[pltpu.VMEM((ML_HEADS, ML_HD, ML_HD), F32), pltpu.VMEM((ML_HEADS, ML_HD), F32),
                        pltpu.VMEM((1, LANES), F32)],
        compiler_params=_cparams("parallel", "arbitrary"),
        name="mlstm",
    )(z, z, z, z, g, cs, grow, csrow, c0, n0, m0p, gn.reshape(1, ML_HD))
    y, c_new, n_new, m_new = outs
    return y, c_new, n_new, m_new[:, 0, :ML_HEADS]


def _merge_kernel(x_ref, ys_ref, yf_ref, ym_ref, g0_ref, g1_ref, g2_ref, ws_ref, wf_ref, wm_ref, wo_ref, o_ref):
    merged = (_sigmoid(g0_ref[...]) * _dot(ys_ref[...].astype(BF16), ws_ref[...])
              + _sigmoid(g1_ref[...]) * _dot(yf_ref[...].astype(BF16), wf_ref[...])
              + _sigmoid(g2_ref[...]) * _dot(ym_ref[...].astype(BF16), wm_ref[...]))
    o_ref[...] = x_ref[...] + _dot(merged.astype(BF16), wo_ref[...])


def merge_out(x, ys, yf, ym, z, ws, wf, wm, wo, tm=256):
    m = x.shape[0]
    tm = min(tm, m)
    row = lambda i: (i, 0)
    const = lambda i: (0, 0)
    gate0 = COL_GATES // 2
    return pl.pallas_call(
        _merge_kernel,
        out_shape=jax.ShapeDtypeStruct((m, D_MODEL), F32),
        grid=(m // tm,),
        in_specs=[pl.BlockSpec((tm, D_MODEL), row),
                  pl.BlockSpec((tm, 512), row), pl.BlockSpec((tm, 512), row), pl.BlockSpec((tm, 512), row),
                  pl.BlockSpec((tm, D_MODEL), lambda i: (i, gate0)),
                  pl.BlockSpec((tm, D_MODEL), lambda i: (i, gate0 + 1)),
                  pl.BlockSpec((tm, D_MODEL), lambda i: (i, gate0 + 2)),
                  pl.BlockSpec((512, D_MODEL), const), pl.BlockSpec((512, D_MODEL), const),
                  pl.BlockSpec((512, D_MODEL), const), pl.BlockSpec((D_MODEL, D_MODEL), const)],
        out_specs=pl.BlockSpec((tm, D_MODEL), row),
        compiler_params=_cparams("parallel"),
        name="merge_out",
    )(x, ys, yf, ym, z, z, z, ws, wf, wm, wo)


def _cross_kernel(q_ref, k_ref, v_ref, o_ref):
    tq = q_ref.shape[0]
    for h in range(MEM_HEADS):
        hs = slice(h * MEM_HD, (h + 1) * MEM_HD)
        qh = _pad_rows(q_ref[:, hs], MIN_BF16_ROWS).astype(BF16)
        s = _dot_nt(qh, k_ref[:, hs].astype(BF16)) * (MEM_HD ** -0.5)
        m = jnp.max(s, axis=-1, keepdims=True)
        p = jnp.exp(s - m)
        l = jnp.sum(p, axis=-1, keepdims=True)
        o_ref[:, hs] = (_dot(p.astype(BF16), v_ref[:, hs].astype(BF16)) / l)[0:tq]


def cross_attend(q, mem_k, mem_v, nb, t, tq):
    nq = t // tq
    n_mem = mem_k.shape[0] // nb
    return pl.pallas_call(
        _cross_kernel,
        out_shape=jax.ShapeDtypeStruct((nb * t, MEM_WIDTH), F32),
        grid=(nb, nq),
        in_specs=[pl.BlockSpec((tq, MEM_WIDTH), lambda b, i: (b * nq + i, 0)),
                  pl.BlockSpec((n_mem, MEM_WIDTH), lambda b, i: (b, 0)),
                  pl.BlockSpec((n_mem, MEM_WIDTH), lambda b, i: (b, 0))],
        out_specs=pl.BlockSpec((tq, MEM_WIDTH), lambda b, i: (b * nq + i, 0)),
        compiler_params=_cparams("parallel", "parallel"),
        name="cross_attend",
    )(q, mem_k, mem_v)


def _proj_residual_kernel(x_ref, a_ref, w_ref, o_ref):
    o_ref[...] = x_ref[...] + _dot(a_ref[...].astype(BF16), w_ref[...])


def proj_residual(x, a, w, tm=512):
    m, n = x.shape
    k = a.shape[1]
    tm = min(tm, m)
    return pl.pallas_call(
        _proj_residual_kernel,
        out_shape=jax.ShapeDtypeStruct((m, n), F32),
        grid=(m // tm,),
        in_specs=[pl.BlockSpec((tm, n), lambda i: (i, 0)), pl.BlockSpec((tm, k), lambda i: (i, 0)),
                  pl.BlockSpec((k, n), lambda i: (0, 0))],
        out_specs=pl.BlockSpec((tm, n), lambda i: (i, 0)),
        compiler_params=_cparams("parallel"),
        name="proj_residual",
    )(x, a, w)


def _mlp_kernel(x_ref, g_ref, wu_ref, wd_ref, o_ref, hn_ref, acc_ref):
    f = pl.program_id(1)

    @pl.when(f == 0)
    def _():
        hn_ref[...] = _rms(x_ref[...], g_ref[...]).astype(BF16)
        acc_ref[...] = jnp.zeros_like(acc_ref)

    a = jnp.maximum(_dot(hn_ref[...], wu_ref[...]), 0.0)
    acc_ref[...] += _dot((a * a).astype(BF16), wd_ref[...])

    @pl.when(f == pl.num_programs(1) - 1)
    def _():
        o_ref[...] = x_ref[...] + acc_ref[...]


def mlp(x, g, wu, wd, tm=512, tf=512):
    m, d = x.shape
    dff = wu.shape[1]
    tm = min(tm, m)
    return pl.pallas_call(
        _mlp_kernel,
        out_shape=jax.ShapeDtypeStruct((m, d), F32),
        grid=(m // tm, dff // tf),
        in_specs=[pl.BlockSpec((tm, d), lambda i, f: (i, 0)), pl.BlockSpec((1, d), lambda i, f: (0, 0)),
                  pl.BlockSpec((d, tf), lambda i, f: (0, f)), pl.BlockSpec((tf, d), lambda i, f: (f, 0))],
        out_specs=pl.BlockSpec((tm, d), lambda i, f: (i, 0)),
        scratch_shapes=[pltpu.VMEM((tm, d), BF16), pltpu.VMEM((tm, d), F32)],
        compiler_params=_cparams("parallel", "arbitrary"),
        name="mlp",
    )(x, g.reshape(1, d), wu, wd)


def _pack_w_in(w_in):
    offs = np.concatenate([[0], np.cumsum(SPLITS)])
    col = lambda i: w_in[:, int(offs[i]):int(offs[i + 1])]
    s5, fq, fk, fv, ff, mq, mk, mv, mi, mf, mo, gates = [col(i) for i in range(12)]
    pad = jnp.zeros((w_in.shape[0], 512 - FOX_HEADS - 2 * ML_HEADS), w_in.dtype)
    return jnp.concatenate([s5, fq, fk, fv, mq, mk, mv, mo, gates, ff, mi, mf, pad], axis=1).astype(BF16)


def _layer_weights(l, g_mix, w_in, s5_a_re, s5_a_im, s5_log_step, s5_b_re, s5_b_im, s5_c_re, s5_c_im, s5_d,
                   s5_w_glu, s5_b_glu, fox_gq, fox_gk, fox_bf, ml_bi, ml_bf, ml_gn, w_br_s5, w_br_fox, w_br_ml,
                   w_out, g_cross, w_cq, cross_gq, g_mem, w_mk, w_mv, cross_gk, w_co, g_mlp, w_up, w_down):
    bias_row = jnp.zeros((1, LANES), F32)
    bias_row = bias_row.at[0, LANE_FOXF:LANE_FOXF + FOX_HEADS].set(fox_bf[l])
    bias_row = bias_row.at[0, LANE_MLI:LANE_MLI + ML_HEADS].set(ml_bi[l])
    bias_row = bias_row.at[0, LANE_MLF:LANE_MLF + ML_HEADS].set(ml_bf[l])
    return dict(
        g_mix=g_mix[l], w_in=_pack_w_in(w_in[l]),
        s5=s5_params(s5_a_re[l], s5_a_im[l], s5_log_step[l], s5_b_re[l], s5_b_im[l], s5_c_re[l], s5_c_im[l],
                     s5_d[l], s5_w_glu[l], s5_b_glu[l]),
        gq=jnp.tile(fox_gq[l], FOX_HEADS).reshape(1, FOX_WIDTH),
        gk=jnp.tile(fox_gk[l], FOX_HEADS).reshape(1, FOX_WIDTH),
        bias_row=bias_row, ml_gn=ml_gn[l],
        w_br_s5=w_br_s5[l].astype(BF16), w_br_fox=w_br_fox[l].astype(BF16), w_br_ml=w_br_ml[l].astype(BF16),
        w_out=w_out[l].astype(BF16), g_cross=g_cross[l], w_cq=w_cq[l].astype(BF16), cross_gq=cross_gq[l],
        g_mem=g_mem[l], w_mk=w_mk[l].astype(BF16), w_mv=w_mv[l].astype(BF16), cross_gk=cross_gk[l],
        w_co=w_co[l].astype(BF16), g_mlp=g_mlp[l], w_up=w_up[l].astype(BF16), w_down=w_down[l].astype(BF16))


def _hybrid_layer(x, W, nb, t, seq_tile, t_valid, s5_state, ml_state, mem_k, mem_v, fox_attend):
    z = norm_matmul(x, W["g_mix"], W["w_in"], tm=min(1024, nb * t))
    qx_dtype = BF16 if seq_tile % LANES == 0 else F32
    qx, kn, g, cs, grow, csrow = gate_prep(z, W["gq"], W["gk"], W["bias_row"], nb, t, seq_tile, t_valid, qx_dtype)
    y_s5, s5_re, s5_im = s5_mixer(z, s5_state[0], s5_state[1], W["s5"], nb, t, seq_tile, t_valid - 1)
    y_fox = fox_attend(qx, kn, z, cs, csrow)
    y_ml, c_new, n_new, m_new = mlstm(z, g, cs, grow, csrow, ml_state[0], ml_state[1], ml_state[2], W["ml_gn"],
                                      nb, t, seq_tile)
    x = merge_out(x, y_s5, y_fox, y_ml, z, W["w_br_s5"], W["w_br_fox"], W["w_br_ml"], W["w_out"])
    qc = norm_matmul(x, W["g_cross"], W["w_cq"], head_gain=W["cross_gq"])
    oc = cross_attend(qc, mem_k, mem_v, nb, t, seq_tile)
    x = proj_residual(x, oc, W["w_co"])
    x = mlp(x, W["g_mlp"], W["w_up"], W["w_down"])
    return x, z, kn, g, s5_re.reshape(nb, S5_GROUPS, S5_STATE), s5_im.reshape(nb, S5_GROUPS, S5_STATE), \
        c_new, n_new, m_new


def kernel(x_prompt, x_sample, mem_prompt, cache_fox_k, cache_fox_v, cache_fox_logf, page_table, state_s5_re, state_s5_im, state_mlstm_C, state_mlstm_n, state_mlstm_m, cache_mem_k, cache_mem_v, g_mix, w_in, s5_a_re, s5_a_im, s5_log_step, s5_b_re, s5_b_im, s5_c_re, s5_c_im, s5_d, s5_w_glu, s5_b_glu, fox_gq, fox_gk, fox_bf, ml_bi, ml_bf, ml_gn, w_br_s5, w_br_fox, w_br_ml, w_out, g_cross, w_cq, cross_gq, g_mem, w_mk, w_mv, cross_gk, w_co, g_mlp, w_up, w_down):
    depth = w_in.shape[0]
    bp, tp, _ = x_prompt.shape
    bs, ts, _ = x_sample.shape
    n_mem = mem_prompt.shape[1]
    n_pool, page = cache_fox_k.shape[1], cache_fox_k.shape[2]

    xp = x_prompt.reshape(bp * tp, D_MODEL)
    xs = jnp.pad(x_sample, ((0, 0), (0, SAMPLE_T - ts), (0, 0))).reshape(bs * SAMPLE_T, D_MODEL)
    mem = mem_prompt.reshape(bp * n_mem, D_MODEL)
    zeros_p = (jnp.zeros((bp, S5_LANES), F32), jnp.zeros((bp, S5_LANES), F32))
    zeros_ml = (jnp.zeros((bp, ML_HEADS, ML_HD, ML_HD), F32), jnp.zeros((bp, ML_HEADS, ML_HD), F32),
                jnp.zeros((bp, ML_HEADS), F32))
    st_p, st_s = [], []
    for l in range(depth):
        W = _layer_weights(l, g_mix, w_in, s5_a_re, s5_a_im, s5_log_step, s5_b_re, s5_b_im, s5_c_re, s5_c_im, s5_d,
                           s5_w_glu, s5_b_glu, fox_gq, fox_gk, fox_bf, ml_bi, ml_bf, ml_gn, w_br_s5, w_br_fox,
                           w_br_ml, w_out, g_cross, w_cq, cross_gq, g_mem, w_mk, w_mv, cross_gk, w_co, g_mlp, w_up,
                           w_down)
        mk_p = norm_matmul(mem, W["g_mem"], W["w_mk"], head_gain=W["cross_gk"])
        mv_p = norm_matmul(mem, W["g_mem"], W["w_mv"])
        flash = functools.partial(fox_flash, nb=bp, t=tp, tq=SEQ_TILE, tk=SEQ_TILE)
        xp, z, kn, g, s5r, s5i, c_new, n_new, m_new = _hybrid_layer(
            xp, W, bp, tp, SEQ_TILE, tp, zeros_p, zeros_ml, mk_p, mv_p, flash)
        st_p.append((kn.reshape(bp, tp, FOX_HEADS, FOX_HD),
                     z[:, COL_FV * 512:(COL_FV + 1) * 512].reshape(bp, tp, FOX_HEADS, FOX_HD),
                     g[:, LANE_FOXF:LANE_FOXF + FOX_HEADS].reshape(bp, tp, FOX_HEADS),
                     s5r, s5i, c_new, n_new, m_new,
                     mk_p.reshape(bp, n_mem, MEM_HEADS, MEM_HD), mv_p.reshape(bp, n_mem, MEM_HEADS, MEM_HD)))
        pool_k = cache_fox_k[l].reshape(n_pool, page, FOX_WIDTH)
        pool_v = cache_fox_v[l].reshape(n_pool, page, FOX_WIDTH)
        pool_f = jnp.swapaxes(cache_fox_logf[l], 1, 2)
        decode = functools.partial(fox_decode_adapter, page_table=page_table, pool_k=pool_k, pool_v=pool_v,
                                   pool_f=pool_f)
        xs, z, kn, g, s5r, s5i, c_new, n_new, m_new = _hybrid_layer(
            xs, W, bs, SAMPLE_T, SAMPLE_T, ts,
            (state_s5_re[l].reshape(bs, S5_LANES), state_s5_im[l].reshape(bs, S5_LANES)),
            (state_mlstm_C[l], state_mlstm_n[l], state_mlstm_m[l]),
            cache_mem_k[l].reshape(bs * n_mem, MEM_WIDTH), cache_mem_v[l].reshape(bs * n_mem, MEM_WIDTH), decode)
        st_s.append((kn.reshape(bs, SAMPLE_T, FOX_HEADS, FOX_HD)[:, :ts],
                     z[:, COL_FV * 512:(COL_FV + 1) * 512].reshape(bs, SAMPLE_T, FOX_HEADS, FOX_HD)[:, :ts],
                     g[:, LANE_FOXF:LANE_FOXF + FOX_HEADS].reshape(bs, SAMPLE_T, FOX_HEADS)[:, :ts],
                     s5r, s5i, c_new, n_new, m_new))
    outs_p = [jnp.stack(a) for a in zip(*st_p)]
    outs_s = [jnp.stack(a) for a in zip(*st_s)]
    yp = xp.reshape(bp, tp, D_MODEL)
    ys = xs.reshape(bs, SAMPLE_T, D_MODEL)[:, :ts]
    return (yp, ys, *outs_p, *outs_s)


def fox_decode_adapter(qx, kn, z, cs, csrow, *, page_table, pool_k, pool_v, pool_f):
    del cs
    return fox_decode(page_table, qx, kn, z, csrow, pool_k, pool_v, pool_f)
```

```python
import functools
import math

import jax
import jax.numpy as jnp
import numpy as np
from jax import lax
from jax.experimental import pallas as pl
from jax.experimental.pallas import tpu as pltpu

F32 = jnp.float32
BF16 = jnp.bfloat16

LANES = 128
SUBLANES = 8
MIN_BF16_ROWS = 16
MXU_DIM = 256
VMEM_LIMIT_BYTES = 48 * 1024 * 1024

D_MODEL = 1024
S5_WIDTH = 512
S5_GROUP = 16
S5_GROUPS = 32
S5_STATE = 64
S5_LANES = S5_GROUPS * S5_STATE
FOX_HEADS = 8
FOX_HD = 64
FOX_WIDTH = 512
ML_HEADS = 4
ML_HD = 128
ML_WIDTH = 512
MEM_HEADS = 4
MEM_HD = 128
MEM_WIDTH = 512
D_FF = 4096
EPS = 1e-6
SPLITS = (S5_WIDTH, FOX_WIDTH, FOX_WIDTH, FOX_WIDTH, FOX_HEADS, ML_WIDTH, ML_WIDTH, ML_WIDTH,
          ML_HEADS, ML_HEADS, ML_WIDTH, 3 * D_MODEL)

COL_S5, COL_FQ, COL_FK, COL_FV, COL_MQ, COL_MK, COL_MV, COL_MO = range(8)
COL_GATES = 8
COL_SMALL = 14
D_PACK = 15 * 512
LANE_FOXF = 0
LANE_MLI = 8
LANE_MLF = 12
SEQ_TILE = 256
FLASH_TILE = 512
SAMPLE_T = 8


def _cparams(*sem):
    return pltpu.CompilerParams(dimension_semantics=sem, vmem_limit_bytes=VMEM_LIMIT_BYTES)


def _dot(a, b):
    return jnp.dot(a, b, preferred_element_type=F32)


def _dot_nt(a, b):
    return lax.dot_general(a, b, (((1,), (1,)), ((), ())), preferred_element_type=F32)


def _dot_tn(a, b):
    return lax.dot_general(a, b, (((0,), (0,)), ((), ())), preferred_element_type=F32)


def _split3(x):
    hi = x.astype(BF16)
    r1 = x - hi.astype(F32)
    mid = r1.astype(BF16)
    lo = (r1 - mid.astype(F32)).astype(BF16)
    return hi, mid, lo


def _dot_exact_rhs(x, ones_rhs):
    hi, mid, lo = _split3(x)
    return _dot(hi, ones_rhs) + _dot(mid, ones_rhs) + _dot(lo, ones_rhs)


def _dot_exact_lhs(ones_lhs, x):
    hi, mid, lo = _split3(x)
    return _dot(ones_lhs, hi) + _dot(ones_lhs, mid) + _dot(ones_lhs, lo)


def _pad_rows(x, n, fill=0.0):
    if x.shape[0] >= n:
        return x
    return jnp.concatenate([x, jnp.full((n - x.shape[0], x.shape[1]), fill, x.dtype)], axis=0)


def _lane_tile(x, n):
    return x if n == 1 else jnp.concatenate([x] * n, axis=1)


def _log_sigmoid(a):
    return jnp.minimum(a, 0.0) - jnp.log1p(jnp.exp(-jnp.abs(a)))


def _sigmoid(a):
    return 1.0 / (1.0 + jnp.exp(-a))


def _gelu_tanh(x):
    c = math.sqrt(2.0 / math.pi)
    return 0.5 * x * (1.0 + jnp.tanh(c * (x + 0.044715 * (x * x * x))))


def _rms(x, g):
    ms = jnp.mean(x * x, axis=-1, keepdims=True)
    return x * lax.rsqrt(ms + EPS) * g


def _norm_matmul_kernel(x_ref, g_ref, w_ref, *rest, head_norm):
    if head_norm:
        hg_ref, o_ref, hn_ref = rest
    else:
        o_ref, hn_ref = rest

    @pl.when(pl.program_id(1) == 0)
    def _():
        hn_ref[...] = _rms(x_ref[...], g_ref[...]).astype(BF16)

    y = _dot(hn_ref[...], w_ref[...])
    if head_norm:
        tn = y.shape[1]
        for s in range(tn // LANES):
            sl = slice(s * LANES, (s + 1) * LANES)
            o_ref[:, sl] = _rms(y[:, sl], hg_ref[...]).astype(o_ref.dtype)
    else:
        o_ref[...] = y.astype(o_ref.dtype)


def norm_matmul(x, g, w, head_gain=None, out_dtype=F32, tm=512, tn=512):
    m, k = x.shape
    n = w.shape[1]
    tm = min(tm, m)
    tn = min(tn, n)
    head_norm = head_gain is not None
    in_specs = [pl.BlockSpec((tm, k), lambda i, j: (i, 0)),
                pl.BlockSpec((1, k), lambda i, j: (0, 0)),
                pl.BlockSpec((k, tn), lambda i, j: (0, j))]
    args = [x, g.reshape(1, k), w]
    if head_norm:
        in_specs.append(pl.BlockSpec((1, LANES), lambda i, j: (0, 0)))
        args.append(head_gain.reshape(1, LANES))
    return pl.pallas_call(
        functools.partial(_norm_matmul_kernel, head_norm=head_norm),
        out_shape=jax.ShapeDtypeStruct((m, n), out_dtype),
        grid=(m // tm, n // tn),
        in_specs=in_specs,
        out_specs=pl.BlockSpec((tm, tn), lambda i, j: (i, j)),
        scratch_shapes=[pltpu.VMEM((tm, k), BF16)],
        compiler_params=_cparams("parallel", "arbitrary"),
        name="norm_matmul",
    )(*args)


def _prep_kernel(fq_ref, fk_ref, fv_ref, sm_ref, gq_ref, gk_ref, bias_ref, gmat_ref, tril_ref, *rest,
                 t_valid, augment):
    if augment:
        qx_ref, kn_ref, g_ref, cs_ref, grow_ref, csrow_ref, kx_ref, vb_ref, carry_ref = rest
    else:
        qx_ref, kn_ref, g_ref, cs_ref, grow_ref, csrow_ref, carry_ref = rest
    c = pl.program_id(1)
    tc = sm_ref.shape[0]
    tp = tril_ref.shape[0]

    @pl.when(c == 0)
    def _():
        carry_ref[...] = jnp.zeros_like(carry_ref)

    gmat = gmat_ref[...]

    def head_rms(x, gain):
        x2 = _pad_rows(x * x, MIN_BF16_ROWS)
        hi = x2.astype(BF16)
        lo = (x2 - hi.astype(F32)).astype(BF16)
        ss = ((_dot(hi, gmat) + _dot(lo, gmat)) * (1.0 / FOX_HD))[0:tc]
        return x * lax.rsqrt(ss + EPS) * gain

    qn = head_rms(fq_ref[...], gq_ref[...]) * (FOX_HD ** -0.5)
    kn = head_rms(fk_ref[...], gk_ref[...])
    kn_ref[...] = kn

    a = _pad_rows(sm_ref[...], tp) + bias_ref[...]
    lane = lax.broadcasted_iota(jnp.int32, a.shape, 1)
    row = lax.broadcasted_iota(jnp.int32, a.shape, 0) + c * tc
    is_i = (lane >= LANE_MLI) & (lane < LANE_MLF)
    used = lane < LANE_MLF + ML_HEADS
    valid = row < t_valid
    g = jnp.where(is_i, a, _log_sigmoid(a))
    g = jnp.where(used, g, 0.0)
    g = jnp.where(valid, g, jnp.where(is_i, -jnp.inf, 0.0))
    gc = jnp.where(is_i, 0.0, g)
    cs = _dot_exact_lhs(tril_ref[...], gc)
    carry = carry_ref[...]
    csg = cs + jnp.where(lane < FOX_HEADS, carry, 0.0)
    carry_ref[...] = carry + cs[tc - 1:tc, :]
    g_ref[...] = g[0:tc]
    cs_ref[...] = csg[0:tc]
    grow_ref[...] = g.T
    csrow_ref[...] = csg.T

    lane1 = lax.broadcasted_iota(jnp.int32, (1, LANES), 1)
    f_hi = csg.astype(BF16).astype(F32)
    f_mid = (csg - f_hi).astype(BF16).astype(F32)
    f_lo = (csg - f_hi - f_mid).astype(BF16).astype(F32)
    for h in range(FOX_HEADS):
        ps = slice((h // 2) * LANES, (h // 2 + 1) * LANES)
        keep = (lane1 >= FOX_HD) if (h % 2) else (lane1 < FOX_HD)
        if not augment:
            qx_ref[:, h * LANES:(h + 1) * LANES] = jnp.where(keep, qn[:, ps], 0.0).astype(qx_ref.dtype)
            continue
        o = 0 if (h % 2) else FOX_HD
        fl = slice(LANE_FOXF + h, LANE_FOXF + h + 1)
        hi, mid, lo = f_hi[0:tc, fl], f_mid[0:tc, fl], f_lo[0:tc, fl]
        one_q = (lane1 >= o + 3) & (lane1 < o + 6)
        aug_q = jnp.where(lane1 == o, hi, jnp.where(lane1 == o + 1, mid, jnp.where(lane1 == o + 2, lo,
                          jnp.where(one_q, 1.0, 0.0))))
        one_k = (lane1 >= o) & (lane1 < o + 3)
        aug_k = jnp.where(lane1 == o + 3, -hi, jnp.where(lane1 == o + 4, -mid, jnp.where(lane1 == o + 5, -lo,
                          jnp.where(one_k, 1.0, 0.0))))
        qx_ref[:, h * LANES:(h + 1) * LANES] = jnp.where(keep, qn[:, ps], aug_q).astype(qx_ref.dtype)
        kx_ref[:, h * LANES:(h + 1) * LANES] = jnp.where(keep, kn[:, ps], aug_k).astype(kx_ref.dtype)
    if augment:
        vb_ref[...] = fv_ref[...].astype(vb_ref.dtype)


def gate_prep(z, gq, gk, bias_row, nb, t, tc, t_valid, augment):
    m = nb * t
    nc = t // tc
    gmat = jnp.asarray(np.kron(np.eye(FOX_HEADS), np.ones((FOX_HD, FOX_HD))), BF16)
    tp = max(tc, LANES)
    tril = jnp.asarray(np.tril(np.ones((tp, tp))), BF16)
    row_map = lambda b, c: (b * nc + c, 0)
    const = lambda b, c: (0, 0)
    out_shape = [jax.ShapeDtypeStruct((m, FOX_HEADS * LANES), BF16 if augment else F32),
                 jax.ShapeDtypeStruct((m, FOX_WIDTH), F32),
                 jax.ShapeDtypeStruct((m, LANES), F32),
                 jax.ShapeDtypeStruct((m, LANES), F32),
                 jax.ShapeDtypeStruct((nb * LANES, nc * tp), F32),
                 jax.ShapeDtypeStruct((nb * LANES, nc * tp), F32)]
    out_specs = [pl.BlockSpec((tc, FOX_HEADS * LANES), row_map),
                 pl.BlockSpec((tc, FOX_WIDTH), row_map),
                 pl.BlockSpec((tc, LANES), row_map),
                 pl.BlockSpec((tc, LANES), row_map),
                 pl.BlockSpec((LANES, tp), lambda b, c: (b, c)),
                 pl.BlockSpec((LANES, tp), lambda b, c: (b, c))]
    if augment:
        out_shape += [jax.ShapeDtypeStruct((m, FOX_HEADS * LANES), BF16), jax.ShapeDtypeStruct((m, FOX_WIDTH), BF16)]
        out_specs += [pl.BlockSpec((tc, FOX_HEADS * LANES), row_map), pl.BlockSpec((tc, FOX_WIDTH), row_map)]
    return pl.pallas_call(
        functools.partial(_prep_kernel, t_valid=t_valid, augment=augment),
        out_shape=tuple(out_shape),
        grid=(nb, nc),
        in_specs=[pl.BlockSpec((tc, 512), lambda b, c: (b * nc + c, COL_FQ)),
                  pl.BlockSpec((tc, 512), lambda b, c: (b * nc + c, COL_FK)),
                  pl.BlockSpec((tc, 512), lambda b, c: (b * nc + c, COL_FV)),
                  pl.BlockSpec((tc, LANES), lambda b, c: (b * nc + c, COL_SMALL * 4)),
                  pl.BlockSpec((1, 512), const), pl.BlockSpec((1, 512), const),
                  pl.BlockSpec((1, LANES), const),
                  pl.BlockSpec((512, 512), const), pl.BlockSpec((tp, tp), const)],
        out_specs=tuple(out_specs),
        scratch_shapes=[pltpu.VMEM((1, LANES), F32)],
        compiler_params=_cparams("parallel", "arbitrary"),
        name="gate_prep",
    )(z, z, z, z, gq, gk, bias_row, gmat, tril)


S5_SCAN_LANES = 512


def _cmul(ar, ai, br, bi):
    return ar * br - ai * bi, ar * bi + ai * br


def _s5_kernel(u_ref, h0r_ref, h0i_ref, perm_ref, permt_ref, bre_ref, bim_ref, lam_ref, pseg_ref, pk_ref,
               cre_ref, cim_ref, d_ref, wglu_ref, bglu_ref, y_ref, hlr_ref, hli_ref, hr_ref, hi_ref, cr_ref, ci_ref,
               *, t_last):
    c = pl.program_id(1)
    nc = pl.num_programs(1)
    tc = u_ref.shape[0]
    R = tc // SUBLANES
    half = S5_WIDTH // 2
    hl = S5_LANES // 2

    @pl.when(c == 0)
    def _():
        cr_ref[...] = h0r_ref[0]
        ci_ref[...] = h0i_ref[0]

    u = u_ref[...]
    ub = _pad_rows(u, MIN_BF16_ROWS).astype(BF16)
    if R > 1:
        ub = _dot(perm_ref[...], ub).astype(BF16)
    for j in range(2):
        uj = ub[:, j * half:(j + 1) * half]
        hr_ref[:, j * hl:(j + 1) * hl] = _dot(uj, bre_ref[j])[0:tc]
        hi_ref[:, j * hl:(j + 1) * hl] = _dot(uj, bim_ref[j])[0:tc]

    sub = lax.broadcasted_iota(jnp.int32, (SUBLANES, S5_SCAN_LANES), 0)
    for lc in range(S5_LANES // S5_SCAN_LANES):
        ls = slice(lc * S5_SCAN_LANES, (lc + 1) * S5_SCAN_LANES)
        lam_r = jnp.broadcast_to(lam_ref[0:1, ls], sub.shape)
        lam_i = jnp.broadcast_to(lam_ref[1:2, ls], sub.shape)
        init_r = jnp.where(sub == 0, jnp.broadcast_to(cr_ref[:, ls], sub.shape), 0.0)
        init_i = jnp.where(sub == 0, jnp.broadcast_to(ci_ref[:, ls], sub.shape), 0.0)

        def local_step(k, carry):
            h_r, h_i = carry
            rows = pl.ds(pl.multiple_of(k * SUBLANES, SUBLANES), SUBLANES)
            m_r, m_i = _cmul(lam_r, lam_i, h_r, h_i)
            h_r = m_r + hr_ref[rows, ls]
            h_i = m_i + hi_ref[rows, ls]
            hr_ref[rows, ls] = h_r
            hi_ref[rows, ls] = h_i
            return h_r, h_i

        e_r, e_i = lax.fori_loop(0, R, local_step, (init_r, init_i), unroll=min(R, 4))
        for j, lag in enumerate((1, 2, 4)):
            m_r, m_i = _cmul(pseg_ref[2 * j, :, ls], pseg_ref[2 * j + 1, :, ls],
                             pltpu.roll(e_r, lag, 0), pltpu.roll(e_i, lag, 0))
            e_r, e_i = e_r + m_r, e_i + m_i
        cr_ref[:, ls] = e_r[SUBLANES - 1:SUBLANES, :]
        ci_ref[:, ls] = e_i[SUBLANES - 1:SUBLANES, :]
        in_r = jnp.where(sub == 0, 0.0, pltpu.roll(e_r, 1, 0))
        in_i = jnp.where(sub == 0, 0.0, pltpu.roll(e_i, 1, 0))

        def fix_step(k, _):
            rows = pl.ds(pl.multiple_of(k * SUBLANES, SUBLANES), SUBLANES)
            p_r = jnp.broadcast_to(pk_ref[0, pl.ds(k, 1), ls], sub.shape)
            p_i = jnp.broadcast_to(pk_ref[1, pl.ds(k, 1), ls], sub.shape)
            m_r, m_i = _cmul(p_r, p_i, in_r, in_i)
            hr_ref[rows, ls] += m_r
            hi_ref[rows, ls] += m_i
            return 0

        lax.fori_loop(0, R, fix_step, 0, unroll=min(R, 4))

    @pl.when(c == nc - 1)
    def _():
        tl = t_last % tc
        pos = (tl % R) * SUBLANES + tl // R
        hlr_ref[0] = hr_ref[pos:pos + 1, :]
        hli_ref[0] = hi_ref[pos:pos + 1, :]

    hrb = _pad_rows(hr_ref[...], MIN_BF16_ROWS).astype(BF16)
    hib = _pad_rows(hi_ref[...], MIN_BF16_ROWS).astype(BF16)
    ys = [(_dot(hrb[:, j * hl:(j + 1) * hl], cre_ref[j]) + _dot(hib[:, j * hl:(j + 1) * hl], cim_ref[j]))
          for j in range(2)]
    y = jnp.concatenate(ys, axis=1)
    if R > 1:
        y = _dot_exact_lhs(permt_ref[...], y)
    y = _gelu_tanh(y[0:tc] + d_ref[...] * u)
    gate = _dot(_pad_rows(y, MIN_BF16_ROWS).astype(BF16), wglu_ref[...])[0:tc]
    y_ref[...] = y * _sigmoid(gate + bglu_ref[...])


def s5_mixer(z, h0r, h0i, sp, nb, t, tc, t_last):
    nc = t // tc
    R = tc // SUBLANES
    tperm = max(tc, MIN_BF16_ROWS)
    perm = np.zeros((tperm, tperm), np.float32)
    for s in range(SUBLANES):
        for k in range(R):
            perm[k * SUBLANES + s, s * R + k] = 1.0
    permt = jnp.asarray(perm.T, BF16)
    perm = jnp.asarray(perm, BF16)
    const = lambda b, c: (0, 0)
    const3 = lambda b, c: (0, 0, 0)
    state_spec = pl.BlockSpec((1, 1, S5_LANES), lambda b, c: (b, 0, 0))
    tabs = sp["tabs"][R]
    return pl.pallas_call(
        functools.partial(_s5_kernel, t_last=t_last),
        out_shape=(jax.ShapeDtypeStruct((nb * t, S5_WIDTH), F32),
                   jax.ShapeDtypeStruct((nb, 1, S5_LANES), F32),
                   jax.ShapeDtypeStruct((nb, 1, S5_LANES), F32)),
        grid=(nb, nc),
        in_specs=[pl.BlockSpec((tc, 512), lambda b, c: (b * nc + c, COL_S5)),
                  state_spec, state_spec,
                  pl.BlockSpec((tperm, tperm), const), pl.BlockSpec((tperm, tperm), const),
                  pl.BlockSpec((2, S5_WIDTH // 2, S5_LANES // 2), const3),
                  pl.BlockSpec((2, S5_WIDTH // 2, S5_LANES // 2), const3),
                  pl.BlockSpec((2, S5_LANES), const),
                  pl.BlockSpec((6, SUBLANES, S5_LANES), const3),
                  pl.BlockSpec((2, R, S5_LANES), const3),
                  pl.BlockSpec((2, S5_LANES // 2, S5_WIDTH // 2), const3),
                  pl.BlockSpec((2, S5_LANES // 2, S5_WIDTH // 2), const3),
                  pl.BlockSpec((1, S5_WIDTH), const),
                  pl.BlockSpec((S5_WIDTH, S5_WIDTH), const), pl.BlockSpec((1, S5_WIDTH), const)],
        out_specs=(pl.BlockSpec((tc, S5_WIDTH), lambda b, c: (b * nc + c, 0)), state_spec, state_spec),
        scratch_shapes=[pltpu.VMEM((tc, S5_LANES), F32), pltpu.VMEM((tc, S5_LANES), F32),
                        pltpu.VMEM((1, S5_LANES), F32), pltpu.VMEM((1, S5_LANES), F32)],
        compiler_params=_cparams("parallel", "arbitrary"),
        name="s5_mixer",
    )(z, h0r.reshape(nb, 1, S5_LANES), h0i.reshape(nb, 1, S5_LANES), perm, permt, sp["bre"], sp["bim"], tabs["lam"],
      tabs["pseg"], tabs["pk"], sp["cre"], sp["cim"], sp["d"], sp["wglu"], sp["bglu"])


def s5_params(a_re, a_im, log_step, b_re, b_im, c_re, c_im, d_skip, w_glu, b_glu, seg_lens):
    dt = jnp.exp(log_step)[:, None]
    mag = jnp.exp(a_re * dt)
    lr = (mag * jnp.cos(a_im * dt)).reshape(1, S5_LANES)
    li = (mag * jnp.sin(a_im * dt)).reshape(1, S5_LANES)
    den = a_re * a_re + a_im * a_im
    xr, xi = lr.reshape(a_re.shape) - 1.0, li.reshape(a_re.shape)
    fr = (xr * a_re + xi * a_im) / den
    fi = (xi * a_re - xr * a_im) / den
    bbr = fr[..., None] * b_re - fi[..., None] * b_im
    bbi = fr[..., None] * b_im + fi[..., None] * b_re
    gh = S5_GROUPS // 2
    eye = jnp.eye(gh, dtype=F32)

    def in_mat(b):
        return jnp.einsum('jgpc,gh->jgchp', b.reshape(2, gh, S5_STATE, S5_GROUP), eye).reshape(
            2, S5_WIDTH // 2, S5_LANES // 2).astype(BF16)

    def out_mat(cm):
        return jnp.einsum('jgcp,gh->jgphc', cm.reshape(2, gh, S5_GROUP, S5_STATE), eye).reshape(
            2, S5_LANES // 2, S5_WIDTH // 2).astype(BF16)

    def powers(pr, pi, n):
        tr, ti, cnt = pr, pi, 1
        while cnt < n:
            lr_, li_ = tr[cnt - 1:cnt], ti[cnt - 1:cnt]
            nr, ni = _cmul(tr, ti, lr_, li_)
            tr, ti, cnt = jnp.concatenate([tr, nr], axis=0), jnp.concatenate([ti, ni], axis=0), 2 * cnt
        return tr, ti

    sub = jnp.arange(SUBLANES)[:, None]
    tabs = {}
    for R in seg_lens:
        kr, ki = powers(lr, li, R)
        sr, si = powers(kr[R - 1:R], ki[R - 1:R], 4)
        pseg = []
        for lag in (1, 2, 4):
            msk = (sub >= lag).astype(F32)
            pseg += [msk * sr[lag - 1:lag], msk * si[lag - 1:lag]]
        tabs[R] = dict(lam=jnp.concatenate([lr, li], axis=0), pseg=jnp.stack(pseg), pk=jnp.stack([kr, ki]))
    return dict(bre=in_mat(bbr), bim=in_mat(bbi), cre=out_mat(c_re), cim=out_mat(-c_im), tabs=tabs,
                d=d_skip.reshape(1, S5_WIDTH), wglu=w_glu.astype(BF16), bglu=b_glu.reshape(1, S5_WIDTH))


def _fox_flash_kernel(qx_ref, kx_ref, v_ref, o_ref, m_ref, l_ref, acc_ref):
    i = pl.program_id(1)
    j = pl.program_id(2)
    tq = qx_ref.shape[0]
    tk = kx_ref.shape[0]

    @pl.when(j == 0)
    def _():
        m_ref[...] = jnp.full_like(m_ref, -jnp.inf)
        l_ref[...] = jnp.zeros_like(l_ref)
        acc_ref[...] = jnp.zeros_like(acc_ref)

    def step(masked):
        if masked:
            keep = (lax.broadcasted_iota(jnp.int32, (tq, tk), 0) + i * tq >=
                    lax.broadcasted_iota(jnp.int32, (tq, tk), 1) + j * tk)
        for h in range(FOX_HEADS):
            hs = slice(h * LANES, (h + 1) * LANES)
            ps = slice((h // 2) * LANES, (h // 2 + 1) * LANES)
            s = _dot_nt(qx_ref[:, hs], kx_ref[:, hs])
            if masked:
                s = jnp.where(keep, s, -jnp.inf)
            m_prev = m_ref[h]
            m_new = jnp.maximum(m_prev, jnp.max(s, axis=-1, keepdims=True))
            alpha = jnp.exp(m_prev - m_new)
            p = jnp.exp(s - _lane_tile(m_new, tk // LANES))
            l_ref[h] = alpha * l_ref[h] + jnp.sum(p, axis=-1, keepdims=True)
            acc_ref[h] = alpha * acc_ref[h] + _dot(p.astype(BF16), v_ref[:, ps])
            m_ref[h] = m_new

    q_lo = i * tq
    q_hi = i * tq + tq - 1
    k_lo = j * tk
    k_hi = j * tk + tk - 1

    @pl.when(k_hi <= q_lo)
    def _():
        step(False)

    @pl.when((k_lo <= q_hi) & (k_hi > q_lo))
    def _():
        step(True)

    @pl.when(j == pl.num_programs(2) - 1)
    def _():
        lane = lax.broadcasted_iota(jnp.int32, (1, LANES), 1)
        for p in range(FOX_HEADS // 2):
            lo = acc_ref[2 * p] / l_ref[2 * p]
            hi = acc_ref[2 * p + 1] / l_ref[2 * p + 1]
            o_ref[:, p * LANES:(p + 1) * LANES] = jnp.where(lane < FOX_HD, lo, hi)


def fox_flash(qx, kx, vb, nb, t, tq, tk):
    nq, nk = t // tq, t // tk

    def kv_idx(i, j):
        return jnp.minimum(j, (i * tq + tq - 1) // tk)

    return pl.pallas_call(
        _fox_flash_kernel,
        out_shape=jax.ShapeDtypeStruct((nb * t, FOX_WIDTH), F32),
        grid=(nb, nq, nk),
        in_specs=[pl.BlockSpec((tq, FOX_HEADS * LANES), lambda b, i, j: (b * nq + i, 0)),
                  pl.BlockSpec((tk, FOX_HEADS * LANES), lambda b, i, j: (b * nk + kv_idx(i, j), 0)),
                  pl.BlockSpec((tk, FOX_WIDTH), lambda b, i, j: (b * nk + kv_idx(i, j), 0))],
        out_specs=pl.BlockSpec((tq, FOX_WIDTH), lambda b, i, j: (b * nq + i, 0)),
        scratch_shapes=[pltpu.VMEM((FOX_HEADS, tq, LANES), F32), pltpu.VMEM((FOX_HEADS, tq, LANES), F32),
                        pltpu.VMEM((FOX_HEADS, tq, LANES), F32)],
        compiler_params=_cparams("parallel", "parallel", "arbitrary"),
        name="fox_flash",
    )(qx, kx, vb)


FOX_PAGES_PER_STEP = 8


def _fox_decode_kernel(pt_ref, qx_ref, kn_ref, vn_ref, csrow_ref, *rest, n_pages):
    npg = FOX_PAGES_PER_STEP
    k_refs = rest[0:npg]
    v_refs = rest[npg:2 * npg]
    f_refs = rest[2 * npg:3 * npg]
    triu_ref = rest[3 * npg]
    o_ref = rest[3 * npg + 1]
    qb_ref, m_ref, l_ref, acc_ref, fc_ref, kpad_ref, vpad_ref = rest[3 * npg + 2:]
    j = pl.program_id(1)
    page = kpad_ref.shape[0]
    nrow = FOX_HEADS * SAMPLE_T

    @pl.when(j == 0)
    def _():
        qb_ref[...] = jnp.zeros_like(qb_ref)
        for h in range(FOX_HEADS):
            ps = slice((h // 2) * LANES, (h // 2 + 1) * LANES)
            qb_ref[h * SAMPLE_T:(h + 1) * SAMPLE_T, ps] = qx_ref[:, h * LANES:(h + 1) * LANES]
        m_ref[...] = jnp.full_like(m_ref, -jnp.inf)
        l_ref[...] = jnp.zeros_like(l_ref)
        acc_ref[...] = jnp.zeros_like(acc_ref)
        fc_ref[...] = jnp.zeros_like(fc_ref)

    qb = qb_ref[...].astype(BF16)

    def update(s, pv_fn):
        m_prev = m_ref[...]
        m_new = jnp.maximum(m_prev, jnp.max(s, axis=-1, keepdims=True))
        alpha = jnp.exp(m_prev - m_new)
        p = jnp.exp(s - _lane_tile(m_new, s.shape[1] // LANES))
        l_ref[...] = alpha * l_ref[...] + jnp.sum(p, axis=-1, keepdims=True)
        acc_ref[...] = _lane_tile(alpha, FOX_WIDTH // LANES) * acc_ref[...] + pv_fn(p.astype(BF16))
        m_ref[...] = m_new

    def head_rows(x):
        return jnp.concatenate([jnp.broadcast_to(x[h:h + 1, :], (SAMPLE_T, x.shape[1])) for h in range(FOX_HEADS)],
                               axis=0)

    fcat = jnp.concatenate([f_refs[i][0, 0] for i in range(npg)], axis=0)
    cum_in = _dot_exact_rhs(fcat, triu_ref[...])
    off = fc_ref[...]
    ss, vts = [], []
    for i in range(npg):
        cum = cum_in[i * FOX_HEADS:(i + 1) * FOX_HEADS] + off
        off = jnp.broadcast_to(cum[:, page - 1:page], off.shape)
        kt = k_refs[i][0, 0].reshape(FOX_WIDTH, page).astype(BF16)
        ss.append(_dot(qb, kt) - head_rows(cum))
        vts.append(v_refs[i][0, 0].reshape(FOX_WIDTH, page).astype(BF16))
    fc_ref[...] = off

    def pv_past(p):
        pv = _dot_nt(p[:, 0:page], vts[0])
        for i in range(1, npg):
            pv = pv + _dot_nt(p[:, i * page:(i + 1) * page], vts[i])
        return pv

    update(jnp.concatenate(ss, axis=1), pv_past)

    @pl.when(j == n_pages // npg - 1)
    def _():
        kpad_ref[...] = jnp.zeros_like(kpad_ref)
        vpad_ref[...] = jnp.zeros_like(vpad_ref)
        kpad_ref[0:SAMPLE_T, :] = kn_ref[...]
        vpad_ref[0:SAMPLE_T, :] = vn_ref[...]
        bias = head_rows(csrow_ref[...] + fc_ref[...])
        s = _dot_nt(qb, kpad_ref[...].astype(BF16)) - bias
        tq_idx = lax.broadcasted_iota(jnp.int32, (nrow, page), 0) % SAMPLE_T
        tk_idx = lax.broadcasted_iota(jnp.int32, (nrow, page), 1)
        s = jnp.where(tk_idx <= tq_idx, s, -jnp.inf)
        update(s, lambda p: _dot(p, vpad_ref[...].astype(BF16)))
        lane = lax.broadcasted_iota(jnp.int32, (1, FOX_WIDTH), 1)
        out = jnp.zeros((SAMPLE_T, FOX_WIDTH), F32)
        for h in range(FOX_HEADS):
            rows = slice(h * SAMPLE_T, (h + 1) * SAMPLE_T)
            oh = acc_ref[rows, :] / _lane_tile(l_ref[rows, :], FOX_WIDTH // LANES)
            out = out + jnp.where((lane >= h * FOX_HD) & (lane < (h + 1) * FOX_HD), oh, 0.0)
        o_ref[...] = out


def fox_decode(page_table, qx, kn, z, csrow, pool_kt, pool_vt, pool_ft, layer):
    nb, n_pages = page_table.shape
    page = pool_kt.shape[-1]
    assert page == LANES and n_pages % FOX_PAGES_PER_STEP == 0
    npg = FOX_PAGES_PER_STEP
    nrow = FOX_HEADS * SAMPLE_T
    pt = page_table.reshape(-1)
    triu = jnp.asarray(np.triu(np.ones((page, page))), BF16)

    def pg5(i):
        return lambda b, j, pt: (layer, pt[b * n_pages + j * npg + i], 0, 0, 0)

    def pg4(i):
        return lambda b, j, pt: (layer, pt[b * n_pages + j * npg + i], 0, 0)

    row = lambda b, j, pt: (b, 0)
    in_specs = [pl.BlockSpec((SAMPLE_T, FOX_HEADS * LANES), row),
                pl.BlockSpec((SAMPLE_T, FOX_WIDTH), row),
                pl.BlockSpec((SAMPLE_T, FOX_WIDTH), lambda b, j, pt: (b, COL_FV)),
                pl.BlockSpec((FOX_HEADS, page), lambda b, j, pt: (b * (LANES // FOX_HEADS), 0))]
    in_specs += [pl.BlockSpec((1, 1, FOX_HEADS, FOX_HD, page), pg5(i)) for i in range(npg)]
    in_specs += [pl.BlockSpec((1, 1, FOX_HEADS, FOX_HD, page), pg5(i)) for i in range(npg)]
    in_specs += [pl.BlockSpec((1, 1, FOX_HEADS, page), pg4(i)) for i in range(npg)]
    in_specs += [pl.BlockSpec((page, page), lambda b, j, pt: (0, 0))]
    return pl.pallas_call(
        functools.partial(_fox_decode_kernel, n_pages=n_pages),
        out_shape=jax.ShapeDtypeStruct((nb * SAMPLE_T, FOX_WIDTH), F32),
        grid_spec=pltpu.PrefetchScalarGridSpec(
            num_scalar_prefetch=1,
            grid=(nb, n_pages // npg),
            in_specs=in_specs,
            out_specs=pl.BlockSpec((SAMPLE_T, FOX_WIDTH), row),
            scratch_shapes=[pltpu.VMEM((nrow, FOX_WIDTH), F32), pltpu.VMEM((nrow, LANES), F32),
                            pltpu.VMEM((nrow, LANES), F32), pltpu.VMEM((nrow, FOX_WIDTH), F32),
                            pltpu.VMEM((FOX_HEADS, page), F32),
                            pltpu.VMEM((page, FOX_WIDTH), F32), pltpu.VMEM((page, FOX_WIDTH), F32)]),
        compiler_params=_cparams("parallel", "arbitrary"),
        name="fox_decode",
    )(pt, qx, kn, z, csrow, *([pool_kt] * npg), *([pool_vt] * npg), *([pool_ft] * npg), triu)


def _mlstm_kernel(q_ref, k_ref, v_ref, o_ref, g_ref, cs_ref, grow_ref, csrow_ref, c0_ref, n0_ref, m0_ref, gn_ref,
                  y_ref, cout_ref, nout_ref, mout_ref, c_ref, n_ref, m_ref):
    c = pl.program_id(1)
    nc = pl.num_programs(1)
    L = q_ref.shape[0]

    @pl.when(c == 0)
    def _():
        c_ref[...] = c0_ref[0]
        n_ref[...] = n0_ref[0]
        m_ref[...] = m0_ref[0]

    Lp = grow_ref.shape[1]
    g = g_ref[...]
    cs = cs_ref[...]
    grow = grow_ref[...]
    csrow = csrow_ref[...]
    causal = (lax.broadcasted_iota(jnp.int32, (Lp, Lp), 0) >= lax.broadcasted_iota(jnp.int32, (Lp, Lp), 1))
    for h in range(ML_HEADS):
        hs = slice(h * ML_HD, (h + 1) * ML_HD)
        qh = _pad_rows(q_ref[:, hs], Lp)
        kh = _pad_rows(k_ref[:, hs], Lp) * (ML_HD ** -0.5)
        vh = _pad_rows(v_ref[:, hs], Lp)
        qb, kb, vb = qh.astype(BF16), kh.astype(BF16), vh.astype(BF16)
        bcol = _pad_rows(cs[:, LANE_MLF + h:LANE_MLF + h + 1], Lp)
        icol = _pad_rows(g[:, LANE_MLI + h:LANE_MLI + h + 1], Lp, -jnp.inf)
        brow = csrow[LANE_MLF + h:LANE_MLF + h + 1, :]
        irow = grow[LANE_MLI + h:LANE_MLI + h + 1, :]
        m_prev = m_ref[0:1, h:h + 1]
        log_d = jnp.where(causal, bcol - brow + irow, -jnp.inf)
        log_inter = bcol + m_prev
        m_t = jnp.maximum(log_inter, jnp.max(log_d, axis=-1, keepdims=True))
        d_w = jnp.exp(log_d - m_t)
        inter_w = jnp.exp(log_inter - m_t)
        s = _dot_nt(qb, kb) * d_w
        ch = c_ref[h]
        n_row = n_ref[h:h + 1, :]
        num = _dot(s.astype(BF16), vb) + inter_w * _dot_nt(qb, ch.astype(BF16))
        den = jnp.sum(s, axis=-1, keepdims=True) + inter_w * jnp.sum(qh * n_row, axis=-1, keepdims=True)
        hh = num / jnp.maximum(jnp.abs(den), jnp.exp(-m_t))
        y_ref[:, hs] = _rms(hh[0:L], gn_ref[...]) * _sigmoid(o_ref[:, hs])
        m_end = m_t[L - 1:L, :]
        a_end = inter_w[L - 1:L, :]
        w_col = jnp.exp(bcol[L - 1:L, :] - bcol + icol - m_end)
        c_ref[h] = a_end * ch + _dot_tn((vh * w_col).astype(BF16), kb)
        n_ref[h:h + 1, :] = a_end * n_row + jnp.sum(kh * w_col, axis=0, keepdims=True)
        m_ref[0:1, h:h + 1] = m_end

    @pl.when(c == nc - 1)
    def _():
        cout_ref[0] = c_ref[...]
        nout_ref[0] = n_ref[...]
        mout_ref[0] = m_ref[...]


def mlstm(z, g, cs, grow, csrow, c0, n0, m0, gn, nb, t, L):
    nc = t // L
    rows = lambda b, c: (b * nc + c, 0)
    rr = 2 * SUBLANES
    m0p = jnp.zeros((nb, 1, LANES), F32).at[:, 0, :ML_HEADS].set(m0)
    outs = pl.pallas_call(
        _mlstm_kernel,
        out_shape=(jax.ShapeDtypeStruct((nb * t, ML_WIDTH), F32),
                   jax.ShapeDtypeStruct((nb, ML_HEADS, ML_HD, ML_HD), F32),
                   jax.ShapeDtypeStruct((nb, ML_HEADS, ML_HD), F32),
                   jax.ShapeDtypeStruct((nb, 1, LANES), F32)),
        grid=(nb, nc),
        in_specs=[pl.BlockSpec((L, 512), lambda b, c: (b * nc + c, COL_MQ)),
                  pl.BlockSpec((L, 512), lambda b, c: (b * nc + c, COL_MK)),
                  pl.BlockSpec((L, 512), lambda b, c: (b * nc + c, COL_MV)),
                  pl.BlockSpec((L, 512), lambda b, c: (b * nc + c, COL_MO)),
                  pl.BlockSpec((L, LANES), rows), pl.BlockSpec((L, LANES), rows),
                  pl.BlockSpec((rr, max(L, LANES)), lambda b, c: (b * (LANES // rr), c)),
                  pl.BlockSpec((rr, max(L, LANES)), lambda b, c: (b * (LANES // rr), c)),
                  pl.BlockSpec((1, ML_HEADS, ML_HD, ML_HD), lambda b, c: (b, 0, 0, 0)),
                  pl.BlockSpec((1, ML_HEADS, ML_HD), lambda b, c: (b, 0, 0)),
                  pl.BlockSpec((1, 1, LANES), lambda b, c: (b, 0, 0)),
                  pl.BlockSpec((1, ML_HD), lambda b, c: (0, 0))],
        out_specs=(pl.BlockSpec((L, ML_WIDTH), rows),
                   pl.BlockSpec((1, ML_HEADS, ML_HD, ML_HD), lambda b, c: (b, 0, 0, 0)),
                   pl.BlockSpec((1, ML_HEADS, ML_HD), lambda b, c: (b, 0, 0)),
                   pl.BlockSpec((1, 1, LANES), lambda b, c: (b, 0, 0))),
        scratch_shapes=[pltpu.VMEM((ML_HEADS, ML_HD, ML_HD), F32), pltpu.VMEM((ML_HEADS, ML_HD), F32),
                        pltpu.VMEM((1, LANES), F32)],
        compiler_params=_cparams("parallel", "arbitrary"),
        name="mlstm",
    )(z, z, z, z, g, cs, grow, csrow, c0, n0, m0p, gn.reshape(1, ML_HD))
    y, c_new, n_new, m_new = outs
    return y, c_new, n_new, m_new[:, 0, :ML_HEADS]


def _merge_kernel(x_ref, ys_ref, yf_ref, ym_ref, g0_ref, g1_ref, g2_ref, ws_ref, wf_ref, wm_ref, wo_ref, o_ref):
    merged = (_sigmoid(g0_ref[...]) * _dot(ys_ref[...].astype(BF16), ws_ref[...])
              + _sigmoid(g1_ref[...]) * _dot(yf_ref[...].astype(BF16), wf_ref[...])
              + _sigmoid(g2_ref[...]) * _dot(ym_ref[...].astype(BF16), wm_ref[...]))
    o_ref[...] = x_ref[...] + _dot(merged.astype(BF16), wo_ref[...])


def merge_out(x, ys, yf, ym, z, ws, wf, wm, wo, tm=256):
    m = x.shape[0]
    tm = min(tm, m)
    row = lambda i: (i, 0)
    const = lambda i: (0, 0)
    gate0 = COL_GATES // 2
    return pl.pallas_call(
        _merge_kernel,
        out_shape=jax.ShapeDtypeStruct((m, D_MODEL), F32),
        grid=(m // tm,),
        in_specs=[pl.BlockSpec((tm, D_MODEL), row),
                  pl.BlockSpec((tm, 512), row), pl.BlockSpec((tm, 512), row), pl.BlockSpec((tm, 512), row),
                  pl.BlockSpec((tm, D_MODEL), lambda i: (i, gate0)),
                  pl.BlockSpec((tm, D_MODEL), lambda i: (i, gate0 + 1)),
                  pl.BlockSpec((tm, D_MODEL), lambda i: (i, gate0 + 2)),
                  pl.BlockSpec((512, D_MODEL), const), pl.BlockSpec((512, D_MODEL), const),
                  pl.BlockSpec((512, D_MODEL), const), pl.BlockSpec((D_MODEL, D_MODEL), const)],
        out_specs=pl.BlockSpec((tm, D_MODEL), row),
        compiler_params=_cparams("parallel"),
        name="merge_out",
    )(x, ys, yf, ym, z, z, z, ws, wf, wm, wo)


def _cross_kernel(q_ref, k_ref, v_ref, o_ref):
    tq = q_ref.shape[0]
    for h in range(MEM_HEADS):
        hs = slice(h * MEM_HD, (h + 1) * MEM_HD)
        qh = _pad_rows(q_ref[:, hs], MIN_BF16_ROWS).astype(BF16)
        s = _dot_nt(qh, k_ref[:, hs].astype(BF16)) * (MEM_HD ** -0.5)
        m = jnp.max(s, axis=-1, keepdims=True)
        p = jnp.exp(s - m)
        l = jnp.sum(p, axis=-1, keepdims=True)
        o_ref[:, hs] = (_dot(p.astype(BF16), v_ref[:, hs].astype(BF16)) / l)[0:tq]


def cross_attend(q, mem_k, mem_v, nb, t, tq):
    nq = t // tq
    n_mem = mem_k.shape[0] // nb
    return pl.pallas_call(
        _cross_kernel,
        out_shape=jax.ShapeDtypeStruct((nb * t, MEM_WIDTH), F32),
        grid=(nb, nq),
        in_specs=[pl.BlockSpec((tq, MEM_WIDTH), lambda b, i: (b * nq + i, 0)),
                  pl.BlockSpec((n_mem, MEM_WIDTH), lambda b, i: (b, 0)),
                  pl.BlockSpec((n_mem, MEM_WIDTH), lambda b, i: (b, 0))],
        out_specs=pl.BlockSpec((tq, MEM_WIDTH), lambda b, i: (b * nq + i, 0)),
        compiler_params=_cparams("parallel", "parallel"),
        name="cross_attend",
    )(q, mem_k, mem_v)


def _proj_residual_kernel(x_ref, a_ref, w_ref, o_ref):
    o_ref[...] = x_ref[...] + _dot(a_ref[...].astype(BF16), w_ref[...])


def proj_residual(x, a, w, tm=512):
    m, n = x.shape
    k = a.shape[1]
    tm = min(tm, m)
    return pl.pallas_call(
        _proj_residual_kernel,
        out_shape=jax.ShapeDtypeStruct((m, n), F32),
        grid=(m // tm,),
        in_specs=[pl.BlockSpec((tm, n), lambda i: (i, 0)), pl.BlockSpec((tm, k), lambda i: (i, 0)),
                  pl.BlockSpec((k, n), lambda i: (0, 0))],
        out_specs=pl.BlockSpec((tm, n), lambda i: (i, 0)),
        compiler_params=_cparams("parallel"),
        name="proj_residual",
    )(x, a, w)


def _mlp_kernel(x_ref, g_ref, wu_ref, wd_ref, o_ref, hn_ref, acc_ref):
    f = pl.program_id(1)

    @pl.when(f == 0)
    def _():
        hn_ref[...] = _rms(x_ref[...], g_ref[...]).astype(BF16)
        acc_ref[...] = jnp.zeros_like(acc_ref)

    a = jnp.maximum(_dot(hn_ref[...], wu_ref[...]), 0.0)
    acc_ref[...] += _dot((a * a).astype(BF16), wd_ref[...])

    @pl.when(f == pl.num_programs(1) - 1)
    def _():
        o_ref[...] = x_ref[...] + acc_ref[...]


def mlp(x, g, wu, wd, tm=512, tf=512):
    m, d = x.shape
    dff = wu.shape[1]
    tm = min(tm, m)
    return pl.pallas_call(
        _mlp_kernel,
        out_shape=jax.ShapeDtypeStruct((m, d), F32),
        grid=(m // tm, dff // tf),
        in_specs=[pl.BlockSpec((tm, d), lambda i, f: (i, 0)), pl.BlockSpec((1, d), lambda i, f: (0, 0)),
                  pl.BlockSpec((d, tf), lambda i, f: (0, f)), pl.BlockSpec((tf, d), lambda i, f: (f, 0))],
        out_specs=pl.BlockSpec((tm, d), lambda i, f: (i, 0)),
        scratch_shapes=[pltpu.VMEM((tm, d), BF16), pltpu.VMEM((tm, d), F32)],
        compiler_params=_cparams("parallel", "arbitrary"),
        name="mlp",
    )(x, g.reshape(1, d), wu, wd)


def _pack_w_in(w_in):
    offs = np.concatenate([[0], np.cumsum(SPLITS)])
    col = lambda i: w_in[:, int(offs[i]):int(offs[i + 1])]
    s5, fq, fk, fv, ff, mq, mk, mv, mi, mf, mo, gates = [col(i) for i in range(12)]
    pad = jnp.zeros((w_in.shape[0], 512 - FOX_HEADS - 2 * ML_HEADS), w_in.dtype)
    return jnp.concatenate([s5, fq, fk, fv, mq, mk, mv, mo, gates, ff, mi, mf, pad], axis=1).astype(BF16)


def _layer_weights(l, g_mix, w_in, s5_a_re, s5_a_im, s5_log_step, s5_b_re, s5_b_im, s5_c_re, s5_c_im, s5_d,
                   s5_w_glu, s5_b_glu, fox_gq, fox_gk, fox_bf, ml_bi, ml_bf, ml_gn, w_br_s5, w_br_fox, w_br_ml,
                   w_out, g_cross, w_cq, cross_gq, g_mem, w_mk, w_mv, cross_gk, w_co, g_mlp, w_up, w_down):
    bias_row = jnp.zeros((1, LANES), F32)
    bias_row = bias_row.at[0, LANE_FOXF:LANE_FOXF + FOX_HEADS].set(fox_bf[l])
    bias_row = bias_row.at[0, LANE_MLI:LANE_MLI + ML_HEADS].set(ml_bi[l])
    bias_row = bias_row.at[0, LANE_MLF:LANE_MLF + ML_HEADS].set(ml_bf[l])
    return dict(
        g_mix=g_mix[l], w_in=_pack_w_in(w_in[l]),
        s5=s5_params(s5_a_re[l], s5_a_im[l], s5_log_step[l], s5_b_re[l], s5_b_im[l], s5_c_re[l], s5_c_im[l],
                     s5_d[l], s5_w_glu[l], s5_b_glu[l], (SEQ_TILE // SUBLANES, SAMPLE_T // SUBLANES)),
        gq=jnp.tile(fox_gq[l], FOX_HEADS).reshape(1, FOX_WIDTH),
        gk=jnp.tile(fox_gk[l], FOX_HEADS).reshape(1, FOX_WIDTH),
        bias_row=bias_row, ml_gn=ml_gn[l],
        w_br_s5=w_br_s5[l].astype(BF16), w_br_fox=w_br_fox[l].astype(BF16), w_br_ml=w_br_ml[l].astype(BF16),
        w_out=w_out[l].astype(BF16), g_cross=g_cross[l], w_cq=w_cq[l].astype(BF16), cross_gq=cross_gq[l],
        g_mem=g_mem[l], w_mk=w_mk[l].astype(BF16), w_mv=w_mv[l].astype(BF16), cross_gk=cross_gk[l],
        w_co=w_co[l].astype(BF16), g_mlp=g_mlp[l], w_up=w_up[l].astype(BF16), w_down=w_down[l].astype(BF16))


def _hybrid_layer(x, W, nb, t, seq_tile, t_valid, s5_state, ml_state, mem_k, mem_v, fox_attend, augment):
    z = norm_matmul(x, W["g_mix"], W["w_in"], tm=min(1024, nb * t))
    prep = gate_prep(z, W["gq"], W["gk"], W["bias_row"], nb, t, seq_tile, t_valid, augment)
    kn, g, cs, grow, csrow = prep[1:6]
    y_s5, s5_re, s5_im = s5_mixer(z, s5_state[0], s5_state[1], W["s5"], nb, t, seq_tile, t_valid - 1)
    y_fox = fox_attend(prep, z)
    y_ml, c_new, n_new, m_new = mlstm(z, g, cs, grow, csrow, ml_state[0], ml_state[1], ml_state[2], W["ml_gn"],
                                      nb, t, seq_tile)
    x = merge_out(x, y_s5, y_fox, y_ml, z, W["w_br_s5"], W["w_br_fox"], W["w_br_ml"], W["w_out"])
    qc = norm_matmul(x, W["g_cross"], W["w_cq"], head_gain=W["cross_gq"])
    oc = cross_attend(qc, mem_k, mem_v, nb, t, seq_tile)
    x = proj_residual(x, oc, W["w_co"])
    x = mlp(x, W["g_mlp"], W["w_up"], W["w_down"])
    return x, z, kn, g, s5_re.reshape(nb, S5_GROUPS, S5_STATE), s5_im.reshape(nb, S5_GROUPS, S5_STATE), \
        c_new, n_new, m_new


def kernel(x_prompt, x_sample, mem_prompt, cache_fox_k, cache_fox_v, cache_fox_logf, page_table, state_s5_re, state_s5_im, state_mlstm_C, state_mlstm_n, state_mlstm_m, cache_mem_k, cache_mem_v, g_mix, w_in, s5_a_re, s5_a_im, s5_log_step, s5_b_re, s5_b_im, s5_c_re, s5_c_im, s5_d, s5_w_glu, s5_b_glu, fox_gq, fox_gk, fox_bf, ml_bi, ml_bf, ml_gn, w_br_s5, w_br_fox, w_br_ml, w_out, g_cross, w_cq, cross_gq, g_mem, w_mk, w_mv, cross_gk, w_co, g_mlp, w_up, w_down):
    depth = w_in.shape[0]
    bp, tp, _ = x_prompt.shape
    bs, ts, _ = x_sample.shape
    n_mem = mem_prompt.shape[1]

    xp = x_prompt.reshape(bp * tp, D_MODEL)
    xs = jnp.pad(x_sample, ((0, 0), (0, SAMPLE_T - ts), (0, 0))).reshape(bs * SAMPLE_T, D_MODEL)
    mem = mem_prompt.reshape(bp * n_mem, D_MODEL)
    zeros_p = (jnp.zeros((bp, S5_LANES), F32), jnp.zeros((bp, S5_LANES), F32))
    zeros_ml = (jnp.zeros((bp, ML_HEADS, ML_HD, ML_HD), F32), jnp.zeros((bp, ML_HEADS, ML_HD), F32),
                jnp.zeros((bp, ML_HEADS), F32))
    pool_kt = jnp.transpose(cache_fox_k, (0, 1, 3, 4, 2))
    pool_vt = jnp.transpose(cache_fox_v, (0, 1, 3, 4, 2))
    pool_ft = jnp.transpose(cache_fox_logf, (0, 1, 3, 2))
    st_p, st_s = [], []
    for l in range(depth):
        W = _layer_weights(l, g_mix, w_in, s5_a_re, s5_a_im, s5_log_step, s5_b_re, s5_b_im, s5_c_re, s5_c_im, s5_d,
                           s5_w_glu, s5_b_glu, fox_gq, fox_gk, fox_bf, ml_bi, ml_bf, ml_gn, w_br_s5, w_br_fox,
                           w_br_ml, w_out, g_cross, w_cq, cross_gq, g_mem, w_mk, w_mv, cross_gk, w_co, g_mlp, w_up,
                           w_down)
        mk_p = norm_matmul(mem, W["g_mem"], W["w_mk"], head_gain=W["cross_gk"])
        mv_p = norm_matmul(mem, W["g_mem"], W["w_mv"])

        def flash(prep, z):
            return fox_flash(prep[0], prep[6], prep[7], bp, tp, FLASH_TILE, FLASH_TILE)

        xp, z, kn, g, s5r, s5i, c_new, n_new, m_new = _hybrid_layer(
            xp, W, bp, tp, SEQ_TILE, tp, zeros_p, zeros_ml, mk_p, mv_p, flash, True)
        st_p.append((kn.reshape(bp, tp, FOX_HEADS, FOX_HD),
                     z[:, COL_FV * 512:(COL_FV + 1) * 512].reshape(bp, tp, FOX_HEADS, FOX_HD),
                     g[:, LANE_FOXF:LANE_FOXF + FOX_HEADS].reshape(bp, tp, FOX_HEADS),
                     s5r, s5i, c_new, n_new, m_new,
                     mk_p.reshape(bp, n_mem, MEM_HEADS, MEM_HD), mv_p.reshape(bp, n_mem, MEM_HEADS, MEM_HD)))

        def decode(prep, z, layer=l):
            return fox_decode(page_table, prep[0], prep[1], z, prep[5], pool_kt, pool_vt, pool_ft, layer)

        xs, z, kn, g, s5r, s5i, c_new, n_new, m_new = _hybrid_layer(
            xs, W, bs, SAMPLE_T, SAMPLE_T, ts,
            (state_s5_re[l].reshape(bs, S5_LANES), state_s5_im[l].reshape(bs, S5_LANES)),
            (state_mlstm_C[l], state_mlstm_n[l], state_mlstm_m[l]),
            cache_mem_k[l].reshape(bs * n_mem, MEM_WIDTH), cache_mem_v[l].reshape(bs * n_mem, MEM_WIDTH),
            decode, False)
        st_s.append((kn.reshape(bs, SAMPLE_T, FOX_HEADS, FOX_HD)[:, :ts],
                     z[:, COL_FV * 512:(COL_FV + 1) * 512].reshape(bs, SAMPLE_T, FOX_HEADS, FOX_HD)[:, :ts],
                     g[:, LANE_FOXF:LANE_FOXF + FOX_HEADS].reshape(bs, SAMPLE_T, FOX_HEADS)[:, :ts],
                     s5r, s5i, c_new, n_new, m_new))
    outs_p = [jnp.stack(a) for a in zip(*st_p)]
    outs_s = [jnp.stack(a) for a in zip(*st_s)]
    yp = xp.reshape(bp, tp, D_MODEL)
    ys = xs.reshape(bs, SAMPLE_T, D_MODEL)[:, :ts]
    return (yp, ys, *outs_p, *outs_s)
```

```python
import functools
import math

import jax
import jax.numpy as jnp
import numpy as np
from jax import lax
from jax.experimental import pallas as pl
from jax.experimental.pallas import tpu as pltpu

F32 = jnp.float32
BF16 = jnp.bfloat16

LANES = 128
SUBLANES = 8
MIN_BF16_ROWS = 16
MXU_DIM = 256
VMEM_LIMIT_BYTES = 48 * 1024 * 1024

D_MODEL = 1024
S5_WIDTH = 512
S5_GROUP = 16
S5_GROUPS = 32
S5_STATE = 64
S5_LANES = S5_GROUPS * S5_STATE
FOX_HEADS = 8
FOX_HD = 64
FOX_WIDTH = 512
ML_HEADS = 4
ML_HD = 128
ML_WIDTH = 512
MEM_HEADS = 4
MEM_HD = 128
MEM_WIDTH = 512
D_FF = 4096
EPS = 1e-6
SPLITS = (S5_WIDTH, FOX_WIDTH, FOX_WIDTH, FOX_WIDTH, FOX_HEADS, ML_WIDTH, ML_WIDTH, ML_WIDTH,
          ML_HEADS, ML_HEADS, ML_WIDTH, 3 * D_MODEL)

COL_S5, COL_FQ, COL_FK, COL_FV, COL_MO, COL_SMALL = range(6)
COL_MQ, COL_MK, COL_MV = range(3)
LANE_FOXF = 0
LANE_MLI = 8
LANE_MLF = 12
SEQ_TILE = 256
FLASH_TILE = 512
FLASH_ROWS = 128
SAMPLE_T = 8


def _cparams(*sem):
    return pltpu.CompilerParams(dimension_semantics=sem, vmem_limit_bytes=VMEM_LIMIT_BYTES)


def _dot(a, b):
    return jnp.dot(a, b, preferred_element_type=F32)


def _dot_nt(a, b):
    return lax.dot_general(a, b, (((1,), (1,)), ((), ())), preferred_element_type=F32)


def _dot_tn(a, b):
    return lax.dot_general(a, b, (((0,), (0,)), ((), ())), preferred_element_type=F32)


def _split3(x):
    hi = x.astype(BF16)
    r1 = x - hi.astype(F32)
    mid = r1.astype(BF16)
    lo = (r1 - mid.astype(F32)).astype(BF16)
    return hi, mid, lo


def _dot_exact_rhs(x, ones_rhs):
    hi, mid, lo = _split3(x)
    return _dot(hi, ones_rhs) + _dot(mid, ones_rhs) + _dot(lo, ones_rhs)


def _dot_exact_lhs(ones_lhs, x):
    hi, mid, lo = _split3(x)
    return _dot(ones_lhs, hi) + _dot(ones_lhs, mid) + _dot(ones_lhs, lo)


def _pad_rows(x, n, fill=0.0):
    if x.shape[0] >= n:
        return x
    return jnp.concatenate([x, jnp.full((n - x.shape[0], x.shape[1]), fill, x.dtype)], axis=0)


def _lane_tile(x, n):
    return x if n == 1 else jnp.concatenate([x] * n, axis=1)


def _log_sigmoid(a):
    return jnp.minimum(a, 0.0) - jnp.log1p(jnp.exp(-jnp.abs(a)))


def _sigmoid(a):
    return 1.0 / (1.0 + jnp.exp(-a))


def _gelu_tanh(x):
    c = math.sqrt(2.0 / math.pi)
    return 0.5 * x * (1.0 + jnp.tanh(c * (x + 0.044715 * (x * x * x))))


def _rms(x, g):
    ms = jnp.mean(x * x, axis=-1, keepdims=True)
    return x * lax.rsqrt(ms + EPS) * g


def _norm_matmul_kernel(x_ref, g_ref, w_ref, *rest, head_norm):
    if head_norm:
        hg_ref, o_ref, hn_ref = rest
    else:
        o_ref, hn_ref = rest

    @pl.when(pl.program_id(1) == 0)
    def _():
        hn_ref[...] = _rms(x_ref[...], g_ref[...]).astype(BF16)

    y = _dot(hn_ref[...], w_ref[...])
    if head_norm:
        tn = y.shape[1]
        for s in range(tn // LANES):
            sl = slice(s * LANES, (s + 1) * LANES)
            o_ref[:, sl] = _rms(y[:, sl], hg_ref[...]).astype(o_ref.dtype)
    else:
        o_ref[...] = y.astype(o_ref.dtype)


def norm_matmul(x, g, w, head_gain=None, out_dtype=F32, tm=512, tn=512):
    m, k = x.shape
    n = w.shape[1]
    tm = min(tm, m)
    tn = min(tn, n)
    head_norm = head_gain is not None
    in_specs = [pl.BlockSpec((tm, k), lambda i, j: (i, 0)),
                pl.BlockSpec((1, k), lambda i, j: (0, 0)),
                pl.BlockSpec((k, tn), lambda i, j: (0, j))]
    args = [x, g.reshape(1, k), w]
    if head_norm:
        in_specs.append(pl.BlockSpec((1, LANES), lambda i, j: (0, 0)))
        args.append(head_gain.reshape(1, LANES))
    return pl.pallas_call(
        functools.partial(_norm_matmul_kernel, head_norm=head_norm),
        out_shape=jax.ShapeDtypeStruct((m, n), out_dtype),
        grid=(m // tm, n // tn),
        in_specs=in_specs,
        out_specs=pl.BlockSpec((tm, tn), lambda i, j: (i, j)),
        scratch_shapes=[pltpu.VMEM((tm, k), BF16)],
        compiler_params=_cparams("parallel", "arbitrary"),
        name="norm_matmul",
    )(*args)


def _prep_kernel(fq_ref, fk_ref, fv_ref, sm_ref, gq_ref, gk_ref, bias_ref, gmat_ref, tril_ref, *rest,
                 t_valid, augment):
    if augment:
        qx_ref, kn_ref, g_ref, cs_ref, grow_ref, csrow_ref, kx_ref, vb_ref, carry_ref = rest
    else:
        qx_ref, kn_ref, g_ref, cs_ref, grow_ref, csrow_ref, carry_ref = rest
    c = pl.program_id(1)
    tc = sm_ref.shape[0]
    tp = tril_ref.shape[0]

    @pl.when(c == 0)
    def _():
        carry_ref[...] = jnp.zeros_like(carry_ref)

    gmat = gmat_ref[...]

    def head_rms(x, gain):
        x2 = _pad_rows(x * x, MIN_BF16_ROWS)
        hi = x2.astype(BF16)
        lo = (x2 - hi.astype(F32)).astype(BF16)
        ss = ((_dot(hi, gmat) + _dot(lo, gmat)) * (1.0 / FOX_HD))[0:tc]
        return x * lax.rsqrt(ss + EPS) * gain

    qn = head_rms(fq_ref[...], gq_ref[...]) * (FOX_HD ** -0.5)
    kn = head_rms(fk_ref[...], gk_ref[...])
    kn_ref[...] = kn

    a = _pad_rows(sm_ref[...], tp) + bias_ref[...]
    lane = lax.broadcasted_iota(jnp.int32, a.shape, 1)
    row = lax.broadcasted_iota(jnp.int32, a.shape, 0) + c * tc
    is_i = (lane >= LANE_MLI) & (lane < LANE_MLF)
    used = lane < LANE_MLF + ML_HEADS
    valid = row < t_valid
    g = jnp.where(is_i, a, _log_sigmoid(a))
    g = jnp.where(used, g, 0.0)
    g = jnp.where(valid, g, jnp.where(is_i, -jnp.inf, 0.0))
    gc = jnp.where(is_i, 0.0, g)
    cs = _dot_exact_lhs(tril_ref[...], gc)
    carry = carry_ref[...]
    csg = cs + jnp.where(lane < FOX_HEADS, carry, 0.0)
    carry_ref[...] = carry + cs[tc - 1:tc, :]
    g_ref[...] = g[0:tc]
    cs_ref[...] = csg[0:tc]
    grow_ref[...] = g.T
    csrow_ref[...] = csg.T

    lane1 = lax.broadcasted_iota(jnp.int32, (1, LANES), 1)
    f_hi = csg.astype(BF16).astype(F32)
    f_mid = (csg - f_hi).astype(BF16).astype(F32)
    f_lo = (csg - f_hi - f_mid).astype(BF16).astype(F32)
    for h in range(FOX_HEADS):
        ps = slice((h // 2) * LANES, (h // 2 + 1) * LANES)
        keep = (lane1 >= FOX_HD) if (h % 2) else (lane1 < FOX_HD)
        if not augment:
            qx_ref[:, h * LANES:(h + 1) * LANES] = jnp.where(keep, qn[:, ps], 0.0).astype(qx_ref.dtype)
            continue
        o = 0 if (h % 2) else FOX_HD
        fl = slice(LANE_FOXF + h, LANE_FOXF + h + 1)
        hi, mid, lo = f_hi[0:tc, fl], f_mid[0:tc, fl], f_lo[0:tc, fl]
        one_q = (lane1 >= o + 3) & (lane1 < o + 6)
        aug_q = jnp.where(lane1 == o, hi, jnp.where(lane1 == o + 1, mid, jnp.where(lane1 == o + 2, lo,
                          jnp.where(one_q, 1.0, 0.0))))
        one_k = (lane1 >= o) & (lane1 < o + 3)
        aug_k = jnp.where(lane1 == o + 3, -hi, jnp.where(lane1 == o + 4, -mid, jnp.where(lane1 == o + 5, -lo,
                          jnp.where(one_k, 1.0, 0.0))))
        qx_ref[:, h * LANES:(h + 1) * LANES] = jnp.where(keep, qn[:, ps], aug_q).astype(qx_ref.dtype)
        kx_ref[:, h * LANES:(h + 1) * LANES] = jnp.where(keep, kn[:, ps], aug_k).astype(kx_ref.dtype)
    if augment:
        vb_ref[...] = fv_ref[...].astype(vb_ref.dtype)


def gate_prep(z, gq, gk, bias_row, nb, t, tc, t_valid, augment):
    m = nb * t
    nc = t // tc
    gmat = jnp.asarray(np.kron(np.eye(FOX_HEADS), np.ones((FOX_HD, FOX_HD))), BF16)
    tp = max(tc, LANES)
    tril = jnp.asarray(np.tril(np.ones((tp, tp))), BF16)
    row_map = lambda b, c: (b * nc + c, 0)
    const = lambda b, c: (0, 0)
    out_shape = [jax.ShapeDtypeStruct((m, FOX_HEADS * LANES), BF16 if augment else F32),
                 jax.ShapeDtypeStruct((m, FOX_WIDTH), F32),
                 jax.ShapeDtypeStruct((m, LANES), F32),
                 jax.ShapeDtypeStruct((m, LANES), F32),
                 jax.ShapeDtypeStruct((nb * LANES, nc * tp), F32),
                 jax.ShapeDtypeStruct((nb * LANES, nc * tp), F32)]
    out_specs = [pl.BlockSpec((tc, FOX_HEADS * LANES), row_map),
                 pl.BlockSpec((tc, FOX_WIDTH), row_map),
                 pl.BlockSpec((tc, LANES), row_map),
                 pl.BlockSpec((tc, LANES), row_map),
                 pl.BlockSpec((LANES, tp), lambda b, c: (b, c)),
                 pl.BlockSpec((LANES, tp), lambda b, c: (b, c))]
    if augment:
        out_shape += [jax.ShapeDtypeStruct((m, FOX_HEADS * LANES), BF16), jax.ShapeDtypeStruct((m, FOX_WIDTH), BF16)]
        out_specs += [pl.BlockSpec((tc, FOX_HEADS * LANES), row_map), pl.BlockSpec((tc, FOX_WIDTH), row_map)]
    return pl.pallas_call(
        functools.partial(_prep_kernel, t_valid=t_valid, augment=augment),
        out_shape=tuple(out_shape),
        grid=(nb, nc),
        in_specs=[pl.BlockSpec((tc, 512), lambda b, c: (b * nc + c, COL_FQ)),
                  pl.BlockSpec((tc, 512), lambda b, c: (b * nc + c, COL_FK)),
                  pl.BlockSpec((tc, 512), lambda b, c: (b * nc + c, COL_FV)),
                  pl.BlockSpec((tc, LANES), lambda b, c: (b * nc + c, COL_SMALL * 4)),
                  pl.BlockSpec((1, 512), const), pl.BlockSpec((1, 512), const),
                  pl.BlockSpec((1, LANES), const),
                  pl.BlockSpec((512, 512), const), pl.BlockSpec((tp, tp), const)],
        out_specs=tuple(out_specs),
        scratch_shapes=[pltpu.VMEM((1, LANES), F32)],
        compiler_params=_cparams("parallel", "arbitrary"),
        name="gate_prep",
    )(z, z, z, z, gq, gk, bias_row, gmat, tril)


S5_SCAN_LANES = 512


def _cmul(ar, ai, br, bi):
    return ar * br - ai * bi, ar * bi + ai * br


def _s5_kernel(u_ref, h0r_ref, h0i_ref, perm_ref, permt_ref, bre_ref, bim_ref, lam_ref, pseg_ref, pk_ref,
               cre_ref, cim_ref, d_ref, wglu_ref, bglu_ref, y_ref, hlr_ref, hli_ref, hr_ref, hi_ref, cr_ref, ci_ref,
               *, t_last):
    c = pl.program_id(1)
    nc = pl.num_programs(1)
    tc = u_ref.shape[0]
    R = tc // SUBLANES
    half = S5_WIDTH // 2
    hl = S5_LANES // 2

    @pl.when(c == 0)
    def _():
        cr_ref[...] = h0r_ref[0]
        ci_ref[...] = h0i_ref[0]

    u = u_ref[...]
    ub = _pad_rows(u, MIN_BF16_ROWS).astype(BF16)
    if R > 1:
        ub = _dot(perm_ref[...], ub).astype(BF16)
    for j in range(2):
        uj = ub[:, j * half:(j + 1) * half]
        hr_ref[:, j * hl:(j + 1) * hl] = _dot(uj, bre_ref[j])[0:tc]
        hi_ref[:, j * hl:(j + 1) * hl] = _dot(uj, bim_ref[j])[0:tc]

    sub = lax.broadcasted_iota(jnp.int32, (SUBLANES, S5_SCAN_LANES), 0)
    for lc in range(S5_LANES // S5_SCAN_LANES):
        ls = slice(lc * S5_SCAN_LANES, (lc + 1) * S5_SCAN_LANES)
        lam_r = jnp.broadcast_to(lam_ref[0:1, ls], sub.shape)
        lam_i = jnp.broadcast_to(lam_ref[1:2, ls], sub.shape)
        init_r = jnp.where(sub == 0, jnp.broadcast_to(cr_ref[:, ls], sub.shape), 0.0)
        init_i = jnp.where(sub == 0, jnp.broadcast_to(ci_ref[:, ls], sub.shape), 0.0)

        def local_step(k, carry):
            h_r, h_i = carry
            rows = pl.ds(pl.multiple_of(k * SUBLANES, SUBLANES), SUBLANES)
            m_r, m_i = _cmul(lam_r, lam_i, h_r, h_i)
            h_r = m_r + hr_ref[rows, ls]
            h_i = m_i + hi_ref[rows, ls]
            hr_ref[rows, ls] = h_r
            hi_ref[rows, ls] = h_i
            return h_r, h_i

        e_r, e_i = lax.fori_loop(0, R, local_step, (init_r, init_i), unroll=min(R, 4))
        for j, lag in enumerate((1, 2, 4)):
            m_r, m_i = _cmul(pseg_ref[2 * j, :, ls], pseg_ref[2 * j + 1, :, ls],
                             pltpu.roll(e_r, lag, 0), pltpu.roll(e_i, lag, 0))
            e_r, e_i = e_r + m_r, e_i + m_i
        cr_ref[:, ls] = e_r[SUBLANES - 1:SUBLANES, :]
        ci_ref[:, ls] = e_i[SUBLANES - 1:SUBLANES, :]
        in_r = jnp.where(sub == 0, 0.0, pltpu.roll(e_r, 1, 0))
        in_i = jnp.where(sub == 0, 0.0, pltpu.roll(e_i, 1, 0))

        def fix_step(k, _):
            rows = pl.ds(pl.multiple_of(k * SUBLANES, SUBLANES), SUBLANES)
            p_r = jnp.broadcast_to(pk_ref[0, pl.ds(k, 1), ls], sub.shape)
            p_i = jnp.broadcast_to(pk_ref[1, pl.ds(k, 1), ls], sub.shape)
            m_r, m_i = _cmul(p_r, p_i, in_r, in_i)
            hr_ref[rows, ls] += m_r
            hi_ref[rows, ls] += m_i
            return 0

        lax.fori_loop(0, R, fix_step, 0, unroll=min(R, 4))

    @pl.when(c == nc - 1)
    def _():
        tl = t_last % tc
        pos = (tl % R) * SUBLANES + tl // R
        hlr_ref[0] = hr_ref[pos:pos + 1, :]
        hli_ref[0] = hi_ref[pos:pos + 1, :]

    hrb = _pad_rows(hr_ref[...], MIN_BF16_ROWS).astype(BF16)
    hib = _pad_rows(hi_ref[...], MIN_BF16_ROWS).astype(BF16)
    ys = [(_dot(hrb[:, j * hl:(j + 1) * hl], cre_ref[j]) + _dot(hib[:, j * hl:(j + 1) * hl], cim_ref[j]))
          for j in range(2)]
    y = jnp.concatenate(ys, axis=1)
    if R > 1:
        y = _dot_exact_lhs(permt_ref[...], y)
    y = _gelu_tanh(y[0:tc] + d_ref[...] * u)
    gate = _dot(_pad_rows(y, MIN_BF16_ROWS).astype(BF16), wglu_ref[...])[0:tc]
    y_ref[...] = y * _sigmoid(gate + bglu_ref[...])


def s5_mixer(z, h0r, h0i, sp, nb, t, tc, t_last):
    nc = t // tc
    R = tc // SUBLANES
    tperm = max(tc, MIN_BF16_ROWS)
    perm = np.zeros((tperm, tperm), np.float32)
    for s in range(SUBLANES):
        for k in range(R):
            perm[k * SUBLANES + s, s * R + k] = 1.0
    permt = jnp.asarray(perm.T, BF16)
    perm = jnp.asarray(perm, BF16)
    const = lambda b, c: (0, 0)
    const3 = lambda b, c: (0, 0, 0)
    state_spec = pl.BlockSpec((1, 1, S5_LANES), lambda b, c: (b, 0, 0))
    tabs = sp["tabs"][R]
    return pl.pallas_call(
        functools.partial(_s5_kernel, t_last=t_last),
        out_shape=(jax.ShapeDtypeStruct((nb * t, S5_WIDTH), F32),
                   jax.ShapeDtypeStruct((nb, 1, S5_LANES), F32),
                   jax.ShapeDtypeStruct((nb, 1, S5_LANES), F32)),
        grid=(nb, nc),
        in_specs=[pl.BlockSpec((tc, 512), lambda b, c: (b * nc + c, COL_S5)),
                  state_spec, state_spec,
                  pl.BlockSpec((tperm, tperm), const), pl.BlockSpec((tperm, tperm), const),
                  pl.BlockSpec((2, S5_WIDTH // 2, S5_LANES // 2), const3),
                  pl.BlockSpec((2, S5_WIDTH // 2, S5_LANES // 2), const3),
                  pl.BlockSpec((2, S5_LANES), const),
                  pl.BlockSpec((6, SUBLANES, S5_LANES), const3),
                  pl.BlockSpec((2, R, S5_LANES), const3),
                  pl.BlockSpec((2, S5_LANES // 2, S5_WIDTH // 2), const3),
                  pl.BlockSpec((2, S5_LANES // 2, S5_WIDTH // 2), const3),
                  pl.BlockSpec((1, S5_WIDTH), const),
                  pl.BlockSpec((S5_WIDTH, S5_WIDTH), const), pl.BlockSpec((1, S5_WIDTH), const)],
        out_specs=(pl.BlockSpec((tc, S5_WIDTH), lambda b, c: (b * nc + c, 0)), state_spec, state_spec),
        scratch_shapes=[pltpu.VMEM((tc, S5_LANES), F32), pltpu.VMEM((tc, S5_LANES), F32),
                        pltpu.VMEM((1, S5_LANES), F32), pltpu.VMEM((1, S5_LANES), F32)],
        compiler_params=_cparams("parallel", "arbitrary"),
        name="s5_mixer",
    )(z, h0r.reshape(nb, 1, S5_LANES), h0i.reshape(nb, 1, S5_LANES), perm, permt, sp["bre"], sp["bim"], tabs["lam"],
      tabs["pseg"], tabs["pk"], sp["cre"], sp["cim"], sp["d"], sp["wglu"], sp["bglu"])


def s5_params(a_re, a_im, log_step, b_re, b_im, c_re, c_im, d_skip, w_glu, b_glu, seg_lens):
    dt = jnp.exp(log_step)[:, None]
    mag = jnp.exp(a_re * dt)
    lr = (mag * jnp.cos(a_im * dt)).reshape(1, S5_LANES)
    li = (mag * jnp.sin(a_im * dt)).reshape(1, S5_LANES)
    den = a_re * a_re + a_im * a_im
    xr, xi = lr.reshape(a_re.shape) - 1.0, li.reshape(a_re.shape)
    fr = (xr * a_re + xi * a_im) / den
    fi = (xi * a_re - xr * a_im) / den
    bbr = fr[..., None] * b_re - fi[..., None] * b_im
    bbi = fr[..., None] * b_im + fi[..., None] * b_re
    gh = S5_GROUPS // 2
    eye = jnp.eye(gh, dtype=F32)

    def in_mat(b):
        return jnp.einsum('jgpc,gh->jgchp', b.reshape(2, gh, S5_STATE, S5_GROUP), eye).reshape(
            2, S5_WIDTH // 2, S5_LANES // 2).astype(BF16)

    def out_mat(cm):
        return jnp.einsum('jgcp,gh->jgphc', cm.reshape(2, gh, S5_GROUP, S5_STATE), eye).reshape(
            2, S5_LANES // 2, S5_WIDTH // 2).astype(BF16)

    def powers(pr, pi, n):
        tr, ti, cnt = pr, pi, 1
        while cnt < n:
            lr_, li_ = tr[cnt - 1:cnt], ti[cnt - 1:cnt]
            nr, ni = _cmul(tr, ti, lr_, li_)
            tr, ti, cnt = jnp.concatenate([tr, nr], axis=0), jnp.concatenate([ti, ni], axis=0), 2 * cnt
        return tr, ti

    sub = jnp.arange(SUBLANES)[:, None]
    tabs = {}
    for R in seg_lens:
        kr, ki = powers(lr, li, R)
        sr, si = powers(kr[R - 1:R], ki[R - 1:R], 4)
        pseg = []
        for lag in (1, 2, 4):
            msk = (sub >= lag).astype(F32)
            pseg += [msk * sr[lag - 1:lag], msk * si[lag - 1:lag]]
        tabs[R] = dict(lam=jnp.concatenate([lr, li], axis=0), pseg=jnp.stack(pseg), pk=jnp.stack([kr, ki]))
    return dict(bre=in_mat(bbr), bim=in_mat(bbi), cre=out_mat(c_re), cim=out_mat(-c_im), tabs=tabs,
                d=d_skip.reshape(1, S5_WIDTH), wglu=w_glu.astype(BF16), bglu=b_glu.reshape(1, S5_WIDTH))


def _fox_flash_kernel(qx_ref, kx_ref, v_ref, o_ref, m_ref, l_ref, acc_ref, s_ref, p_ref):
    i = pl.program_id(1)
    j = pl.program_id(2)
    tq = qx_ref.shape[0]
    tk = kx_ref.shape[0]

    @pl.when(j == 0)
    def _():
        m_ref[...] = jnp.full_like(m_ref, -jnp.inf)
        l_ref[...] = jnp.zeros_like(l_ref)
        acc_ref[...] = jnp.zeros_like(acc_ref)

    def step(masked):
        nr, ncol = tq // FLASH_ROWS, tk // LANES
        if masked:
            diff = (lax.broadcasted_iota(jnp.int32, (FLASH_ROWS, LANES), 0) -
                    lax.broadcasted_iota(jnp.int32, (FLASH_ROWS, LANES), 1))
        for h in range(FOX_HEADS):
            hs = slice(h * LANES, (h + 1) * LANES)
            ps = slice((h // 2) * LANES, (h // 2 + 1) * LANES)
            s_ref[...] = _dot_nt(qx_ref[:, hs], kx_ref[:, hs])
            for r in range(nr):
                rs = slice(r * FLASH_ROWS, (r + 1) * FLASH_ROWS)

                live = [cidx for cidx in range(ncol) if not (masked and cidx * LANES >= (r + 1) * FLASH_ROWS)]

                def piece(cidx):
                    sc = s_ref[rs, cidx * LANES:(cidx + 1) * LANES]
                    if masked and (cidx + 1) * LANES - 1 > r * FLASH_ROWS:
                        sc = jnp.where(diff >= (cidx * LANES - r * FLASH_ROWS), sc, -jnp.inf)
                    return sc

                mx = piece(live[0])
                for cidx in live[1:]:
                    mx = jnp.maximum(mx, piece(cidx))
                m_prev = m_ref[h, rs, :]
                m_new = jnp.maximum(m_prev, jnp.max(mx, axis=-1, keepdims=True))
                alpha = jnp.exp(m_prev - m_new)
                lsum = jnp.zeros((FLASH_ROWS, LANES), F32)
                for cidx in range(ncol):
                    cs_ = slice(cidx * LANES, (cidx + 1) * LANES)
                    if cidx not in live:
                        p_ref[rs, cs_] = jnp.zeros((FLASH_ROWS, LANES), BF16)
                        continue
                    pc = jnp.exp(piece(cidx) - m_new)
                    lsum = lsum + pc
                    p_ref[rs, cs_] = pc.astype(BF16)
                l_ref[h, rs, :] = alpha * l_ref[h, rs, :] + lsum
                acc_ref[h, rs, :] = alpha * acc_ref[h, rs, :]
                m_ref[h, rs, :] = m_new
            acc_ref[h] += _dot(p_ref[...], v_ref[:, ps])

    q_lo = i * tq
    q_hi = i * tq + tq - 1
    k_lo = j * tk
    k_hi = j * tk + tk - 1

    @pl.when(k_hi <= q_lo)
    def _():
        step(False)

    @pl.when((k_lo <= q_hi) & (k_hi > q_lo))
    def _():
        step(True)

    @pl.when(j == pl.num_programs(2) - 1)
    def _():
        lane = lax.broadcasted_iota(jnp.int32, (1, LANES), 1)
        for p in range(FOX_HEADS // 2):
            lo = acc_ref[2 * p] / jnp.sum(l_ref[2 * p], axis=-1, keepdims=True)
            hi = acc_ref[2 * p + 1] / jnp.sum(l_ref[2 * p + 1], axis=-1, keepdims=True)
            o_ref[:, p * LANES:(p + 1) * LANES] = jnp.where(lane < FOX_HD, lo, hi)


def fox_flash(qx, kx, vb, nb, t, tq, tk):
    assert tq == tk and tq % FLASH_ROWS == 0
    nq, nk = t // tq, t // tk

    def kv_idx(i, j):
        return jnp.minimum(j, (i * tq + tq - 1) // tk)

    return pl.pallas_call(
        _fox_flash_kernel,
        out_shape=jax.ShapeDtypeStruct((nb * t, FOX_WIDTH), F32),
        grid=(nb, nq, nk),
        in_specs=[pl.BlockSpec((tq, FOX_HEADS * LANES), lambda b, i, j: (b * nq + i, 0)),
                  pl.BlockSpec((tk, FOX_HEADS * LANES), lambda b, i, j: (b * nk + kv_idx(i, j), 0)),
                  pl.BlockSpec((tk, FOX_WIDTH), lambda b, i, j: (b * nk + kv_idx(i, j), 0))],
        out_specs=pl.BlockSpec((tq, FOX_WIDTH), lambda b, i, j: (b * nq + i, 0)),
        scratch_shapes=[pltpu.VMEM((FOX_HEADS, tq, LANES), F32), pltpu.VMEM((FOX_HEADS, tq, LANES), F32),
                        pltpu.VMEM((FOX_HEADS, tq, LANES), F32),
                        pltpu.VMEM((tq, tk), F32), pltpu.VMEM((tq, tk), BF16)],
        compiler_params=_cparams("parallel", "parallel", "arbitrary"),
        name="fox_flash",
    )(qx, kx, vb)


FOX_PAGES_PER_STEP = 8


def _fox_decode_kernel(pt_ref, qx_ref, kn_ref, vn_ref, csrow_ref, *rest, n_pages):
    npg = FOX_PAGES_PER_STEP
    k_refs = rest[0:npg]
    v_refs = rest[npg:2 * npg]
    f_refs = rest[2 * npg:3 * npg]
    triu_ref = rest[3 * npg]
    o_ref = rest[3 * npg + 1]
    qb_ref, m_ref, l_ref, acc_ref, fc_ref, kpad_ref, vpad_ref = rest[3 * npg + 2:]
    j = pl.program_id(1)
    page = kpad_ref.shape[0]
    nrow = FOX_HEADS * SAMPLE_T

    @pl.when(j == 0)
    def _():
        qb_ref[...] = jnp.zeros_like(qb_ref)
        for h in range(FOX_HEADS):
            ps = slice((h // 2) * LANES, (h // 2 + 1) * LANES)
            qb_ref[h * SAMPLE_T:(h + 1) * SAMPLE_T, ps] = qx_ref[:, h * LANES:(h + 1) * LANES]
        m_ref[...] = jnp.full_like(m_ref, -jnp.inf)
        l_ref[...] = jnp.zeros_like(l_ref)
        acc_ref[...] = jnp.zeros_like(acc_ref)
        fc_ref[...] = jnp.zeros_like(fc_ref)

    qb = qb_ref[...].astype(BF16)

    def update(s, pv_fn):
        m_prev = m_ref[...]
        m_new = jnp.maximum(m_prev, jnp.max(s, axis=-1, keepdims=True))
        alpha = jnp.exp(m_prev - m_new)
        p = jnp.exp(s - _lane_tile(m_new, s.shape[1] // LANES))
        l_ref[...] = alpha * l_ref[...] + jnp.sum(p, axis=-1, keepdims=True)
        acc_ref[...] = _lane_tile(alpha, FOX_WIDTH // LANES) * acc_ref[...] + pv_fn(p.astype(BF16))
        m_ref[...] = m_new

    def head_rows(x):
        return jnp.concatenate([jnp.broadcast_to(x[h:h + 1, :], (SAMPLE_T, x.shape[1])) for h in range(FOX_HEADS)],
                               axis=0)

    fcat = jnp.concatenate([f_refs[i][0, 0] for i in range(npg)], axis=0)
    cum_in = _dot_exact_rhs(fcat, triu_ref[...])
    off = fc_ref[...]
    ss, vts = [], []
    for i in range(npg):
        cum = cum_in[i * FOX_HEADS:(i + 1) * FOX_HEADS] + off
        off = jnp.broadcast_to(cum[:, page - 1:page], off.shape)
        kt = k_refs[i][0, 0].reshape(FOX_WIDTH, page).astype(BF16)
        ss.append(_dot(qb, kt) - head_rows(cum))
        vts.append(v_refs[i][0, 0].reshape(FOX_WIDTH, page).astype(BF16))
    fc_ref[...] = off

    def pv_past(p):
        pv = _dot_nt(p[:, 0:page], vts[0])
        for i in range(1, npg):
            pv = pv + _dot_nt(p[:, i * page:(i + 1) * page], vts[i])
        return pv

    update(jnp.concatenate(ss, axis=1), pv_past)

    @pl.when(j == n_pages // npg - 1)
    def _():
        kpad_ref[...] = jnp.zeros_like(kpad_ref)
        vpad_ref[...] = jnp.zeros_like(vpad_ref)
        kpad_ref[0:SAMPLE_T, :] = kn_ref[...]
        vpad_ref[0:SAMPLE_T, :] = vn_ref[...]
        bias = head_rows(csrow_ref[...] + fc_ref[...])
        s = _dot_nt(qb, kpad_ref[...].astype(BF16)) - bias
        tq_idx = lax.broadcasted_iota(jnp.int32, (nrow, page), 0) % SAMPLE_T
        tk_idx = lax.broadcasted_iota(jnp.int32, (nrow, page), 1)
        s = jnp.where(tk_idx <= tq_idx, s, -jnp.inf)
        update(s, lambda p: _dot(p, vpad_ref[...].astype(BF16)))
        lane = lax.broadcasted_iota(jnp.int32, (1, FOX_WIDTH), 1)
        out = jnp.zeros((SAMPLE_T, FOX_WIDTH), F32)
        for h in range(FOX_HEADS):
            rows = slice(h * SAMPLE_T, (h + 1) * SAMPLE_T)
            oh = acc_ref[rows, :] / _lane_tile(l_ref[rows, :], FOX_WIDTH // LANES)
            out = out + jnp.where((lane >= h * FOX_HD) & (lane < (h + 1) * FOX_HD), oh, 0.0)
        o_ref[...] = out


def fox_decode(page_table, qx, kn, z, csrow, pool_kt, pool_vt, pool_ft, layer):
    nb, n_pages = page_table.shape
    page = pool_kt.shape[-1]
    assert page == LANES and n_pages % FOX_PAGES_PER_STEP == 0
    npg = FOX_PAGES_PER_STEP
    nrow = FOX_HEADS * SAMPLE_T
    pt = page_table.reshape(-1)
    triu = jnp.asarray(np.triu(np.ones((page, page))), BF16)

    def pg5(i):
        return lambda b, j, pt: (layer, pt[b * n_pages + j * npg + i], 0, 0, 0)

    def pg4(i):
        return lambda b, j, pt: (layer, pt[b * n_pages + j * npg + i], 0, 0)

    row = lambda b, j, pt: (b, 0)
    in_specs = [pl.BlockSpec((SAMPLE_T, FOX_HEADS * LANES), row),
                pl.BlockSpec((SAMPLE_T, FOX_WIDTH), row),
                pl.BlockSpec((SAMPLE_T, FOX_WIDTH), lambda b, j, pt: (b, COL_FV)),
                pl.BlockSpec((FOX_HEADS, page), lambda b, j, pt: (b * (LANES // FOX_HEADS), 0))]
    in_specs += [pl.BlockSpec((1, 1, FOX_HEADS, FOX_HD, page), pg5(i)) for i in range(npg)]
    in_specs += [pl.BlockSpec((1, 1, FOX_HEADS, FOX_HD, page), pg5(i)) for i in range(npg)]
    in_specs += [pl.BlockSpec((1, 1, FOX_HEADS, page), pg4(i)) for i in range(npg)]
    in_specs += [pl.BlockSpec((page, page), lambda b, j, pt: (0, 0))]
    return pl.pallas_call(
        functools.partial(_fox_decode_kernel, n_pages=n_pages),
        out_shape=jax.ShapeDtypeStruct((nb * SAMPLE_T, FOX_WIDTH), F32),
        grid_spec=pltpu.PrefetchScalarGridSpec(
            num_scalar_prefetch=1,
            grid=(nb, n_pages // npg),
            in_specs=in_specs,
            out_specs=pl.BlockSpec((SAMPLE_T, FOX_WIDTH), row),
            scratch_shapes=[pltpu.VMEM((nrow, FOX_WIDTH), F32), pltpu.VMEM((nrow, LANES), F32),
                            pltpu.VMEM((nrow, LANES), F32), pltpu.VMEM((nrow, FOX_WIDTH), F32),
                            pltpu.VMEM((FOX_HEADS, page), F32),
                            pltpu.VMEM((page, FOX_WIDTH), F32), pltpu.VMEM((page, FOX_WIDTH), F32)]),
        compiler_params=_cparams("parallel", "arbitrary"),
        name="fox_decode",
    )(pt, qx, kn, z, csrow, *([pool_kt] * npg), *([pool_vt] * npg), *([pool_ft] * npg), triu)


def _mlstm_kernel(q_ref, k_ref, v_ref, o_ref, g_ref, cs_ref, grow_ref, csrow_ref, c0_ref, n0_ref, m0_ref, gn_ref,
                  y_ref, cout_ref, nout_ref, mout_ref, c_ref, n_ref, m_ref):
    c = pl.program_id(1)
    nc = pl.num_programs(1)
    L = q_ref.shape[0]

    @pl.when(c == 0)
    def _():
        c_ref[...] = c0_ref[0]
        n_ref[...] = n0_ref[0]
        m_ref[...] = m0_ref[0]

    Lp = grow_ref.shape[1]
    g = g_ref[...]
    cs = cs_ref[...]
    grow = grow_ref[...]
    csrow = csrow_ref[...]
    causal = (lax.broadcasted_iota(jnp.int32, (Lp, Lp), 0) >= lax.broadcasted_iota(jnp.int32, (Lp, Lp), 1))
    for h in range(ML_HEADS):
        hs = slice(h * ML_HD, (h + 1) * ML_HD)
        scale = ML_HD ** -0.5
        qb = _pad_rows(q_ref[:, hs], Lp).astype(BF16)
        kb = _pad_rows(k_ref[:, hs], Lp).astype(BF16)
        vb = _pad_rows(v_ref[:, hs], Lp).astype(BF16)
        qh, kh, vh = qb.astype(F32), kb.astype(F32), vb.astype(F32)
        bcol = _pad_rows(cs[:, LANE_MLF + h:LANE_MLF + h + 1], Lp)
        icol = _pad_rows(g[:, LANE_MLI + h:LANE_MLI + h + 1], Lp, -jnp.inf)
        brow = csrow[LANE_MLF + h:LANE_MLF + h + 1, :]
        irow = grow[LANE_MLI + h:LANE_MLI + h + 1, :]
        m_prev = m_ref[0:1, h:h + 1]
        log_d = jnp.where(causal, bcol - brow + irow, -jnp.inf)
        log_inter = bcol + m_prev
        m_t = jnp.maximum(log_inter, jnp.max(log_d, axis=-1, keepdims=True))
        d_w = jnp.exp(log_d - m_t)
        inter_w = jnp.exp(log_inter - m_t)
        s = _dot_nt(qb, kb) * (d_w * scale)
        ch = c_ref[h]
        n_row = n_ref[h:h + 1, :]
        num = _dot(s.astype(BF16), vb) + inter_w * _dot_nt(qb, ch.astype(BF16))
        den = jnp.sum(s, axis=-1, keepdims=True) + inter_w * jnp.sum(qh * n_row, axis=-1, keepdims=True)
        hh = num / jnp.maximum(jnp.abs(den), jnp.exp(-m_t))
        y_ref[:, hs] = _rms(hh[0:L], gn_ref[...]) * _sigmoid(o_ref[:, hs])
        m_end = m_t[L - 1:L, :]
        a_end = inter_w[L - 1:L, :]
        w_col = jnp.exp(bcol[L - 1:L, :] - bcol + icol - m_end) * scale
        c_ref[h] = a_end * ch + _dot_tn((vh * w_col).astype(BF16), kb)
        n_ref[h:h + 1, :] = a_end * n_row + jnp.sum(kh * w_col, axis=0, keepdims=True)
        m_ref[0:1, h:h + 1] = m_end

    @pl.when(c == nc - 1)
    def _():
        cout_ref[0] = c_ref[...]
        nout_ref[0] = n_ref[...]
        mout_ref[0] = m_ref[...]


def mlstm(za, zb, g, cs, grow, csrow, c0, n0, m0, gn, nb, t, L):
    nc = t // L
    rows = lambda b, c: (b * nc + c, 0)
    rr = 2 * SUBLANES
    m0p = jnp.zeros((nb, 1, LANES), F32).at[:, 0, :ML_HEADS].set(m0)
    outs = pl.pallas_call(
        _mlstm_kernel,
        out_shape=(jax.ShapeDtypeStruct((nb * t, ML_WIDTH), F32),
                   jax.ShapeDtypeStruct((nb, ML_HEADS, ML_HD, ML_HD), F32),
                   jax.ShapeDtypeStruct((nb, ML_HEADS, ML_HD), F32),
                   jax.ShapeDtypeStruct((nb, 1, LANES), F32)),
        grid=(nb, nc),
        in_specs=[pl.BlockSpec((L, 512), lambda b, c: (b * nc + c, COL_MQ)),
                  pl.BlockSpec((L, 512), lambda b, c: (b * nc + c, COL_MK)),
                  pl.BlockSpec((L, 512), lambda b, c: (b * nc + c, COL_MV)),
                  pl.BlockSpec((L, 512), lambda b, c: (b * nc + c, COL_MO)),
                  pl.BlockSpec((L, LANES), rows), pl.BlockSpec((L, LANES), rows),
                  pl.BlockSpec((rr, max(L, LANES)), lambda b, c: (b * (LANES // rr), c)),
                  pl.BlockSpec((rr, max(L, LANES)), lambda b, c: (b * (LANES // rr), c)),
                  pl.BlockSpec((1, ML_HEADS, ML_HD, ML_HD), lambda b, c: (b, 0, 0, 0)),
                  pl.BlockSpec((1, ML_HEADS, ML_HD), lambda b, c: (b, 0, 0)),
                  pl.BlockSpec((1, 1, LANES), lambda b, c: (b, 0, 0)),
                  pl.BlockSpec((1, ML_HD), lambda b, c: (0, 0))],
        out_specs=(pl.BlockSpec((L, ML_WIDTH), rows),
                   pl.BlockSpec((1, ML_HEADS, ML_HD, ML_HD), lambda b, c: (b, 0, 0, 0)),
                   pl.BlockSpec((1, ML_HEADS, ML_HD), lambda b, c: (b, 0, 0)),
                   pl.BlockSpec((1, 1, LANES), lambda b, c: (b, 0, 0))),
        scratch_shapes=[pltpu.VMEM((ML_HEADS, ML_HD, ML_HD), F32), pltpu.VMEM((ML_HEADS, ML_HD), F32),
                        pltpu.VMEM((1, LANES), F32)],
        compiler_params=_cparams("parallel", "arbitrary"),
        name="mlstm",
    )(zb, zb, zb, za, g, cs, grow, csrow, c0, n0, m0p, gn.reshape(1, ML_HD))
    y, c_new, n_new, m_new = outs
    return y, c_new, n_new, m_new[:, 0, :ML_HEADS]


def _merge_kernel(x_ref, g_ref, wg_ref, ys_ref, yf_ref, ym_ref, ws_ref, wf_ref, wm_ref, wo_ref, o_ref):
    x = x_ref[...]
    hn = _rms(x, g_ref[...]).astype(BF16)
    branches = (ys_ref, ws_ref), (yf_ref, wf_ref), (ym_ref, wm_ref)
    merged = None
    for b, (y_ref, w_ref) in enumerate(branches):
        gate = _sigmoid(_dot(hn, wg_ref[:, b * D_MODEL:(b + 1) * D_MODEL]))
        term = gate * _dot(y_ref[...].astype(BF16), w_ref[...])
        merged = term if merged is None else merged + term
    o_ref[...] = x + _dot(merged.astype(BF16), wo_ref[...])


def merge_out(x, g_mix, w_gates, ys, yf, ym, ws, wf, wm, wo, tm=256):
    m = x.shape[0]
    tm = min(tm, m)
    row = lambda i: (i, 0)
    const = lambda i: (0, 0)
    return pl.pallas_call(
        _merge_kernel,
        out_shape=jax.ShapeDtypeStruct((m, D_MODEL), F32),
        grid=(m // tm,),
        in_specs=[pl.BlockSpec((tm, D_MODEL), row), pl.BlockSpec((1, D_MODEL), const),
                  pl.BlockSpec((D_MODEL, 3 * D_MODEL), const),
                  pl.BlockSpec((tm, 512), row), pl.BlockSpec((tm, 512), row), pl.BlockSpec((tm, 512), row),
                  pl.BlockSpec((512, D_MODEL), const), pl.BlockSpec((512, D_MODEL), const),
                  pl.BlockSpec((512, D_MODEL), const), pl.BlockSpec((D_MODEL, D_MODEL), const)],
        out_specs=pl.BlockSpec((tm, D_MODEL), row),
        compiler_params=_cparams("parallel"),
        name="merge_out",
    )(x, g_mix.reshape(1, D_MODEL), w_gates, ys, yf, ym, ws, wf, wm, wo)


def _cross_kernel(q_ref, k_ref, v_ref, o_ref, *, cached):
    tq = q_ref.shape[0]
    for h in range(MEM_HEADS):
        hs = slice(h * MEM_HD, (h + 1) * MEM_HD)
        qh = _pad_rows(q_ref[:, hs], MIN_BF16_ROWS).astype(BF16)
        kh = k_ref[0, 0, :, h, :] if cached else k_ref[:, hs]
        vh = v_ref[0, 0, :, h, :] if cached else v_ref[:, hs]
        s = _dot_nt(qh, kh.astype(BF16)) * (MEM_HD ** -0.5)
        m = jnp.max(s, axis=-1, keepdims=True)
        p = jnp.exp(s - m)
        l = jnp.sum(p, axis=-1, keepdims=True)
        o_ref[:, hs] = (_dot(p.astype(BF16), vh.astype(BF16)) / l)[0:tq]


def cross_attend(q, mem_k, mem_v, nb, t, tq, layer=None):
    nq = t // tq
    cached = layer is not None
    if cached:
        n_mem = mem_k.shape[2]
        kv_spec = pl.BlockSpec((1, 1, n_mem, MEM_HEADS, MEM_HD), lambda b, i: (layer, b, 0, 0, 0))
    else:
        n_mem = mem_k.shape[0] // nb
        kv_spec = pl.BlockSpec((n_mem, MEM_WIDTH), lambda b, i: (b, 0))
    return pl.pallas_call(
        functools.partial(_cross_kernel, cached=cached),
        out_shape=jax.ShapeDtypeStruct((nb * t, MEM_WIDTH), F32),
        grid=(nb, nq),
        in_specs=[pl.BlockSpec((tq, MEM_WIDTH), lambda b, i: (b * nq + i, 0)), kv_spec, kv_spec],
        out_specs=pl.BlockSpec((tq, MEM_WIDTH), lambda b, i: (b * nq + i, 0)),
        compiler_params=_cparams("parallel", "parallel"),
        name="cross_attend",
    )(q, mem_k, mem_v)


def _proj_residual_kernel(x_ref, a_ref, w_ref, o_ref):
    o_ref[...] = x_ref[...] + _dot(a_ref[...].astype(BF16), w_ref[...])


def proj_residual(x, a, w, tm=512):
    m, n = x.shape
    k = a.shape[1]
    tm = min(tm, m)
    return pl.pallas_call(
        _proj_residual_kernel,
        out_shape=jax.ShapeDtypeStruct((m, n), F32),
        grid=(m // tm,),
        in_specs=[pl.BlockSpec((tm, n), lambda i: (i, 0)), pl.BlockSpec((tm, k), lambda i: (i, 0)),
                  pl.BlockSpec((k, n), lambda i: (0, 0))],
        out_specs=pl.BlockSpec((tm, n), lambda i: (i, 0)),
        compiler_params=_cparams("parallel"),
        name="proj_residual",
    )(x, a, w)


def _mlp_kernel(x_ref, g_ref, wu_ref, wd_ref, o_ref, hn_ref, acc_ref):
    f = pl.program_id(1)

    @pl.when(f == 0)
    def _():
        hn_ref[...] = _rms(x_ref[...], g_ref[...]).astype(BF16)
        acc_ref[...] = jnp.zeros_like(acc_ref)

    a = jnp.maximum(_dot(hn_ref[...], wu_ref[...]), 0.0)
    acc_ref[...] += _dot((a * a).astype(BF16), wd_ref[...])

    @pl.when(f == pl.num_programs(1) - 1)
    def _():
        o_ref[...] = x_ref[...] + acc_ref[...]


def mlp(x, g, wu, wd, tm=1024, tf=512):
    m, d = x.shape
    dff = wu.shape[1]
    tm = min(tm, m)
    return pl.pallas_call(
        _mlp_kernel,
        out_shape=jax.ShapeDtypeStruct((m, d), F32),
        grid=(m // tm, dff // tf),
        in_specs=[pl.BlockSpec((tm, d), lambda i, f: (i, 0)), pl.BlockSpec((1, d), lambda i, f: (0, 0)),
                  pl.BlockSpec((d, tf), lambda i, f: (0, f)), pl.BlockSpec((tf, d), lambda i, f: (f, 0))],
        out_specs=pl.BlockSpec((tm, d), lambda i, f: (i, 0)),
        scratch_shapes=[pltpu.VMEM((tm, d), BF16), pltpu.VMEM((tm, d), F32)],
        compiler_params=_cparams("parallel", "arbitrary"),
        name="mlp",
    )(x, g.reshape(1, d), wu, wd)


def _pack_w_in(w_in):
    offs = np.concatenate([[0], np.cumsum(SPLITS)])
    col = lambda i: w_in[:, int(offs[i]):int(offs[i + 1])]
    s5, fq, fk, fv, ff, mq, mk, mv, mi, mf, mo, gates = [col(i) for i in range(12)]
    pad = jnp.zeros((w_in.shape[0], 512 - FOX_HEADS - 2 * ML_HEADS), w_in.dtype)
    w_a = jnp.concatenate([s5, fq, fk, fv, mo, ff, mi, mf, pad], axis=1).astype(BF16)
    w_b = jnp.concatenate([mq, mk, mv], axis=1).astype(BF16)
    return w_a, w_b, gates.astype(BF16)


def _layer_weights(l, g_mix, w_in, s5_a_re, s5_a_im, s5_log_step, s5_b_re, s5_b_im, s5_c_re, s5_c_im, s5_d,
                   s5_w_glu, s5_b_glu, fox_gq, fox_gk, fox_bf, ml_bi, ml_bf, ml_gn, w_br_s5, w_br_fox, w_br_ml,
                   w_out, g_cross, w_cq, cross_gq, g_mem, w_mk, w_mv, cross_gk, w_co, g_mlp, w_up, w_down):
    bias_row = jnp.zeros((1, LANES), F32)
    bias_row = bias_row.at[0, LANE_FOXF:LANE_FOXF + FOX_HEADS].set(fox_bf[l])
    bias_row = bias_row.at[0, LANE_MLI:LANE_MLI + ML_HEADS].set(ml_bi[l])
    bias_row = bias_row.at[0, LANE_MLF:LANE_MLF + ML_HEADS].set(ml_bf[l])
    w_a, w_b, w_gates = _pack_w_in(w_in[l])
    return dict(
        g_mix=g_mix[l], w_a=w_a, w_b=w_b, w_gates=w_gates,
        s5=s5_params(s5_a_re[l], s5_a_im[l], s5_log_step[l], s5_b_re[l], s5_b_im[l], s5_c_re[l], s5_c_im[l],
                     s5_d[l], s5_w_glu[l], s5_b_glu[l], (SEQ_TILE // SUBLANES, SAMPLE_T // SUBLANES)),
        gq=jnp.tile(fox_gq[l], FOX_HEADS).reshape(1, FOX_WIDTH),
        gk=jnp.tile(fox_gk[l], FOX_HEADS).reshape(1, FOX_WIDTH),
        bias_row=bias_row, ml_gn=ml_gn[l],
        w_br_s5=w_br_s5[l].astype(BF16), w_br_fox=w_br_fox[l].astype(BF16), w_br_ml=w_br_ml[l].astype(BF16),
        w_out=w_out[l].astype(BF16), g_cross=g_cross[l], w_cq=w_cq[l].astype(BF16), cross_gq=cross_gq[l],
        g_mem=g_mem[l], w_mk=w_mk[l].astype(BF16), w_mv=w_mv[l].astype(BF16), cross_gk=cross_gk[l],
        w_co=w_co[l].astype(BF16), g_mlp=g_mlp[l], w_up=w_up[l].astype(BF16), w_down=w_down[l].astype(BF16))


def _hybrid_layer(x, W, nb, t, seq_tile, t_valid, s5_state, ml_state, mem_k, mem_v, fox_attend, augment,
                  mem_layer=None):
    z = norm_matmul(x, W["g_mix"], W["w_a"], tm=min(1024, nb * t))
    zb = norm_matmul(x, W["g_mix"], W["w_b"], tm=min(1024, nb * t),
                     out_dtype=BF16 if seq_tile % MIN_BF16_ROWS == 0 else F32)
    prep = gate_prep(z, W["gq"], W["gk"], W["bias_row"], nb, t, seq_tile, t_valid, augment)
    kn, g, cs, grow, csrow = prep[1:6]
    y_s5, s5_re, s5_im = s5_mixer(z, s5_state[0], s5_state[1], W["s5"], nb, t, seq_tile, t_valid - 1)
    y_fox = fox_attend(prep, z)
    y_ml, c_new, n_new, m_new = mlstm(z, zb, g, cs, grow, csrow, ml_state[0], ml_state[1], ml_state[2], W["ml_gn"],
                                      nb, t, seq_tile)
    x = merge_out(x, W["g_mix"], W["w_gates"], y_s5, y_fox, y_ml, W["w_br_s5"], W["w_br_fox"], W["w_br_ml"],
                  W["w_out"])
    qc = norm_matmul(x, W["g_cross"], W["w_cq"], head_gain=W["cross_gq"])
    oc = cross_attend(qc, mem_k, mem_v, nb, t, seq_tile, mem_layer)
    x = proj_residual(x, oc, W["w_co"])
    x = mlp(x, W["g_mlp"], W["w_up"], W["w_down"])
    return x, z, kn, g, s5_re.reshape(nb, S5_GROUPS, S5_STATE), s5_im.reshape(nb, S5_GROUPS, S5_STATE), \
        c_new, n_new, m_new


def kernel(x_prompt, x_sample, mem_prompt, cache_fox_k, cache_fox_v, cache_fox_logf, page_table, state_s5_re, state_s5_im, state_mlstm_C, state_mlstm_n, state_mlstm_m, cache_mem_k, cache_mem_v, g_mix, w_in, s5_a_re, s5_a_im, s5_log_step, s5_b_re, s5_b_im, s5_c_re, s5_c_im, s5_d, s5_w_glu, s5_b_glu, fox_gq, fox_gk, fox_bf, ml_bi, ml_bf, ml_gn, w_br_s5, w_br_fox, w_br_ml, w_out, g_cross, w_cq, cross_gq, g_mem, w_mk, w_mv, cross_gk, w_co, g_mlp, w_up, w_down):
    depth = w_in.shape[0]
    bp, tp, _ = x_prompt.shape
    bs, ts, _ = x_sample.shape
    n_mem = mem_prompt.shape[1]

    xp = x_prompt.reshape(bp * tp, D_MODEL)
    xs = jnp.pad(x_sample, ((0, 0), (0, SAMPLE_T - ts), (0, 0))).reshape(bs * SAMPLE_T, D_MODEL)
    mem = mem_prompt.reshape(bp * n_mem, D_MODEL)
    zeros_p = (jnp.zeros((bp, S5_LANES), F32), jnp.zeros((bp, S5_LANES), F32))
    zeros_ml = (jnp.zeros((bp, ML_HEADS, ML_HD, ML_HD), F32), jnp.zeros((bp, ML_HEADS, ML_HD), F32),
                jnp.zeros((bp, ML_HEADS), F32))
    pool_kt = jnp.transpose(cache_fox_k, (0, 1, 3, 4, 2))
    pool_vt = jnp.transpose(cache_fox_v, (0, 1, 3, 4, 2))
    pool_ft = jnp.transpose(cache_fox_logf, (0, 1, 3, 2))
    st_p, st_s = [], []
    for l in range(depth):
        W = _layer_weights(l, g_mix, w_in, s5_a_re, s5_a_im, s5_log_step, s5_b_re, s5_b_im, s5_c_re, s5_c_im, s5_d,
                           s5_w_glu, s5_b_glu, fox_gq, fox_gk, fox_bf, ml_bi, ml_bf, ml_gn, w_br_s5, w_br_fox,
                           w_br_ml, w_out, g_cross, w_cq, cross_gq, g_mem, w_mk, w_mv, cross_gk, w_co, g_mlp, w_up,
                           w_down)
        mk_p = norm_matmul(mem, W["g_mem"], W["w_mk"], head_gain=W["cross_gk"])
        mv_p = norm_matmul(mem, W["g_mem"], W["w_mv"])

        def flash(prep, z):
            return fox_flash(prep[0], prep[6], prep[7], bp, tp, FLASH_TILE, FLASH_TILE)

        xp, z, kn, g, s5r, s5i, c_new, n_new, m_new = _hybrid_layer(
            xp, W, bp, tp, SEQ_TILE, tp, zeros_p, zeros_ml, mk_p, mv_p, flash, True)
        st_p.append((kn.reshape(bp, tp, FOX_HEADS, FOX_HD),
                     z[:, COL_FV * 512:(COL_FV + 1) * 512].reshape(bp, tp, FOX_HEADS, FOX_HD),
                     g[:, LANE_FOXF:LANE_FOXF + FOX_HEADS].reshape(bp, tp, FOX_HEADS),
                     s5r, s5i, c_new, n_new, m_new,
                     mk_p.reshape(bp, n_mem, MEM_HEADS, MEM_HD), mv_p.reshape(bp, n_mem, MEM_HEADS, MEM_HD)))

        def decode(prep, z, layer=l):
            return fox_decode(page_table, prep[0], prep[1], z, prep[5], pool_kt, pool_vt, pool_ft, layer)

        xs, z, kn, g, s5r, s5i, c_new, n_new, m_new = _hybrid_layer(
            xs, W, bs, SAMPLE_T, SAMPLE_T, ts,
            (state_s5_re[l].reshape(bs, S5_LANES), state_s5_im[l].reshape(bs, S5_LANES)),
            (state_mlstm_C[l], state_mlstm_n[l], state_mlstm_m[l]),
            cache_mem_k, cache_mem_v, decode, False, mem_layer=l)
        st_s.append((kn.reshape(bs, SAMPLE_T, FOX_HEADS, FOX_HD)[:, :ts],
                     z[:, COL_FV * 512:(COL_FV + 1) * 512].reshape(bs, SAMPLE_T, FOX_HEADS, FOX_HD)[:, :ts],
                     g[:, LANE_FOXF:LANE_FOXF + FOX_HEADS].reshape(bs, SAMPLE_T, FOX_HEADS)[:, :ts],
                     s5r, s5i, c_new, n_new, m_new))
    outs_p = [jnp.stack(a) for a in zip(*st_p)]
    outs_s = [jnp.stack(a) for a in zip(*st_s)]
    yp = xp.reshape(bp, tp, D_MODEL)
    ys = xs.reshape(bs, SAMPLE_T, D_MODEL)[:, :ts]
    return (yp, ys, *outs_p, *outs_s)
```

```python
import functools
import math

import jax
import jax.numpy as jnp
import numpy as np
from jax import lax
from jax.experimental import pallas as pl
from jax.experimental.pallas import tpu as pltpu

F32 = jnp.float32
BF16 = jnp.bfloat16

LANES = 128
SUBLANES = 8
MIN_BF16_ROWS = 16
MXU_DIM = 256
VMEM_LIMIT_BYTES = 48 * 1024 * 1024

D_MODEL = 1024
S5_WIDTH = 512
S5_GROUP = 16
S5_GROUPS = 32
S5_STATE = 64
S5_LANES = S5_GROUPS * S5_STATE
FOX_HEADS = 8
FOX_HD = 64
FOX_WIDTH = 512
ML_HEADS = 4
ML_HD = 128
ML_WIDTH = 512
MEM_HEADS = 4
MEM_HD = 128
MEM_WIDTH = 512
D_FF = 4096
EPS = 1e-6
SPLITS = (S5_WIDTH, FOX_WIDTH, FOX_WIDTH, FOX_WIDTH, FOX_HEADS, ML_WIDTH, ML_WIDTH, ML_WIDTH,
          ML_HEADS, ML_HEADS, ML_WIDTH, 3 * D_MODEL)

COL_S5, COL_FQ, COL_FK, COL_FV, COL_MO, COL_SMALL = range(6)
COL_MQ, COL_MK, COL_MV = range(3)
LANE_FOXF = 0
LANE_MLI = 8
LANE_MLF = 12
SEQ_TILE = 256
FLASH_TILE = 512
FLASH_ROWS = 128
SAMPLE_T = 8


def _cparams(*sem):
    return pltpu.CompilerParams(dimension_semantics=sem, vmem_limit_bytes=VMEM_LIMIT_BYTES)


def _dot(a, b):
    return jnp.dot(a, b, preferred_element_type=F32)


def _dot_nt(a, b):
    return lax.dot_general(a, b, (((1,), (1,)), ((), ())), preferred_element_type=F32)


def _dot_tn(a, b):
    return lax.dot_general(a, b, (((0,), (0,)), ((), ())), preferred_element_type=F32)


def _split3(x):
    hi = x.astype(BF16)
    r1 = x - hi.astype(F32)
    mid = r1.astype(BF16)
    lo = (r1 - mid.astype(F32)).astype(BF16)
    return hi, mid, lo


def _dot_exact_rhs(x, ones_rhs):
    hi, mid, lo = _split3(x)
    return _dot(hi, ones_rhs) + _dot(mid, ones_rhs) + _dot(lo, ones_rhs)


def _dot_exact_lhs(ones_lhs, x):
    hi, mid, lo = _split3(x)
    return _dot(ones_lhs, hi) + _dot(ones_lhs, mid) + _dot(ones_lhs, lo)


def _pad_rows(x, n, fill=0.0):
    if x.shape[0] >= n:
        return x
    return jnp.concatenate([x, jnp.full((n - x.shape[0], x.shape[1]), fill, x.dtype)], axis=0)


def _lane_tile(x, n):
    return x if n == 1 else jnp.concatenate([x] * n, axis=1)


def _log_sigmoid(a):
    return jnp.minimum(a, 0.0) - jnp.log1p(jnp.exp(-jnp.abs(a)))


def _sigmoid(a):
    return 1.0 / (1.0 + jnp.exp(-a))


def _gelu_tanh(x):
    c = math.sqrt(2.0 / math.pi)
    return 0.5 * x * (1.0 + jnp.tanh(c * (x + 0.044715 * (x * x * x))))


def _rms(x, g):
    ms = jnp.mean(x * x, axis=-1, keepdims=True)
    return x * lax.rsqrt(ms + EPS) * g


def _norm_matmul_kernel(x_ref, g_ref, w_ref, *rest, head_norm):
    if head_norm:
        hg_ref, o_ref, hn_ref = rest
    else:
        o_ref, hn_ref = rest

    @pl.when(pl.program_id(1) == 0)
    def _():
        hn_ref[...] = _rms(x_ref[...], g_ref[...]).astype(BF16)

    y = _dot(hn_ref[...], w_ref[...])
    if head_norm:
        tn = y.shape[1]
        for s in range(tn // LANES):
            sl = slice(s * LANES, (s + 1) * LANES)
            o_ref[:, sl] = _rms(y[:, sl], hg_ref[...]).astype(o_ref.dtype)
    else:
        o_ref[...] = y.astype(o_ref.dtype)


def norm_matmul(x, g, w, head_gain=None, out_dtype=F32, tm=512, tn=512):
    m, k = x.shape
    n = w.shape[1]
    tm = min(tm, m)
    tn = min(tn, n)
    head_norm = head_gain is not None
    in_specs = [pl.BlockSpec((tm, k), lambda i, j: (i, 0)),
                pl.BlockSpec((1, k), lambda i, j: (0, 0)),
                pl.BlockSpec((k, tn), lambda i, j: (0, j))]
    args = [x, g.reshape(1, k), w]
    if head_norm:
        in_specs.append(pl.BlockSpec((1, LANES), lambda i, j: (0, 0)))
        args.append(head_gain.reshape(1, LANES))
    return pl.pallas_call(
        functools.partial(_norm_matmul_kernel, head_norm=head_norm),
        out_shape=jax.ShapeDtypeStruct((m, n), out_dtype),
        grid=(m // tm, n // tn),
        in_specs=in_specs,
        out_specs=pl.BlockSpec((tm, tn), lambda i, j: (i, j)),
        scratch_shapes=[pltpu.VMEM((tm, k), BF16)],
        compiler_params=_cparams("parallel", "arbitrary"),
        name="norm_matmul",
    )(*args)


def _prep_kernel(fq_ref, fk_ref, fv_ref, sm_ref, gq_ref, gk_ref, bias_ref, gmat_ref, tril_ref, *rest,
                 t_valid, augment):
    if augment:
        qx_ref, kn_ref, g_ref, cs_ref, grow_ref, csrow_ref, kx_ref, vb_ref, carry_ref = rest
    else:
        qx_ref, kn_ref, g_ref, cs_ref, grow_ref, csrow_ref, carry_ref = rest
    c = pl.program_id(1)
    tc = sm_ref.shape[0]
    tp = tril_ref.shape[0]

    @pl.when(c == 0)
    def _():
        carry_ref[...] = jnp.zeros_like(carry_ref)

    gmat = gmat_ref[...]

    def head_rms(x, gain):
        x2 = _pad_rows(x * x, MIN_BF16_ROWS)
        hi = x2.astype(BF16)
        lo = (x2 - hi.astype(F32)).astype(BF16)
        ss = ((_dot(hi, gmat) + _dot(lo, gmat)) * (1.0 / FOX_HD))[0:tc]
        return x * lax.rsqrt(ss + EPS) * gain

    qn = head_rms(fq_ref[...], gq_ref[...]) * (FOX_HD ** -0.5)
    kn = head_rms(fk_ref[...], gk_ref[...])
    kn_ref[...] = kn

    a = _pad_rows(sm_ref[...], tp) + bias_ref[...]
    lane = lax.broadcasted_iota(jnp.int32, a.shape, 1)
    row = lax.broadcasted_iota(jnp.int32, a.shape, 0) + c * tc
    is_i = (lane >= LANE_MLI) & (lane < LANE_MLF)
    used = lane < LANE_MLF + ML_HEADS
    valid = row < t_valid
    g = jnp.where(is_i, a, _log_sigmoid(a))
    g = jnp.where(used, g, 0.0)
    g = jnp.where(valid, g, jnp.where(is_i, -jnp.inf, 0.0))
    gc = jnp.where(is_i, 0.0, g)
    cs = _dot_exact_lhs(tril_ref[...], gc)
    carry = carry_ref[...]
    csg = cs + jnp.where(lane < FOX_HEADS, carry, 0.0)
    carry_ref[...] = carry + cs[tc - 1:tc, :]
    g_ref[...] = g[0:tc]
    cs_ref[...] = csg[0:tc]
    grow_ref[...] = g.T
    csrow_ref[...] = csg.T

    lane1 = lax.broadcasted_iota(jnp.int32, (1, LANES), 1)
    f_hi = csg.astype(BF16).astype(F32)
    f_mid = (csg - f_hi).astype(BF16).astype(F32)
    f_lo = (csg - f_hi - f_mid).astype(BF16).astype(F32)
    for h in range(FOX_HEADS):
        ps = slice((h // 2) * LANES, (h // 2 + 1) * LANES)
        keep = (lane1 >= FOX_HD) if (h % 2) else (lane1 < FOX_HD)
        if not augment:
            qx_ref[:, h * LANES:(h + 1) * LANES] = jnp.where(keep, qn[:, ps], 0.0).astype(qx_ref.dtype)
            continue
        o = 0 if (h % 2) else FOX_HD
        fl = slice(LANE_FOXF + h, LANE_FOXF + h + 1)
        hi, mid, lo = f_hi[0:tc, fl], f_mid[0:tc, fl], f_lo[0:tc, fl]
        one_q = (lane1 >= o + 3) & (lane1 < o + 6)
        aug_q = jnp.where(lane1 == o, hi, jnp.where(lane1 == o + 1, mid, jnp.where(lane1 == o + 2, lo,
                          jnp.where(one_q, 1.0, 0.0))))
        one_k = (lane1 >= o) & (lane1 < o + 3)
        aug_k = jnp.where(lane1 == o + 3, -hi, jnp.where(lane1 == o + 4, -mid, jnp.where(lane1 == o + 5, -lo,
                          jnp.where(one_k, 1.0, 0.0))))
        qx_ref[:, h * LANES:(h + 1) * LANES] = jnp.where(keep, qn[:, ps], aug_q).astype(qx_ref.dtype)
        kx_ref[:, h * LANES:(h + 1) * LANES] = jnp.where(keep, kn[:, ps], aug_k).astype(kx_ref.dtype)
    if augment:
        vb_ref[...] = fv_ref[...].astype(vb_ref.dtype)


def gate_prep(z, gq, gk, bias_row, nb, t, tc, t_valid, augment):
    m = nb * t
    nc = t // tc
    gmat = jnp.asarray(np.kron(np.eye(FOX_HEADS), np.ones((FOX_HD, FOX_HD))), BF16)
    tp = max(tc, LANES)
    tril = jnp.asarray(np.tril(np.ones((tp, tp))), BF16)
    row_map = lambda b, c: (b * nc + c, 0)
    const = lambda b, c: (0, 0)
    out_shape = [jax.ShapeDtypeStruct((m, FOX_HEADS * LANES), BF16 if augment else F32),
                 jax.ShapeDtypeStruct((m, FOX_WIDTH), F32),
                 jax.ShapeDtypeStruct((m, LANES), F32),
                 jax.ShapeDtypeStruct((m, LANES), F32),
                 jax.ShapeDtypeStruct((nb * LANES, nc * tp), F32),
                 jax.ShapeDtypeStruct((nb * LANES, nc * tp), F32)]
    out_specs = [pl.BlockSpec((tc, FOX_HEADS * LANES), row_map),
                 pl.BlockSpec((tc, FOX_WIDTH), row_map),
                 pl.BlockSpec((tc, LANES), row_map),
                 pl.BlockSpec((tc, LANES), row_map),
                 pl.BlockSpec((LANES, tp), lambda b, c: (b, c)),
                 pl.BlockSpec((LANES, tp), lambda b, c: (b, c))]
    if augment:
        out_shape += [jax.ShapeDtypeStruct((m, FOX_HEADS * LANES), BF16), jax.ShapeDtypeStruct((m, FOX_WIDTH), BF16)]
        out_specs += [pl.BlockSpec((tc, FOX_HEADS * LANES), row_map), pl.BlockSpec((tc, FOX_WIDTH), row_map)]
    return pl.pallas_call(
        functools.partial(_prep_kernel, t_valid=t_valid, augment=augment),
        out_shape=tuple(out_shape),
        grid=(nb, nc),
        in_specs=[pl.BlockSpec((tc, 512), lambda b, c: (b * nc + c, COL_FQ)),
                  pl.BlockSpec((tc, 512), lambda b, c: (b * nc + c, COL_FK)),
                  pl.BlockSpec((tc, 512), lambda b, c: (b * nc + c, COL_FV)),
                  pl.BlockSpec((tc, LANES), lambda b, c: (b * nc + c, COL_SMALL * 4)),
                  pl.BlockSpec((1, 512), const), pl.BlockSpec((1, 512), const),
                  pl.BlockSpec((1, LANES), const),
                  pl.BlockSpec((512, 512), const), pl.BlockSpec((tp, tp), const)],
        out_specs=tuple(out_specs),
        scratch_shapes=[pltpu.VMEM((1, LANES), F32)],
        compiler_params=_cparams("parallel", "arbitrary"),
        name="gate_prep",
    )(z, z, z, z, gq, gk, bias_row, gmat, tril)


S5_SCAN_LANES = 512


def _cmul(ar, ai, br, bi):
    return ar * br - ai * bi, ar * bi + ai * br


def _s5_kernel(u_ref, h0r_ref, h0i_ref, perm_ref, permt_ref, bre_ref, bim_ref, lam_ref, pseg_ref, pk_ref,
               cre_ref, cim_ref, d_ref, wglu_ref, bglu_ref, y_ref, hlr_ref, hli_ref, hr_ref, hi_ref, cr_ref, ci_ref,
               *, t_last):
    c = pl.program_id(1)
    nc = pl.num_programs(1)
    tc = u_ref.shape[0]
    R = tc // SUBLANES
    half = S5_WIDTH // 2
    hl = S5_LANES // 2

    @pl.when(c == 0)
    def _():
        cr_ref[...] = h0r_ref[0]
        ci_ref[...] = h0i_ref[0]

    u = u_ref[...]
    ub = _pad_rows(u, MIN_BF16_ROWS).astype(BF16)
    if R > 1:
        ub = _dot(perm_ref[...], ub).astype(BF16)
    for j in range(2):
        uj = ub[:, j * half:(j + 1) * half]
        hr_ref[:, j * hl:(j + 1) * hl] = _dot(uj, bre_ref[j])[0:tc]
        hi_ref[:, j * hl:(j + 1) * hl] = _dot(uj, bim_ref[j])[0:tc]

    sub = lax.broadcasted_iota(jnp.int32, (SUBLANES, S5_SCAN_LANES), 0)
    for lc in range(S5_LANES // S5_SCAN_LANES):
        ls = slice(lc * S5_SCAN_LANES, (lc + 1) * S5_SCAN_LANES)
        lam_r = jnp.broadcast_to(lam_ref[0:1, ls], sub.shape)
        lam_i = jnp.broadcast_to(lam_ref[1:2, ls], sub.shape)
        init_r = jnp.where(sub == 0, jnp.broadcast_to(cr_ref[:, ls], sub.shape), 0.0)
        init_i = jnp.where(sub == 0, jnp.broadcast_to(ci_ref[:, ls], sub.shape), 0.0)

        def local_step(k, carry):
            h_r, h_i = carry
            rows = pl.ds(pl.multiple_of(k * SUBLANES, SUBLANES), SUBLANES)
            m_r, m_i = _cmul(lam_r, lam_i, h_r, h_i)
            h_r = m_r + hr_ref[rows, ls]
            h_i = m_i + hi_ref[rows, ls]
            hr_ref[rows, ls] = h_r
            hi_ref[rows, ls] = h_i
            return h_r, h_i

        e_r, e_i = lax.fori_loop(0, R, local_step, (init_r, init_i), unroll=min(R, 4))
        for j, lag in enumerate((1, 2, 4)):
            m_r, m_i = _cmul(pseg_ref[2 * j, :, ls], pseg_ref[2 * j + 1, :, ls],
                             pltpu.roll(e_r, lag, 0), pltpu.roll(e_i, lag, 0))
            e_r, e_i = e_r + m_r, e_i + m_i
        cr_ref[:, ls] = e_r[SUBLANES - 1:SUBLANES, :]
        ci_ref[:, ls] = e_i[SUBLANES - 1:SUBLANES, :]
        in_r = jnp.where(sub == 0, 0.0, pltpu.roll(e_r, 1, 0))
        in_i = jnp.where(sub == 0, 0.0, pltpu.roll(e_i, 1, 0))

        def fix_step(k, _):
            rows = pl.ds(pl.multiple_of(k * SUBLANES, SUBLANES), SUBLANES)
            p_r = jnp.broadcast_to(pk_ref[0, pl.ds(k, 1), ls], sub.shape)
            p_i = jnp.broadcast_to(pk_ref[1, pl.ds(k, 1), ls], sub.shape)
            m_r, m_i = _cmul(p_r, p_i, in_r, in_i)
            hr_ref[rows, ls] += m_r
            hi_ref[rows, ls] += m_i
            return 0

        lax.fori_loop(0, R, fix_step, 0, unroll=min(R, 4))

    @pl.when(c == nc - 1)
    def _():
        tl = t_last % tc
        pos = (tl % R) * SUBLANES + tl // R
        hlr_ref[0] = hr_ref[pos:pos + 1, :]
        hli_ref[0] = hi_ref[pos:pos + 1, :]

    hrb = _pad_rows(hr_ref[...], MIN_BF16_ROWS).astype(BF16)
    hib = _pad_rows(hi_ref[...], MIN_BF16_ROWS).astype(BF16)
    ys = [(_dot(hrb[:, j * hl:(j + 1) * hl], cre_ref[j]) + _dot(hib[:, j * hl:(j + 1) * hl], cim_ref[j]))
          for j in range(2)]
    y = jnp.concatenate(ys, axis=1)
    if R > 1:
        y = _dot_exact_lhs(permt_ref[...], y)
    y = _gelu_tanh(y[0:tc] + d_ref[...] * u)
    gate = _dot(_pad_rows(y, MIN_BF16_ROWS).astype(BF16), wglu_ref[...])[0:tc]
    y_ref[...] = y * _sigmoid(gate + bglu_ref[...])


def s5_mixer(z, h0r, h0i, sp, nb, t, tc, t_last):
    nc = t // tc
    R = tc // SUBLANES
    tperm = max(tc, MIN_BF16_ROWS)
    perm = np.zeros((tperm, tperm), np.float32)
    for s in range(SUBLANES):
        for k in range(R):
            perm[k * SUBLANES + s, s * R + k] = 1.0
    permt = jnp.asarray(perm.T, BF16)
    perm = jnp.asarray(perm, BF16)
    const = lambda b, c: (0, 0)
    const3 = lambda b, c: (0, 0, 0)
    state_spec = pl.BlockSpec((1, 1, S5_LANES), lambda b, c: (b, 0, 0))
    tabs = sp["tabs"][R]
    return pl.pallas_call(
        functools.partial(_s5_kernel, t_last=t_last),
        out_shape=(jax.ShapeDtypeStruct((nb * t, S5_WIDTH), F32),
                   jax.ShapeDtypeStruct((nb, 1, S5_LANES), F32),
                   jax.ShapeDtypeStruct((nb, 1, S5_LANES), F32)),
        grid=(nb, nc),
        in_specs=[pl.BlockSpec((tc, 512), lambda b, c: (b * nc + c, COL_S5)),
                  state_spec, state_spec,
                  pl.BlockSpec((tperm, tperm), const), pl.BlockSpec((tperm, tperm), const),
                  pl.BlockSpec((2, S5_WIDTH // 2, S5_LANES // 2), const3),
                  pl.BlockSpec((2, S5_WIDTH // 2, S5_LANES // 2), const3),
                  pl.BlockSpec((2, S5_LANES), const),
                  pl.BlockSpec((6, SUBLANES, S5_LANES), const3),
                  pl.BlockSpec((2, R, S5_LANES), const3),
                  pl.BlockSpec((2, S5_LANES // 2, S5_WIDTH // 2), const3),
                  pl.BlockSpec((2, S5_LANES // 2, S5_WIDTH // 2), const3),
                  pl.BlockSpec((1, S5_WIDTH), const),
                  pl.BlockSpec((S5_WIDTH, S5_WIDTH), const), pl.BlockSpec((1, S5_WIDTH), const)],
        out_specs=(pl.BlockSpec((tc, S5_WIDTH), lambda b, c: (b * nc + c, 0)), state_spec, state_spec),
        scratch_shapes=[pltpu.VMEM((tc, S5_LANES), F32), pltpu.VMEM((tc, S5_LANES), F32),
                        pltpu.VMEM((1, S5_LANES), F32), pltpu.VMEM((1, S5_LANES), F32)],
        compiler_params=_cparams("parallel", "arbitrary"),
        name="s5_mixer",
    )(z, h0r.reshape(nb, 1, S5_LANES), h0i.reshape(nb, 1, S5_LANES), perm, permt, sp["bre"], sp["bim"], tabs["lam"],
      tabs["pseg"], tabs["pk"], sp["cre"], sp["cim"], sp["d"], sp["wglu"], sp["bglu"])


def s5_params(a_re, a_im, log_step, b_re, b_im, c_re, c_im, d_skip, w_glu, b_glu, seg_lens):
    dt = jnp.exp(log_step)[:, None]
    mag = jnp.exp(a_re * dt)
    lr = (mag * jnp.cos(a_im * dt)).reshape(1, S5_LANES)
    li = (mag * jnp.sin(a_im * dt)).reshape(1, S5_LANES)
    den = a_re * a_re + a_im * a_im
    xr, xi = lr.reshape(a_re.shape) - 1.0, li.reshape(a_re.shape)
    fr = (xr * a_re + xi * a_im) / den
    fi = (xi * a_re - xr * a_im) / den
    bbr = fr[..., None] * b_re - fi[..., None] * b_im
    bbi = fr[..., None] * b_im + fi[..., None] * b_re
    gh = S5_GROUPS // 2
    eye = jnp.eye(gh, dtype=F32)

    def in_mat(b):
        return jnp.einsum('jgpc,gh->jgchp', b.reshape(2, gh, S5_STATE, S5_GROUP), eye).reshape(
            2, S5_WIDTH // 2, S5_LANES // 2).astype(BF16)

    def out_mat(cm):
        return jnp.einsum('jgcp,gh->jgphc', cm.reshape(2, gh, S5_GROUP, S5_STATE), eye).reshape(
            2, S5_LANES // 2, S5_WIDTH // 2).astype(BF16)

    def powers(pr, pi, n):
        tr, ti, cnt = pr, pi, 1
        while cnt < n:
            lr_, li_ = tr[cnt - 1:cnt], ti[cnt - 1:cnt]
            nr, ni = _cmul(tr, ti, lr_, li_)
            tr, ti, cnt = jnp.concatenate([tr, nr], axis=0), jnp.concatenate([ti, ni], axis=0), 2 * cnt
        return tr, ti

    sub = jnp.arange(SUBLANES)[:, None]
    tabs = {}
    for R in seg_lens:
        kr, ki = powers(lr, li, R)
        sr, si = powers(kr[R - 1:R], ki[R - 1:R], 4)
        pseg = []
        for lag in (1, 2, 4):
            msk = (sub >= lag).astype(F32)
            pseg += [msk * sr[lag - 1:lag], msk * si[lag - 1:lag]]
        tabs[R] = dict(lam=jnp.concatenate([lr, li], axis=0), pseg=jnp.stack(pseg), pk=jnp.stack([kr, ki]))
    return dict(bre=in_mat(bbr), bim=in_mat(bbi), cre=out_mat(c_re), cim=out_mat(-c_im), tabs=tabs,
                d=d_skip.reshape(1, S5_WIDTH), wglu=w_glu.astype(BF16), bglu=b_glu.reshape(1, S5_WIDTH))


def _fox_flash_kernel(qx_ref, kx_ref, v_ref, o_ref, m_ref, l_ref, acc_ref, s_ref, p_ref):
    i = pl.program_id(1)
    j = pl.program_id(2)
    tq = qx_ref.shape[0]
    tk = kx_ref.shape[0]

    @pl.when(j == 0)
    def _():
        m_ref[...] = jnp.full_like(m_ref, -jnp.inf)
        l_ref[...] = jnp.zeros_like(l_ref)
        acc_ref[...] = jnp.zeros_like(acc_ref)

    def step(masked):
        nr, ncol = tq // FLASH_ROWS, tk // LANES
        if masked:
            diff = (lax.broadcasted_iota(jnp.int32, (FLASH_ROWS, LANES), 0) -
                    lax.broadcasted_iota(jnp.int32, (FLASH_ROWS, LANES), 1))
        for h in range(FOX_HEADS):
            hs = slice(h * LANES, (h + 1) * LANES)
            ps = slice((h // 2) * LANES, (h // 2 + 1) * LANES)
            s_buf, p_buf = s_ref.at[h % 2], p_ref.at[h % 2]
            if h == 0:
                s_buf[...] = _dot_nt(qx_ref[:, hs], kx_ref[:, hs])
            if h + 1 < FOX_HEADS:
                nhs = slice((h + 1) * LANES, (h + 2) * LANES)
                s_ref[(h + 1) % 2] = _dot_nt(qx_ref[:, nhs], kx_ref[:, nhs])
            for r in range(nr):
                rs = slice(r * FLASH_ROWS, (r + 1) * FLASH_ROWS)

                live = [cidx for cidx in range(ncol) if not (masked and cidx * LANES >= (r + 1) * FLASH_ROWS)]

                def piece(cidx):
                    sc = s_buf[rs, cidx * LANES:(cidx + 1) * LANES]
                    if masked and (cidx + 1) * LANES - 1 > r * FLASH_ROWS:
                        sc = jnp.where(diff >= (cidx * LANES - r * FLASH_ROWS), sc, -jnp.inf)
                    return sc

                mx = piece(live[0])
                for cidx in live[1:]:
                    mx = jnp.maximum(mx, piece(cidx))
                m_prev = m_ref[h, rs, :]
                m_new = jnp.maximum(m_prev, jnp.max(mx, axis=-1, keepdims=True))
                alpha = jnp.exp(m_prev - m_new)
                lsum = jnp.zeros((FLASH_ROWS, LANES), F32)
                for cidx in range(ncol):
                    cs_ = slice(cidx * LANES, (cidx + 1) * LANES)
                    if cidx not in live:
                        p_buf[rs, cs_] = jnp.zeros((FLASH_ROWS, LANES), BF16)
                        continue
                    pc = jnp.exp(piece(cidx) - m_new)
                    lsum = lsum + pc
                    p_buf[rs, cs_] = pc.astype(BF16)
                l_ref[h, rs, :] = alpha * l_ref[h, rs, :] + lsum
                acc_ref[h, rs, :] = alpha * acc_ref[h, rs, :]
                m_ref[h, rs, :] = m_new
            acc_ref[h] += _dot(p_buf[...], v_ref[:, ps])

    q_lo = i * tq
    q_hi = i * tq + tq - 1
    k_lo = j * tk
    k_hi = j * tk + tk - 1

    @pl.when(k_hi <= q_lo)
    def _():
        step(False)

    @pl.when((k_lo <= q_hi) & (k_hi > q_lo))
    def _():
        step(True)

    @pl.when(j == pl.num_programs(2) - 1)
    def _():
        lane = lax.broadcasted_iota(jnp.int32, (1, LANES), 1)
        for p in range(FOX_HEADS // 2):
            lo = acc_ref[2 * p] / jnp.sum(l_ref[2 * p], axis=-1, keepdims=True)
            hi = acc_ref[2 * p + 1] / jnp.sum(l_ref[2 * p + 1], axis=-1, keepdims=True)
            o_ref[:, p * LANES:(p + 1) * LANES] = jnp.where(lane < FOX_HD, lo, hi)


def fox_flash(qx, kx, vb, nb, t, tq, tk):
    assert tq == tk and tq % FLASH_ROWS == 0
    nq, nk = t // tq, t // tk

    def kv_idx(i, j):
        return jnp.minimum(j, (i * tq + tq - 1) // tk)

    return pl.pallas_call(
        _fox_flash_kernel,
        out_shape=jax.ShapeDtypeStruct((nb * t, FOX_WIDTH), F32),
        grid=(nb, nq, nk),
        in_specs=[pl.BlockSpec((tq, FOX_HEADS * LANES), lambda b, i, j: (b * nq + i, 0)),
                  pl.BlockSpec((tk, FOX_HEADS * LANES), lambda b, i, j: (b * nk + kv_idx(i, j), 0)),
                  pl.BlockSpec((tk, FOX_WIDTH), lambda b, i, j: (b * nk + kv_idx(i, j), 0))],
        out_specs=pl.BlockSpec((tq, FOX_WIDTH), lambda b, i, j: (b * nq + i, 0)),
        scratch_shapes=[pltpu.VMEM((FOX_HEADS, tq, LANES), F32), pltpu.VMEM((FOX_HEADS, tq, LANES), F32),
                        pltpu.VMEM((FOX_HEADS, tq, LANES), F32),
                        pltpu.VMEM((2, tq, tk), F32), pltpu.VMEM((2, tq, tk), BF16)],
        compiler_params=_cparams("parallel", "parallel", "arbitrary"),
        name="fox_flash",
    )(qx, kx, vb)


def _fox_decode_kernel(pt_ref, qb_ref, kn_ref, vn_ref, csrow_ref, *rest, n_pages):
    k_refs = rest[0:n_pages]
    v_refs = rest[n_pages:2 * n_pages]
    f_refs = rest[2 * n_pages:3 * n_pages]
    triu_ref, o_ref, kpad_ref, vpad_ref = rest[3 * n_pages:]
    page = kpad_ref.shape[0]
    nrow = FOX_HEADS * SAMPLE_T
    qb = qb_ref[...].astype(BF16)

    def per_query(x):
        return jnp.concatenate([x] * SAMPLE_T, axis=0)

    fcat = jnp.concatenate([f_refs[i][0, 0] for i in range(n_pages)], axis=0)
    cum_in = _dot_exact_rhs(fcat, triu_ref[...])
    totals = jnp.broadcast_to(cum_in[:, page - 1:page], cum_in.shape)
    off = jnp.zeros((FOX_HEADS, page), F32)
    ss = []
    for i in range(n_pages):
        rows = slice(i * FOX_HEADS, (i + 1) * FOX_HEADS)
        kt = k_refs[i][0, 0].reshape(FOX_WIDTH, page).astype(BF16)
        ss.append(_dot(qb, kt) - per_query(cum_in[rows] + off))
        off = off + totals[rows]
    kpad_ref[...] = jnp.zeros_like(kpad_ref)
    vpad_ref[...] = jnp.zeros_like(vpad_ref)
    kpad_ref[0:SAMPLE_T, :] = kn_ref[...]
    vpad_ref[0:SAMPLE_T, :] = vn_ref[...]
    s_new = _dot_nt(qb, kpad_ref[...].astype(BF16)) - per_query(csrow_ref[...] + off)
    tq_idx = lax.broadcasted_iota(jnp.int32, (nrow, page), 0) // FOX_HEADS
    tk_idx = lax.broadcasted_iota(jnp.int32, (nrow, page), 1)
    ss.append(jnp.where(tk_idx <= tq_idx, s_new, -jnp.inf))

    m = ss[0]
    for s in ss[1:]:
        m = jnp.maximum(m, s)
    m = jnp.max(m, axis=-1, keepdims=True)
    lsum = jnp.zeros((nrow, page), F32)
    acc = jnp.zeros((nrow, FOX_WIDTH), F32)
    for i, s in enumerate(ss):
        p = jnp.exp(s - m)
        lsum = lsum + p
        if i < n_pages:
            acc = acc + _dot_nt(p.astype(BF16), v_refs[i][0, 0].reshape(FOX_WIDTH, page).astype(BF16))
        else:
            acc = acc + _dot(p.astype(BF16), vpad_ref[...].astype(BF16))
    acc = acc / jnp.sum(lsum, axis=-1, keepdims=True)
    lane = lax.broadcasted_iota(jnp.int32, (nrow, FOX_WIDTH), 1) // FOX_HD
    head = lax.broadcasted_iota(jnp.int32, (nrow, FOX_WIDTH), 0) % FOX_HEADS
    picked = jnp.where(lane == head, acc, 0.0)
    o_ref[...] = jnp.sum(picked.reshape(SAMPLE_T, FOX_HEADS, FOX_WIDTH), axis=1)


def fox_decode(page_table, qx, kn, z, csrow, pool_kt, pool_vt, pool_ft, layer):
    nb, n_pages = page_table.shape
    page = pool_kt.shape[-1]
    assert page == LANES
    nrow = FOX_HEADS * SAMPLE_T
    pt = page_table.reshape(-1)
    triu = jnp.asarray(np.triu(np.ones((page, page))), BF16)
    pair_of_head = jnp.asarray(np.arange(FOX_HEADS)[:, None] // 2 == np.arange(FOX_HEADS // 2)[None, :], F32)
    qb = (qx.reshape(nb, SAMPLE_T, FOX_HEADS, 1, LANES) * pair_of_head[None, None, :, :, None]).reshape(
        nb * nrow, FOX_WIDTH)

    def pg5(i):
        return lambda b, pt: (layer, pt[b * n_pages + i], 0, 0, 0)

    def pg4(i):
        return lambda b, pt: (layer, pt[b * n_pages + i], 0, 0)

    row = lambda b, pt: (b, 0)
    in_specs = [pl.BlockSpec((nrow, FOX_WIDTH), row),
                pl.BlockSpec((SAMPLE_T, FOX_WIDTH), row),
                pl.BlockSpec((SAMPLE_T, FOX_WIDTH), lambda b, pt: (b, COL_FV)),
                pl.BlockSpec((FOX_HEADS, page), lambda b, pt: (b * (LANES // FOX_HEADS), 0))]
    in_specs += [pl.BlockSpec((1, 1, FOX_HEADS, FOX_HD, page), pg5(i)) for i in range(n_pages)]
    in_specs += [pl.BlockSpec((1, 1, FOX_HEADS, FOX_HD, page), pg5(i)) for i in range(n_pages)]
    in_specs += [pl.BlockSpec((1, 1, FOX_HEADS, page), pg4(i)) for i in range(n_pages)]
    in_specs += [pl.BlockSpec((page, page), lambda b, pt: (0, 0))]
    return pl.pallas_call(
        functools.partial(_fox_decode_kernel, n_pages=n_pages),
        out_shape=jax.ShapeDtypeStruct((nb * SAMPLE_T, FOX_WIDTH), F32),
        grid_spec=pltpu.PrefetchScalarGridSpec(
            num_scalar_prefetch=1,
            grid=(nb,),
            in_specs=in_specs,
            out_specs=pl.BlockSpec((SAMPLE_T, FOX_WIDTH), row),
            scratch_shapes=[pltpu.VMEM((page, FOX_WIDTH), F32), pltpu.VMEM((page, FOX_WIDTH), F32)]),
        compiler_params=_cparams("parallel"),
        name="fox_decode",
    )(pt, qb, kn, z, csrow, *([pool_kt] * n_pages), *([pool_vt] * n_pages), *([pool_ft] * n_pages), triu)


def _mlstm_kernel(q_ref, k_ref, v_ref, o_ref, g_ref, cs_ref, grow_ref, csrow_ref, c0_ref, n0_ref, m0_ref, gn_ref,
                  y_ref, cout_ref, nout_ref, mout_ref, c_ref, n_ref, m_ref):
    c = pl.program_id(1)
    nc = pl.num_programs(1)
    L = q_ref.shape[0]

    @pl.when(c == 0)
    def _():
        c_ref[...] = c0_ref[0]
        n_ref[...] = n0_ref[0]
        m_ref[...] = m0_ref[0]

    Lk = grow_ref.shape[1]
    Lq = max(L, MIN_BF16_ROWS)
    g = g_ref[...]
    cs = cs_ref[...]
    grow = grow_ref[...]
    csrow = csrow_ref[...]
    causal = (lax.broadcasted_iota(jnp.int32, (Lq, Lk), 0) >= lax.broadcasted_iota(jnp.int32, (Lq, Lk), 1))
    for h in range(ML_HEADS):
        hs = slice(h * ML_HD, (h + 1) * ML_HD)
        scale = ML_HD ** -0.5
        qb = _pad_rows(q_ref[:, hs], Lq).astype(BF16)
        kb = _pad_rows(k_ref[:, hs], Lk).astype(BF16)
        vb = _pad_rows(v_ref[:, hs], Lk).astype(BF16)
        qh, kh, vh = qb.astype(F32), kb.astype(F32), vb.astype(F32)
        bcol_k = _pad_rows(cs[:, LANE_MLF + h:LANE_MLF + h + 1], Lk)
        bcol = bcol_k[0:Lq]
        icol = _pad_rows(g[:, LANE_MLI + h:LANE_MLI + h + 1], Lk, -jnp.inf)
        brow = csrow[LANE_MLF + h:LANE_MLF + h + 1, :]
        irow = grow[LANE_MLI + h:LANE_MLI + h + 1, :]
        m_prev = m_ref[0:1, h:h + 1]
        log_d = jnp.where(causal, bcol - brow + irow, -jnp.inf)
        log_inter = bcol + m_prev
        m_t = jnp.maximum(log_inter, jnp.max(log_d, axis=-1, keepdims=True))
        d_w = jnp.exp(log_d - m_t)
        inter_w = jnp.exp(log_inter - m_t)
        s = _dot_nt(qb, kb) * (d_w * scale)
        ch = c_ref[h]
        n_row = n_ref[h:h + 1, :]
        num = _dot(s.astype(BF16), vb) + inter_w * _dot_nt(qb, ch.astype(BF16))
        den = jnp.sum(s, axis=-1, keepdims=True) + inter_w * jnp.sum(qh * n_row, axis=-1, keepdims=True)
        hh = num / jnp.maximum(jnp.abs(den), jnp.exp(-m_t))
        y_ref[:, hs] = _rms(hh[0:L], gn_ref[...]) * _sigmoid(o_ref[:, hs])
        m_end = m_t[L - 1:L, :]
        a_end = inter_w[L - 1:L, :]
        w_col = jnp.exp(bcol[L - 1:L, :] - bcol_k + icol - m_end) * scale
        c_ref[h] = a_end * ch + _dot_tn((vh * w_col).astype(BF16), kb)
        n_ref[h:h + 1, :] = a_end * n_row + jnp.sum(kh * w_col, axis=0, keepdims=True)
        m_ref[0:1, h:h + 1] = m_end

    @pl.when(c == nc - 1)
    def _():
        cout_ref[0] = c_ref[...]
        nout_ref[0] = n_ref[...]
        mout_ref[0] = m_ref[...]


def mlstm(za, zb, g, cs, grow, csrow, c0, n0, m0, gn, nb, t, L):
    nc = t // L
    rows = lambda b, c: (b * nc + c, 0)
    rr = 2 * SUBLANES
    m0p = jnp.zeros((nb, 1, LANES), F32).at[:, 0, :ML_HEADS].set(m0)
    outs = pl.pallas_call(
        _mlstm_kernel,
        out_shape=(jax.ShapeDtypeStruct((nb * t, ML_WIDTH), F32),
                   jax.ShapeDtypeStruct((nb, ML_HEADS, ML_HD, ML_HD), F32),
                   jax.ShapeDtypeStruct((nb, ML_HEADS, ML_HD), F32),
                   jax.ShapeDtypeStruct((nb, 1, LANES), F32)),
        grid=(nb, nc),
        in_specs=[pl.BlockSpec((L, 512), lambda b, c: (b * nc + c, COL_MQ)),
                  pl.BlockSpec((L, 512), lambda b, c: (b * nc + c, COL_MK)),
                  pl.BlockSpec((L, 512), lambda b, c: (b * nc + c, COL_MV)),
                  pl.BlockSpec((L, 512), lambda b, c: (b * nc + c, COL_MO)),
                  pl.BlockSpec((L, LANES), rows), pl.BlockSpec((L, LANES), rows),
                  pl.BlockSpec((rr, max(L, LANES)), lambda b, c: (b * (LANES // rr), c)),
                  pl.BlockSpec((rr, max(L, LANES)), lambda b, c: (b * (LANES // rr), c)),
                  pl.BlockSpec((1, ML_HEADS, ML_HD, ML_HD), lambda b, c: (b, 0, 0, 0)),
                  pl.BlockSpec((1, ML_HEADS, ML_HD), lambda b, c: (b, 0, 0)),
                  pl.BlockSpec((1, 1, LANES), lambda b, c: (b, 0, 0)),
                  pl.BlockSpec((1, ML_HD), lambda b, c: (0, 0))],
        out_specs=(pl.BlockSpec((L, ML_WIDTH), rows),
                   pl.BlockSpec((1, ML_HEADS, ML_HD, ML_HD), lambda b, c: (b, 0, 0, 0)),
                   pl.BlockSpec((1, ML_HEADS, ML_HD), lambda b, c: (b, 0, 0)),
                   pl.BlockSpec((1, 1, LANES), lambda b, c: (b, 0, 0))),
        scratch_shapes=[pltpu.VMEM((ML_HEADS, ML_HD, ML_HD), F32), pltpu.VMEM((ML_HEADS, ML_HD), F32),
                        pltpu.VMEM((1, LANES), F32)],
        compiler_params=_cparams("parallel", "arbitrary"),
        name="mlstm",
    )(zb, zb, zb, za, g, cs, grow, csrow, c0, n0, m0p, gn.reshape(1, ML_HD))
    y, c_new, n_new, m_new = outs
    return y, c_new, n_new, m_new[:, 0, :ML_HEADS]


def _merge_kernel(x_ref, g_ref, wg_ref, ys_ref, yf_ref, ym_ref, ws_ref, wf_ref, wm_ref, wo_ref, o_ref):
    x = x_ref[...]
    hn = _rms(x, g_ref[...]).astype(BF16)
    branches = (ys_ref, ws_ref), (yf_ref, wf_ref), (ym_ref, wm_ref)
    merged = None
    for b, (y_ref, w_ref) in enumerate(branches):
        gate = _sigmoid(_dot(hn, wg_ref[:, b * D_MODEL:(b + 1) * D_MODEL]))
        term = gate * _dot(y_ref[...].astype(BF16), w_ref[...])
        merged = term if merged is None else merged + term
    o_ref[...] = x + _dot(merged.astype(BF16), wo_ref[...])


def merge_out(x, g_mix, w_gates, ys, yf, ym, ws, wf, wm, wo, tm=256):
    m = x.shape[0]
    tm = min(tm, m)
    row = lambda i: (i, 0)
    const = lambda i: (0, 0)
    return pl.pallas_call(
        _merge_kernel,
        out_shape=jax.ShapeDtypeStruct((m, D_MODEL), F32),
        grid=(m // tm,),
        in_specs=[pl.BlockSpec((tm, D_MODEL), row), pl.BlockSpec((1, D_MODEL), const),
                  pl.BlockSpec((D_MODEL, 3 * D_MODEL), const),
                  pl.BlockSpec((tm, 512), row), pl.BlockSpec((tm, 512), row), pl.BlockSpec((tm, 512), row),
                  pl.BlockSpec((512, D_MODEL), const), pl.BlockSpec((512, D_MODEL), const),
                  pl.BlockSpec((512, D_MODEL), const), pl.BlockSpec((D_MODEL, D_MODEL), const)],
        out_specs=pl.BlockSpec((tm, D_MODEL), row),
        compiler_params=_cparams("parallel"),
        name="merge_out",
    )(x, g_mix.reshape(1, D_MODEL), w_gates, ys, yf, ym, ws, wf, wm, wo)


def _cross_kernel(q_ref, k_ref, v_ref, o_ref, *, cached):
    tq = q_ref.shape[0]
    for h in range(MEM_HEADS):
        hs = slice(h * MEM_HD, (h + 1) * MEM_HD)
        qh = _pad_rows(q_ref[:, hs], MIN_BF16_ROWS).astype(BF16)
        kh = k_ref[0, 0, :, h, :] if cached else k_ref[:, hs]
        vh = v_ref[0, 0, :, h, :] if cached else v_ref[:, hs]
        s = _dot_nt(qh, kh.astype(BF16)) * (MEM_HD ** -0.5)
        m = jnp.max(s, axis=-1, keepdims=True)
        p = jnp.exp(s - m)
        l = jnp.sum(p, axis=-1, keepdims=True)
        o_ref[:, hs] = (_dot(p.astype(BF16), vh.astype(BF16)) / l)[0:tq]


def cross_attend(q, mem_k, mem_v, nb, t, tq, layer=None):
    nq = t // tq
    cached = layer is not None
    if cached:
        n_mem = mem_k.shape[2]
        kv_spec = pl.BlockSpec((1, 1, n_mem, MEM_HEADS, MEM_HD), lambda b, i: (layer, b, 0, 0, 0))
    else:
        n_mem = mem_k.shape[0] // nb
        kv_spec = pl.BlockSpec((n_mem, MEM_WIDTH), lambda b, i: (b, 0))
    return pl.pallas_call(
        functools.partial(_cross_kernel, cached=cached),
        out_shape=jax.ShapeDtypeStruct((nb * t, MEM_WIDTH), F32),
        grid=(nb, nq),
        in_specs=[pl.BlockSpec((tq, MEM_WIDTH), lambda b, i: (b * nq + i, 0)), kv_spec, kv_spec],
        out_specs=pl.BlockSpec((tq, MEM_WIDTH), lambda b, i: (b * nq + i, 0)),
        compiler_params=_cparams("parallel", "parallel"),
        name="cross_attend",
    )(q, mem_k, mem_v)


def _proj_residual_kernel(x_ref, a_ref, w_ref, o_ref):
    o_ref[...] = x_ref[...] + _dot(a_ref[...].astype(BF16), w_ref[...])


def proj_residual(x, a, w, tm=512):
    m, n = x.shape
    k = a.shape[1]
    tm = min(tm, m)
    return pl.pallas_call(
        _proj_residual_kernel,
        out_shape=jax.ShapeDtypeStruct((m, n), F32),
        grid=(m // tm,),
        in_specs=[pl.BlockSpec((tm, n), lambda i: (i, 0)), pl.BlockSpec((tm, k), lambda i: (i, 0)),
                  pl.BlockSpec((k, n), lambda i: (0, 0))],
        out_specs=pl.BlockSpec((tm, n), lambda i: (i, 0)),
        compiler_params=_cparams("parallel"),
        name="proj_residual",
    )(x, a, w)


def _mlp_kernel(x_ref, g_ref, wu_ref, wd_ref, o_ref, hn_ref, acc_ref):
    f = pl.program_id(1)

    @pl.when(f == 0)
    def _():
        hn_ref[...] = _rms(x_ref[...], g_ref[...]).astype(BF16)
        acc_ref[...] = jnp.zeros_like(acc_ref)

    a = jnp.maximum(_dot(hn_ref[...], wu_ref[...]), 0.0)
    acc_ref[...] += _dot((a * a).astype(BF16), wd_ref[...])

    @pl.when(f == pl.num_programs(1) - 1)
    def _():
        o_ref[...] = x_ref[...] + acc_ref[...]


def mlp(x, g, wu, wd, tm=1024, tf=512):
    m, d = x.shape
    dff = wu.shape[1]
    tm = min(tm, m)
    return pl.pallas_call(
        _mlp_kernel,
        out_shape=jax.ShapeDtypeStruct((m, d), F32),
        grid=(m // tm, dff // tf),
        in_specs=[pl.BlockSpec((tm, d), lambda i, f: (i, 0)), pl.BlockSpec((1, d), lambda i, f: (0, 0)),
                  pl.BlockSpec((d, tf), lambda i, f: (0, f)), pl.BlockSpec((tf, d), lambda i, f: (f, 0))],
        out_specs=pl.BlockSpec((tm, d), lambda i, f: (i, 0)),
        scratch_shapes=[pltpu.VMEM((tm, d), BF16), pltpu.VMEM((tm, d), F32)],
        compiler_params=_cparams("parallel", "arbitrary"),
        name="mlp",
    )(x, g.reshape(1, d), wu, wd)


def _pack_w_in(w_in):
    offs = np.concatenate([[0], np.cumsum(SPLITS)])
    col = lambda i: w_in[:, int(offs[i]):int(offs[i + 1])]
    s5, fq, fk, fv, ff, mq, mk, mv, mi, mf, mo, gates = [col(i) for i in range(12)]
    pad = jnp.zeros((w_in.shape[0], 512 - FOX_HEADS - 2 * ML_HEADS), w_in.dtype)
    w_a = jnp.concatenate([s5, fq, fk, fv, mo, ff, mi, mf, pad], axis=1).astype(BF16)
    w_b = jnp.concatenate([mq, mk, mv], axis=1).astype(BF16)
    return w_a, w_b, gates.astype(BF16)


def _layer_weights(l, g_mix, w_in, s5_a_re, s5_a_im, s5_log_step, s5_b_re, s5_b_im, s5_c_re, s5_c_im, s5_d,
                   s5_w_glu, s5_b_glu, fox_gq, fox_gk, fox_bf, ml_bi, ml_bf, ml_gn, w_br_s5, w_br_fox, w_br_ml,
                   w_out, g_cross, w_cq, cross_gq, g_mem, w_mk, w_mv, cross_gk, w_co, g_mlp, w_up, w_down):
    bias_row = jnp.zeros((1, LANES), F32)
    bias_row = bias_row.at[0, LANE_FOXF:LANE_FOXF + FOX_HEADS].set(fox_bf[l])
    bias_row = bias_row.at[0, LANE_MLI:LANE_MLI + ML_HEADS].set(ml_bi[l])
    bias_row = bias_row.at[0, LANE_MLF:LANE_MLF + ML_HEADS].set(ml_bf[l])
    w_a, w_b, w_gates = _pack_w_in(w_in[l])
    return dict(
        g_mix=g_mix[l], w_a=w_a, w_b=w_b, w_gates=w_gates,
        s5=s5_params(s5_a_re[l], s5_a_im[l], s5_log_step[l], s5_b_re[l], s5_b_im[l], s5_c_re[l], s5_c_im[l],
                     s5_d[l], s5_w_glu[l], s5_b_glu[l], (SEQ_TILE // SUBLANES, SAMPLE_T // SUBLANES)),
        gq=jnp.tile(fox_gq[l], FOX_HEADS).reshape(1, FOX_WIDTH),
        gk=jnp.tile(fox_gk[l], FOX_HEADS).reshape(1, FOX_WIDTH),
        bias_row=bias_row, ml_gn=ml_gn[l],
        w_br_s5=w_br_s5[l].astype(BF16), w_br_fox=w_br_fox[l].astype(BF16), w_br_ml=w_br_ml[l].astype(BF16),
        w_out=w_out[l].astype(BF16), g_cross=g_cross[l], w_cq=w_cq[l].astype(BF16), cross_gq=cross_gq[l],
        g_mem=g_mem[l], w_mk=w_mk[l].astype(BF16), w_mv=w_mv[l].astype(BF16), cross_gk=cross_gk[l],
        w_co=w_co[l].astype(BF16), g_mlp=g_mlp[l], w_up=w_up[l].astype(BF16), w_down=w_down[l].astype(BF16))


def _hybrid_layer(x, W, nb, t, seq_tile, t_valid, s5_state, ml_state, mem_k, mem_v, fox_attend, augment,
                  mem_layer=None):
    z = norm_matmul(x, W["g_mix"], W["w_a"], tm=min(1024, nb * t))
    zb = norm_matmul(x, W["g_mix"], W["w_b"], tm=min(1024, nb * t),
                     out_dtype=BF16 if seq_tile % MIN_BF16_ROWS == 0 else F32)
    prep = gate_prep(z, W["gq"], W["gk"], W["bias_row"], nb, t, seq_tile, t_valid, augment)
    kn, g, cs, grow, csrow = prep[1:6]
    y_s5, s5_re, s5_im = s5_mixer(z, s5_state[0], s5_state[1], W["s5"], nb, t, seq_tile, t_valid - 1)
    y_fox = fox_attend(prep, z)
    y_ml, c_new, n_new, m_new = mlstm(z, zb, g, cs, grow, csrow, ml_state[0], ml_state[1], ml_state[2], W["ml_gn"],
                                      nb, t, seq_tile)
    x = merge_out(x, W["g_mix"], W["w_gates"], y_s5, y_fox, y_ml, W["w_br_s5"], W["w_br_fox"], W["w_br_ml"],
                  W["w_out"])
    qc = norm_matmul(x, W["g_cross"], W["w_cq"], head_gain=W["cross_gq"])
    oc = cross_attend(qc, mem_k, mem_v, nb, t, seq_tile, mem_layer)
    x = proj_residual(x, oc, W["w_co"])
    x = mlp(x, W["g_mlp"], W["w_up"], W["w_down"])
    return x, z, kn, g, s5_re.reshape(nb, S5_GROUPS, S5_STATE), s5_im.reshape(nb, S5_GROUPS, S5_STATE), \
        c_new, n_new, m_new


def kernel(x_prompt, x_sample, mem_prompt, cache_fox_k, cache_fox_v, cache_fox_logf, page_table, state_s5_re, state_s5_im, state_mlstm_C, state_mlstm_n, state_mlstm_m, cache_mem_k, cache_mem_v, g_mix, w_in, s5_a_re, s5_a_im, s5_log_step, s5_b_re, s5_b_im, s5_c_re, s5_c_im, s5_d, s5_w_glu, s5_b_glu, fox_gq, fox_gk, fox_bf, ml_bi, ml_bf, ml_gn, w_br_s5, w_br_fox, w_br_ml, w_out, g_cross, w_cq, cross_gq, g_mem, w_mk, w_mv, cross_gk, w_co, g_mlp, w_up, w_down):
    depth = w_in.shape[0]
    bp, tp, _ = x_prompt.shape
    bs, ts, _ = x_sample.shape
    n_mem = mem_prompt.shape[1]

    xp = x_prompt.reshape(bp * tp, D_MODEL)
    xs = jnp.pad(x_sample, ((0, 0), (0, SAMPLE_T - ts), (0, 0))).reshape(bs * SAMPLE_T, D_MODEL)
    mem = mem_prompt.reshape(bp * n_mem, D_MODEL)
    zeros_p = (jnp.zeros((bp, S5_LANES), F32), jnp.zeros((bp, S5_LANES), F32))
    zeros_ml = (jnp.zeros((bp, ML_HEADS, ML_HD, ML_HD), F32), jnp.zeros((bp, ML_HEADS, ML_HD), F32),
                jnp.zeros((bp, ML_HEADS), F32))
    pool_kt = jnp.transpose(cache_fox_k, (0, 1, 3, 4, 2))
    pool_vt = jnp.transpose(cache_fox_v, (0, 1, 3, 4, 2))
    pool_ft = jnp.transpose(cache_fox_logf, (0, 1, 3, 2))
    st_p, st_s = [], []
    for l in range(depth):
        W = _layer_weights(l, g_mix, w_in, s5_a_re, s5_a_im, s5_log_step, s5_b_re, s5_b_im, s5_c_re, s5_c_im, s5_d,
                           s5_w_glu, s5_b_glu, fox_gq, fox_gk, fox_bf, ml_bi, ml_bf, ml_gn, w_br_s5, w_br_fox,
                           w_br_ml, w_out, g_cross, w_cq, cross_gq, g_mem, w_mk, w_mv, cross_gk, w_co, g_mlp, w_up,
                           w_down)
        mk_p = norm_matmul(mem, W["g_mem"], W["w_mk"], head_gain=W["cross_gk"])
        mv_p = norm_matmul(mem, W["g_mem"], W["w_mv"])

        def flash(prep, z):
            return fox_flash(prep[0], prep[6], prep[7], bp, tp, FLASH_TILE, FLASH_TILE)

        xp, z, kn, g, s5r, s5i, c_new, n_new, m_new = _hybrid_layer(
            xp, W, bp, tp, SEQ_TILE, tp, zeros_p, zeros_ml, mk_p, mv_p, flash, True)
        st_p.append((kn.reshape(bp, tp, FOX_HEADS, FOX_HD),
                     z[:, COL_FV * 512:(COL_FV + 1) * 512].reshape(bp, tp, FOX_HEADS, FOX_HD),
                     g[:, LANE_FOXF:LANE_FOXF + FOX_HEADS].reshape(bp, tp, FOX_HEADS),
                     s5r, s5i, c_new, n_new, m_new,
                     mk_p.reshape(bp, n_mem, MEM_HEADS, MEM_HD), mv_p.reshape(bp, n_mem, MEM_HEADS, MEM_HD)))

        def decode(prep, z, layer=l):
            return fox_decode(page_table, prep[0], prep[1], z, prep[5], pool_kt, pool_vt, pool_ft, layer)

        xs, z, kn, g, s5r, s5i, c_new, n_new, m_new = _hybrid_layer(
            xs, W, bs, SAMPLE_T, SAMPLE_T, ts,
            (state_s5_re[l].reshape(bs, S5_LANES), state_s5_im[l].reshape(bs, S5_LANES)),
            (state_mlstm_C[l], state_mlstm_n[l], state_mlstm_m[l]),
            cache_mem_k, cache_mem_v, decode, False, mem_layer=l)
        st_s.append((kn.reshape(bs, SAMPLE_T, FOX_HEADS, FOX_HD)[:, :ts],
                     z[:, COL_FV * 512:(COL_FV + 1) * 512].reshape(bs, SAMPLE_T, FOX_HEADS, FOX_HD)[:, :ts],
                     g[:, LANE_FOXF:LANE_FOXF + FOX_HEADS].reshape(bs, SAMPLE_T, FOX_HEADS)[:, :ts],
                     s5r, s5i, c_new, n_new, m_new))
    outs_p = [jnp.stack(a) for a in zip(*st_p)]
    outs_s = [jnp.stack(a) for a in zip(*st_s)]
    yp = xp.reshape(bp, tp, D_MODEL)
    ys = xs.reshape(bs, SAMPLE_T, D_MODEL)[:, :ts]
    return (yp, ys, *outs_p, *outs_s)
```

```python
import functools
import math

import jax
import jax.numpy as jnp
import numpy as np
from jax import lax
from jax.experimental import pallas as pl
from jax.experimental.pallas import tpu as pltpu

F32 = jnp.float32
BF16 = jnp.bfloat16

LANES = 128
SUBLANES = 8
MIN_BF16_ROWS = 16
MXU_DIM = 256
VMEM_LIMIT_BYTES = 48 * 1024 * 1024

D_MODEL = 1024
S5_WIDTH = 512
S5_GROUP = 16
S5_GROUPS = 32
S5_STATE = 64
S5_LANES = S5_GROUPS * S5_STATE
FOX_HEADS = 8
FOX_HD = 64
FOX_WIDTH = 512
ML_HEADS = 4
ML_HD = 128
ML_WIDTH = 512
MEM_HEADS = 4
MEM_HD = 128
MEM_WIDTH = 512
D_FF = 4096
EPS = 1e-6
SPLITS = (S5_WIDTH, FOX_WIDTH, FOX_WIDTH, FOX_WIDTH, FOX_HEADS, ML_WIDTH, ML_WIDTH, ML_WIDTH,
          ML_HEADS, ML_HEADS, ML_WIDTH, 3 * D_MODEL)

COL_S5, COL_FQ, COL_FK, COL_FV, COL_MO, COL_SMALL = range(6)
COL_MQ, COL_MK, COL_MV = range(3)
LANE_FOXF = 0
LANE_MLI = 8
LANE_MLF = 12
SEQ_TILE = 256
FLASH_TILE = 512
FLASH_ROWS = 128
SAMPLE_T = 8


def _cparams(*sem):
    return pltpu.CompilerParams(dimension_semantics=sem, vmem_limit_bytes=VMEM_LIMIT_BYTES)


def _dot(a, b):
    return jnp.dot(a, b, preferred_element_type=F32)


def _dot_nt(a, b):
    return lax.dot_general(a, b, (((1,), (1,)), ((), ())), preferred_element_type=F32)


def _dot_tn(a, b):
    return lax.dot_general(a, b, (((0,), (0,)), ((), ())), preferred_element_type=F32)


def _split3(x):
    hi = x.astype(BF16)
    r1 = x - hi.astype(F32)
    mid = r1.astype(BF16)
    lo = (r1 - mid.astype(F32)).astype(BF16)
    return hi, mid, lo


def _dot_exact_rhs(x, ones_rhs):
    hi, mid, lo = _split3(x)
    return _dot(hi, ones_rhs) + _dot(mid, ones_rhs) + _dot(lo, ones_rhs)


def _dot_exact_lhs(ones_lhs, x):
    hi, mid, lo = _split3(x)
    return _dot(ones_lhs, hi) + _dot(ones_lhs, mid) + _dot(ones_lhs, lo)


def _pad_rows(x, n, fill=0.0):
    if x.shape[0] >= n:
        return x
    return jnp.concatenate([x, jnp.full((n - x.shape[0], x.shape[1]), fill, x.dtype)], axis=0)


def _lane_tile(x, n):
    return x if n == 1 else jnp.concatenate([x] * n, axis=1)


def _log_sigmoid(a):
    return jnp.minimum(a, 0.0) - jnp.log1p(jnp.exp(-jnp.abs(a)))


def _sigmoid(a):
    return 1.0 / (1.0 + jnp.exp(-a))


def _gelu_tanh(x):
    c = math.sqrt(2.0 / math.pi)
    return 0.5 * x * (1.0 + jnp.tanh(c * (x + 0.044715 * (x * x * x))))


def _rms(x, g):
    ms = jnp.mean(x * x, axis=-1, keepdims=True)
    return x * lax.rsqrt(ms + EPS) * g


def _norm_matmul_kernel(x_ref, g_ref, w_ref, *rest, head_norm):
    if head_norm:
        hg_ref, o_ref, hn_ref = rest
    else:
        o_ref, hn_ref = rest

    @pl.when(pl.program_id(1) == 0)
    def _():
        hn_ref[...] = _rms(x_ref[...], g_ref[...]).astype(BF16)

    y = _dot(hn_ref[...], w_ref[...])
    if head_norm:
        tn = y.shape[1]
        for s in range(tn // LANES):
            sl = slice(s * LANES, (s + 1) * LANES)
            o_ref[:, sl] = _rms(y[:, sl], hg_ref[...]).astype(o_ref.dtype)
    else:
        o_ref[...] = y.astype(o_ref.dtype)


def norm_matmul(x, g, w, head_gain=None, out_dtype=F32, tm=512, tn=512):
    m, k = x.shape
    n = w.shape[1]
    tm = min(tm, m)
    tn = min(tn, n)
    head_norm = head_gain is not None
    in_specs = [pl.BlockSpec((tm, k), lambda i, j: (i, 0)),
                pl.BlockSpec((1, k), lambda i, j: (0, 0)),
                pl.BlockSpec((k, tn), lambda i, j: (0, j))]
    args = [x, g.reshape(1, k), w]
    if head_norm:
        in_specs.append(pl.BlockSpec((1, LANES), lambda i, j: (0, 0)))
        args.append(head_gain.reshape(1, LANES))
    return pl.pallas_call(
        functools.partial(_norm_matmul_kernel, head_norm=head_norm),
        out_shape=jax.ShapeDtypeStruct((m, n), out_dtype),
        grid=(m // tm, n // tn),
        in_specs=in_specs,
        out_specs=pl.BlockSpec((tm, tn), lambda i, j: (i, j)),
        scratch_shapes=[pltpu.VMEM((tm, k), BF16)],
        compiler_params=_cparams("parallel", "arbitrary"),
        name="norm_matmul",
    )(*args)


def _prep_kernel(fq_ref, fk_ref, fv_ref, sm_ref, gq_ref, gk_ref, bias_ref, gmat_ref, tril_ref, sel_ref, aug_ref,
                 *rest, t_valid, augment):
    if augment:
        qx_ref, kn_ref, g_ref, cs_ref, grow_ref, csrow_ref, kx_ref, vb_ref, carry_ref = rest
    else:
        qx_ref, kn_ref, g_ref, cs_ref, grow_ref, csrow_ref, carry_ref = rest
    c = pl.program_id(1)
    tc = sm_ref.shape[0]
    tp = tril_ref.shape[0]

    @pl.when(c == 0)
    def _():
        carry_ref[...] = jnp.zeros_like(carry_ref)

    gmat = gmat_ref[...]

    def head_rms(x, gain):
        x2 = _pad_rows(x * x, MIN_BF16_ROWS)
        hi = x2.astype(BF16)
        lo = (x2 - hi.astype(F32)).astype(BF16)
        ss = ((_dot(hi, gmat) + _dot(lo, gmat)) * (1.0 / FOX_HD))[0:tc]
        return x * lax.rsqrt(ss + EPS) * gain

    qn = head_rms(fq_ref[...], gq_ref[...]) * (FOX_HD ** -0.5)
    kn = head_rms(fk_ref[...], gk_ref[...])
    kn_ref[...] = kn

    a = _pad_rows(sm_ref[...], tp) + bias_ref[...]
    lane = lax.broadcasted_iota(jnp.int32, a.shape, 1)
    row = lax.broadcasted_iota(jnp.int32, a.shape, 0) + c * tc
    is_i = (lane >= LANE_MLI) & (lane < LANE_MLF)
    used = lane < LANE_MLF + ML_HEADS
    valid = row < t_valid
    g = jnp.where(is_i, a, _log_sigmoid(a))
    g = jnp.where(used, g, 0.0)
    g = jnp.where(valid, g, jnp.where(is_i, -jnp.inf, 0.0))
    gc = jnp.where(is_i, 0.0, g)
    cs = _dot_exact_lhs(tril_ref[...], gc)
    carry = carry_ref[...]
    csg = cs + jnp.where(lane < FOX_HEADS, carry, 0.0)
    carry_ref[...] = carry + cs[tc - 1:tc, :]
    g_ref[...] = g[0:tc]
    cs_ref[...] = csg[0:tc]
    grow_ref[...] = g.T
    csrow_ref[...] = csg.T

    nx = FOX_HEADS * LANES
    lanex = lax.broadcasted_iota(jnp.int32, (1, nx), 1)
    keep = ((lanex % LANES) >= FOX_HD) == ((lanex // LANES) % 2 == 1)
    q_exp = jnp.concatenate([qn[:, (h // 2) * LANES:(h // 2 + 1) * LANES] for h in range(FOX_HEADS)], axis=1)
    if not augment:
        qx_ref[...] = jnp.where(keep, q_exp, 0.0).astype(qx_ref.dtype)
        return
    k_exp = jnp.concatenate([kn[:, (h // 2) * LANES:(h // 2 + 1) * LANES] for h in range(FOX_HEADS)], axis=1)
    hi, mid, lo = _split3(jnp.where(lane < FOX_HEADS, csg, 0.0)[0:tc])
    packed = (hi.astype(F32) + pltpu.roll(mid.astype(F32), FOX_HEADS, 1)
              + pltpu.roll(lo.astype(F32), 2 * FOX_HEADS, 1)).astype(BF16)
    aug = aug_ref[...] + _dot(packed, sel_ref[...])
    qx_ref[...] = jnp.where(keep, q_exp, aug[:, 0:nx]).astype(qx_ref.dtype)
    kx_ref[...] = jnp.where(keep, k_exp, aug[:, nx:2 * nx]).astype(kx_ref.dtype)
    vb_ref[...] = fv_ref[...].astype(vb_ref.dtype)


def gate_prep(z, gq, gk, bias_row, nb, t, tc, t_valid, augment):
    m = nb * t
    nc = t // tc
    gmat = jnp.asarray(np.kron(np.eye(FOX_HEADS), np.ones((FOX_HD, FOX_HD))), BF16)
    tp = max(tc, LANES)
    tril = jnp.asarray(np.tril(np.ones((tp, tp))), BF16)
    nx = FOX_HEADS * LANES
    sel = np.zeros((LANES, 2 * nx), np.float32)
    aug_const = np.zeros((1, 2 * nx), np.float32)
    for h in range(FOX_HEADS):
        o = h * LANES + (0 if h % 2 else FOX_HD)
        for p in range(3):
            sel[p * FOX_HEADS + h, o + p] = 1.0
            sel[p * FOX_HEADS + h, nx + o + 3 + p] = -1.0
            aug_const[0, o + 3 + p] = 1.0
            aug_const[0, nx + o + p] = 1.0
    sel, aug_const = jnp.asarray(sel, BF16), jnp.asarray(aug_const, F32)
    row_map = lambda b, c: (b * nc + c, 0)
    const = lambda b, c: (0, 0)
    out_shape = [jax.ShapeDtypeStruct((m, FOX_HEADS * LANES), BF16 if augment else F32),
                 jax.ShapeDtypeStruct((m, FOX_WIDTH), F32),
                 jax.ShapeDtypeStruct((m, LANES), F32),
                 jax.ShapeDtypeStruct((m, LANES), F32),
                 jax.ShapeDtypeStruct((nb * LANES, nc * tp), F32),
                 jax.ShapeDtypeStruct((nb * LANES, nc * tp), F32)]
    out_specs = [pl.BlockSpec((tc, FOX_HEADS * LANES), row_map),
                 pl.BlockSpec((tc, FOX_WIDTH), row_map),
                 pl.BlockSpec((tc, LANES), row_map),
                 pl.BlockSpec((tc, LANES), row_map),
                 pl.BlockSpec((LANES, tp), lambda b, c: (b, c)),
                 pl.BlockSpec((LANES, tp), lambda b, c: (b, c))]
    if augment:
        out_shape += [jax.ShapeDtypeStruct((m, FOX_HEADS * LANES), BF16), jax.ShapeDtypeStruct((m, FOX_WIDTH), BF16)]
        out_specs += [pl.BlockSpec((tc, FOX_HEADS * LANES), row_map), pl.BlockSpec((tc, FOX_WIDTH), row_map)]
    return pl.pallas_call(
        functools.partial(_prep_kernel, t_valid=t_valid, augment=augment),
        out_shape=tuple(out_shape),
        grid=(nb, nc),
        in_specs=[pl.BlockSpec((tc, 512), lambda b, c: (b * nc + c, COL_FQ)),
                  pl.BlockSpec((tc, 512), lambda b, c: (b * nc + c, COL_FK)),
                  pl.BlockSpec((tc, 512), lambda b, c: (b * nc + c, COL_FV)),
                  pl.BlockSpec((tc, LANES), lambda b, c: (b * nc + c, COL_SMALL * 4)),
                  pl.BlockSpec((1, 512), const), pl.BlockSpec((1, 512), const),
                  pl.BlockSpec((1, LANES), const),
                  pl.BlockSpec((512, 512), const), pl.BlockSpec((tp, tp), const),
                  pl.BlockSpec((LANES, 2 * nx), const), pl.BlockSpec((1, 2 * nx), const)],
        out_specs=tuple(out_specs),
        scratch_shapes=[pltpu.VMEM((1, LANES), F32)],
        compiler_params=_cparams("parallel", "arbitrary"),
        name="gate_prep",
    )(z, z, z, z, gq, gk, bias_row, gmat, tril, sel, aug_const)


S5_SCAN_LANES = 512
S5_SHORT_SEQS = 16


def _cmul(ar, ai, br, bi):
    return ar * br - ai * bi, ar * bi + ai * br


def _s5_kernel(u_ref, h0r_ref, h0i_ref, perm_ref, permt_ref, bre_ref, bim_ref, lam_ref, pseg_ref, pk_ref,
               cre_ref, cim_ref, d_ref, wglu_ref, bglu_ref, y_ref, hlr_ref, hli_ref, hr_ref, hi_ref, cr_ref, ci_ref,
               *, t_last, seqs):
    c = pl.program_id(1)
    nc = pl.num_programs(1)
    tc = u_ref.shape[0]
    R = tc // SUBLANES if seqs == 1 else 1
    half = S5_WIDTH // 2
    hl = S5_LANES // 2

    if seqs == 1:
        @pl.when(c == 0)
        def _():
            cr_ref[...] = h0r_ref[0]
            ci_ref[...] = h0i_ref[0]

    u = u_ref[...]
    ub = _pad_rows(u, MIN_BF16_ROWS).astype(BF16)
    if R > 1:
        ub = _dot(perm_ref[...], ub).astype(BF16)
    for j in range(2):
        uj = ub[:, j * half:(j + 1) * half]
        hr_ref[:, j * hl:(j + 1) * hl] = _dot(uj, bre_ref[j])[0:tc]
        hi_ref[:, j * hl:(j + 1) * hl] = _dot(uj, bim_ref[j])[0:tc]

    sub = lax.broadcasted_iota(jnp.int32, (SUBLANES, S5_SCAN_LANES), 0)
    for lc in range(S5_LANES // S5_SCAN_LANES):
        ls = slice(lc * S5_SCAN_LANES, (lc + 1) * S5_SCAN_LANES)
        lam_r = jnp.broadcast_to(lam_ref[0:1, ls], sub.shape)
        lam_i = jnp.broadcast_to(lam_ref[1:2, ls], sub.shape)
        if seqs > 1:
            def seq_scan(g, _):
                rows = pl.ds(pl.multiple_of(g * SUBLANES, SUBLANES), SUBLANES)
                s_r = jnp.where(sub == 0, jnp.broadcast_to(h0r_ref[0, pl.ds(g, 1), ls], sub.shape), 0.0)
                s_i = jnp.where(sub == 0, jnp.broadcast_to(h0i_ref[0, pl.ds(g, 1), ls], sub.shape), 0.0)
                m_r, m_i = _cmul(lam_r, lam_i, s_r, s_i)
                x_r = hr_ref[rows, ls] + m_r
                x_i = hi_ref[rows, ls] + m_i
                for j, lag in enumerate((1, 2, 4)):
                    m_r, m_i = _cmul(pseg_ref[2 * j, :, ls], pseg_ref[2 * j + 1, :, ls],
                                     pltpu.roll(x_r, lag, 0), pltpu.roll(x_i, lag, 0))
                    x_r, x_i = x_r + m_r, x_i + m_i
                hr_ref[rows, ls] = x_r
                hi_ref[rows, ls] = x_i
                hlr_ref[0, pl.ds(g, 1), ls] = x_r[t_last:t_last + 1, :]
                hli_ref[0, pl.ds(g, 1), ls] = x_i[t_last:t_last + 1, :]
                return 0

            lax.fori_loop(0, seqs, seq_scan, 0, unroll=2)
            continue
        init_r = jnp.where(sub == 0, jnp.broadcast_to(cr_ref[:, ls], sub.shape), 0.0)
        init_i = jnp.where(sub == 0, jnp.broadcast_to(ci_ref[:, ls], sub.shape), 0.0)

        def local_step(k, carry):
            h_r, h_i = carry
            rows = pl.ds(pl.multiple_of(k * SUBLANES, SUBLANES), SUBLANES)
            m_r, m_i = _cmul(lam_r, lam_i, h_r, h_i)
            h_r = m_r + hr_ref[rows, ls]
            h_i = m_i + hi_ref[rows, ls]
            hr_ref[rows, ls] = h_r
            hi_ref[rows, ls] = h_i
            return h_r, h_i

        e_r, e_i = lax.fori_loop(0, R, local_step, (init_r, init_i), unroll=min(R, 4))
        for j, lag in enumerate((1, 2, 4)):
            m_r, m_i = _cmul(pseg_ref[2 * j, :, ls], pseg_ref[2 * j + 1, :, ls],
                             pltpu.roll(e_r, lag, 0), pltpu.roll(e_i, lag, 0))
            e_r, e_i = e_r + m_r, e_i + m_i
        cr_ref[:, ls] = e_r[SUBLANES - 1:SUBLANES, :]
        ci_ref[:, ls] = e_i[SUBLANES - 1:SUBLANES, :]
        in_r = jnp.where(sub == 0, 0.0, pltpu.roll(e_r, 1, 0))
        in_i = jnp.where(sub == 0, 0.0, pltpu.roll(e_i, 1, 0))

        def fix_step(k, _):
            rows = pl.ds(pl.multiple_of(k * SUBLANES, SUBLANES), SUBLANES)
            p_r = jnp.broadcast_to(pk_ref[0, pl.ds(k, 1), ls], sub.shape)
            p_i = jnp.broadcast_to(pk_ref[1, pl.ds(k, 1), ls], sub.shape)
            m_r, m_i = _cmul(p_r, p_i, in_r, in_i)
            hr_ref[rows, ls] += m_r
            hi_ref[rows, ls] += m_i
            return 0

        lax.fori_loop(0, R, fix_step, 0, unroll=min(R, 4))

    if seqs == 1:
        @pl.when(c == nc - 1)
        def _():
            tl = t_last % tc
            pos = (tl % R) * SUBLANES + tl // R
            hlr_ref[0] = hr_ref[pos:pos + 1, :]
            hli_ref[0] = hi_ref[pos:pos + 1, :]

    hrb = _pad_rows(hr_ref[...], MIN_BF16_ROWS).astype(BF16)
    hib = _pad_rows(hi_ref[...], MIN_BF16_ROWS).astype(BF16)
    ys = [(_dot(hrb[:, j * hl:(j + 1) * hl], cre_ref[j]) + _dot(hib[:, j * hl:(j + 1) * hl], cim_ref[j]))
          for j in range(2)]
    y = jnp.concatenate(ys, axis=1)
    if R > 1:
        y = _dot_exact_lhs(permt_ref[...], y)
    y = _gelu_tanh(y[0:tc] + d_ref[...] * u)
    gate = _dot(_pad_rows(y, MIN_BF16_ROWS).astype(BF16), wglu_ref[...])[0:tc]
    y_ref[...] = y * _sigmoid(gate + bglu_ref[...])


def s5_mixer(z, h0r, h0i, sp, nb, t, tc, t_last, seqs=1):
    assert seqs == 1 or (tc == SUBLANES and t == tc)
    nc = t // tc
    R = tc // SUBLANES
    tc = tc * seqs
    nbg = nb // seqs
    tperm = max(tc, MIN_BF16_ROWS) if seqs == 1 else MIN_BF16_ROWS
    perm = np.zeros((tperm, tperm), np.float32)
    if seqs == 1:
        for s in range(SUBLANES):
            for k in range(R):
                perm[k * SUBLANES + s, s * R + k] = 1.0
    permt = jnp.asarray(perm.T, BF16)
    perm = jnp.asarray(perm, BF16)
    const = lambda b, c: (0, 0)
    const3 = lambda b, c: (0, 0, 0)
    state_spec = pl.BlockSpec((1, seqs, S5_LANES), lambda b, c: (b, 0, 0))
    tabs = sp["tabs"][R]
    return pl.pallas_call(
        functools.partial(_s5_kernel, t_last=t_last, seqs=seqs),
        out_shape=(jax.ShapeDtypeStruct((nb * t, S5_WIDTH), F32),
                   jax.ShapeDtypeStruct((nbg, seqs, S5_LANES), F32),
                   jax.ShapeDtypeStruct((nbg, seqs, S5_LANES), F32)),
        grid=(nbg, nc),
        in_specs=[pl.BlockSpec((tc, 512), lambda b, c: (b * nc + c, COL_S5)),
                  state_spec, state_spec,
                  pl.BlockSpec((tperm, tperm), const), pl.BlockSpec((tperm, tperm), const),
                  pl.BlockSpec((2, S5_WIDTH // 2, S5_LANES // 2), const3),
                  pl.BlockSpec((2, S5_WIDTH // 2, S5_LANES // 2), const3),
                  pl.BlockSpec((2, S5_LANES), const),
                  pl.BlockSpec((6, SUBLANES, S5_LANES), const3),
                  pl.BlockSpec((2, R, S5_LANES), const3),
                  pl.BlockSpec((2, S5_LANES // 2, S5_WIDTH // 2), const3),
                  pl.BlockSpec((2, S5_LANES // 2, S5_WIDTH // 2), const3),
                  pl.BlockSpec((1, S5_WIDTH), const),
                  pl.BlockSpec((S5_WIDTH, S5_WIDTH), const), pl.BlockSpec((1, S5_WIDTH), const)],
        out_specs=(pl.BlockSpec((tc, S5_WIDTH), lambda b, c: (b * nc + c, 0)), state_spec, state_spec),
        scratch_shapes=[pltpu.VMEM((tc, S5_LANES), F32), pltpu.VMEM((tc, S5_LANES), F32),
                        pltpu.VMEM((1, S5_LANES), F32), pltpu.VMEM((1, S5_LANES), F32)],
        compiler_params=_cparams("parallel", "arbitrary"),
        name="s5_mixer",
    )(z, h0r.reshape(nbg, seqs, S5_LANES), h0i.reshape(nbg, seqs, S5_LANES), perm, permt, sp["bre"], sp["bim"],
      tabs["lam"],
      tabs["pseg"], tabs["pk"], sp["cre"], sp["cim"], sp["d"], sp["wglu"], sp["bglu"])


def s5_params(a_re, a_im, log_step, b_re, b_im, c_re, c_im, d_skip, w_glu, b_glu, seg_lens):
    dt = jnp.exp(log_step)[:, None]
    mag = jnp.exp(a_re * dt)
    lr = (mag * jnp.cos(a_im * dt)).reshape(1, S5_LANES)
    li = (mag * jnp.sin(a_im * dt)).reshape(1, S5_LANES)
    den = a_re * a_re + a_im * a_im
    xr, xi = lr.reshape(a_re.shape) - 1.0, li.reshape(a_re.shape)
    fr = (xr * a_re + xi * a_im) / den
    fi = (xi * a_re - xr * a_im) / den
    bbr = fr[..., None] * b_re - fi[..., None] * b_im
    bbi = fr[..., None] * b_im + fi[..., None] * b_re
    gh = S5_GROUPS // 2
    eye = jnp.eye(gh, dtype=F32)

    def in_mat(b):
        return jnp.einsum('jgpc,gh->jgchp', b.reshape(2, gh, S5_STATE, S5_GROUP), eye).reshape(
            2, S5_WIDTH // 2, S5_LANES // 2).astype(BF16)

    def out_mat(cm):
        return jnp.einsum('jgcp,gh->jgphc', cm.reshape(2, gh, S5_GROUP, S5_STATE), eye).reshape(
            2, S5_LANES // 2, S5_WIDTH // 2).astype(BF16)

    def powers(pr, pi, n):
        tr, ti, cnt = pr, pi, 1
        while cnt < n:
            lr_, li_ = tr[cnt - 1:cnt], ti[cnt - 1:cnt]
            nr, ni = _cmul(tr, ti, lr_, li_)
            tr, ti, cnt = jnp.concatenate([tr, nr], axis=0), jnp.concatenate([ti, ni], axis=0), 2 * cnt
        return tr, ti

    sub = jnp.arange(SUBLANES)[:, None]
    tabs = {}
    for R in seg_lens:
        kr, ki = powers(lr, li, R)
        sr, si = powers(kr[R - 1:R], ki[R - 1:R], 4)
        pseg = []
        for lag in (1, 2, 4):
            msk = (sub >= lag).astype(F32)
            pseg += [msk * sr[lag - 1:lag], msk * si[lag - 1:lag]]
        tabs[R] = dict(lam=jnp.concatenate([lr, li], axis=0), pseg=jnp.stack(pseg), pk=jnp.stack([kr, ki]))
    return dict(bre=in_mat(bbr), bim=in_mat(bbi), cre=out_mat(c_re), cim=out_mat(-c_im), tabs=tabs,
                d=d_skip.reshape(1, S5_WIDTH), wglu=w_glu.astype(BF16), bglu=b_glu.reshape(1, S5_WIDTH))


def _fox_flash_kernel(qx_ref, kx_ref, v_ref, o_ref, m_ref, l_ref, acc_ref, s_ref, p_ref):
    i = pl.program_id(1)
    j = pl.program_id(2)
    tq = qx_ref.shape[0]
    tk = kx_ref.shape[0]

    @pl.when(j == 0)
    def _():
        m_ref[...] = jnp.full_like(m_ref, -jnp.inf)
        l_ref[...] = jnp.zeros_like(l_ref)
        acc_ref[...] = jnp.zeros_like(acc_ref)

    def step(masked):
        nr, ncol = tq // FLASH_ROWS, tk // LANES
        if masked:
            diff = (lax.broadcasted_iota(jnp.int32, (FLASH_ROWS, LANES), 0) -
                    lax.broadcasted_iota(jnp.int32, (FLASH_ROWS, LANES), 1))
        for h in range(FOX_HEADS):
            hs = slice(h * LANES, (h + 1) * LANES)
            ps = slice((h // 2) * LANES, (h // 2 + 1) * LANES)
            s_buf, p_buf = s_ref.at[h % 2], p_ref.at[h % 2]
            if h == 0:
                s_buf[...] = _dot_nt(qx_ref[:, hs], kx_ref[:, hs])
            if h + 1 < FOX_HEADS:
                nhs = slice((h + 1) * LANES, (h + 2) * LANES)
                s_ref[(h + 1) % 2] = _dot_nt(qx_ref[:, nhs], kx_ref[:, nhs])
            for r in range(nr):
                rs = slice(r * FLASH_ROWS, (r + 1) * FLASH_ROWS)

                live = [cidx for cidx in range(ncol) if not (masked and cidx * LANES >= (r + 1) * FLASH_ROWS)]

                def piece(cidx):
                    sc = s_buf[rs, cidx * LANES:(cidx + 1) * LANES]
                    if masked and (cidx + 1) * LANES - 1 > r * FLASH_ROWS:
                        sc = jnp.where(diff >= (cidx * LANES - r * FLASH_ROWS), sc, -jnp.inf)
                    return sc

                mx = piece(live[0])
                for cidx in live[1:]:
                    mx = jnp.maximum(mx, piece(cidx))
                m_prev = m_ref[h, rs, :]
                m_new = jnp.maximum(m_prev, jnp.max(mx, axis=-1, keepdims=True))
                alpha = jnp.exp(m_prev - m_new)
                lsum = jnp.zeros((FLASH_ROWS, LANES), F32)
                for cidx in range(ncol):
                    cs_ = slice(cidx * LANES, (cidx + 1) * LANES)
                    if cidx not in live:
                        p_buf[rs, cs_] = jnp.zeros((FLASH_ROWS, LANES), BF16)
                        continue
                    pc = jnp.exp(piece(cidx) - m_new)
                    lsum = lsum + pc
                    p_buf[rs, cs_] = pc.astype(BF16)
                l_ref[h, rs, :] = alpha * l_ref[h, rs, :] + lsum
                acc_ref[h, rs, :] = alpha * acc_ref[h, rs, :]
                m_ref[h, rs, :] = m_new
            acc_ref[h] += _dot(p_buf[...], v_ref[:, ps])

    q_lo = i * tq
    q_hi = i * tq + tq - 1
    k_lo = j * tk
    k_hi = j * tk + tk - 1

    @pl.when(k_hi <= q_lo)
    def _():
        step(False)

    @pl.when((k_lo <= q_hi) & (k_hi > q_lo))
    def _():
        step(True)

    @pl.when(j == pl.num_programs(2) - 1)
    def _():
        lane = lax.broadcasted_iota(jnp.int32, (1, LANES), 1)
        for p in range(FOX_HEADS // 2):
            lo = acc_ref[2 * p] / jnp.sum(l_ref[2 * p], axis=-1, keepdims=True)
            hi = acc_ref[2 * p + 1] / jnp.sum(l_ref[2 * p + 1], axis=-1, keepdims=True)
            o_ref[:, p * LANES:(p + 1) * LANES] = jnp.where(lane < FOX_HD, lo, hi)


def fox_flash(qx, kx, vb, nb, t, tq, tk):
    assert tq == tk and tq % FLASH_ROWS == 0
    nq, nk = t // tq, t // tk

    def kv_idx(i, j):
        return jnp.minimum(j, (i * tq + tq - 1) // tk)

    return pl.pallas_call(
        _fox_flash_kernel,
        out_shape=jax.ShapeDtypeStruct((nb * t, FOX_WIDTH), F32),
        grid=(nb, nq, nk),
        in_specs=[pl.BlockSpec((tq, FOX_HEADS * LANES), lambda b, i, j: (b * nq + i, 0)),
                  pl.BlockSpec((tk, FOX_HEADS * LANES), lambda b, i, j: (b * nk + kv_idx(i, j), 0)),
                  pl.BlockSpec((tk, FOX_WIDTH), lambda b, i, j: (b * nk + kv_idx(i, j), 0))],
        out_specs=pl.BlockSpec((tq, FOX_WIDTH), lambda b, i, j: (b * nq + i, 0)),
        scratch_shapes=[pltpu.VMEM((FOX_HEADS, tq, LANES), F32), pltpu.VMEM((FOX_HEADS, tq, LANES), F32),
                        pltpu.VMEM((FOX_HEADS, tq, LANES), F32),
                        pltpu.VMEM((2, tq, tk), F32), pltpu.VMEM((2, tq, tk), BF16)],
        compiler_params=_cparams("parallel", "parallel", "arbitrary"),
        name="fox_flash",
    )(qx, kx, vb)


def _fox_decode_kernel(pt_ref, qb_ref, kn_ref, vn_ref, csrow_ref, *rest, n_pages):
    k_refs = rest[0:n_pages]
    v_refs = rest[n_pages:2 * n_pages]
    f_refs = rest[2 * n_pages:3 * n_pages]
    triu_ref, o_ref, kpad_ref, vpad_ref = rest[3 * n_pages:]
    page = kpad_ref.shape[0]
    nrow = FOX_HEADS * SAMPLE_T
    qb = qb_ref[...].astype(BF16)

    def per_query(x):
        return jnp.concatenate([x] * SAMPLE_T, axis=0)

    fcat = jnp.concatenate([f_refs[i][0, 0] for i in range(n_pages)], axis=0)
    cum_in = _dot_exact_rhs(fcat, triu_ref[...])
    totals = jnp.broadcast_to(cum_in[:, page - 1:page], cum_in.shape)
    off = jnp.zeros((FOX_HEADS, page), F32)
    ss = []
    for i in range(n_pages):
        rows = slice(i * FOX_HEADS, (i + 1) * FOX_HEADS)
        kt = k_refs[i][0, 0].reshape(FOX_WIDTH, page).astype(BF16)
        ss.append(_dot(qb, kt) - per_query(cum_in[rows] + off))
        off = off + totals[rows]
    kpad_ref[...] = jnp.zeros_like(kpad_ref)
    vpad_ref[...] = jnp.zeros_like(vpad_ref)
    kpad_ref[0:SAMPLE_T, :] = kn_ref[...]
    vpad_ref[0:SAMPLE_T, :] = vn_ref[...]
    s_new = _dot_nt(qb, kpad_ref[...].astype(BF16)) - per_query(csrow_ref[...] + off)
    tq_idx = lax.broadcasted_iota(jnp.int32, (nrow, page), 0) // FOX_HEADS
    tk_idx = lax.broadcasted_iota(jnp.int32, (nrow, page), 1)
    ss.append(jnp.where(tk_idx <= tq_idx, s_new, -jnp.inf))

    m = ss[0]
    for s in ss[1:]:
        m = jnp.maximum(m, s)
    m = jnp.max(m, axis=-1, keepdims=True)
    lsum = jnp.zeros((nrow, page), F32)
    acc = jnp.zeros((nrow, FOX_WIDTH), F32)
    for i, s in enumerate(ss):
        p = jnp.exp(s - m)
        lsum = lsum + p
        if i < n_pages:
            acc = acc + _dot_nt(p.astype(BF16), v_refs[i][0, 0].reshape(FOX_WIDTH, page).astype(BF16))
        else:
            acc = acc + _dot(p.astype(BF16), vpad_ref[...].astype(BF16))
    acc = acc / jnp.sum(lsum, axis=-1, keepdims=True)
    lane = lax.broadcasted_iota(jnp.int32, (nrow, FOX_WIDTH), 1) // FOX_HD
    head = lax.broadcasted_iota(jnp.int32, (nrow, FOX_WIDTH), 0) % FOX_HEADS
    picked = jnp.where(lane == head, acc, 0.0)
    o_ref[...] = jnp.sum(picked.reshape(SAMPLE_T, FOX_HEADS, FOX_WIDTH), axis=1)


def fox_decode(page_table, qx, kn, z, csrow, pool_kt, pool_vt, pool_ft, layer):
    nb, n_pages = page_table.shape
    page = pool_kt.shape[-1]
    assert page == LANES
    nrow = FOX_HEADS * SAMPLE_T
    pt = page_table.reshape(-1)
    triu = jnp.asarray(np.triu(np.ones((page, page))), BF16)
    pair_of_head = jnp.asarray(np.arange(FOX_HEADS)[:, None] // 2 == np.arange(FOX_HEADS // 2)[None, :], F32)
    qb = (qx.reshape(nb, SAMPLE_T, FOX_HEADS, 1, LANES) * pair_of_head[None, None, :, :, None]).reshape(
        nb * nrow, FOX_WIDTH)

    def pg5(i):
        return lambda b, pt: (layer, pt[b * n_pages + i], 0, 0, 0)

    def pg4(i):
        return lambda b, pt: (layer, pt[b * n_pages + i], 0, 0)

    row = lambda b, pt: (b, 0)
    in_specs = [pl.BlockSpec((nrow, FOX_WIDTH), row),
                pl.BlockSpec((SAMPLE_T, FOX_WIDTH), row),
                pl.BlockSpec((SAMPLE_T, FOX_WIDTH), lambda b, pt: (b, COL_FV)),
                pl.BlockSpec((FOX_HEADS, page), lambda b, pt: (b * (LANES // FOX_HEADS), 0))]
    in_specs += [pl.BlockSpec((1, 1, FOX_HEADS, FOX_HD, page), pg5(i)) for i in range(n_pages)]
    in_specs += [pl.BlockSpec((1, 1, FOX_HEADS, FOX_HD, page), pg5(i)) for i in range(n_pages)]
    in_specs += [pl.BlockSpec((1, 1, FOX_HEADS, page), pg4(i)) for i in range(n_pages)]
    in_specs += [pl.BlockSpec((page, page), lambda b, pt: (0, 0))]
    return pl.pallas_call(
        functools.partial(_fox_decode_kernel, n_pages=n_pages),
        out_shape=jax.ShapeDtypeStruct((nb * SAMPLE_T, FOX_WIDTH), F32),
        grid_spec=pltpu.PrefetchScalarGridSpec(
            num_scalar_prefetch=1,
            grid=(nb,),
            in_specs=in_specs,
            out_specs=pl.BlockSpec((SAMPLE_T, FOX_WIDTH), row),
            scratch_shapes=[pltpu.VMEM((page, FOX_WIDTH), F32), pltpu.VMEM((page, FOX_WIDTH), F32)]),
        compiler_params=_cparams("parallel"),
        name="fox_decode",
    )(pt, qb, kn, z, csrow, *([pool_kt] * n_pages), *([pool_vt] * n_pages), *([pool_ft] * n_pages), triu)


def _mlstm_kernel(q_ref, k_ref, v_ref, o_ref, g_ref, cs_ref, grow_ref, csrow_ref, c0_ref, n0_ref, m0_ref, gn_ref,
                  y_ref, cout_ref, nout_ref, mout_ref, c_ref, n_ref, m_ref):
    c = pl.program_id(1)
    nc = pl.num_programs(1)
    L = q_ref.shape[0]

    @pl.when(c == 0)
    def _():
        c_ref[...] = c0_ref[0]
        n_ref[...] = n0_ref[0]
        m_ref[...] = m0_ref[0]

    Lk = grow_ref.shape[1]
    Lq = max(L, MIN_BF16_ROWS)
    g = g_ref[...]
    cs = cs_ref[...]
    grow = grow_ref[...]
    csrow = csrow_ref[...]
    causal = (lax.broadcasted_iota(jnp.int32, (Lq, Lk), 0) >= lax.broadcasted_iota(jnp.int32, (Lq, Lk), 1))
    for h in range(ML_HEADS):
        hs = slice(h * ML_HD, (h + 1) * ML_HD)
        scale = ML_HD ** -0.5
        qb = _pad_rows(q_ref[:, hs], Lq).astype(BF16)
        kb = _pad_rows(k_ref[:, hs], Lk).astype(BF16)
        vb = _pad_rows(v_ref[:, hs], Lk).astype(BF16)
        qh, kh, vh = qb.astype(F32), kb.astype(F32), vb.astype(F32)
        bcol_k = _pad_rows(cs[:, LANE_MLF + h:LANE_MLF + h + 1], Lk)
        bcol = bcol_k[0:Lq]
        icol = _pad_rows(g[:, LANE_MLI + h:LANE_MLI + h + 1], Lk, -jnp.inf)
        brow = csrow[LANE_MLF + h:LANE_MLF + h + 1, :]
        irow = grow[LANE_MLI + h:LANE_MLI + h + 1, :]
        m_prev = m_ref[0:1, h:h + 1]
        log_d = jnp.where(causal, bcol - brow + irow, -jnp.inf)
        log_inter = bcol + m_prev
        m_t = jnp.maximum(log_inter, jnp.max(log_d, axis=-1, keepdims=True))
        d_w = jnp.exp(log_d - m_t)
        inter_w = jnp.exp(log_inter - m_t)
        s = _dot_nt(qb, kb) * (d_w * scale)
        ch = c_ref[h]
        n_row = n_ref[h:h + 1, :]
        num = _dot(s.astype(BF16), vb) + inter_w * _dot_nt(qb, ch.astype(BF16))
        den = jnp.sum(s, axis=-1, keepdims=True) + inter_w * jnp.sum(qh * n_row, axis=-1, keepdims=True)
        hh = num / jnp.maximum(jnp.abs(den), jnp.exp(-m_t))
        y_ref[:, hs] = _rms(hh[0:L], gn_ref[...]) * _sigmoid(o_ref[:, hs])
        m_end = m_t[L - 1:L, :]
        a_end = inter_w[L - 1:L, :]
        w_col = jnp.exp(bcol[L - 1:L, :] - bcol_k + icol - m_end) * scale
        c_ref[h] = a_end * ch + _dot_tn((vh * w_col).astype(BF16), kb)
        n_ref[h:h + 1, :] = a_end * n_row + jnp.sum(kh * w_col, axis=0, keepdims=True)
        m_ref[0:1, h:h + 1] = m_end

    @pl.when(c == nc - 1)
    def _():
        cout_ref[0] = c_ref[...]
        nout_ref[0] = n_ref[...]
        mout_ref[0] = m_ref[...]


def mlstm(za, zb, g, cs, grow, csrow, c0, n0, m0, gn, nb, t, L):
    nc = t // L
    rows = lambda b, c: (b * nc + c, 0)
    rr = 2 * SUBLANES
    m0p = jnp.zeros((nb, 1, LANES), F32).at[:, 0, :ML_HEADS].set(m0)
    outs = pl.pallas_call(
        _mlstm_kernel,
        out_shape=(jax.ShapeDtypeStruct((nb * t, ML_WIDTH), F32),
                   jax.ShapeDtypeStruct((nb, ML_HEADS, ML_HD, ML_HD), F32),
                   jax.ShapeDtypeStruct((nb, ML_HEADS, ML_HD), F32),
                   jax.ShapeDtypeStruct((nb, 1, LANES), F32)),
        grid=(nb, nc),
        in_specs=[pl.BlockSpec((L, 512), lambda b, c: (b * nc + c, COL_MQ)),
                  pl.BlockSpec((L, 512), lambda b, c: (b * nc + c, COL_MK)),
                  pl.BlockSpec((L, 512), lambda b, c: (b * nc + c, COL_MV)),
                  pl.BlockSpec((L, 512), lambda b, c: (b * nc + c, COL_MO)),
                  pl.BlockSpec((L, LANES), rows), pl.BlockSpec((L, LANES), rows),
                  pl.BlockSpec((rr, max(L, LANES)), lambda b, c: (b * (LANES // rr), c)),
                  pl.BlockSpec((rr, max(L, LANES)), lambda b, c: (b * (LANES // rr), c)),
                  pl.BlockSpec((1, ML_HEADS, ML_HD, ML_HD), lambda b, c: (b, 0, 0, 0)),
                  pl.BlockSpec((1, ML_HEADS, ML_HD), lambda b, c: (b, 0, 0)),
                  pl.BlockSpec((1, 1, LANES), lambda b, c: (b, 0, 0)),
                  pl.BlockSpec((1, ML_HD), lambda b, c: (0, 0))],
        out_specs=(pl.BlockSpec((L, ML_WIDTH), rows),
                   pl.BlockSpec((1, ML_HEADS, ML_HD, ML_HD), lambda b, c: (b, 0, 0, 0)),
                   pl.BlockSpec((1, ML_HEADS, ML_HD), lambda b, c: (b, 0, 0)),
                   pl.BlockSpec((1, 1, LANES), lambda b, c: (b, 0, 0))),
        scratch_shapes=[pltpu.VMEM((ML_HEADS, ML_HD, ML_HD), F32), pltpu.VMEM((ML_HEADS, ML_HD), F32),
                        pltpu.VMEM((1, LANES), F32)],
        compiler_params=_cparams("parallel", "arbitrary"),
        name="mlstm",
    )(zb, zb, zb, za, g, cs, grow, csrow, c0, n0, m0p, gn.reshape(1, ML_HD))
    y, c_new, n_new, m_new = outs
    return y, c_new, n_new, m_new[:, 0, :ML_HEADS]


def _merge_kernel(x_ref, g_ref, wg_ref, ys_ref, yf_ref, ym_ref, ws_ref, wf_ref, wm_ref, wo_ref, o_ref):
    x = x_ref[...]
    hn = _rms(x, g_ref[...]).astype(BF16)
    branches = (ys_ref, ws_ref), (yf_ref, wf_ref), (ym_ref, wm_ref)
    merged = None
    for b, (y_ref, w_ref) in enumerate(branches):
        gate = _sigmoid(_dot(hn, wg_ref[:, b * D_MODEL:(b + 1) * D_MODEL]))
        term = gate * _dot(y_ref[...].astype(BF16), w_ref[...])
        merged = term if merged is None else merged + term
    o_ref[...] = x + _dot(merged.astype(BF16), wo_ref[...])


def merge_out(x, g_mix, w_gates, ys, yf, ym, ws, wf, wm, wo, tm=256):
    m = x.shape[0]
    tm = min(tm, m)
    row = lambda i: (i, 0)
    const = lambda i: (0, 0)
    return pl.pallas_call(
        _merge_kernel,
        out_shape=jax.ShapeDtypeStruct((m, D_MODEL), F32),
        grid=(m // tm,),
        in_specs=[pl.BlockSpec((tm, D_MODEL), row), pl.BlockSpec((1, D_MODEL), const),
                  pl.BlockSpec((D_MODEL, 3 * D_MODEL), const),
                  pl.BlockSpec((tm, 512), row), pl.BlockSpec((tm, 512), row), pl.BlockSpec((tm, 512), row),
                  pl.BlockSpec((512, D_MODEL), const), pl.BlockSpec((512, D_MODEL), const),
                  pl.BlockSpec((512, D_MODEL), const), pl.BlockSpec((D_MODEL, D_MODEL), const)],
        out_specs=pl.BlockSpec((tm, D_MODEL), row),
        compiler_params=_cparams("parallel"),
        name="merge_out",
    )(x, g_mix.reshape(1, D_MODEL), w_gates, ys, yf, ym, ws, wf, wm, wo)


def _cross_kernel(q_ref, k_ref, v_ref, o_ref):
    for h in range(MEM_HEADS):
        hs = slice(h * MEM_HD, (h + 1) * MEM_HD)
        s = _dot_nt(q_ref[:, hs].astype(BF16), k_ref[:, hs].astype(BF16)) * (MEM_HD ** -0.5)
        m = jnp.max(s, axis=-1, keepdims=True)
        p = jnp.exp(s - m)
        l = jnp.sum(p, axis=-1, keepdims=True)
        o_ref[:, hs] = _dot(p.astype(BF16), v_ref[:, hs].astype(BF16)) / l


def _cross_cached_kernel(q_ref, k_ref, v_ref, o_ref):
    tq = q_ref.shape[0]
    q = q_ref[...]
    qr = jnp.concatenate([q[:, h * MEM_HD:(h + 1) * MEM_HD] for h in range(MEM_HEADS)], axis=0)
    kf = k_ref[0, 0].astype(BF16)
    vf = v_ref[0, 0].astype(BF16)
    s = _dot_nt(qr.astype(BF16), kf) * (MEM_HD ** -0.5)
    row_head = lax.broadcasted_iota(jnp.int32, s.shape, 0) // tq
    col_head = lax.broadcasted_iota(jnp.int32, s.shape, 1) % MEM_HEADS
    s = jnp.where(row_head == col_head, s, -jnp.inf)
    m = jnp.max(s, axis=-1, keepdims=True)
    p = jnp.exp(s - m)
    l = jnp.sum(p, axis=-1, keepdims=True)
    o = _dot(p.astype(BF16), vf) / l
    for h in range(MEM_HEADS):
        o_ref[:, h * MEM_HD:(h + 1) * MEM_HD] = o[h * tq:(h + 1) * tq]


def cross_attend(q, mem_k, mem_v, nb, t, tq, layer=None):
    nq = t // tq
    cached = layer is not None
    if cached:
        kv_spec = pl.BlockSpec((1, 1, mem_k.shape[2], MEM_HD), lambda b, i: (layer, b, 0, 0))
    else:
        n_mem = mem_k.shape[0] // nb
        kv_spec = pl.BlockSpec((n_mem, MEM_WIDTH), lambda b, i: (b, 0))
    return pl.pallas_call(
        _cross_cached_kernel if cached else _cross_kernel,
        out_shape=jax.ShapeDtypeStruct((nb * t, MEM_WIDTH), F32),
        grid=(nb, nq),
        in_specs=[pl.BlockSpec((tq, MEM_WIDTH), lambda b, i: (b * nq + i, 0)), kv_spec, kv_spec],
        out_specs=pl.BlockSpec((tq, MEM_WIDTH), lambda b, i: (b * nq + i, 0)),
        compiler_params=_cparams("parallel", "parallel"),
        name="cross_attend",
    )(q, mem_k, mem_v)


def _proj_residual_kernel(x_ref, a_ref, w_ref, o_ref):
    o_ref[...] = x_ref[...] + _dot(a_ref[...].astype(BF16), w_ref[...])


def proj_residual(x, a, w, tm=512):
    m, n = x.shape
    k = a.shape[1]
    tm = min(tm, m)
    return pl.pallas_call(
        _proj_residual_kernel,
        out_shape=jax.ShapeDtypeStruct((m, n), F32),
        grid=(m // tm,),
        in_specs=[pl.BlockSpec((tm, n), lambda i: (i, 0)), pl.BlockSpec((tm, k), lambda i: (i, 0)),
                  pl.BlockSpec((k, n), lambda i: (0, 0))],
        out_specs=pl.BlockSpec((tm, n), lambda i: (i, 0)),
        compiler_params=_cparams("parallel"),
        name="proj_residual",
    )(x, a, w)


def _mlp_kernel(x_ref, g_ref, wu_ref, wd_ref, o_ref, hn_ref, acc_ref):
    f = pl.program_id(1)

    @pl.when(f == 0)
    def _():
        hn_ref[...] = _rms(x_ref[...], g_ref[...]).astype(BF16)
        acc_ref[...] = jnp.zeros_like(acc_ref)

    a = jnp.maximum(_dot(hn_ref[...], wu_ref[...]), 0.0)
    acc_ref[...] += _dot((a * a).astype(BF16), wd_ref[...])

    @pl.when(f == pl.num_programs(1) - 1)
    def _():
        o_ref[...] = x_ref[...] + acc_ref[...]


def mlp(x, g, wu, wd, tm=1024, tf=512):
    m, d = x.shape
    dff = wu.shape[1]
    tm = min(tm, m)
    return pl.pallas_call(
        _mlp_kernel,
        out_shape=jax.ShapeDtypeStruct((m, d), F32),
        grid=(m // tm, dff // tf),
        in_specs=[pl.BlockSpec((tm, d), lambda i, f: (i, 0)), pl.BlockSpec((1, d), lambda i, f: (0, 0)),
                  pl.BlockSpec((d, tf), lambda i, f: (0, f)), pl.BlockSpec((tf, d), lambda i, f: (f, 0))],
        out_specs=pl.BlockSpec((tm, d), lambda i, f: (i, 0)),
        scratch_shapes=[pltpu.VMEM((tm, d), BF16), pltpu.VMEM((tm, d), F32)],
        compiler_params=_cparams("parallel", "arbitrary"),
        name="mlp",
    )(x, g.reshape(1, d), wu, wd)


def _pack_w_in(w_in):
    offs = np.concatenate([[0], np.cumsum(SPLITS)])
    col = lambda i: w_in[:, int(offs[i]):int(offs[i + 1])]
    s5, fq, fk, fv, ff, mq, mk, mv, mi, mf, mo, gates = [col(i) for i in range(12)]
    pad = jnp.zeros((w_in.shape[0], 512 - FOX_HEADS - 2 * ML_HEADS), w_in.dtype)
    w_a = jnp.concatenate([s5, fq, fk, fv, mo, ff, mi, mf, pad], axis=1).astype(BF16)
    w_b = jnp.concatenate([mq, mk, mv], axis=1).astype(BF16)
    return w_a, w_b, gates.astype(BF16)


def _layer_weights(l, g_mix, w_in, s5_a_re, s5_a_im, s5_log_step, s5_b_re, s5_b_im, s5_c_re, s5_c_im, s5_d,
                   s5_w_glu, s5_b_glu, fox_gq, fox_gk, fox_bf, ml_bi, ml_bf, ml_gn, w_br_s5, w_br_fox, w_br_ml,
                   w_out, g_cross, w_cq, cross_gq, g_mem, w_mk, w_mv, cross_gk, w_co, g_mlp, w_up, w_down):
    bias_row = jnp.zeros((1, LANES), F32)
    bias_row = bias_row.at[0, LANE_FOXF:LANE_FOXF + FOX_HEADS].set(fox_bf[l])
    bias_row = bias_row.at[0, LANE_MLI:LANE_MLI + ML_HEADS].set(ml_bi[l])
    bias_row = bias_row.at[0, LANE_MLF:LANE_MLF + ML_HEADS].set(ml_bf[l])
    w_a, w_b, w_gates = _pack_w_in(w_in[l])
    return dict(
        g_mix=g_mix[l], w_a=w_a, w_b=w_b, w_gates=w_gates,
        s5=s5_params(s5_a_re[l], s5_a_im[l], s5_log_step[l], s5_b_re[l], s5_b_im[l], s5_c_re[l], s5_c_im[l],
                     s5_d[l], s5_w_glu[l], s5_b_glu[l], (SEQ_TILE // SUBLANES, SAMPLE_T // SUBLANES)),
        gq=jnp.tile(fox_gq[l], FOX_HEADS).reshape(1, FOX_WIDTH),
        gk=jnp.tile(fox_gk[l], FOX_HEADS).reshape(1, FOX_WIDTH),
        bias_row=bias_row, ml_gn=ml_gn[l],
        w_br_s5=w_br_s5[l].astype(BF16), w_br_fox=w_br_fox[l].astype(BF16), w_br_ml=w_br_ml[l].astype(BF16),
        w_out=w_out[l].astype(BF16), g_cross=g_cross[l], w_cq=w_cq[l].astype(BF16), cross_gq=cross_gq[l],
        g_mem=g_mem[l], w_mk=w_mk[l].astype(BF16), w_mv=w_mv[l].astype(BF16), cross_gk=cross_gk[l],
        w_co=w_co[l].astype(BF16), g_mlp=g_mlp[l], w_up=w_up[l].astype(BF16), w_down=w_down[l].astype(BF16))


def _hybrid_layer(x, W, nb, t, seq_tile, t_valid, s5_state, ml_state, mem_k, mem_v, fox_attend, augment,
                  mem_layer=None):
    z = norm_matmul(x, W["g_mix"], W["w_a"], tm=min(1024, nb * t))
    zb = norm_matmul(x, W["g_mix"], W["w_b"], tm=min(1024, nb * t),
                     out_dtype=BF16 if seq_tile % MIN_BF16_ROWS == 0 else F32)
    prep = gate_prep(z, W["gq"], W["gk"], W["bias_row"], nb, t, seq_tile, t_valid, augment)
    kn, g, cs, grow, csrow = prep[1:6]
    y_s5, s5_re, s5_im = s5_mixer(z, s5_state[0], s5_state[1], W["s5"], nb, t, seq_tile, t_valid - 1,
                                  seqs=S5_SHORT_SEQS if t == SUBLANES else 1)
    y_fox = fox_attend(prep, z)
    y_ml, c_new, n_new, m_new = mlstm(z, zb, g, cs, grow, csrow, ml_state[0], ml_state[1], ml_state[2], W["ml_gn"],
                                      nb, t, seq_tile)
    x = merge_out(x, W["g_mix"], W["w_gates"], y_s5, y_fox, y_ml, W["w_br_s5"], W["w_br_fox"], W["w_br_ml"],
                  W["w_out"])
    qc = norm_matmul(x, W["g_cross"], W["w_cq"], head_gain=W["cross_gq"])
    oc = cross_attend(qc, mem_k, mem_v, nb, t, seq_tile, mem_layer)
    x = proj_residual(x, oc, W["w_co"])
    x = mlp(x, W["g_mlp"], W["w_up"], W["w_down"])
    return x, z, kn, g, s5_re.reshape(nb, S5_GROUPS, S5_STATE), s5_im.reshape(nb, S5_GROUPS, S5_STATE), \
        c_new, n_new, m_new


def kernel(x_prompt, x_sample, mem_prompt, cache_fox_k, cache_fox_v, cache_fox_logf, page_table, state_s5_re, state_s5_im, state_mlstm_C, state_mlstm_n, state_mlstm_m, cache_mem_k, cache_mem_v, g_mix, w_in, s5_a_re, s5_a_im, s5_log_step, s5_b_re, s5_b_im, s5_c_re, s5_c_im, s5_d, s5_w_glu, s5_b_glu, fox_gq, fox_gk, fox_bf, ml_bi, ml_bf, ml_gn, w_br_s5, w_br_fox, w_br_ml, w_out, g_cross, w_cq, cross_gq, g_mem, w_mk, w_mv, cross_gk, w_co, g_mlp, w_up, w_down):
    depth = w_in.shape[0]
    bp, tp, _ = x_prompt.shape
    bs, ts, _ = x_sample.shape
    n_mem = mem_prompt.shape[1]

    xp = x_prompt.reshape(bp * tp, D_MODEL)
    xs = jnp.pad(x_sample, ((0, 0), (0, SAMPLE_T - ts), (0, 0))).reshape(bs * SAMPLE_T, D_MODEL)
    mem = mem_prompt.reshape(bp * n_mem, D_MODEL)
    zeros_p = (jnp.zeros((bp, S5_LANES), F32), jnp.zeros((bp, S5_LANES), F32))
    zeros_ml = (jnp.zeros((bp, ML_HEADS, ML_HD, ML_HD), F32), jnp.zeros((bp, ML_HEADS, ML_HD), F32),
                jnp.zeros((bp, ML_HEADS), F32))
    pool_kt = jnp.transpose(cache_fox_k, (0, 1, 3, 4, 2))
    pool_vt = jnp.transpose(cache_fox_v, (0, 1, 3, 4, 2))
    pool_ft = jnp.transpose(cache_fox_logf, (0, 1, 3, 2))
    st_p, st_s = [], []
    for l in range(depth):
        W = _layer_weights(l, g_mix, w_in, s5_a_re, s5_a_im, s5_log_step, s5_b_re, s5_b_im, s5_c_re, s5_c_im, s5_d,
                           s5_w_glu, s5_b_glu, fox_gq, fox_gk, fox_bf, ml_bi, ml_bf, ml_gn, w_br_s5, w_br_fox,
                           w_br_ml, w_out, g_cross, w_cq, cross_gq, g_mem, w_mk, w_mv, cross_gk, w_co, g_mlp, w_up,
                           w_down)
        mk_p = norm_matmul(mem, W["g_mem"], W["w_mk"], head_gain=W["cross_gk"])
        mv_p = norm_matmul(mem, W["g_mem"], W["w_mv"])

        def flash(prep, z):
            return fox_flash(prep[0], prep[6], prep[7], bp, tp, FLASH_TILE, FLASH_TILE)

        xp, z, kn, g, s5r, s5i, c_new, n_new, m_new = _hybrid_layer(
            xp, W, bp, tp, SEQ_TILE, tp, zeros_p, zeros_ml, mk_p, mv_p, flash, True)
        st_p.append((kn.reshape(bp, tp, FOX_HEADS, FOX_HD),
                     z[:, COL_FV * 512:(COL_FV + 1) * 512].reshape(bp, tp, FOX_HEADS, FOX_HD),
                     g[:, LANE_FOXF:LANE_FOXF + FOX_HEADS].reshape(bp, tp, FOX_HEADS),
                     s5r, s5i, c_new, n_new, m_new,
                     mk_p.reshape(bp, n_mem, MEM_HEADS, MEM_HD), mv_p.reshape(bp, n_mem, MEM_HEADS, MEM_HD)))

        def decode(prep, z, layer=l):
            return fox_decode(page_table, prep[0], prep[1], z, prep[5], pool_kt, pool_vt, pool_ft, layer)

        xs, z, kn, g, s5r, s5i, c_new, n_new, m_new = _hybrid_layer(
            xs, W, bs, SAMPLE_T, SAMPLE_T, ts,
            (state_s5_re[l].reshape(bs, S5_LANES), state_s5_im[l].reshape(bs, S5_LANES)),
            (state_mlstm_C[l], state_mlstm_n[l], state_mlstm_m[l]),
            cache_mem_k.reshape(depth, bs, n_mem * MEM_HEADS, MEM_HD),
            cache_mem_v.reshape(depth, bs, n_mem * MEM_HEADS, MEM_HD), decode, False, mem_layer=l)
        st_s.append((kn.reshape(bs, SAMPLE_T, FOX_HEADS, FOX_HD)[:, :ts],
                     z[:, COL_FV * 512:(COL_FV + 1) * 512].reshape(bs, SAMPLE_T, FOX_HEADS, FOX_HD)[:, :ts],
                     g[:, LANE_FOXF:LANE_FOXF + FOX_HEADS].reshape(bs, SAMPLE_T, FOX_HEADS)[:, :ts],
                     s5r, s5i, c_new, n_new, m_new))
    outs_p = [jnp.stack(a) for a in zip(*st_p)]
    outs_s = [jnp.stack(a) for a in zip(*st_s)]
    yp = xp.reshape(bp, tp, D_MODEL)
    ys = xs.reshape(bs, SAMPLE_T, D_MODEL)[:, :ts]
    return (yp, ys, *outs_p, *outs_s)
```

```python
import functools
import math

import jax
import jax.numpy as jnp
import numpy as np
from jax import lax
from jax.experimental import pallas as pl
from jax.experimental.pallas import tpu as pltpu

F32 = jnp.float32
BF16 = jnp.bfloat16

LANES = 128
SUBLANES = 8
MIN_BF16_ROWS = 16
MXU_DIM = 256
VMEM_LIMIT_BYTES = 48 * 1024 * 1024

D_MODEL = 1024
S5_WIDTH = 512
S5_GROUP = 16
S5_GROUPS = 32
S5_STATE = 64
S5_LANES = S5_GROUPS * S5_STATE
FOX_HEADS = 8
FOX_HD = 64
FOX_WIDTH = 512
ML_HEADS = 4
ML_HD = 128
ML_WIDTH = 512
MEM_HEADS = 4
MEM_HD = 128
MEM_WIDTH = 512
D_FF = 4096
EPS = 1e-6
SPLITS = (S5_WIDTH, FOX_WIDTH, FOX_WIDTH, FOX_WIDTH, FOX_HEADS, ML_WIDTH, ML_WIDTH, ML_WIDTH,
          ML_HEADS, ML_HEADS, ML_WIDTH, 3 * D_MODEL)

COL_S5, COL_FQ, COL_FK, COL_FV, COL_MO, COL_SMALL = range(6)
COL_MQ, COL_MK, COL_MV = range(3)
LANE_FOXF = 0
LANE_MLI = 8
LANE_MLF = 12
SEQ_TILE = 256
FLASH_TILE = 512
FLASH_ROWS = 128
SAMPLE_T = 8


def _cparams(*sem):
    return pltpu.CompilerParams(dimension_semantics=sem, vmem_limit_bytes=VMEM_LIMIT_BYTES)


def _dot(a, b):
    return jnp.dot(a, b, preferred_element_type=F32)


def _dot_nt(a, b):
    return lax.dot_general(a, b, (((1,), (1,)), ((), ())), preferred_element_type=F32)


def _dot_tn(a, b):
    return lax.dot_general(a, b, (((0,), (0,)), ((), ())), preferred_element_type=F32)


def _split3(x):
    hi = x.astype(BF16)
    r1 = x - hi.astype(F32)
    mid = r1.astype(BF16)
    lo = (r1 - mid.astype(F32)).astype(BF16)
    return hi, mid, lo


def _dot_exact_rhs(x, ones_rhs):
    hi, mid, lo = _split3(x)
    return _dot(hi, ones_rhs) + _dot(mid, ones_rhs) + _dot(lo, ones_rhs)


def _dot_exact_lhs(ones_lhs, x):
    hi, mid, lo = _split3(x)
    return _dot(ones_lhs, hi) + _dot(ones_lhs, mid) + _dot(ones_lhs, lo)


def _pad_rows(x, n, fill=0.0):
    if x.shape[0] >= n:
        return x
    return jnp.concatenate([x, jnp.full((n - x.shape[0], x.shape[1]), fill, x.dtype)], axis=0)


def _lane_tile(x, n):
    return x if n == 1 else jnp.concatenate([x] * n, axis=1)


def _log_sigmoid(a):
    return jnp.minimum(a, 0.0) - jnp.log1p(jnp.exp(-jnp.abs(a)))


def _sigmoid(a):
    return 1.0 / (1.0 + jnp.exp(-a))


def _gelu_tanh(x):
    c = math.sqrt(2.0 / math.pi)
    return 0.5 * x * (1.0 + jnp.tanh(c * (x + 0.044715 * (x * x * x))))


def _rms(x, g):
    ms = jnp.mean(x * x, axis=-1, keepdims=True)
    return x * lax.rsqrt(ms + EPS) * g


def _norm_matmul_kernel(x_ref, g_ref, w_ref, *rest, head_norm):
    if head_norm:
        hg_ref, o_ref, hn_ref = rest
    else:
        o_ref, hn_ref = rest

    @pl.when(pl.program_id(1) == 0)
    def _():
        hn_ref[...] = _rms(x_ref[...], g_ref[...]).astype(BF16)

    y = _dot(hn_ref[...], w_ref[...])
    if head_norm:
        tn = y.shape[1]
        for s in range(tn // LANES):
            sl = slice(s * LANES, (s + 1) * LANES)
            o_ref[:, sl] = _rms(y[:, sl], hg_ref[...]).astype(o_ref.dtype)
    else:
        o_ref[...] = y.astype(o_ref.dtype)


def norm_matmul(x, g, w, head_gain=None, out_dtype=F32, tm=512, tn=512):
    m, k = x.shape
    n = w.shape[1]
    tm = min(tm, m)
    tn = min(tn, n)
    head_norm = head_gain is not None
    in_specs = [pl.BlockSpec((tm, k), lambda i, j: (i, 0)),
                pl.BlockSpec((1, k), lambda i, j: (0, 0)),
                pl.BlockSpec((k, tn), lambda i, j: (0, j))]
    args = [x, g.reshape(1, k), w]
    if head_norm:
        in_specs.append(pl.BlockSpec((1, LANES), lambda i, j: (0, 0)))
        args.append(head_gain.reshape(1, LANES))
    return pl.pallas_call(
        functools.partial(_norm_matmul_kernel, head_norm=head_norm),
        out_shape=jax.ShapeDtypeStruct((m, n), out_dtype),
        grid=(m // tm, n // tn),
        in_specs=in_specs,
        out_specs=pl.BlockSpec((tm, tn), lambda i, j: (i, j)),
        scratch_shapes=[pltpu.VMEM((tm, k), BF16)],
        compiler_params=_cparams("parallel", "arbitrary"),
        name="norm_matmul",
    )(*args)


def _prep_kernel(fq_ref, fk_ref, fv_ref, sm_ref, gq_ref, gk_ref, bias_ref, gmat_ref, tril_ref, sel_ref, aug_ref,
                 *rest, t_valid, augment):
    if augment:
        qx_ref, kn_ref, g_ref, cs_ref, grow_ref, csrow_ref, kx_ref, vb_ref, carry_ref = rest
    else:
        qx_ref, kn_ref, g_ref, cs_ref, grow_ref, csrow_ref, carry_ref = rest
    c = pl.program_id(1)
    tc = sm_ref.shape[0]
    tp = tril_ref.shape[0]

    @pl.when(c == 0)
    def _():
        carry_ref[...] = jnp.zeros_like(carry_ref)

    gmat = gmat_ref[...]

    def head_rms(x, gain):
        x2 = _pad_rows(x * x, MIN_BF16_ROWS)
        hi = x2.astype(BF16)
        lo = (x2 - hi.astype(F32)).astype(BF16)
        ss = ((_dot(hi, gmat) + _dot(lo, gmat)) * (1.0 / FOX_HD))[0:tc]
        return x * lax.rsqrt(ss + EPS) * gain

    qn = head_rms(fq_ref[...], gq_ref[...]) * (FOX_HD ** -0.5)
    kn = head_rms(fk_ref[...], gk_ref[...])
    kn_ref[...] = kn

    a = _pad_rows(sm_ref[...], tp) + bias_ref[...]
    lane = lax.broadcasted_iota(jnp.int32, a.shape, 1)
    row = lax.broadcasted_iota(jnp.int32, a.shape, 0) + c * tc
    is_i = (lane >= LANE_MLI) & (lane < LANE_MLF)
    used = lane < LANE_MLF + ML_HEADS
    valid = row < t_valid
    g = jnp.where(is_i, a, _log_sigmoid(a))
    g = jnp.where(used, g, 0.0)
    g = jnp.where(valid, g, jnp.where(is_i, -jnp.inf, 0.0))
    gc = jnp.where(is_i, 0.0, g)
    cs = _dot_exact_lhs(tril_ref[...], gc)
    carry = carry_ref[...]
    csg = cs + jnp.where(lane < FOX_HEADS, carry, 0.0)
    carry_ref[...] = carry + cs[tc - 1:tc, :]
    g_ref[...] = g[0:tc]
    cs_ref[...] = csg[0:tc]
    grow_ref[...] = g.T
    csrow_ref[...] = csg.T

    nx = FOX_HEADS * LANES
    lanex = lax.broadcasted_iota(jnp.int32, (1, nx), 1)
    keep = ((lanex % LANES) >= FOX_HD) == ((lanex // LANES) % 2 == 1)
    q_exp = jnp.concatenate([qn[:, (h // 2) * LANES:(h // 2 + 1) * LANES] for h in range(FOX_HEADS)], axis=1)
    if not augment:
        qx_ref[...] = jnp.where(keep, q_exp, 0.0).astype(qx_ref.dtype)
        return
    k_exp = jnp.concatenate([kn[:, (h // 2) * LANES:(h // 2 + 1) * LANES] for h in range(FOX_HEADS)], axis=1)
    hi, mid, lo = _split3(jnp.where(lane < FOX_HEADS, csg, 0.0)[0:tc])
    packed = (hi.astype(F32) + pltpu.roll(mid.astype(F32), FOX_HEADS, 1)
              + pltpu.roll(lo.astype(F32), 2 * FOX_HEADS, 1)).astype(BF16)
    aug = aug_ref[...] + _dot(packed, sel_ref[...])
    qx_ref[...] = jnp.where(keep, q_exp, aug[:, 0:nx]).astype(qx_ref.dtype)
    kx_ref[...] = jnp.where(keep, k_exp, aug[:, nx:2 * nx]).astype(kx_ref.dtype)
    vb_ref[...] = fv_ref[...].astype(vb_ref.dtype)


def gate_prep(z, gq, gk, bias_row, nb, t, tc, t_valid, augment):
    m = nb * t
    nc = t // tc
    gmat = jnp.asarray(np.kron(np.eye(FOX_HEADS), np.ones((FOX_HD, FOX_HD))), BF16)
    tp = max(tc, LANES)
    tril = jnp.asarray(np.tril(np.ones((tp, tp))), BF16)
    nx = FOX_HEADS * LANES
    sel = np.zeros((LANES, 2 * nx), np.float32)
    aug_const = np.zeros((1, 2 * nx), np.float32)
    for h in range(FOX_HEADS):
        o = h * LANES + (0 if h % 2 else FOX_HD)
        for p in range(3):
            sel[p * FOX_HEADS + h, o + p] = 1.0
            sel[p * FOX_HEADS + h, nx + o + 3 + p] = -1.0
            aug_const[0, o + 3 + p] = 1.0
            aug_const[0, nx + o + p] = 1.0
    sel, aug_const = jnp.asarray(sel, BF16), jnp.asarray(aug_const, F32)
    row_map = lambda b, c: (b * nc + c, 0)
    const = lambda b, c: (0, 0)
    out_shape = [jax.ShapeDtypeStruct((m, FOX_HEADS * LANES), BF16 if augment else F32),
                 jax.ShapeDtypeStruct((m, FOX_WIDTH), F32),
                 jax.ShapeDtypeStruct((m, LANES), F32),
                 jax.ShapeDtypeStruct((m, LANES), F32),
                 jax.ShapeDtypeStruct((nb * LANES, nc * tp), F32),
                 jax.ShapeDtypeStruct((nb * LANES, nc * tp), F32)]
    out_specs = [pl.BlockSpec((tc, FOX_HEADS * LANES), row_map),
                 pl.BlockSpec((tc, FOX_WIDTH), row_map),
                 pl.BlockSpec((tc, LANES), row_map),
                 pl.BlockSpec((tc, LANES), row_map),
                 pl.BlockSpec((LANES, tp), lambda b, c: (b, c)),
                 pl.BlockSpec((LANES, tp), lambda b, c: (b, c))]
    if augment:
        out_shape += [jax.ShapeDtypeStruct((m, FOX_HEADS * LANES), BF16), jax.ShapeDtypeStruct((m, FOX_WIDTH), BF16)]
        out_specs += [pl.BlockSpec((tc, FOX_HEADS * LANES), row_map), pl.BlockSpec((tc, FOX_WIDTH), row_map)]
    return pl.pallas_call(
        functools.partial(_prep_kernel, t_valid=t_valid, augment=augment),
        out_shape=tuple(out_shape),
        grid=(nb, nc),
        in_specs=[pl.BlockSpec((tc, 512), lambda b, c: (b * nc + c, COL_FQ)),
                  pl.BlockSpec((tc, 512), lambda b, c: (b * nc + c, COL_FK)),
                  pl.BlockSpec((tc, 512), lambda b, c: (b * nc + c, COL_FV)),
                  pl.BlockSpec((tc, LANES), lambda b, c: (b * nc + c, COL_SMALL * 4)),
                  pl.BlockSpec((1, 512), const), pl.BlockSpec((1, 512), const),
                  pl.BlockSpec((1, LANES), const),
                  pl.BlockSpec((512, 512), const), pl.BlockSpec((tp, tp), const),
                  pl.BlockSpec((LANES, 2 * nx), const), pl.BlockSpec((1, 2 * nx), const)],
        out_specs=tuple(out_specs),
        scratch_shapes=[pltpu.VMEM((1, LANES), F32)],
        compiler_params=_cparams("parallel", "arbitrary"),
        name="gate_prep",
    )(z, z, z, z, gq, gk, bias_row, gmat, tril, sel, aug_const)


S5_SCAN_LANES = 512
S5_SHORT_SEQS = 16


def _cmul(ar, ai, br, bi):
    return ar * br - ai * bi, ar * bi + ai * br


def _s5_kernel(u_ref, h0r_ref, h0i_ref, perm_ref, permt_ref, bre_ref, bim_ref, lam_ref, pseg_ref, pk_ref,
               cre_ref, cim_ref, d_ref, wglu_ref, bglu_ref, y_ref, hlr_ref, hli_ref, hr_ref, hi_ref, cr_ref, ci_ref,
               *, t_last, seqs):
    c = pl.program_id(1)
    nc = pl.num_programs(1)
    tc = u_ref.shape[0]
    R = tc // SUBLANES if seqs == 1 else 1
    half = S5_WIDTH // 2
    hl = S5_LANES // 2

    if seqs == 1:
        @pl.when(c == 0)
        def _():
            cr_ref[...] = h0r_ref[0]
            ci_ref[...] = h0i_ref[0]

    u = u_ref[...]
    ub = _pad_rows(u, MIN_BF16_ROWS).astype(BF16)
    if R > 1:
        ub = _dot(perm_ref[...], ub).astype(BF16)
    for j in range(2):
        uj = ub[:, j * half:(j + 1) * half]
        hr_ref[:, j * hl:(j + 1) * hl] = _dot(uj, bre_ref[j])[0:tc]
        hi_ref[:, j * hl:(j + 1) * hl] = _dot(uj, bim_ref[j])[0:tc]

    sub = lax.broadcasted_iota(jnp.int32, (SUBLANES, S5_SCAN_LANES), 0)
    for lc in range(S5_LANES // S5_SCAN_LANES):
        ls = slice(lc * S5_SCAN_LANES, (lc + 1) * S5_SCAN_LANES)
        lam_r = jnp.broadcast_to(lam_ref[0:1, ls], sub.shape)
        lam_i = jnp.broadcast_to(lam_ref[1:2, ls], sub.shape)
        if seqs > 1:
            def seq_scan(g, _):
                rows = pl.ds(pl.multiple_of(g * SUBLANES, SUBLANES), SUBLANES)
                s_r = jnp.where(sub == 0, jnp.broadcast_to(h0r_ref[0, pl.ds(g, 1), ls], sub.shape), 0.0)
                s_i = jnp.where(sub == 0, jnp.broadcast_to(h0i_ref[0, pl.ds(g, 1), ls], sub.shape), 0.0)
                m_r, m_i = _cmul(lam_r, lam_i, s_r, s_i)
                x_r = hr_ref[rows, ls] + m_r
                x_i = hi_ref[rows, ls] + m_i
                for j, lag in enumerate((1, 2, 4)):
                    m_r, m_i = _cmul(pseg_ref[2 * j, :, ls], pseg_ref[2 * j + 1, :, ls],
                                     pltpu.roll(x_r, lag, 0), pltpu.roll(x_i, lag, 0))
                    x_r, x_i = x_r + m_r, x_i + m_i
                hr_ref[rows, ls] = x_r
                hi_ref[rows, ls] = x_i
                hlr_ref[0, pl.ds(g, 1), ls] = x_r[t_last:t_last + 1, :]
                hli_ref[0, pl.ds(g, 1), ls] = x_i[t_last:t_last + 1, :]
                return 0

            lax.fori_loop(0, seqs, seq_scan, 0, unroll=2)
            continue
        init_r = jnp.where(sub == 0, jnp.broadcast_to(cr_ref[:, ls], sub.shape), 0.0)
        init_i = jnp.where(sub == 0, jnp.broadcast_to(ci_ref[:, ls], sub.shape), 0.0)

        def local_step(k, carry):
            h_r, h_i = carry
            rows = pl.ds(pl.multiple_of(k * SUBLANES, SUBLANES), SUBLANES)
            m_r, m_i = _cmul(lam_r, lam_i, h_r, h_i)
            h_r = m_r + hr_ref[rows, ls]
            h_i = m_i + hi_ref[rows, ls]
            hr_ref[rows, ls] = h_r
            hi_ref[rows, ls] = h_i
            return h_r, h_i

        e_r, e_i = lax.fori_loop(0, R, local_step, (init_r, init_i), unroll=min(R, 4))
        for j, lag in enumerate((1, 2, 4)):
            m_r, m_i = _cmul(pseg_ref[2 * j, :, ls], pseg_ref[2 * j + 1, :, ls],
                             pltpu.roll(e_r, lag, 0), pltpu.roll(e_i, lag, 0))
            e_r, e_i = e_r + m_r, e_i + m_i
        cr_ref[:, ls] = e_r[SUBLANES - 1:SUBLANES, :]
        ci_ref[:, ls] = e_i[SUBLANES - 1:SUBLANES, :]
        in_r = jnp.where(sub == 0, 0.0, pltpu.roll(e_r, 1, 0))
        in_i = jnp.where(sub == 0, 0.0, pltpu.roll(e_i, 1, 0))

        def fix_step(k, _):
            rows = pl.ds(pl.multiple_of(k * SUBLANES, SUBLANES), SUBLANES)
            p_r = jnp.broadcast_to(pk_ref[0, pl.ds(k, 1), ls], sub.shape)
            p_i = jnp.broadcast_to(pk_ref[1, pl.ds(k, 1), ls], sub.shape)
            m_r, m_i = _cmul(p_r, p_i, in_r, in_i)
            hr_ref[rows, ls] += m_r
            hi_ref[rows, ls] += m_i
            return 0

        lax.fori_loop(0, R, fix_step, 0, unroll=min(R, 4))

    if seqs == 1:
        @pl.when(c == nc - 1)
        def _():
            tl = t_last % tc
            pos = (tl % R) * SUBLANES + tl // R
            hlr_ref[0] = hr_ref[pos:pos + 1, :]
            hli_ref[0] = hi_ref[pos:pos + 1, :]

    hrb = _pad_rows(hr_ref[...], MIN_BF16_ROWS).astype(BF16)
    hib = _pad_rows(hi_ref[...], MIN_BF16_ROWS).astype(BF16)
    ys = [(_dot(hrb[:, j * hl:(j + 1) * hl], cre_ref[j]) + _dot(hib[:, j * hl:(j + 1) * hl], cim_ref[j]))
          for j in range(2)]
    y = jnp.concatenate(ys, axis=1)
    if R > 1:
        y = _dot_exact_lhs(permt_ref[...], y)
    y = _gelu_tanh(y[0:tc] + d_ref[...] * u)
    gate = _dot(_pad_rows(y, MIN_BF16_ROWS).astype(BF16), wglu_ref[...])[0:tc]
    y_ref[...] = y * _sigmoid(gate + bglu_ref[...])


def s5_mixer(z, h0r, h0i, sp, nb, t, tc, t_last, seqs=1):
    assert seqs == 1 or (tc == SUBLANES and t == tc)
    nc = t // tc
    R = tc // SUBLANES
    tc = tc * seqs
    nbg = nb // seqs
    tperm = max(tc, MIN_BF16_ROWS) if seqs == 1 else MIN_BF16_ROWS
    perm = np.zeros((tperm, tperm), np.float32)
    if seqs == 1:
        for s in range(SUBLANES):
            for k in range(R):
                perm[k * SUBLANES + s, s * R + k] = 1.0
    permt = jnp.asarray(perm.T, BF16)
    perm = jnp.asarray(perm, BF16)
    const = lambda b, c: (0, 0)
    const3 = lambda b, c: (0, 0, 0)
    state_spec = pl.BlockSpec((1, seqs, S5_LANES), lambda b, c: (b, 0, 0))
    tabs = sp["tabs"][R]
    return pl.pallas_call(
        functools.partial(_s5_kernel, t_last=t_last, seqs=seqs),
        out_shape=(jax.ShapeDtypeStruct((nb * t, S5_WIDTH), F32),
                   jax.ShapeDtypeStruct((nbg, seqs, S5_LANES), F32),
                   jax.ShapeDtypeStruct((nbg, seqs, S5_LANES), F32)),
        grid=(nbg, nc),
        in_specs=[pl.BlockSpec((tc, 512), lambda b, c: (b * nc + c, COL_S5)),
                  state_spec, state_spec,
                  pl.BlockSpec((tperm, tperm), const), pl.BlockSpec((tperm, tperm), const),
                  pl.BlockSpec((2, S5_WIDTH // 2, S5_LANES // 2), const3),
                  pl.BlockSpec((2, S5_WIDTH // 2, S5_LANES // 2), const3),
                  pl.BlockSpec((2, S5_LANES), const),
                  pl.BlockSpec((6, SUBLANES, S5_LANES), const3),
                  pl.BlockSpec((2, R, S5_LANES), const3),
                  pl.BlockSpec((2, S5_LANES // 2, S5_WIDTH // 2), const3),
                  pl.BlockSpec((2, S5_LANES // 2, S5_WIDTH // 2), const3),
                  pl.BlockSpec((1, S5_WIDTH), const),
                  pl.BlockSpec((S5_WIDTH, S5_WIDTH), const), pl.BlockSpec((1, S5_WIDTH), const)],
        out_specs=(pl.BlockSpec((tc, S5_WIDTH), lambda b, c: (b * nc + c, 0)), state_spec, state_spec),
        scratch_shapes=[pltpu.VMEM((tc, S5_LANES), F32), pltpu.VMEM((tc, S5_LANES), F32),
                        pltpu.VMEM((1, S5_LANES), F32), pltpu.VMEM((1, S5_LANES), F32)],
        compiler_params=_cparams("parallel", "arbitrary"),
        name="s5_mixer",
    )(z, h0r.reshape(nbg, seqs, S5_LANES), h0i.reshape(nbg, seqs, S5_LANES), perm, permt, sp["bre"], sp["bim"],
      tabs["lam"],
      tabs["pseg"], tabs["pk"], sp["cre"], sp["cim"], sp["d"], sp["wglu"], sp["bglu"])


def s5_params(a_re, a_im, log_step, b_re, b_im, c_re, c_im, d_skip, w_glu, b_glu, seg_lens):
    dt = jnp.exp(log_step)[:, None]
    mag = jnp.exp(a_re * dt)
    lr = (mag * jnp.cos(a_im * dt)).reshape(1, S5_LANES)
    li = (mag * jnp.sin(a_im * dt)).reshape(1, S5_LANES)
    den = a_re * a_re + a_im * a_im
    xr, xi = lr.reshape(a_re.shape) - 1.0, li.reshape(a_re.shape)
    fr = (xr * a_re + xi * a_im) / den
    fi = (xi * a_re - xr * a_im) / den
    bbr = fr[..., None] * b_re - fi[..., None] * b_im
    bbi = fr[..., None] * b_im + fi[..., None] * b_re
    gh = S5_GROUPS // 2
    eye = jnp.eye(gh, dtype=F32)

    def in_mat(b):
        return jnp.einsum('jgpc,gh->jgchp', b.reshape(2, gh, S5_STATE, S5_GROUP), eye).reshape(
            2, S5_WIDTH // 2, S5_LANES // 2).astype(BF16)

    def out_mat(cm):
        return jnp.einsum('jgcp,gh->jgphc', cm.reshape(2, gh, S5_GROUP, S5_STATE), eye).reshape(
            2, S5_LANES // 2, S5_WIDTH // 2).astype(BF16)

    def powers(pr, pi, n):
        tr, ti, cnt = pr, pi, 1
        while cnt < n:
            lr_, li_ = tr[cnt - 1:cnt], ti[cnt - 1:cnt]
            nr, ni = _cmul(tr, ti, lr_, li_)
            tr, ti, cnt = jnp.concatenate([tr, nr], axis=0), jnp.concatenate([ti, ni], axis=0), 2 * cnt
        return tr, ti

    sub = jnp.arange(SUBLANES)[:, None]
    tabs = {}
    for R in seg_lens:
        kr, ki = powers(lr, li, R)
        sr, si = powers(kr[R - 1:R], ki[R - 1:R], 4)
        pseg = []
        for lag in (1, 2, 4):
            msk = (sub >= lag).astype(F32)
            pseg += [msk * sr[lag - 1:lag], msk * si[lag - 1:lag]]
        tabs[R] = dict(lam=jnp.concatenate([lr, li], axis=0), pseg=jnp.stack(pseg), pk=jnp.stack([kr, ki]))
    return dict(bre=in_mat(bbr), bim=in_mat(bbi), cre=out_mat(c_re), cim=out_mat(-c_im), tabs=tabs,
                d=d_skip.reshape(1, S5_WIDTH), wglu=w_glu.astype(BF16), bglu=b_glu.reshape(1, S5_WIDTH))


def _fox_flash_kernel(it_ref, jt_ref, qx_ref, kx_ref, v_ref, o_ref, m_ref, l_ref, acc_ref, s_ref, p_ref):
    i = it_ref[pl.program_id(1)]
    j = jt_ref[pl.program_id(1)]
    tq = qx_ref.shape[0]
    tk = kx_ref.shape[0]

    @pl.when(j == 0)
    def _():
        m_ref[...] = jnp.full_like(m_ref, -jnp.inf)
        l_ref[...] = jnp.zeros_like(l_ref)
        acc_ref[...] = jnp.zeros_like(acc_ref)

    def step(masked):
        nr, ncol = tq // FLASH_ROWS, tk // LANES
        if masked:
            diff = (lax.broadcasted_iota(jnp.int32, (FLASH_ROWS, LANES), 0) -
                    lax.broadcasted_iota(jnp.int32, (FLASH_ROWS, LANES), 1))
        for h in range(FOX_HEADS):
            hs = slice(h * LANES, (h + 1) * LANES)
            ps = slice((h // 2) * LANES, (h // 2 + 1) * LANES)
            s_buf, p_buf = s_ref.at[h % 2], p_ref.at[h % 2]
            if h == 0:
                s_buf[...] = _dot_nt(qx_ref[:, hs], kx_ref[:, hs])
            if h + 1 < FOX_HEADS:
                nhs = slice((h + 1) * LANES, (h + 2) * LANES)
                s_ref[(h + 1) % 2] = _dot_nt(qx_ref[:, nhs], kx_ref[:, nhs])
            for r in range(nr):
                rs = slice(r * FLASH_ROWS, (r + 1) * FLASH_ROWS)

                live = [cidx for cidx in range(ncol) if not (masked and cidx * LANES >= (r + 1) * FLASH_ROWS)]

                def piece(cidx):
                    sc = s_buf[rs, cidx * LANES:(cidx + 1) * LANES]
                    if masked and (cidx + 1) * LANES - 1 > r * FLASH_ROWS:
                        sc = jnp.where(diff >= (cidx * LANES - r * FLASH_ROWS), sc, -jnp.inf)
                    return sc

                mx = piece(live[0])
                for cidx in live[1:]:
                    mx = jnp.maximum(mx, piece(cidx))
                m_prev = m_ref[h, rs, :]
                m_new = jnp.maximum(m_prev, jnp.max(mx, axis=-1, keepdims=True))
                alpha = jnp.exp(m_prev - m_new)
                lsum = jnp.zeros((FLASH_ROWS, LANES), F32)
                for cidx in range(ncol):
                    cs_ = slice(cidx * LANES, (cidx + 1) * LANES)
                    if cidx not in live:
                        p_buf[rs, cs_] = jnp.zeros((FLASH_ROWS, LANES), BF16)
                        continue
                    pc = jnp.exp(piece(cidx) - m_new)
                    lsum = lsum + pc
                    p_buf[rs, cs_] = pc.astype(BF16)
                l_ref[h, rs, :] = alpha * l_ref[h, rs, :] + lsum
                acc_ref[h, rs, :] = alpha * acc_ref[h, rs, :]
                m_ref[h, rs, :] = m_new
            acc_ref[h] += _dot(p_buf[...], v_ref[:, ps])

    @pl.when(j < i)
    def _():
        step(False)

    @pl.when(j == i)
    def _():
        step(True)
        lane = lax.broadcasted_iota(jnp.int32, (1, LANES), 1)
        for p in range(FOX_HEADS // 2):
            lo = acc_ref[2 * p] / jnp.sum(l_ref[2 * p], axis=-1, keepdims=True)
            hi = acc_ref[2 * p + 1] / jnp.sum(l_ref[2 * p + 1], axis=-1, keepdims=True)
            o_ref[:, p * LANES:(p + 1) * LANES] = jnp.where(lane < FOX_HD, lo, hi)


def fox_flash(qx, kx, vb, nb, t, tile):
    assert tile % FLASH_ROWS == 0
    nt = t // tile
    pairs = [(i, j) for i in range(nt) for j in range(i + 1)]
    it = jnp.asarray([p[0] for p in pairs], jnp.int32)
    jt = jnp.asarray([p[1] for p in pairs], jnp.int32)
    return pl.pallas_call(
        _fox_flash_kernel,
        out_shape=jax.ShapeDtypeStruct((nb * t, FOX_WIDTH), F32),
        grid_spec=pltpu.PrefetchScalarGridSpec(
            num_scalar_prefetch=2,
            grid=(nb, len(pairs)),
            in_specs=[pl.BlockSpec((tile, FOX_HEADS * LANES), lambda b, p, it, jt: (b * nt + it[p], 0)),
                      pl.BlockSpec((tile, FOX_HEADS * LANES), lambda b, p, it, jt: (b * nt + jt[p], 0)),
                      pl.BlockSpec((tile, FOX_WIDTH), lambda b, p, it, jt: (b * nt + jt[p], 0))],
            out_specs=pl.BlockSpec((tile, FOX_WIDTH), lambda b, p, it, jt: (b * nt + it[p], 0)),
            scratch_shapes=[pltpu.VMEM((FOX_HEADS, tile, LANES), F32), pltpu.VMEM((FOX_HEADS, tile, LANES), F32),
                            pltpu.VMEM((FOX_HEADS, tile, LANES), F32),
                            pltpu.VMEM((2, tile, tile), F32), pltpu.VMEM((2, tile, tile), BF16)]),
        compiler_params=_cparams("parallel", "arbitrary"),
        name="fox_flash",
    )(it, jt, qx, kx, vb)


def _fox_decode_kernel(pt_ref, qb_ref, kn_ref, vn_ref, csrow_ref, *rest, n_pages):
    k_refs = rest[0:n_pages]
    v_refs = rest[n_pages:2 * n_pages]
    f_refs = rest[2 * n_pages:3 * n_pages]
    triu_ref, o_ref, kpad_ref, vpad_ref = rest[3 * n_pages:]
    page = kpad_ref.shape[0]
    nrow = FOX_HEADS * SAMPLE_T
    qb = qb_ref[...].astype(BF16)

    def per_query(x):
        return jnp.concatenate([x] * SAMPLE_T, axis=0)

    fcat = jnp.concatenate([f_refs[i][0, 0] for i in range(n_pages)], axis=0)
    cum_in = _dot_exact_rhs(fcat, triu_ref[...])
    totals = jnp.broadcast_to(cum_in[:, page - 1:page], cum_in.shape)
    off = jnp.zeros((FOX_HEADS, page), F32)
    ss = []
    for i in range(n_pages):
        rows = slice(i * FOX_HEADS, (i + 1) * FOX_HEADS)
        kt = k_refs[i][0, 0].reshape(FOX_WIDTH, page).astype(BF16)
        ss.append(_dot(qb, kt) - per_query(cum_in[rows] + off))
        off = off + totals[rows]
    kpad_ref[...] = jnp.zeros_like(kpad_ref)
    vpad_ref[...] = jnp.zeros_like(vpad_ref)
    kpad_ref[0:SAMPLE_T, :] = kn_ref[...]
    vpad_ref[0:SAMPLE_T, :] = vn_ref[...]
    s_new = _dot_nt(qb, kpad_ref[...].astype(BF16)) - per_query(csrow_ref[...] + off)
    tq_idx = lax.broadcasted_iota(jnp.int32, (nrow, page), 0) // FOX_HEADS
    tk_idx = lax.broadcasted_iota(jnp.int32, (nrow, page), 1)
    ss.append(jnp.where(tk_idx <= tq_idx, s_new, -jnp.inf))

    m = ss[0]
    for s in ss[1:]:
        m = jnp.maximum(m, s)
    m = jnp.max(m, axis=-1, keepdims=True)
    lsum = jnp.zeros((nrow, page), F32)
    acc = jnp.zeros((nrow, FOX_WIDTH), F32)
    for i, s in enumerate(ss):
        p = jnp.exp(s - m)
        lsum = lsum + p
        if i < n_pages:
            acc = acc + _dot_nt(p.astype(BF16), v_refs[i][0, 0].reshape(FOX_WIDTH, page).astype(BF16))
        else:
            acc = acc + _dot(p.astype(BF16), vpad_ref[...].astype(BF16))
    acc = acc / jnp.sum(lsum, axis=-1, keepdims=True)
    lane = lax.broadcasted_iota(jnp.int32, (nrow, FOX_WIDTH), 1) // FOX_HD
    head = lax.broadcasted_iota(jnp.int32, (nrow, FOX_WIDTH), 0) % FOX_HEADS
    picked = jnp.where(lane == head, acc, 0.0)
    o_ref[...] = jnp.sum(picked.reshape(SAMPLE_T, FOX_HEADS, FOX_WIDTH), axis=1)


def fox_decode(page_table, qx, kn, z, csrow, pool_kt, pool_vt, pool_ft, layer):
    nb, n_pages = page_table.shape
    page = pool_kt.shape[-1]
    assert page == LANES
    nrow = FOX_HEADS * SAMPLE_T
    pt = page_table.reshape(-1)
    triu = jnp.asarray(np.triu(np.ones((page, page))), BF16)
    pair_of_head = jnp.asarray(np.arange(FOX_HEADS)[:, None] // 2 == np.arange(FOX_HEADS // 2)[None, :], F32)
    qb = (qx.reshape(nb, SAMPLE_T, FOX_HEADS, 1, LANES) * pair_of_head[None, None, :, :, None]).reshape(
        nb * nrow, FOX_WIDTH)

    def pg5(i):
        return lambda b, pt: (layer, pt[b * n_pages + i], 0, 0, 0)

    def pg4(i):
        return lambda b, pt: (layer, pt[b * n_pages + i], 0, 0)

    row = lambda b, pt: (b, 0)
    in_specs = [pl.BlockSpec((nrow, FOX_WIDTH), row),
                pl.BlockSpec((SAMPLE_T, FOX_WIDTH), row),
                pl.BlockSpec((SAMPLE_T, FOX_WIDTH), lambda b, pt: (b, COL_FV)),
                pl.BlockSpec((FOX_HEADS, page), lambda b, pt: (b * (LANES // FOX_HEADS), 0))]
    in_specs += [pl.BlockSpec((1, 1, FOX_HEADS, FOX_HD, page), pg5(i)) for i in range(n_pages)]
    in_specs += [pl.BlockSpec((1, 1, FOX_HEADS, FOX_HD, page), pg5(i)) for i in range(n_pages)]
    in_specs += [pl.BlockSpec((1, 1, FOX_HEADS, page), pg4(i)) for i in range(n_pages)]
    in_specs += [pl.BlockSpec((page, page), lambda b, pt: (0, 0))]
    return pl.pallas_call(
        functools.partial(_fox_decode_kernel, n_pages=n_pages),
        out_shape=jax.ShapeDtypeStruct((nb * SAMPLE_T, FOX_WIDTH), F32),
        grid_spec=pltpu.PrefetchScalarGridSpec(
            num_scalar_prefetch=1,
            grid=(nb,),
            in_specs=in_specs,
            out_specs=pl.BlockSpec((SAMPLE_T, FOX_WIDTH), row),
            scratch_shapes=[pltpu.VMEM((page, FOX_WIDTH), F32), pltpu.VMEM((page, FOX_WIDTH), F32)]),
        compiler_params=_cparams("parallel"),
        name="fox_decode",
    )(pt, qb, kn, z, csrow, *([pool_kt] * n_pages), *([pool_vt] * n_pages), *([pool_ft] * n_pages), triu)


def _mlstm_kernel(q_ref, k_ref, v_ref, o_ref, g_ref, cs_ref, grow_ref, csrow_ref, c0_ref, n0_ref, m0_ref, gn_ref,
                  y_ref, cout_ref, nout_ref, mout_ref, c_ref, n_ref, m_ref):
    c = pl.program_id(1)
    nc = pl.num_programs(1)
    L = q_ref.shape[0]

    @pl.when(c == 0)
    def _():
        c_ref[...] = c0_ref[0]
        n_ref[...] = n0_ref[0]
        m_ref[...] = m0_ref[0]

    Lk = grow_ref.shape[1]
    Lq = max(L, MIN_BF16_ROWS)
    g = g_ref[...]
    cs = cs_ref[...]
    grow = grow_ref[...]
    csrow = csrow_ref[...]
    causal = (lax.broadcasted_iota(jnp.int32, (Lq, Lk), 0) >= lax.broadcasted_iota(jnp.int32, (Lq, Lk), 1))
    for h in range(ML_HEADS):
        hs = slice(h * ML_HD, (h + 1) * ML_HD)
        scale = ML_HD ** -0.5
        qb = _pad_rows(q_ref[:, hs], Lq).astype(BF16)
        kb = _pad_rows(k_ref[:, hs], Lk).astype(BF16)
        vb = _pad_rows(v_ref[:, hs], Lk).astype(BF16)
        qh, kh, vh = qb.astype(F32), kb.astype(F32), vb.astype(F32)
        bcol_k = _pad_rows(cs[:, LANE_MLF + h:LANE_MLF + h + 1], Lk)
        bcol = bcol_k[0:Lq]
        icol = _pad_rows(g[:, LANE_MLI + h:LANE_MLI + h + 1], Lk, -jnp.inf)
        brow = csrow[LANE_MLF + h:LANE_MLF + h + 1, :]
        irow = grow[LANE_MLI + h:LANE_MLI + h + 1, :]
        m_prev = m_ref[0:1, h:h + 1]
        log_d = jnp.where(causal, bcol - brow + irow, -jnp.inf)
        log_inter = bcol + m_prev
        m_t = jnp.maximum(log_inter, jnp.max(log_d, axis=-1, keepdims=True))
        d_w = jnp.exp(log_d - m_t)
        inter_w = jnp.exp(log_inter - m_t)
        s = _dot_nt(qb, kb) * (d_w * scale)
        ch = c_ref[h]
        n_row = n_ref[h:h + 1, :]
        num = _dot(s.astype(BF16), vb) + inter_w * _dot_nt(qb, ch.astype(BF16))
        den = jnp.sum(s, axis=-1, keepdims=True) + inter_w * jnp.sum(qh * n_row, axis=-1, keepdims=True)
        hh = num / jnp.maximum(jnp.abs(den), jnp.exp(-m_t))
        y_ref[:, hs] = _rms(hh[0:L], gn_ref[...]) * _sigmoid(o_ref[:, hs])
        m_end = m_t[L - 1:L, :]
        a_end = inter_w[L - 1:L, :]
        w_col = jnp.exp(bcol[L - 1:L, :] - bcol_k + icol - m_end) * scale
        c_ref[h] = a_end * ch + _dot_tn((vh * w_col).astype(BF16), kb)
        n_ref[h:h + 1, :] = a_end * n_row + jnp.sum(kh * w_col, axis=0, keepdims=True)
        m_ref[0:1, h:h + 1] = m_end

    @pl.when(c == nc - 1)
    def _():
        cout_ref[0] = c_ref[...]
        nout_ref[0] = n_ref[...]
        mout_ref[0] = m_ref[...]


def mlstm(za, zb, g, cs, grow, csrow, c0, n0, m0, gn, nb, t, L):
    nc = t // L
    rows = lambda b, c: (b * nc + c, 0)
    rr = 2 * SUBLANES
    m0p = jnp.zeros((nb, 1, LANES), F32).at[:, 0, :ML_HEADS].set(m0)
    outs = pl.pallas_call(
        _mlstm_kernel,
        out_shape=(jax.ShapeDtypeStruct((nb * t, ML_WIDTH), F32),
                   jax.ShapeDtypeStruct((nb, ML_HEADS, ML_HD, ML_HD), F32),
                   jax.ShapeDtypeStruct((nb, ML_HEADS, ML_HD), F32),
                   jax.ShapeDtypeStruct((nb, 1, LANES), F32)),
        grid=(nb, nc),
        in_specs=[pl.BlockSpec((L, 512), lambda b, c: (b * nc + c, COL_MQ)),
                  pl.BlockSpec((L, 512), lambda b, c: (b * nc + c, COL_MK)),
                  pl.BlockSpec((L, 512), lambda b, c: (b * nc + c, COL_MV)),
                  pl.BlockSpec((L, 512), lambda b, c: (b * nc + c, COL_MO)),
                  pl.BlockSpec((L, LANES), rows), pl.BlockSpec((L, LANES), rows),
                  pl.BlockSpec((rr, max(L, LANES)), lambda b, c: (b * (LANES // rr), c)),
                  pl.BlockSpec((rr, max(L, LANES)), lambda b, c: (b * (LANES // rr), c)),
                  pl.BlockSpec((1, ML_HEADS, ML_HD, ML_HD), lambda b, c: (b, 0, 0, 0)),
                  pl.BlockSpec((1, ML_HEADS, ML_HD), lambda b, c: (b, 0, 0)),
                  pl.BlockSpec((1, 1, LANES), lambda b, c: (b, 0, 0)),
                  pl.BlockSpec((1, ML_HD), lambda b, c: (0, 0))],
        out_specs=(pl.BlockSpec((L, ML_WIDTH), rows),
                   pl.BlockSpec((1, ML_HEADS, ML_HD, ML_HD), lambda b, c: (b, 0, 0, 0)),
                   pl.BlockSpec((1, ML_HEADS, ML_HD), lambda b, c: (b, 0, 0)),
                   pl.BlockSpec((1, 1, LANES), lambda b, c: (b, 0, 0))),
        scratch_shapes=[pltpu.VMEM((ML_HEADS, ML_HD, ML_HD), F32), pltpu.VMEM((ML_HEADS, ML_HD), F32),
                        pltpu.VMEM((1, LANES), F32)],
        compiler_params=_cparams("parallel", "arbitrary"),
        name="mlstm",
    )(zb, zb, zb, za, g, cs, grow, csrow, c0, n0, m0p, gn.reshape(1, ML_HD))
    y, c_new, n_new, m_new = outs
    return y, c_new, n_new, m_new[:, 0, :ML_HEADS]


def _merge_kernel(x_ref, g_ref, wg_ref, ys_ref, yf_ref, ym_ref, ws_ref, wf_ref, wm_ref, wo_ref, o_ref):
    x = x_ref[...]
    hn = _rms(x, g_ref[...]).astype(BF16)
    branches = (ys_ref, ws_ref), (yf_ref, wf_ref), (ym_ref, wm_ref)
    merged = None
    for b, (y_ref, w_ref) in enumerate(branches):
        gate = _sigmoid(_dot(hn, wg_ref[:, b * D_MODEL:(b + 1) * D_MODEL]))
        term = gate * _dot(y_ref[...].astype(BF16), w_ref[...])
        merged = term if merged is None else merged + term
    o_ref[...] = x + _dot(merged.astype(BF16), wo_ref[...])


def merge_out(x, g_mix, w_gates, ys, yf, ym, ws, wf, wm, wo, tm=256):
    m = x.shape[0]
    tm = min(tm, m)
    row = lambda i: (i, 0)
    const = lambda i: (0, 0)
    return pl.pallas_call(
        _merge_kernel,
        out_shape=jax.ShapeDtypeStruct((m, D_MODEL), F32),
        grid=(m // tm,),
        in_specs=[pl.BlockSpec((tm, D_MODEL), row), pl.BlockSpec((1, D_MODEL), const),
                  pl.BlockSpec((D_MODEL, 3 * D_MODEL), const),
                  pl.BlockSpec((tm, 512), row), pl.BlockSpec((tm, 512), row), pl.BlockSpec((tm, 512), row),
                  pl.BlockSpec((512, D_MODEL), const), pl.BlockSpec((512, D_MODEL), const),
                  pl.BlockSpec((512, D_MODEL), const), pl.BlockSpec((D_MODEL, D_MODEL), const)],
        out_specs=pl.BlockSpec((tm, D_MODEL), row),
        compiler_params=_cparams("parallel"),
        name="merge_out",
    )(x, g_mix.reshape(1, D_MODEL), w_gates, ys, yf, ym, ws, wf, wm, wo)


def _cross_block_kernel(x_ref, g_ref, wq_ref, gq_ref, k_ref, v_ref, wo_ref, o_ref):
    x = x_ref[...]
    q = _dot(_rms(x, g_ref[...]).astype(BF16), wq_ref[...])
    outs = []
    for h in range(MEM_HEADS):
        hs = slice(h * MEM_HD, (h + 1) * MEM_HD)
        qh = _rms(q[:, hs], gq_ref[...]).astype(BF16)
        s = _dot_nt(qh, k_ref[:, hs].astype(BF16)) * (MEM_HD ** -0.5)
        m = jnp.max(s, axis=-1, keepdims=True)
        p = jnp.exp(s - m)
        l = jnp.sum(p, axis=-1, keepdims=True)
        outs.append((_dot(p.astype(BF16), v_ref[:, hs].astype(BF16)) / l).astype(BF16))
    o_ref[...] = x + _dot(jnp.concatenate(outs, axis=1), wo_ref[...])


def cross_block(x, g, wq, gq, mem_k, mem_v, wo, nb, t, tq):
    nq = t // tq
    n_mem = mem_k.shape[0] // nb
    const = lambda b, i: (0, 0)
    row = lambda b, i: (b * nq + i, 0)
    return pl.pallas_call(
        _cross_block_kernel,
        out_shape=jax.ShapeDtypeStruct((nb * t, D_MODEL), F32),
        grid=(nb, nq),
        in_specs=[pl.BlockSpec((tq, D_MODEL), row), pl.BlockSpec((1, D_MODEL), const),
                  pl.BlockSpec((D_MODEL, MEM_WIDTH), const), pl.BlockSpec((1, MEM_HD), const),
                  pl.BlockSpec((n_mem, MEM_WIDTH), lambda b, i: (b, 0)),
                  pl.BlockSpec((n_mem, MEM_WIDTH), lambda b, i: (b, 0)),
                  pl.BlockSpec((MEM_WIDTH, D_MODEL), const)],
        out_specs=pl.BlockSpec((tq, D_MODEL), row),
        compiler_params=_cparams("parallel", "parallel"),
        name="cross_block",
    )(x, g.reshape(1, D_MODEL), wq, gq.reshape(1, MEM_HD), mem_k, mem_v, wo)


def _cross_cached_kernel(q_ref, k_ref, v_ref, o_ref):
    nseq = k_ref.shape[1]
    tq = q_ref.shape[0] // nseq
    for sq in range(nseq):
        q = q_ref[sq * tq:(sq + 1) * tq, :]
        qr = jnp.concatenate([q[:, h * MEM_HD:(h + 1) * MEM_HD] for h in range(MEM_HEADS)], axis=0)
        kf = k_ref[0, sq].astype(BF16)
        vf = v_ref[0, sq].astype(BF16)
        s = _dot_nt(qr.astype(BF16), kf) * (MEM_HD ** -0.5)
        row_head = lax.broadcasted_iota(jnp.int32, s.shape, 0) // tq
        col_head = lax.broadcasted_iota(jnp.int32, s.shape, 1) % MEM_HEADS
        s = jnp.where(row_head == col_head, s, -jnp.inf)
        m = jnp.max(s, axis=-1, keepdims=True)
        p = jnp.exp(s - m)
        l = jnp.sum(p, axis=-1, keepdims=True)
        o = _dot(p.astype(BF16), vf) / l
        for h in range(MEM_HEADS):
            o_ref[sq * tq:(sq + 1) * tq, h * MEM_HD:(h + 1) * MEM_HD] = o[h * tq:(h + 1) * tq]


CROSS_CACHED_SEQS = 4


def cross_attend_cached(q, mem_k, mem_v, nb, t, layer):
    ns = CROSS_CACHED_SEQS
    kv_spec = pl.BlockSpec((1, ns, mem_k.shape[2], MEM_HD), lambda b: (layer, b, 0, 0))
    return pl.pallas_call(
        _cross_cached_kernel,
        out_shape=jax.ShapeDtypeStruct((nb * t, MEM_WIDTH), F32),
        grid=(nb // ns,),
        in_specs=[pl.BlockSpec((ns * t, MEM_WIDTH), lambda b: (b, 0)), kv_spec, kv_spec],
        out_specs=pl.BlockSpec((ns * t, MEM_WIDTH), lambda b: (b, 0)),
        compiler_params=_cparams("parallel"),
        name="cross_attend_cached",
    )(q, mem_k, mem_v)


def _proj_residual_kernel(x_ref, a_ref, w_ref, o_ref):
    o_ref[...] = x_ref[...] + _dot(a_ref[...].astype(BF16), w_ref[...])


def proj_residual(x, a, w, tm=512):
    m, n = x.shape
    k = a.shape[1]
    tm = min(tm, m)
    return pl.pallas_call(
        _proj_residual_kernel,
        out_shape=jax.ShapeDtypeStruct((m, n), F32),
        grid=(m // tm,),
        in_specs=[pl.BlockSpec((tm, n), lambda i: (i, 0)), pl.BlockSpec((tm, k), lambda i: (i, 0)),
                  pl.BlockSpec((k, n), lambda i: (0, 0))],
        out_specs=pl.BlockSpec((tm, n), lambda i: (i, 0)),
        compiler_params=_cparams("parallel"),
        name="proj_residual",
    )(x, a, w)


def _mlp_kernel(x_ref, g_ref, wu_ref, wd_ref, o_ref, hn_ref, acc_ref):
    f = pl.program_id(1)

    @pl.when(f == 0)
    def _():
        hn_ref[...] = _rms(x_ref[...], g_ref[...]).astype(BF16)
        acc_ref[...] = jnp.zeros_like(acc_ref)

    a = jnp.maximum(_dot(hn_ref[...], wu_ref[...]), 0.0)
    acc_ref[...] += _dot((a * a).astype(BF16), wd_ref[...])

    @pl.when(f == pl.num_programs(1) - 1)
    def _():
        o_ref[...] = x_ref[...] + acc_ref[...]


def mlp(x, g, wu, wd, tm=1024, tf=512):
    m, d = x.shape
    dff = wu.shape[1]
    tm = min(tm, m)
    return pl.pallas_call(
        _mlp_kernel,
        out_shape=jax.ShapeDtypeStruct((m, d), F32),
        grid=(m // tm, dff // tf),
        in_specs=[pl.BlockSpec((tm, d), lambda i, f: (i, 0)), pl.BlockSpec((1, d), lambda i, f: (0, 0)),
                  pl.BlockSpec((d, tf), lambda i, f: (0, f)), pl.BlockSpec((tf, d), lambda i, f: (f, 0))],
        out_specs=pl.BlockSpec((tm, d), lambda i, f: (i, 0)),
        scratch_shapes=[pltpu.VMEM((tm, d), BF16), pltpu.VMEM((tm, d), F32)],
        compiler_params=_cparams("parallel", "arbitrary"),
        name="mlp",
    )(x, g.reshape(1, d), wu, wd)


def _pack_w_in(w_in):
    offs = np.concatenate([[0], np.cumsum(SPLITS)])
    col = lambda i: w_in[:, int(offs[i]):int(offs[i + 1])]
    s5, fq, fk, fv, ff, mq, mk, mv, mi, mf, mo, gates = [col(i) for i in range(12)]
    pad = jnp.zeros((w_in.shape[0], 512 - FOX_HEADS - 2 * ML_HEADS), w_in.dtype)
    w_a = jnp.concatenate([s5, fq, fk, fv, mo, ff, mi, mf, pad], axis=1).astype(BF16)
    w_b = jnp.concatenate([mq, mk, mv], axis=1).astype(BF16)
    return w_a, w_b, gates.astype(BF16)


def _layer_weights(l, g_mix, w_in, s5_a_re, s5_a_im, s5_log_step, s5_b_re, s5_b_im, s5_c_re, s5_c_im, s5_d,
                   s5_w_glu, s5_b_glu, fox_gq, fox_gk, fox_bf, ml_bi, ml_bf, ml_gn, w_br_s5, w_br_fox, w_br_ml,
                   w_out, g_cross, w_cq, cross_gq, g_mem, w_mk, w_mv, cross_gk, w_co, g_mlp, w_up, w_down):
    bias_row = jnp.zeros((1, LANES), F32)
    bias_row = bias_row.at[0, LANE_FOXF:LANE_FOXF + FOX_HEADS].set(fox_bf[l])
    bias_row = bias_row.at[0, LANE_MLI:LANE_MLI + ML_HEADS].set(ml_bi[l])
    bias_row = bias_row.at[0, LANE_MLF:LANE_MLF + ML_HEADS].set(ml_bf[l])
    w_a, w_b, w_gates = _pack_w_in(w_in[l])
    return dict(
        g_mix=g_mix[l], w_a=w_a, w_b=w_b, w_gates=w_gates,
        s5=s5_params(s5_a_re[l], s5_a_im[l], s5_log_step[l], s5_b_re[l], s5_b_im[l], s5_c_re[l], s5_c_im[l],
                     s5_d[l], s5_w_glu[l], s5_b_glu[l], (SEQ_TILE // SUBLANES, SAMPLE_T // SUBLANES)),
        gq=jnp.tile(fox_gq[l], FOX_HEADS).reshape(1, FOX_WIDTH),
        gk=jnp.tile(fox_gk[l], FOX_HEADS).reshape(1, FOX_WIDTH),
        bias_row=bias_row, ml_gn=ml_gn[l],
        w_br_s5=w_br_s5[l].astype(BF16), w_br_fox=w_br_fox[l].astype(BF16), w_br_ml=w_br_ml[l].astype(BF16),
        w_out=w_out[l].astype(BF16), g_cross=g_cross[l], w_cq=w_cq[l].astype(BF16), cross_gq=cross_gq[l],
        g_mem=g_mem[l], w_mk=w_mk[l].astype(BF16), w_mv=w_mv[l].astype(BF16), cross_gk=cross_gk[l],
        w_co=w_co[l].astype(BF16), g_mlp=g_mlp[l], w_up=w_up[l].astype(BF16), w_down=w_down[l].astype(BF16))


def _hybrid_layer(x, W, nb, t, seq_tile, t_valid, s5_state, ml_state, mem_k, mem_v, fox_attend, augment,
                  mem_layer=None):
    z = norm_matmul(x, W["g_mix"], W["w_a"], tm=min(1024, nb * t))
    zb = norm_matmul(x, W["g_mix"], W["w_b"], tm=min(1024, nb * t),
                     out_dtype=BF16 if seq_tile % MIN_BF16_ROWS == 0 else F32)
    prep = gate_prep(z, W["gq"], W["gk"], W["bias_row"], nb, t, seq_tile, t_valid, augment)
    kn, g, cs, grow, csrow = prep[1:6]
    y_s5, s5_re, s5_im = s5_mixer(z, s5_state[0], s5_state[1], W["s5"], nb, t, seq_tile, t_valid - 1,
                                  seqs=S5_SHORT_SEQS if t == SUBLANES else 1)
    y_fox = fox_attend(prep, z)
    y_ml, c_new, n_new, m_new = mlstm(z, zb, g, cs, grow, csrow, ml_state[0], ml_state[1], ml_state[2], W["ml_gn"],
                                      nb, t, seq_tile)
    x = merge_out(x, W["g_mix"], W["w_gates"], y_s5, y_fox, y_ml, W["w_br_s5"], W["w_br_fox"], W["w_br_ml"],
                  W["w_out"])
    if mem_layer is None:
        x = cross_block(x, W["g_cross"], W["w_cq"], W["cross_gq"], mem_k, mem_v, W["w_co"], nb, t, seq_tile)
    else:
        qc = norm_matmul(x, W["g_cross"], W["w_cq"], head_gain=W["cross_gq"])
        oc = cross_attend_cached(qc, mem_k, mem_v, nb, t, mem_layer)
        x = proj_residual(x, oc, W["w_co"])
    x = mlp(x, W["g_mlp"], W["w_up"], W["w_down"])
    return x, z, kn, g, s5_re.reshape(nb, S5_GROUPS, S5_STATE), s5_im.reshape(nb, S5_GROUPS, S5_STATE), \
        c_new, n_new, m_new


def kernel(x_prompt, x_sample, mem_prompt, cache_fox_k, cache_fox_v, cache_fox_logf, page_table, state_s5_re, state_s5_im, state_mlstm_C, state_mlstm_n, state_mlstm_m, cache_mem_k, cache_mem_v, g_mix, w_in, s5_a_re, s5_a_im, s5_log_step, s5_b_re, s5_b_im, s5_c_re, s5_c_im, s5_d, s5_w_glu, s5_b_glu, fox_gq, fox_gk, fox_bf, ml_bi, ml_bf, ml_gn, w_br_s5, w_br_fox, w_br_ml, w_out, g_cross, w_cq, cross_gq, g_mem, w_mk, w_mv, cross_gk, w_co, g_mlp, w_up, w_down):
    depth = w_in.shape[0]
    bp, tp, _ = x_prompt.shape
    bs, ts, _ = x_sample.shape
    n_mem = mem_prompt.shape[1]

    xp = x_prompt.reshape(bp * tp, D_MODEL)
    xs = jnp.pad(x_sample, ((0, 0), (0, SAMPLE_T - ts), (0, 0))).reshape(bs * SAMPLE_T, D_MODEL)
    mem = mem_prompt.reshape(bp * n_mem, D_MODEL)
    zeros_p = (jnp.zeros((bp, S5_LANES), F32), jnp.zeros((bp, S5_LANES), F32))
    zeros_ml = (jnp.zeros((bp, ML_HEADS, ML_HD, ML_HD), F32), jnp.zeros((bp, ML_HEADS, ML_HD), F32),
                jnp.zeros((bp, ML_HEADS), F32))
    pool_kt = jnp.transpose(cache_fox_k, (0, 1, 3, 4, 2))
    pool_vt = jnp.transpose(cache_fox_v, (0, 1, 3, 4, 2))
    pool_ft = jnp.transpose(cache_fox_logf, (0, 1, 3, 2))
    st_p, st_s = [], []
    for l in range(depth):
        W = _layer_weights(l, g_mix, w_in, s5_a_re, s5_a_im, s5_log_step, s5_b_re, s5_b_im, s5_c_re, s5_c_im, s5_d,
                           s5_w_glu, s5_b_glu, fox_gq, fox_gk, fox_bf, ml_bi, ml_bf, ml_gn, w_br_s5, w_br_fox,
                           w_br_ml, w_out, g_cross, w_cq, cross_gq, g_mem, w_mk, w_mv, cross_gk, w_co, g_mlp, w_up,
                           w_down)
        mk_p = norm_matmul(mem, W["g_mem"], W["w_mk"], head_gain=W["cross_gk"])
        mv_p = norm_matmul(mem, W["g_mem"], W["w_mv"])

        def flash(prep, z):
            return fox_flash(prep[0], prep[6], prep[7], bp, tp, FLASH_TILE)

        xp, z, kn, g, s5r, s5i, c_new, n_new, m_new = _hybrid_layer(
            xp, W, bp, tp, SEQ_TILE, tp, zeros_p, zeros_ml, mk_p, mv_p, flash, True)
        st_p.append((kn.reshape(bp, tp, FOX_HEADS, FOX_HD),
                     z[:, COL_FV * 512:(COL_FV + 1) * 512].reshape(bp, tp, FOX_HEADS, FOX_HD),
                     g[:, LANE_FOXF:LANE_FOXF + FOX_HEADS].reshape(bp, tp, FOX_HEADS),
                     s5r, s5i, c_new, n_new, m_new,
                     mk_p.reshape(bp, n_mem, MEM_HEADS, MEM_HD), mv_p.reshape(bp, n_mem, MEM_HEADS, MEM_HD)))

        def decode(prep, z, layer=l):
            return fox_decode(page_table, prep[0], prep[1], z, prep[5], pool_kt, pool_vt, pool_ft, layer)

        xs, z, kn, g, s5r, s5i, c_new, n_new, m_new = _hybrid_layer(
            xs, W, bs, SAMPLE_T, SAMPLE_T, ts,
            (state_s5_re[l].reshape(bs, S5_LANES), state_s5_im[l].reshape(bs, S5_LANES)),
            (state_mlstm_C[l], state_mlstm_n[l], state_mlstm_m[l]),
            cache_mem_k.reshape(depth, bs, n_mem * MEM_HEADS, MEM_HD),
            cache_mem_v.reshape(depth, bs, n_mem * MEM_HEADS, MEM_HD), decode, False, mem_layer=l)
        st_s.append((kn.reshape(bs, SAMPLE_T, FOX_HEADS, FOX_HD)[:, :ts],
                     z[:, COL_FV * 512:(COL_FV + 1) * 512].reshape(bs, SAMPLE_T, FOX_HEADS, FOX_HD)[:, :ts],
                     g[:, LANE_FOXF:LANE_FOXF + FOX_HEADS].reshape(bs, SAMPLE_T, FOX_HEADS)[:, :ts],
                     s5r, s5i, c_new, n_new, m_new))
    outs_p = [jnp.stack(a) for a in zip(*st_p)]
    outs_s = [jnp.stack(a) for a in zip(*st_s)]
    yp = xp.reshape(bp, tp, D_MODEL)
    ys = xs.reshape(bs, SAMPLE_T, D_MODEL)[:, :ts]
    return (yp, ys, *outs_p, *outs_s)
```

```python
import functools
import math

import jax
import jax.numpy as jnp
import numpy as np
from jax import lax
from jax.experimental import pallas as pl
from jax.experimental.pallas import tpu as pltpu

F32 = jnp.float32
BF16 = jnp.bfloat16

LANES = 128
SUBLANES = 8
MIN_BF16_ROWS = 16
MXU_DIM = 256
VMEM_LIMIT_BYTES = 48 * 1024 * 1024

D_MODEL = 1024
S5_WIDTH = 512
S5_GROUP = 16
S5_GROUPS = 32
S5_STATE = 64
S5_LANES = S5_GROUPS * S5_STATE
FOX_HEADS = 8
FOX_HD = 64
FOX_WIDTH = 512
ML_HEADS = 4
ML_HD = 128
ML_WIDTH = 512
MEM_HEADS = 4
MEM_HD = 128
MEM_WIDTH = 512
D_FF = 4096
EPS = 1e-6
LOG2E = math.log2(math.e)
SPLITS =(S5_WIDTH, FOX_WIDTH, FOX_WIDTH, FOX_WIDTH, FOX_HEADS, ML_WIDTH, ML_WIDTH, ML_WIDTH,
          ML_HEADS, ML_HEADS, ML_WIDTH, 3 * D_MODEL)

COL_S5, COL_FQ, COL_FK, COL_FV, COL_MO, COL_SMALL = range(6)
COL_MQ, COL_MK, COL_MV = range(3)
LANE_FOXF = 0
LANE_MLI = 8
LANE_MLF = 12
SEQ_TILE = 256
FLASH_TILE = 512
FLASH_ROWS = 128
SAMPLE_T = 8


def _cparams(*sem):
    return pltpu.CompilerParams(dimension_semantics=sem, vmem_limit_bytes=VMEM_LIMIT_BYTES)


def _dot(a, b):
    return jnp.dot(a, b, preferred_element_type=F32)


def _dot_nt(a, b):
    return lax.dot_general(a, b, (((1,), (1,)), ((), ())), preferred_element_type=F32)


def _dot_tn(a, b):
    return lax.dot_general(a, b, (((0,), (0,)), ((), ())), preferred_element_type=F32)


def _split3(x):
    hi = x.astype(BF16)
    r1 = x - hi.astype(F32)
    mid = r1.astype(BF16)
    lo = (r1 - mid.astype(F32)).astype(BF16)
    return hi, mid, lo


def _dot_exact_rhs(x, ones_rhs):
    hi, mid, lo = _split3(x)
    return _dot(hi, ones_rhs) + _dot(mid, ones_rhs) + _dot(lo, ones_rhs)


def _dot_exact_lhs(ones_lhs, x):
    hi, mid, lo = _split3(x)
    return _dot(ones_lhs, hi) + _dot(ones_lhs, mid) + _dot(ones_lhs, lo)


def _pad_rows(x, n, fill=0.0):
    if x.shape[0] >= n:
        return x
    return jnp.concatenate([x, jnp.full((n - x.shape[0], x.shape[1]), fill, x.dtype)], axis=0)


def _lane_tile(x, n):
    return x if n == 1 else jnp.concatenate([x] * n, axis=1)


def _log_sigmoid(a):
    return jnp.minimum(a, 0.0) - jnp.log1p(jnp.exp(-jnp.abs(a)))


def _sigmoid(a):
    return 1.0 / (1.0 + jnp.exp(-a))


def _gelu_tanh(x):
    c = math.sqrt(2.0 / math.pi)
    return 0.5 * x * (1.0 + jnp.tanh(c * (x + 0.044715 * (x * x * x))))


def _rms(x, g):
    ms = jnp.mean(x * x, axis=-1, keepdims=True)
    return x * lax.rsqrt(ms + EPS) * g


def _norm_matmul_kernel(x_ref, g_ref, w_ref, *rest, head_norm):
    if head_norm:
        hg_ref, o_ref, hn_ref = rest
    else:
        o_ref, hn_ref = rest

    @pl.when(pl.program_id(1) == 0)
    def _():
        hn_ref[...] = _rms(x_ref[...], g_ref[...]).astype(BF16)

    y = _dot(hn_ref[...], w_ref[...])
    if head_norm:
        tn = y.shape[1]
        for s in range(tn // LANES):
            sl = slice(s * LANES, (s + 1) * LANES)
            o_ref[:, sl] = _rms(y[:, sl], hg_ref[...]).astype(o_ref.dtype)
    else:
        o_ref[...] = y.astype(o_ref.dtype)


def norm_matmul(x, g, w, head_gain=None, out_dtype=F32, tm=512, tn=512):
    m, k = x.shape
    n = w.shape[1]
    tm = min(tm, m)
    tn = min(tn, n)
    head_norm = head_gain is not None
    in_specs = [pl.BlockSpec((tm, k), lambda i, j: (i, 0)),
                pl.BlockSpec((1, k), lambda i, j: (0, 0)),
                pl.BlockSpec((k, tn), lambda i, j: (0, j))]
    args = [x, g.reshape(1, k), w]
    if head_norm:
        in_specs.append(pl.BlockSpec((1, LANES), lambda i, j: (0, 0)))
        args.append(head_gain.reshape(1, LANES))
    return pl.pallas_call(
        functools.partial(_norm_matmul_kernel, head_norm=head_norm),
        out_shape=jax.ShapeDtypeStruct((m, n), out_dtype),
        grid=(m // tm, n // tn),
        in_specs=in_specs,
        out_specs=pl.BlockSpec((tm, tn), lambda i, j: (i, j)),
        scratch_shapes=[pltpu.VMEM((tm, k), BF16)],
        compiler_params=_cparams("parallel", "arbitrary"),
        name="norm_matmul",
    )(*args)


def _prep_kernel(fq_ref, fk_ref, fv_ref, sm_ref, gq_ref, gk_ref, bias_ref, gmat_ref, tril_ref, sel_ref, aug_ref,
                 *rest, seq_rows, t_valid, augment):
    if augment:
        qx_ref, kn_ref, g_ref, cs_ref, grow_ref, csrow_ref, kx_ref, vb_ref, carry_ref = rest
    else:
        qx_ref, kn_ref, g_ref, cs_ref, carry_ref = rest
    c = pl.program_id(1)
    tc = sm_ref.shape[0]

    @pl.when(c == 0)
    def _():
        carry_ref[...] = jnp.zeros_like(carry_ref)

    gmat = gmat_ref[...]

    def head_rms(x, gain):
        x2 = x * x
        hi = x2.astype(BF16)
        lo = (x2 - hi.astype(F32)).astype(BF16)
        ss = (_dot(hi, gmat) + _dot(lo, gmat)) * (1.0 / FOX_HD)
        return x * lax.rsqrt(ss + EPS) * gain

    qn = head_rms(fq_ref[...], gq_ref[...]) * ((FOX_HD ** -0.5) * (LOG2E if augment else 1.0))
    kn = head_rms(fk_ref[...], gk_ref[...])
    kn_ref[...] = kn

    a = sm_ref[...] + bias_ref[...]
    lane = lax.broadcasted_iota(jnp.int32, a.shape, 1)
    pos = (lax.broadcasted_iota(jnp.int32, a.shape, 0) + c * tc) % seq_rows
    is_i = (lane >= LANE_MLI) & (lane < LANE_MLF)
    used = lane < LANE_MLF + ML_HEADS
    valid = pos < t_valid
    g = jnp.where(is_i, a, _log_sigmoid(a))
    g = jnp.where(used, g, 0.0)
    g = jnp.where(valid, g, jnp.where(is_i, -jnp.inf, 0.0))
    gc = jnp.where(is_i, 0.0, g)
    cs = _dot_exact_lhs(tril_ref[...], gc)
    carry = carry_ref[...]
    csg = cs + jnp.where(lane < FOX_HEADS, carry, 0.0)
    carry_ref[...] = carry + cs[tc - 1:tc, :]
    g_ref[...] = g
    cs_ref[...] = csg
    if augment:
        grow_ref[...] = g.T
        csrow_ref[...] = csg.T

    nx = FOX_HEADS * LANES
    lanex = lax.broadcasted_iota(jnp.int32, (1, nx), 1)
    keep = ((lanex % LANES) >= FOX_HD) == ((lanex // LANES) % 2 == 1)
    q_exp = jnp.concatenate([qn[:, (h // 2) * LANES:(h // 2 + 1) * LANES] for h in range(FOX_HEADS)], axis=1)
    if not augment:
        qx_ref[...] = jnp.where(keep, q_exp, 0.0).astype(qx_ref.dtype)
        return
    k_exp = jnp.concatenate([kn[:, (h // 2) * LANES:(h // 2 + 1) * LANES] for h in range(FOX_HEADS)], axis=1)
    hi, mid, lo = _split3(jnp.where(lane < FOX_HEADS, csg * LOG2E, 0.0))
    packed = (hi.astype(F32) + pltpu.roll(mid.astype(F32), FOX_HEADS, 1)
              + pltpu.roll(lo.astype(F32), 2 * FOX_HEADS, 1)).astype(BF16)
    aug = aug_ref[...] + _dot(packed, sel_ref[...])
    qx_ref[...] = jnp.where(keep, q_exp, aug[:, 0:nx]).astype(qx_ref.dtype)
    kx_ref[...] = jnp.where(keep, k_exp, aug[:, nx:2 * nx]).astype(kx_ref.dtype)
    vb_ref[...] = fv_ref[...].astype(vb_ref.dtype)


PREP_SHORT_ROWS = 128


def gate_prep(z, gq, gk, bias_row, nb, t, tc, t_valid, augment):
    m = nb * t
    gmat = jnp.asarray(np.kron(np.eye(FOX_HEADS), np.ones((FOX_HD, FOX_HD))), BF16)
    if augment:
        nc, tp, grid_rows = t // tc, tc, nb
        tril = np.tril(np.ones((tc, tc)))
    else:
        tc = tp = PREP_SHORT_ROWS
        nc, grid_rows = 1, m // tc
        tril = np.kron(np.eye(tc // t), np.tril(np.ones((t, t))))
    tril = jnp.asarray(tril, BF16)
    nx = FOX_HEADS * LANES
    sel = np.zeros((LANES, 2 * nx), np.float32)
    aug_const = np.zeros((1, 2 * nx), np.float32)
    for h in range(FOX_HEADS):
        o = h * LANES + (0 if h % 2 else FOX_HD)
        for p in range(3):
            sel[p * FOX_HEADS + h, o + p] = 1.0
            sel[p * FOX_HEADS + h, nx + o + 3 + p] = -1.0
            aug_const[0, o + 3 + p] = 1.0
            aug_const[0, nx + o + p] = 1.0
    sel, aug_const = jnp.asarray(sel, BF16), jnp.asarray(aug_const, F32)
    row_map = lambda b, c: (b * nc + c, 0)
    const = lambda b, c: (0, 0)
    out_shape = [jax.ShapeDtypeStruct((m, FOX_HEADS * LANES), BF16 if augment else F32),
                 jax.ShapeDtypeStruct((m, FOX_WIDTH), F32),
                 jax.ShapeDtypeStruct((m, LANES), F32),
                 jax.ShapeDtypeStruct((m, LANES), F32)]
    out_specs = [pl.BlockSpec((tc, FOX_HEADS * LANES), row_map),
                 pl.BlockSpec((tc, FOX_WIDTH), row_map),
                 pl.BlockSpec((tc, LANES), row_map),
                 pl.BlockSpec((tc, LANES), row_map)]
    if augment:
        out_shape += [jax.ShapeDtypeStruct((nb * LANES, t), F32), jax.ShapeDtypeStruct((nb * LANES, t), F32),
                      jax.ShapeDtypeStruct((m, FOX_HEADS * LANES), BF16), jax.ShapeDtypeStruct((m, FOX_WIDTH), BF16)]
        out_specs += [pl.BlockSpec((LANES, tc), lambda b, c: (b, c)), pl.BlockSpec((LANES, tc), lambda b, c: (b, c)),
                      pl.BlockSpec((tc, FOX_HEADS * LANES), row_map), pl.BlockSpec((tc, FOX_WIDTH), row_map)]
    return pl.pallas_call(
        functools.partial(_prep_kernel, seq_rows=t, t_valid=t_valid, augment=augment),
        out_shape=tuple(out_shape),
        grid=(grid_rows, nc),
        in_specs=[pl.BlockSpec((tc, 512), lambda b, c: (b * nc + c, COL_FQ)),
                  pl.BlockSpec((tc, 512), lambda b, c: (b * nc + c, COL_FK)),
                  pl.BlockSpec((tc, 512), lambda b, c: (b * nc + c, COL_FV)),
                  pl.BlockSpec((tc, LANES), lambda b, c: (b * nc + c, COL_SMALL * 4)),
                  pl.BlockSpec((1, 512), const), pl.BlockSpec((1, 512), const),
                  pl.BlockSpec((1, LANES), const),
                  pl.BlockSpec((512, 512), const), pl.BlockSpec((tp, tp), const),
                  pl.BlockSpec((LANES, 2 * nx), const), pl.BlockSpec((1, 2 * nx), const)],
        out_specs=tuple(out_specs),
        scratch_shapes=[pltpu.VMEM((1, LANES), F32)],
        compiler_params=_cparams("parallel", "arbitrary"),
        name="gate_prep",
    )(z, z, z, z, gq, gk, bias_row, gmat, tril, sel, aug_const)


S5_SCAN_LANES = 512
S5_SHORT_SEQS = 16


def _cmul(ar, ai, br, bi):
    return ar * br - ai * bi, ar * bi + ai * br


def _s5_kernel(u_ref, h0r_ref, h0i_ref, perm_ref, permt_ref, bre_ref, bim_ref, lam_ref, pseg_ref, pk_ref,
               cre_ref, cim_ref, d_ref, wglu_ref, bglu_ref, y_ref, hlr_ref, hli_ref, hr_ref, hi_ref, cr_ref, ci_ref,
               *, t_last, seqs):
    c = pl.program_id(1)
    nc = pl.num_programs(1)
    tc = u_ref.shape[0]
    R = tc // SUBLANES if seqs == 1 else 1
    half = S5_WIDTH // 2
    hl = S5_LANES // 2

    if seqs == 1:
        @pl.when(c == 0)
        def _():
            cr_ref[...] = h0r_ref[0]
            ci_ref[...] = h0i_ref[0]

    u = u_ref[...]
    ub = _pad_rows(u, MIN_BF16_ROWS).astype(BF16)
    if R > 1:
        ub = _dot(perm_ref[...], ub).astype(BF16)
    for j in range(2):
        uj = ub[:, j * half:(j + 1) * half]
        hr_ref[:, j * hl:(j + 1) * hl] = _dot(uj, bre_ref[j])[0:tc]
        hi_ref[:, j * hl:(j + 1) * hl] = _dot(uj, bim_ref[j])[0:tc]

    sub = lax.broadcasted_iota(jnp.int32, (SUBLANES, S5_SCAN_LANES), 0)
    for lc in range(S5_LANES // S5_SCAN_LANES):
        ls = slice(lc * S5_SCAN_LANES, (lc + 1) * S5_SCAN_LANES)
        lam_r = jnp.broadcast_to(lam_ref[0:1, ls], sub.shape)
        lam_i = jnp.broadcast_to(lam_ref[1:2, ls], sub.shape)
        if seqs > 1:
            def seq_scan(g, _):
                rows = pl.ds(pl.multiple_of(g * SUBLANES, SUBLANES), SUBLANES)
                s_r = jnp.where(sub == 0, jnp.broadcast_to(h0r_ref[0, pl.ds(g, 1), ls], sub.shape), 0.0)
                s_i = jnp.where(sub == 0, jnp.broadcast_to(h0i_ref[0, pl.ds(g, 1), ls], sub.shape), 0.0)
                m_r, m_i = _cmul(lam_r, lam_i, s_r, s_i)
                x_r = hr_ref[rows, ls] + m_r
                x_i = hi_ref[rows, ls] + m_i
                for j, lag in enumerate((1, 2, 4)):
                    m_r, m_i = _cmul(pseg_ref[2 * j, :, ls], pseg_ref[2 * j + 1, :, ls],
                                     pltpu.roll(x_r, lag, 0), pltpu.roll(x_i, lag, 0))
                    x_r, x_i = x_r + m_r, x_i + m_i
                hr_ref[rows, ls] = x_r
                hi_ref[rows, ls] = x_i
                hlr_ref[0, pl.ds(g, 1), ls] = x_r[t_last:t_last + 1, :]
                hli_ref[0, pl.ds(g, 1), ls] = x_i[t_last:t_last + 1, :]
                return 0

            lax.fori_loop(0, seqs, seq_scan, 0, unroll=2)
            continue
        init_r = jnp.where(sub == 0, jnp.broadcast_to(cr_ref[:, ls], sub.shape), 0.0)
        init_i = jnp.where(sub == 0, jnp.broadcast_to(ci_ref[:, ls], sub.shape), 0.0)

        def local_step(k, carry):
            h_r, h_i = carry
            rows = pl.ds(pl.multiple_of(k * SUBLANES, SUBLANES), SUBLANES)
            m_r, m_i = _cmul(lam_r, lam_i, h_r, h_i)
            h_r = m_r + hr_ref[rows, ls]
            h_i = m_i + hi_ref[rows, ls]
            hr_ref[rows, ls] = h_r
            hi_ref[rows, ls] = h_i
            return h_r, h_i

        e_r, e_i = lax.fori_loop(0, R, local_step, (init_r, init_i), unroll=min(R, 4))
        for j, lag in enumerate((1, 2, 4)):
            m_r, m_i = _cmul(pseg_ref[2 * j, :, ls], pseg_ref[2 * j + 1, :, ls],
                             pltpu.roll(e_r, lag, 0), pltpu.roll(e_i, lag, 0))
            e_r, e_i = e_r + m_r, e_i + m_i
        cr_ref[:, ls] = e_r[SUBLANES - 1:SUBLANES, :]
        ci_ref[:, ls] = e_i[SUBLANES - 1:SUBLANES, :]
        in_r = jnp.where(sub == 0, 0.0, pltpu.roll(e_r, 1, 0))
        in_i = jnp.where(sub == 0, 0.0, pltpu.roll(e_i, 1, 0))

        def fix_step(k, _):
            rows = pl.ds(pl.multiple_of(k * SUBLANES, SUBLANES), SUBLANES)
            p_r = jnp.broadcast_to(pk_ref[0, pl.ds(k, 1), ls], sub.shape)
            p_i = jnp.broadcast_to(pk_ref[1, pl.ds(k, 1), ls], sub.shape)
            m_r, m_i = _cmul(p_r, p_i, in_r, in_i)
            hr_ref[rows, ls] += m_r
            hi_ref[rows, ls] += m_i
            return 0

        lax.fori_loop(0, R, fix_step, 0, unroll=min(R, 4))

    if seqs == 1:
        @pl.when(c == nc - 1)
        def _():
            tl = t_last % tc
            pos = (tl % R) * SUBLANES + tl // R
            hlr_ref[0] = hr_ref[pos:pos + 1, :]
            hli_ref[0] = hi_ref[pos:pos + 1, :]

    hrb = _pad_rows(hr_ref[...], MIN_BF16_ROWS).astype(BF16)
    hib = _pad_rows(hi_ref[...], MIN_BF16_ROWS).astype(BF16)
    ys = [(_dot(hrb[:, j * hl:(j + 1) * hl], cre_ref[j]) + _dot(hib[:, j * hl:(j + 1) * hl], cim_ref[j]))
          for j in range(2)]
    y = jnp.concatenate(ys, axis=1)
    if R > 1:
        y = _dot_exact_lhs(permt_ref[...], y)
    y = _gelu_tanh(y[0:tc] + d_ref[...] * u)
    gate = _dot(_pad_rows(y, MIN_BF16_ROWS).astype(BF16), wglu_ref[...])[0:tc]
    y_ref[...] = y * _sigmoid(gate + bglu_ref[...])


def s5_mixer(z, h0r, h0i, sp, nb, t, tc, t_last, seqs=1):
    assert seqs == 1 or (tc == SUBLANES and t == tc)
    nc = t // tc
    R = tc // SUBLANES
    tc = tc * seqs
    nbg = nb // seqs
    tperm = max(tc, MIN_BF16_ROWS) if seqs == 1 else MIN_BF16_ROWS
    perm = np.zeros((tperm, tperm), np.float32)
    if seqs == 1:
        for s in range(SUBLANES):
            for k in range(R):
                perm[k * SUBLANES + s, s * R + k] = 1.0
    permt = jnp.asarray(perm.T, BF16)
    perm = jnp.asarray(perm, BF16)
    const = lambda b, c: (0, 0)
    const3 = lambda b, c: (0, 0, 0)
    state_spec = pl.BlockSpec((1, seqs, S5_LANES), lambda b, c: (b, 0, 0))
    tabs = sp["tabs"][R]
    return pl.pallas_call(
        functools.partial(_s5_kernel, t_last=t_last, seqs=seqs),
        out_shape=(jax.ShapeDtypeStruct((nb * t, S5_WIDTH), F32),
                   jax.ShapeDtypeStruct((nbg, seqs, S5_LANES), F32),
                   jax.ShapeDtypeStruct((nbg, seqs, S5_LANES), F32)),
        grid=(nbg, nc),
        in_specs=[pl.BlockSpec((tc, 512), lambda b, c: (b * nc + c, COL_S5)),
                  state_spec, state_spec,
                  pl.BlockSpec((tperm, tperm), const), pl.BlockSpec((tperm, tperm), const),
                  pl.BlockSpec((2, S5_WIDTH // 2, S5_LANES // 2), const3),
                  pl.BlockSpec((2, S5_WIDTH // 2, S5_LANES // 2), const3),
                  pl.BlockSpec((2, S5_LANES), const),
                  pl.BlockSpec((6, SUBLANES, S5_LANES), const3),
                  pl.BlockSpec((2, R, S5_LANES), const3),
                  pl.BlockSpec((2, S5_LANES // 2, S5_WIDTH // 2), const3),
                  pl.BlockSpec((2, S5_LANES // 2, S5_WIDTH // 2), const3),
                  pl.BlockSpec((1, S5_WIDTH), const),
                  pl.BlockSpec((S5_WIDTH, S5_WIDTH), const), pl.BlockSpec((1, S5_WIDTH), const)],
        out_specs=(pl.BlockSpec((tc, S5_WIDTH), lambda b, c: (b * nc + c, 0)), state_spec, state_spec),
        scratch_shapes=[pltpu.VMEM((tc, S5_LANES), F32), pltpu.VMEM((tc, S5_LANES), F32),
                        pltpu.VMEM((1, S5_LANES), F32), pltpu.VMEM((1, S5_LANES), F32)],
        compiler_params=_cparams("parallel", "arbitrary"),
        name="s5_mixer",
    )(z, h0r.reshape(nbg, seqs, S5_LANES), h0i.reshape(nbg, seqs, S5_LANES), perm, permt, sp["bre"], sp["bim"],
      tabs["lam"],
      tabs["pseg"], tabs["pk"], sp["cre"], sp["cim"], sp["d"], sp["wglu"], sp["bglu"])


def s5_params(a_re, a_im, log_step, b_re, b_im, c_re, c_im, d_skip, w_glu, b_glu, seg_lens):
    dt = jnp.exp(log_step)[:, None]
    mag = jnp.exp(a_re * dt)
    lr = (mag * jnp.cos(a_im * dt)).reshape(1, S5_LANES)
    li = (mag * jnp.sin(a_im * dt)).reshape(1, S5_LANES)
    den = a_re * a_re + a_im * a_im
    xr, xi = lr.reshape(a_re.shape) - 1.0, li.reshape(a_re.shape)
    fr = (xr * a_re + xi * a_im) / den
    fi = (xi * a_re - xr * a_im) / den
    bbr = fr[..., None] * b_re - fi[..., None] * b_im
    bbi = fr[..., None] * b_im + fi[..., None] * b_re
    gh = S5_GROUPS // 2
    eye = jnp.eye(gh, dtype=F32)

    def in_mat(b):
        return jnp.einsum('jgpc,gh->jgchp', b.reshape(2, gh, S5_STATE, S5_GROUP), eye).reshape(
            2, S5_WIDTH // 2, S5_LANES // 2).astype(BF16)

    def out_mat(cm):
        return jnp.einsum('jgcp,gh->jgphc', cm.reshape(2, gh, S5_GROUP, S5_STATE), eye).reshape(
            2, S5_LANES // 2, S5_WIDTH // 2).astype(BF16)

    def powers(pr, pi, n):
        tr, ti, cnt = pr, pi, 1
        while cnt < n:
            lr_, li_ = tr[cnt - 1:cnt], ti[cnt - 1:cnt]
            nr, ni = _cmul(tr, ti, lr_, li_)
            tr, ti, cnt = jnp.concatenate([tr, nr], axis=0), jnp.concatenate([ti, ni], axis=0), 2 * cnt
        return tr, ti

    sub = jnp.arange(SUBLANES)[:, None]
    tabs = {}
    for R in seg_lens:
        kr, ki = powers(lr, li, R)
        sr, si = powers(kr[R - 1:R], ki[R - 1:R], 4)
        pseg = []
        for lag in (1, 2, 4):
            msk = (sub >= lag).astype(F32)
            pseg += [msk * sr[lag - 1:lag], msk * si[lag - 1:lag]]
        tabs[R] = dict(lam=jnp.concatenate([lr, li], axis=0), pseg=jnp.stack(pseg), pk=jnp.stack([kr, ki]))
    return dict(bre=in_mat(bbr), bim=in_mat(bbi), cre=out_mat(c_re), cim=out_mat(-c_im), tabs=tabs,
                d=d_skip.reshape(1, S5_WIDTH), wglu=w_glu.astype(BF16), bglu=b_glu.reshape(1, S5_WIDTH))


def _fox_flash_kernel(it_ref, jt_ref, qx_ref, kx_ref, v_ref, o_ref, m_ref, l_ref, acc_ref, s_ref, p_ref):
    i = it_ref[pl.program_id(1)]
    j = jt_ref[pl.program_id(1)]
    tq = qx_ref.shape[0]
    tk = kx_ref.shape[0]

    @pl.when(j == 0)
    def _():
        m_ref[...] = jnp.full_like(m_ref, -jnp.inf)
        l_ref[...] = jnp.zeros_like(l_ref)
        acc_ref[...] = jnp.zeros_like(acc_ref)

    def step(masked):
        nr, ncol = tq // FLASH_ROWS, tk // LANES
        if masked:
            diff = (lax.broadcasted_iota(jnp.int32, (FLASH_ROWS, LANES), 0) -
                    lax.broadcasted_iota(jnp.int32, (FLASH_ROWS, LANES), 1))
        for h in range(FOX_HEADS):
            hs = slice(h * LANES, (h + 1) * LANES)
            ps = slice((h // 2) * LANES, (h // 2 + 1) * LANES)
            s_buf, p_buf = s_ref.at[h % 2], p_ref.at[h % 2]
            if h == 0:
                s_buf[...] = _dot_nt(qx_ref[:, hs], kx_ref[:, hs])
            if h + 1 < FOX_HEADS:
                nhs = slice((h + 1) * LANES, (h + 2) * LANES)
                s_ref[(h + 1) % 2] = _dot_nt(qx_ref[:, nhs], kx_ref[:, nhs])
            for r in range(nr):
                rs = slice(r * FLASH_ROWS, (r + 1) * FLASH_ROWS)

                live = [cidx for cidx in range(ncol) if not (masked and cidx * LANES >= (r + 1) * FLASH_ROWS)]

                def piece(cidx):
                    sc = s_buf[rs, cidx * LANES:(cidx + 1) * LANES]
                    if masked and (cidx + 1) * LANES - 1 > r * FLASH_ROWS:
                        sc = jnp.where(diff >= (cidx * LANES - r * FLASH_ROWS), sc, -jnp.inf)
                    return sc

                mx = piece(live[0])
                for cidx in live[1:]:
                    mx = jnp.maximum(mx, piece(cidx))
                m_prev = m_ref[h, rs, :]
                m_new = jnp.maximum(m_prev, jnp.max(mx, axis=-1, keepdims=True))
                alpha = jnp.exp2(m_prev - m_new)
                lsum = jnp.zeros((FLASH_ROWS, LANES), F32)
                for cidx in range(ncol):
                    cs_ = slice(cidx * LANES, (cidx + 1) * LANES)
                    if cidx not in live:
                        p_buf[rs, cs_] = jnp.zeros((FLASH_ROWS, LANES), BF16)
                        continue
                    pc = jnp.exp2(piece(cidx) - m_new)
                    lsum = lsum + pc
                    p_buf[rs, cs_] = pc.astype(BF16)
                l_ref[h, rs, :] = alpha * l_ref[h, rs, :] + lsum
                acc_ref[h, rs, :] = alpha * acc_ref[h, rs, :]
                m_ref[h, rs, :] = m_new
            acc_ref[h] += _dot(p_buf[...], v_ref[:, ps])

    @pl.when(j < i)
    def _():
        step(False)

    @pl.when(j == i)
    def _():
        step(True)
        lane = lax.broadcasted_iota(jnp.int32, (1, LANES), 1)
        for p in range(FOX_HEADS // 2):
            lo = acc_ref[2 * p] / jnp.sum(l_ref[2 * p], axis=-1, keepdims=True)
            hi = acc_ref[2 * p + 1] / jnp.sum(l_ref[2 * p + 1], axis=-1, keepdims=True)
            o_ref[:, p * LANES:(p + 1) * LANES] = jnp.where(lane < FOX_HD, lo, hi)


def fox_flash(qx, kx, vb, nb, t, tile):
    assert tile % FLASH_ROWS == 0
    nt = t // tile
    pairs = [(i, j) for i in range(nt) for j in range(i + 1)]
    it = jnp.asarray([p[0] for p in pairs], jnp.int32)
    jt = jnp.asarray([p[1] for p in pairs], jnp.int32)
    return pl.pallas_call(
        _fox_flash_kernel,
        out_shape=jax.ShapeDtypeStruct((nb * t, FOX_WIDTH), F32),
        grid_spec=pltpu.PrefetchScalarGridSpec(
            num_scalar_prefetch=2,
            grid=(nb, len(pairs)),
            in_specs=[pl.BlockSpec((tile, FOX_HEADS * LANES), lambda b, p, it, jt: (b * nt + it[p], 0)),
                      pl.BlockSpec((tile, FOX_HEADS * LANES), lambda b, p, it, jt: (b * nt + jt[p], 0)),
                      pl.BlockSpec((tile, FOX_WIDTH), lambda b, p, it, jt: (b * nt + jt[p], 0))],
            out_specs=pl.BlockSpec((tile, FOX_WIDTH), lambda b, p, it, jt: (b * nt + it[p], 0)),
            scratch_shapes=[pltpu.VMEM((FOX_HEADS, tile, LANES), F32), pltpu.VMEM((FOX_HEADS, tile, LANES), F32),
                            pltpu.VMEM((FOX_HEADS, tile, LANES), F32),
                            pltpu.VMEM((2, tile, tile), F32), pltpu.VMEM((2, tile, tile), BF16)]),
        compiler_params=_cparams("parallel", "arbitrary"),
        name="fox_flash",
    )(it, jt, qx, kx, vb)


def _fox_decode_kernel(pt_ref, qb_ref, kn_ref, vn_ref, cs_ref, *rest, n_pages):
    k_refs = rest[0:n_pages]
    v_refs = rest[n_pages:2 * n_pages]
    f_refs = rest[2 * n_pages:3 * n_pages]
    triu_ref, o_ref, kpad_ref, vpad_ref = rest[3 * n_pages:]
    page = kpad_ref.shape[0]
    nrow = FOX_HEADS * SAMPLE_T
    qb = qb_ref[...].astype(BF16)

    def per_query(x):
        return jnp.concatenate([x] * SAMPLE_T, axis=0)

    fcat = jnp.concatenate([f_refs[i][0, 0] for i in range(n_pages)], axis=0)
    cum_in = _dot_exact_rhs(fcat, triu_ref[...])
    totals = jnp.broadcast_to(cum_in[:, page - 1:page], cum_in.shape)
    off = jnp.zeros((FOX_HEADS, page), F32)
    ss = []
    for i in range(n_pages):
        rows = slice(i * FOX_HEADS, (i + 1) * FOX_HEADS)
        kt = k_refs[i][0, 0].reshape(FOX_WIDTH, page).astype(BF16)
        ss.append(_dot(qb, kt) - per_query(cum_in[rows] + off))
        off = off + totals[rows]
    kpad_ref[...] = jnp.zeros_like(kpad_ref)
    vpad_ref[...] = jnp.zeros_like(vpad_ref)
    kpad_ref[0:SAMPLE_T, :] = kn_ref[...]
    vpad_ref[0:SAMPLE_T, :] = vn_ref[...]
    csrow = _pad_rows(cs_ref[...], page).T[LANE_FOXF:LANE_FOXF + FOX_HEADS]
    s_new = _dot_nt(qb, kpad_ref[...].astype(BF16)) - per_query(csrow + off)
    tq_idx = lax.broadcasted_iota(jnp.int32, (nrow, page), 0) // FOX_HEADS
    tk_idx = lax.broadcasted_iota(jnp.int32, (nrow, page), 1)
    ss.append(jnp.where(tk_idx <= tq_idx, s_new, -jnp.inf))

    m = ss[0]
    for s in ss[1:]:
        m = jnp.maximum(m, s)
    m = jnp.max(m, axis=-1, keepdims=True)
    lsum = jnp.zeros((nrow, page), F32)
    acc = jnp.zeros((nrow, FOX_WIDTH), F32)
    for i, s in enumerate(ss):
        p = jnp.exp(s - m)
        lsum = lsum + p
        if i < n_pages:
            acc = acc + _dot_nt(p.astype(BF16), v_refs[i][0, 0].reshape(FOX_WIDTH, page).astype(BF16))
        else:
            acc = acc + _dot(p.astype(BF16), vpad_ref[...].astype(BF16))
    acc = acc / jnp.sum(lsum, axis=-1, keepdims=True)
    lane = lax.broadcasted_iota(jnp.int32, (nrow, FOX_WIDTH), 1) // FOX_HD
    head = lax.broadcasted_iota(jnp.int32, (nrow, FOX_WIDTH), 0) % FOX_HEADS
    picked = jnp.where(lane == head, acc, 0.0)
    o_ref[...] = jnp.sum(picked.reshape(SAMPLE_T, FOX_HEADS, FOX_WIDTH), axis=1)


def fox_decode(page_table, qx, kn, z, cs, pool_kt, pool_vt, pool_ft, layer):
    nb, n_pages = page_table.shape
    page = pool_kt.shape[-1]
    assert page == LANES
    nrow = FOX_HEADS * SAMPLE_T
    pt = page_table.reshape(-1)
    triu = jnp.asarray(np.triu(np.ones((page, page))), BF16)
    pair_of_head = jnp.asarray(np.arange(FOX_HEADS)[:, None] // 2 == np.arange(FOX_HEADS // 2)[None, :], F32)
    qb = (qx.reshape(nb, SAMPLE_T, FOX_HEADS, 1, LANES) * pair_of_head[None, None, :, :, None]).reshape(
        nb * nrow, FOX_WIDTH)

    def pg5(i):
        return lambda b, pt: (layer, pt[b * n_pages + i], 0, 0, 0)

    def pg4(i):
        return lambda b, pt: (layer, pt[b * n_pages + i], 0, 0)

    row = lambda b, pt: (b, 0)
    in_specs = [pl.BlockSpec((nrow, FOX_WIDTH), row),
                pl.BlockSpec((SAMPLE_T, FOX_WIDTH), row),
                pl.BlockSpec((SAMPLE_T, FOX_WIDTH), lambda b, pt: (b, COL_FV)),
                pl.BlockSpec((SAMPLE_T, LANES), row)]
    in_specs += [pl.BlockSpec((1, 1, FOX_HEADS, FOX_HD, page), pg5(i)) for i in range(n_pages)]
    in_specs += [pl.BlockSpec((1, 1, FOX_HEADS, FOX_HD, page), pg5(i)) for i in range(n_pages)]
    in_specs += [pl.BlockSpec((1, 1, FOX_HEADS, page), pg4(i)) for i in range(n_pages)]
    in_specs += [pl.BlockSpec((page, page), lambda b, pt: (0, 0))]
    return pl.pallas_call(
        functools.partial(_fox_decode_kernel, n_pages=n_pages),
        out_shape=jax.ShapeDtypeStruct((nb * SAMPLE_T, FOX_WIDTH), F32),
        grid_spec=pltpu.PrefetchScalarGridSpec(
            num_scalar_prefetch=1,
            grid=(nb,),
            in_specs=in_specs,
            out_specs=pl.BlockSpec((SAMPLE_T, FOX_WIDTH), row),
            scratch_shapes=[pltpu.VMEM((page, FOX_WIDTH), F32), pltpu.VMEM((page, FOX_WIDTH), F32)]),
        compiler_params=_cparams("parallel"),
        name="fox_decode",
    )(pt, qb, kn, z, cs, *([pool_kt] * n_pages), *([pool_vt] * n_pages), *([pool_ft] * n_pages), triu)


def _mlstm_kernel(q_ref, k_ref, v_ref, o_ref, g_ref, cs_ref, *rest, short):
    if short:
        c0_ref, n0_ref, m0_ref, gn_ref, y_ref, cout_ref, nout_ref, mout_ref, c_ref, n_ref, m_ref = rest
    else:
        (grow_ref, csrow_ref, c0_ref, n0_ref, m0_ref, gn_ref, y_ref, cout_ref, nout_ref, mout_ref,
         c_ref, n_ref, m_ref) = rest
    c = pl.program_id(1)
    nc = pl.num_programs(1)
    L = q_ref.shape[0]

    @pl.when(c == 0)
    def _():
        c_ref[...] = c0_ref[0]
        n_ref[...] = n0_ref[0]
        m_ref[...] = m0_ref[0]

    Lk = max(L, LANES)
    Lq = max(L, MIN_BF16_ROWS)
    g = g_ref[...]
    cs = cs_ref[...]
    if short:
        grow = _pad_rows(g, Lk).T
        csrow = _pad_rows(cs, Lk).T
    else:
        grow = grow_ref[...]
        csrow = csrow_ref[...]
    causal = (lax.broadcasted_iota(jnp.int32, (Lq, Lk), 0) >= lax.broadcasted_iota(jnp.int32, (Lq, Lk), 1))
    for h in range(ML_HEADS):
        hs = slice(h * ML_HD, (h + 1) * ML_HD)
        scale = ML_HD ** -0.5
        qb = _pad_rows(q_ref[:, hs], Lq).astype(BF16)
        kb = _pad_rows(k_ref[:, hs], Lk).astype(BF16)
        vb = _pad_rows(v_ref[:, hs], Lk).astype(BF16)
        qh, kh, vh = qb.astype(F32), kb.astype(F32), vb.astype(F32)
        bcol_k = _pad_rows(cs[:, LANE_MLF + h:LANE_MLF + h + 1], Lk)
        bcol = bcol_k[0:Lq]
        icol = _pad_rows(g[:, LANE_MLI + h:LANE_MLI + h + 1], Lk, -jnp.inf)
        brow = csrow[LANE_MLF + h:LANE_MLF + h + 1, :]
        irow = grow[LANE_MLI + h:LANE_MLI + h + 1, :]
        m_prev = m_ref[0:1, h:h + 1]
        log_d = jnp.where(causal, bcol - brow + irow, -jnp.inf)
        log_inter = bcol + m_prev
        m_t = jnp.maximum(log_inter, jnp.max(log_d, axis=-1, keepdims=True))
        d_w = jnp.exp(log_d - m_t)
        inter_w = jnp.exp(log_inter - m_t)
        s = _dot_nt(qb, kb) * (d_w * scale)
        ch = c_ref[h]
        n_row = n_ref[h:h + 1, :]
        num = _dot(s.astype(BF16), vb) + inter_w * _dot_nt(qb, ch.astype(BF16))
        den = jnp.sum(s, axis=-1, keepdims=True) + inter_w * jnp.sum(qh * n_row, axis=-1, keepdims=True)
        hh = num / jnp.maximum(jnp.abs(den), jnp.exp(-m_t))
        y_ref[:, hs] = _rms(hh[0:L], gn_ref[...]) * _sigmoid(o_ref[:, hs])
        m_end = m_t[L - 1:L, :]
        a_end = inter_w[L - 1:L, :]
        w_col = jnp.exp(bcol[L - 1:L, :] - bcol_k + icol - m_end) * scale
        c_ref[h] = a_end * ch + _dot_tn((vh * w_col).astype(BF16), kb)
        n_ref[h:h + 1, :] = a_end * n_row + jnp.sum(kh * w_col, axis=0, keepdims=True)
        m_ref[0:1, h:h + 1] = m_end

    @pl.when(c == nc - 1)
    def _():
        cout_ref[0] = c_ref[...]
        nout_ref[0] = n_ref[...]
        mout_ref[0] = m_ref[...]


def mlstm(za, zb, g, cs, grow, csrow, c0, n0, m0, gn, nb, t, L):
    nc = t // L
    rows = lambda b, c: (b * nc + c, 0)
    rr = 2 * SUBLANES
    m0p = jnp.zeros((nb, 1, LANES), F32).at[:, 0, :ML_HEADS].set(m0)
    short = grow is None
    row_specs = [] if short else [pl.BlockSpec((rr, L), lambda b, c: (b * (LANES // rr), c))] * 2
    row_args = [] if short else [grow, csrow]
    outs = pl.pallas_call(
        functools.partial(_mlstm_kernel, short=short),
        out_shape=(jax.ShapeDtypeStruct((nb * t, ML_WIDTH), F32),
                   jax.ShapeDtypeStruct((nb, ML_HEADS, ML_HD, ML_HD), F32),
                   jax.ShapeDtypeStruct((nb, ML_HEADS, ML_HD), F32),
                   jax.ShapeDtypeStruct((nb, 1, LANES), F32)),
        grid=(nb, nc),
        in_specs=[pl.BlockSpec((L, 512), lambda b, c: (b * nc + c, COL_MQ)),
                  pl.BlockSpec((L, 512), lambda b, c: (b * nc + c, COL_MK)),
                  pl.BlockSpec((L, 512), lambda b, c: (b * nc + c, COL_MV)),
                  pl.BlockSpec((L, 512), lambda b, c: (b * nc + c, COL_MO)),
                  pl.BlockSpec((L, LANES), rows), pl.BlockSpec((L, LANES), rows), *row_specs,
                  pl.BlockSpec((1, ML_HEADS, ML_HD, ML_HD), lambda b, c: (b, 0, 0, 0)),
                  pl.BlockSpec((1, ML_HEADS, ML_HD), lambda b, c: (b, 0, 0)),
                  pl.BlockSpec((1, 1, LANES), lambda b, c: (b, 0, 0)),
                  pl.BlockSpec((1, ML_HD), lambda b, c: (0, 0))],
        out_specs=(pl.BlockSpec((L, ML_WIDTH), rows),
                   pl.BlockSpec((1, ML_HEADS, ML_HD, ML_HD), lambda b, c: (b, 0, 0, 0)),
                   pl.BlockSpec((1, ML_HEADS, ML_HD), lambda b, c: (b, 0, 0)),
                   pl.BlockSpec((1, 1, LANES), lambda b, c: (b, 0, 0))),
        scratch_shapes=[pltpu.VMEM((ML_HEADS, ML_HD, ML_HD), F32), pltpu.VMEM((ML_HEADS, ML_HD), F32),
                        pltpu.VMEM((1, LANES), F32)],
        compiler_params=_cparams("parallel", "arbitrary"),
        name="mlstm",
    )(zb, zb, zb, za, g, cs, *row_args, c0, n0, m0p, gn.reshape(1, ML_HD))
    y, c_new, n_new, m_new = outs
    return y, c_new, n_new, m_new[:, 0, :ML_HEADS]


def _merge_kernel(x_ref, g_ref, wg_ref, ys_ref, yf_ref, ym_ref, ws_ref, wf_ref, wm_ref, wo_ref, o_ref):
    x = x_ref[...]
    hn = _rms(x, g_ref[...]).astype(BF16)
    branches = (ys_ref, ws_ref), (yf_ref, wf_ref), (ym_ref, wm_ref)
    merged = None
    for b, (y_ref, w_ref) in enumerate(branches):
        gate = _sigmoid(_dot(hn, wg_ref[:, b * D_MODEL:(b + 1) * D_MODEL]))
        term = gate * _dot(y_ref[...].astype(BF16), w_ref[...])
        merged = term if merged is None else merged + term
    o_ref[...] = x + _dot(merged.astype(BF16), wo_ref[...])


def merge_out(x, g_mix, w_gates, ys, yf, ym, ws, wf, wm, wo, tm=256):
    m = x.shape[0]
    tm = min(tm, m)
    row = lambda i: (i, 0)
    const = lambda i: (0, 0)
    return pl.pallas_call(
        _merge_kernel,
        out_shape=jax.ShapeDtypeStruct((m, D_MODEL), F32),
        grid=(m // tm,),
        in_specs=[pl.BlockSpec((tm, D_MODEL), row), pl.BlockSpec((1, D_MODEL), const),
                  pl.BlockSpec((D_MODEL, 3 * D_MODEL), const),
                  pl.BlockSpec((tm, 512), row), pl.BlockSpec((tm, 512), row), pl.BlockSpec((tm, 512), row),
                  pl.BlockSpec((512, D_MODEL), const), pl.BlockSpec((512, D_MODEL), const),
                  pl.BlockSpec((512, D_MODEL), const), pl.BlockSpec((D_MODEL, D_MODEL), const)],
        out_specs=pl.BlockSpec((tm, D_MODEL), row),
        compiler_params=_cparams("parallel"),
        name="merge_out",
    )(x, g_mix.reshape(1, D_MODEL), w_gates, ys, yf, ym, ws, wf, wm, wo)


def _cross_block_kernel(x_ref, g_ref, wq_ref, gq_ref, k_ref, v_ref, wo_ref, o_ref):
    x = x_ref[...]
    q = _dot(_rms(x, g_ref[...]).astype(BF16), wq_ref[...])
    outs = []
    for h in range(MEM_HEADS):
        hs = slice(h * MEM_HD, (h + 1) * MEM_HD)
        qh = _rms(q[:, hs], gq_ref[...]).astype(BF16)
        s = _dot_nt(qh, k_ref[:, hs].astype(BF16)) * (MEM_HD ** -0.5)
        m = jnp.max(s, axis=-1, keepdims=True)
        p = jnp.exp(s - m)
        l = jnp.sum(p, axis=-1, keepdims=True)
        outs.append((_dot(p.astype(BF16), v_ref[:, hs].astype(BF16)) / l).astype(BF16))
    o_ref[...] = x + _dot(jnp.concatenate(outs, axis=1), wo_ref[...])


def cross_block(x, g, wq, gq, mem_k, mem_v, wo, nb, t, tq):
    nq = t // tq
    n_mem = mem_k.shape[0] // nb
    const = lambda b, i: (0, 0)
    row = lambda b, i: (b * nq + i, 0)
    return pl.pallas_call(
        _cross_block_kernel,
        out_shape=jax.ShapeDtypeStruct((nb * t, D_MODEL), F32),
        grid=(nb, nq),
        in_specs=[pl.BlockSpec((tq, D_MODEL), row), pl.BlockSpec((1, D_MODEL), const),
                  pl.BlockSpec((D_MODEL, MEM_WIDTH), const), pl.BlockSpec((1, MEM_HD), const),
                  pl.BlockSpec((n_mem, MEM_WIDTH), lambda b, i: (b, 0)),
                  pl.BlockSpec((n_mem, MEM_WIDTH), lambda b, i: (b, 0)),
                  pl.BlockSpec((MEM_WIDTH, D_MODEL), const)],
        out_specs=pl.BlockSpec((tq, D_MODEL), row),
        compiler_params=_cparams("parallel", "parallel"),
        name="cross_block",
    )(x, g.reshape(1, D_MODEL), wq, gq.reshape(1, MEM_HD), mem_k, mem_v, wo)


def _cross_cached_kernel(q_ref, k_ref, v_ref, o_ref):
    nseq = k_ref.shape[1]
    tq = q_ref.shape[0] // nseq
    for sq in range(nseq):
        q = q_ref[sq * tq:(sq + 1) * tq, :]
        qr = jnp.concatenate([q[:, h * MEM_HD:(h + 1) * MEM_HD] for h in range(MEM_HEADS)], axis=0)
        kf = k_ref[0, sq].astype(BF16)
        vf = v_ref[0, sq].astype(BF16)
        s = _dot_nt(qr.astype(BF16), kf) * (MEM_HD ** -0.5)
        row_head = lax.broadcasted_iota(jnp.int32, s.shape, 0) // tq
        col_head = lax.broadcasted_iota(jnp.int32, s.shape, 1) % MEM_HEADS
        s = jnp.where(row_head == col_head, s, -jnp.inf)
        m = jnp.max(s, axis=-1, keepdims=True)
        p = jnp.exp(s - m)
        l = jnp.sum(p, axis=-1, keepdims=True)
        o = _dot(p.astype(BF16), vf) / l
        for h in range(MEM_HEADS):
            o_ref[sq * tq:(sq + 1) * tq, h * MEM_HD:(h + 1) * MEM_HD] = o[h * tq:(h + 1) * tq]


CROSS_CACHED_SEQS = 4


def cross_attend_cached(q, mem_k, mem_v, nb, t, layer):
    ns = CROSS_CACHED_SEQS
    kv_spec = pl.BlockSpec((1, ns, mem_k.shape[2], MEM_HD), lambda b: (layer, b, 0, 0))
    return pl.pallas_call(
        _cross_cached_kernel,
        out_shape=jax.ShapeDtypeStruct((nb * t, MEM_WIDTH), F32),
        grid=(nb // ns,),
        in_specs=[pl.BlockSpec((ns * t, MEM_WIDTH), lambda b: (b, 0)), kv_spec, kv_spec],
        out_specs=pl.BlockSpec((ns * t, MEM_WIDTH), lambda b: (b, 0)),
        compiler_params=_cparams("parallel"),
        name="cross_attend_cached",
    )(q, mem_k, mem_v)


def _proj_residual_kernel(x_ref, a_ref, w_ref, o_ref):
    o_ref[...] = x_ref[...] + _dot(a_ref[...].astype(BF16), w_ref[...])


def proj_residual(x, a, w, tm=512):
    m, n = x.shape
    k = a.shape[1]
    tm = min(tm, m)
    return pl.pallas_call(
        _proj_residual_kernel,
        out_shape=jax.ShapeDtypeStruct((m, n), F32),
        grid=(m // tm,),
        in_specs=[pl.BlockSpec((tm, n), lambda i: (i, 0)), pl.BlockSpec((tm, k), lambda i: (i, 0)),
                  pl.BlockSpec((k, n), lambda i: (0, 0))],
        out_specs=pl.BlockSpec((tm, n), lambda i: (i, 0)),
        compiler_params=_cparams("parallel"),
        name="proj_residual",
    )(x, a, w)


def _mlp_kernel(x_ref, g_ref, wu_ref, wd_ref, o_ref, hn_ref, acc_ref):
    f = pl.program_id(1)

    @pl.when(f == 0)
    def _():
        hn_ref[...] = _rms(x_ref[...], g_ref[...]).astype(BF16)
        acc_ref[...] = jnp.zeros_like(acc_ref)

    a = jnp.maximum(_dot(hn_ref[...], wu_ref[...]), 0.0)
    acc_ref[...] += _dot((a * a).astype(BF16), wd_ref[...])

    @pl.when(f == pl.num_programs(1) - 1)
    def _():
        o_ref[...] = x_ref[...] + acc_ref[...]


def mlp(x, g, wu, wd, tm=1024, tf=512):
    m, d = x.shape
    dff = wu.shape[1]
    tm = min(tm, m)
    return pl.pallas_call(
        _mlp_kernel,
        out_shape=jax.ShapeDtypeStruct((m, d), F32),
        grid=(m // tm, dff // tf),
        in_specs=[pl.BlockSpec((tm, d), lambda i, f: (i, 0)), pl.BlockSpec((1, d), lambda i, f: (0, 0)),
                  pl.BlockSpec((d, tf), lambda i, f: (0, f)), pl.BlockSpec((tf, d), lambda i, f: (f, 0))],
        out_specs=pl.BlockSpec((tm, d), lambda i, f: (i, 0)),
        scratch_shapes=[pltpu.VMEM((tm, d), BF16), pltpu.VMEM((tm, d), F32)],
        compiler_params=_cparams("parallel", "arbitrary"),
        name="mlp",
    )(x, g.reshape(1, d), wu, wd)


def _pack_w_in(w_in):
    offs = np.concatenate([[0], np.cumsum(SPLITS)])
    col = lambda i: w_in[:, int(offs[i]):int(offs[i + 1])]
    s5, fq, fk, fv, ff, mq, mk, mv, mi, mf, mo, gates = [col(i) for i in range(12)]
    pad = jnp.zeros((w_in.shape[0], 512 - FOX_HEADS - 2 * ML_HEADS), w_in.dtype)
    w_a = jnp.concatenate([s5, fq, fk, fv, mo, ff, mi, mf, pad], axis=1).astype(BF16)
    w_b = jnp.concatenate([mq, mk, mv], axis=1).astype(BF16)
    return w_a, w_b, gates.astype(BF16)


def _layer_weights(l, g_mix, w_in, s5_a_re, s5_a_im, s5_log_step, s5_b_re, s5_b_im, s5_c_re, s5_c_im, s5_d,
                   s5_w_glu, s5_b_glu, fox_gq, fox_gk, fox_bf, ml_bi, ml_bf, ml_gn, w_br_s5, w_br_fox, w_br_ml,
                   w_out, g_cross, w_cq, cross_gq, g_mem, w_mk, w_mv, cross_gk, w_co, g_mlp, w_up, w_down):
    bias_row = jnp.zeros((1, LANES), F32)
    bias_row = bias_row.at[0, LANE_FOXF:LANE_FOXF + FOX_HEADS].set(fox_bf[l])
    bias_row = bias_row.at[0, LANE_MLI:LANE_MLI + ML_HEADS].set(ml_bi[l])
    bias_row = bias_row.at[0, LANE_MLF:LANE_MLF + ML_HEADS].set(ml_bf[l])
    w_a, w_b, w_gates = _pack_w_in(w_in[l])
    return dict(
        g_mix=g_mix[l], w_a=w_a, w_b=w_b, w_gates=w_gates,
        s5=s5_params(s5_a_re[l], s5_a_im[l], s5_log_step[l], s5_b_re[l], s5_b_im[l], s5_c_re[l], s5_c_im[l],
                     s5_d[l], s5_w_glu[l], s5_b_glu[l], (SEQ_TILE // SUBLANES, SAMPLE_T // SUBLANES)),
        gq=jnp.tile(fox_gq[l], FOX_HEADS).reshape(1, FOX_WIDTH),
        gk=jnp.tile(fox_gk[l], FOX_HEADS).reshape(1, FOX_WIDTH),
        bias_row=bias_row, ml_gn=ml_gn[l],
        w_br_s5=w_br_s5[l].astype(BF16), w_br_fox=w_br_fox[l].astype(BF16), w_br_ml=w_br_ml[l].astype(BF16),
        w_out=w_out[l].astype(BF16), g_cross=g_cross[l], w_cq=w_cq[l].astype(BF16), cross_gq=cross_gq[l],
        g_mem=g_mem[l], w_mk=w_mk[l].astype(BF16), w_mv=w_mv[l].astype(BF16), cross_gk=cross_gk[l],
        w_co=w_co[l].astype(BF16), g_mlp=g_mlp[l], w_up=w_up[l].astype(BF16), w_down=w_down[l].astype(BF16))


def _hybrid_layer(x, W, nb, t, seq_tile, t_valid, s5_state, ml_state, mem_k, mem_v, fox_attend, augment,
                  mem_layer=None):
    z = norm_matmul(x, W["g_mix"], W["w_a"], tm=min(1024, nb * t))
    zb = norm_matmul(x, W["g_mix"], W["w_b"], tm=min(1024, nb * t),
                     out_dtype=BF16 if seq_tile % MIN_BF16_ROWS == 0 else F32)
    prep = gate_prep(z, W["gq"], W["gk"], W["bias_row"], nb, t, seq_tile, t_valid, augment)
    kn, g, cs = prep[1:4]
    grow, csrow = prep[4:6] if augment else (None, None)
    y_s5, s5_re, s5_im = s5_mixer(z, s5_state[0], s5_state[1], W["s5"], nb, t, seq_tile, t_valid - 1,
                                  seqs=S5_SHORT_SEQS if t == SUBLANES else 1)
    y_fox = fox_attend(prep, z)
    y_ml, c_new, n_new, m_new = mlstm(z, zb, g, cs, grow, csrow, ml_state[0], ml_state[1], ml_state[2], W["ml_gn"],
                                      nb, t, seq_tile)
    x = merge_out(x, W["g_mix"], W["w_gates"], y_s5, y_fox, y_ml, W["w_br_s5"], W["w_br_fox"], W["w_br_ml"],
                  W["w_out"])
    if mem_layer is None:
        x = cross_block(x, W["g_cross"], W["w_cq"], W["cross_gq"], mem_k, mem_v, W["w_co"], nb, t, seq_tile)
    else:
        qc = norm_matmul(x, W["g_cross"], W["w_cq"], head_gain=W["cross_gq"])
        oc = cross_attend_cached(qc, mem_k, mem_v, nb, t, mem_layer)
        x = proj_residual(x, oc, W["w_co"])
    x = mlp(x, W["g_mlp"], W["w_up"], W["w_down"])
    return x, z, kn, g, s5_re.reshape(nb, S5_GROUPS, S5_STATE), s5_im.reshape(nb, S5_GROUPS, S5_STATE), \
        c_new, n_new, m_new


def kernel(x_prompt, x_sample, mem_prompt, cache_fox_k, cache_fox_v, cache_fox_logf, page_table, state_s5_re, state_s5_im, state_mlstm_C, state_mlstm_n, state_mlstm_m, cache_mem_k, cache_mem_v, g_mix, w_in, s5_a_re, s5_a_im, s5_log_step, s5_b_re, s5_b_im, s5_c_re, s5_c_im, s5_d, s5_w_glu, s5_b_glu, fox_gq, fox_gk, fox_bf, ml_bi, ml_bf, ml_gn, w_br_s5, w_br_fox, w_br_ml, w_out, g_cross, w_cq, cross_gq, g_mem, w_mk, w_mv, cross_gk, w_co, g_mlp, w_up, w_down):
    depth = w_in.shape[0]
    bp, tp, _ = x_prompt.shape
    bs, ts, _ = x_sample.shape
    n_mem = mem_prompt.shape[1]

    xp = x_prompt.reshape(bp * tp, D_MODEL)
    xs = jnp.pad(x_sample, ((0, 0), (0, SAMPLE_T - ts), (0, 0))).reshape(bs * SAMPLE_T, D_MODEL)
    mem = mem_prompt.reshape(bp * n_mem, D_MODEL)
    zeros_p = (jnp.zeros((bp, S5_LANES), F32), jnp.zeros((bp, S5_LANES), F32))
    zeros_ml = (jnp.zeros((bp, ML_HEADS, ML_HD, ML_HD), F32), jnp.zeros((bp, ML_HEADS, ML_HD), F32),
                jnp.zeros((bp, ML_HEADS), F32))
    pool_kt = jnp.transpose(cache_fox_k, (0, 1, 3, 4, 2))
    pool_vt = jnp.transpose(cache_fox_v, (0, 1, 3, 4, 2))
    pool_ft = jnp.transpose(cache_fox_logf, (0, 1, 3, 2))
    st_p, st_s = [], []
    for l in range(depth):
        W = _layer_weights(l, g_mix, w_in, s5_a_re, s5_a_im, s5_log_step, s5_b_re, s5_b_im, s5_c_re, s5_c_im, s5_d,
                           s5_w_glu, s5_b_glu, fox_gq, fox_gk, fox_bf, ml_bi, ml_bf, ml_gn, w_br_s5, w_br_fox,
                           w_br_ml, w_out, g_cross, w_cq, cross_gq, g_mem, w_mk, w_mv, cross_gk, w_co, g_mlp, w_up,
                           w_down)
        mk_p = norm_matmul(mem, W["g_mem"], W["w_mk"], head_gain=W["cross_gk"])
        mv_p = norm_matmul(mem, W["g_mem"], W["w_mv"])

        def flash(prep, z):
            return fox_flash(prep[0], prep[6], prep[7], bp, tp, FLASH_TILE)

        xp, z, kn, g, s5r, s5i, c_new, n_new, m_new = _hybrid_layer(
            xp, W, bp, tp, SEQ_TILE, tp, zeros_p, zeros_ml, mk_p, mv_p, flash, True)
        st_p.append((kn.reshape(bp, tp, FOX_HEADS, FOX_HD),
                     z[:, COL_FV * 512:(COL_FV + 1) * 512].reshape(bp, tp, FOX_HEADS, FOX_HD),
                     g[:, LANE_FOXF:LANE_FOXF + FOX_HEADS].reshape(bp, tp, FOX_HEADS),
                     s5r, s5i, c_new, n_new, m_new,
                     mk_p.reshape(bp, n_mem, MEM_HEADS, MEM_HD), mv_p.reshape(bp, n_mem, MEM_HEADS, MEM_HD)))

        def decode(prep, z, layer=l):
            return fox_decode(page_table, prep[0], prep[1], z, prep[3], pool_kt, pool_vt, pool_ft, layer)

        xs, z, kn, g, s5r, s5i, c_new, n_new, m_new = _hybrid_layer(
            xs, W, bs, SAMPLE_T, SAMPLE_T, ts,
            (state_s5_re[l].reshape(bs, S5_LANES), state_s5_im[l].reshape(bs, S5_LANES)),
            (state_mlstm_C[l], state_mlstm_n[l], state_mlstm_m[l]),
            cache_mem_k.reshape(depth, bs, n_mem * MEM_HEADS, MEM_HD),
            cache_mem_v.reshape(depth, bs, n_mem * MEM_HEADS, MEM_HD), decode, False, mem_layer=l)
        st_s.append((kn.reshape(bs, SAMPLE_T, FOX_HEADS, FOX_HD)[:, :ts],
                     z[:, COL_FV * 512:(COL_FV + 1) * 512].reshape(bs, SAMPLE_T, FOX_HEADS, FOX_HD)[:, :ts],
                     g[:, LANE_FOXF:LANE_FOXF + FOX_HEADS].reshape(bs, SAMPLE_T, FOX_HEADS)[:, :ts],
                     s5r, s5i, c_new, n_new, m_new))
    outs_p = [jnp.stack(a) for a in zip(*st_p)]
    outs_s = [jnp.stack(a) for a in zip(*st_s)]
    yp = xp.reshape(bp, tp, D_MODEL)
    ys = xs.reshape(bs, SAMPLE_T, D_MODEL)[:, :ts]
    return (yp, ys, *outs_p, *outs_s)
```

```python
import functools
import math

import jax
import jax.numpy as jnp
import numpy as np
from jax import lax
from jax.experimental import pallas as pl
from jax.experimental.pallas import tpu as pltpu

F32 = jnp.float32
BF16 = jnp.bfloat16

LANES = 128
SUBLANES = 8
MIN_BF16_ROWS = 16
MXU_DIM = 256
VMEM_LIMIT_BYTES = 48 * 1024 * 1024

D_MODEL = 1024
S5_WIDTH = 512
S5_GROUP = 16
S5_GROUPS = 32
S5_STATE = 64
S5_LANES = S5_GROUPS * S5_STATE
FOX_HEADS = 8
FOX_HD = 64
FOX_WIDTH = 512
ML_HEADS = 4
ML_HD = 128
ML_WIDTH = 512
MEM_HEADS = 4
MEM_HD = 128
MEM_WIDTH = 512
D_FF = 4096
EPS = 1e-6
LOG2E = math.log2(math.e)
SPLITS =(S5_WIDTH, FOX_WIDTH, FOX_WIDTH, FOX_WIDTH, FOX_HEADS, ML_WIDTH, ML_WIDTH, ML_WIDTH,
          ML_HEADS, ML_HEADS, ML_WIDTH, 3 * D_MODEL)

COL_S5, COL_FQ, COL_FK, COL_FV, COL_MO, COL_SMALL = range(6)
COL_MQ, COL_MK, COL_MV = range(3)
LANE_FOXF = 0
LANE_MLI = 8
LANE_MLF = 12
SEQ_TILE = 256
FLASH_TILE = 512
FLASH_ROWS = 128
SAMPLE_T = 8


def _cparams(*sem):
    return pltpu.CompilerParams(dimension_semantics=sem, vmem_limit_bytes=VMEM_LIMIT_BYTES)


def _dot(a, b):
    return jnp.dot(a, b, preferred_element_type=F32)


def _dot_nt(a, b):
    return lax.dot_general(a, b, (((1,), (1,)), ((), ())), preferred_element_type=F32)


def _dot_tn(a, b):
    return lax.dot_general(a, b, (((0,), (0,)), ((), ())), preferred_element_type=F32)


def _split3(x):
    hi = x.astype(BF16)
    r1 = x - hi.astype(F32)
    mid = r1.astype(BF16)
    lo = (r1 - mid.astype(F32)).astype(BF16)
    return hi, mid, lo


def _dot_exact_rhs(x, ones_rhs):
    hi, mid, lo = _split3(x)
    return _dot(hi, ones_rhs) + _dot(mid, ones_rhs) + _dot(lo, ones_rhs)


def _dot_exact_lhs(ones_lhs, x):
    hi, mid, lo = _split3(x)
    return _dot(ones_lhs, hi) + _dot(ones_lhs, mid) + _dot(ones_lhs, lo)


def _pad_rows(x, n, fill=0.0):
    if x.shape[0] >= n:
        return x
    return jnp.concatenate([x, jnp.full((n - x.shape[0], x.shape[1]), fill, x.dtype)], axis=0)


def _lane_tile(x, n):
    return x if n == 1 else jnp.concatenate([x] * n, axis=1)


def _log_sigmoid(a):
    return jnp.minimum(a, 0.0) - jnp.log1p(jnp.exp(-jnp.abs(a)))


def _sigmoid(a):
    return 1.0 / (1.0 + jnp.exp(-a))


def _gelu_tanh(x):
    c = math.sqrt(2.0 / math.pi)
    return 0.5 * x * (1.0 + jnp.tanh(c * (x + 0.044715 * (x * x * x))))


def _rms(x, g):
    ms = jnp.mean(x * x, axis=-1, keepdims=True)
    return x * lax.rsqrt(ms + EPS) * g


def _norm_matmul_kernel(x_ref, g_ref, w_ref, *rest, head_norm):
    if head_norm:
        hg_ref, o_ref, hn_ref = rest
    else:
        o_ref, hn_ref = rest

    @pl.when(pl.program_id(1) == 0)
    def _():
        hn_ref[...] = _rms(x_ref[...], g_ref[...]).astype(BF16)

    y = _dot(hn_ref[...], w_ref[...])
    if head_norm:
        tn = y.shape[1]
        for s in range(tn // LANES):
            sl = slice(s * LANES, (s + 1) * LANES)
            o_ref[:, sl] = _rms(y[:, sl], hg_ref[...]).astype(o_ref.dtype)
    else:
        o_ref[...] = y.astype(o_ref.dtype)


def norm_matmul(x, g, w, head_gain=None, out_dtype=F32, tm=512, tn=512):
    m, k = x.shape
    n = w.shape[1]
    tm = min(tm, m)
    tn = min(tn, n)
    head_norm = head_gain is not None
    in_specs = [pl.BlockSpec((tm, k), lambda i, j: (i, 0)),
                pl.BlockSpec((1, k), lambda i, j: (0, 0)),
                pl.BlockSpec((k, tn), lambda i, j: (0, j))]
    args = [x, g.reshape(1, k), w]
    if head_norm:
        in_specs.append(pl.BlockSpec((1, LANES), lambda i, j: (0, 0)))
        args.append(head_gain.reshape(1, LANES))
    return pl.pallas_call(
        functools.partial(_norm_matmul_kernel, head_norm=head_norm),
        out_shape=jax.ShapeDtypeStruct((m, n), out_dtype),
        grid=(m // tm, n // tn),
        in_specs=in_specs,
        out_specs=pl.BlockSpec((tm, tn), lambda i, j: (i, j)),
        scratch_shapes=[pltpu.VMEM((tm, k), BF16)],
        compiler_params=_cparams("parallel", "arbitrary"),
        name="norm_matmul",
    )(*args)


def _prep_kernel(fq_ref, fk_ref, fv_ref, sm_ref, gq_ref, gk_ref, bias_ref, gmat_ref, tril_ref, sel_ref, aug_ref,
                 *rest, seq_rows, t_valid, augment):
    if augment:
        qx_ref, kn_ref, g_ref, cs_ref, grow_ref, csrow_ref, kx_ref, vb_ref, vt_ref, carry_ref = rest
    else:
        qx_ref, kn_ref, g_ref, cs_ref, carry_ref = rest
    c = pl.program_id(1)
    tc = sm_ref.shape[0]

    @pl.when(c == 0)
    def _():
        carry_ref[...] = jnp.zeros_like(carry_ref)

    gmat = gmat_ref[...]

    def head_rms(x, gain):
        x2 = x * x
        hi = x2.astype(BF16)
        lo = (x2 - hi.astype(F32)).astype(BF16)
        ss = (_dot(hi, gmat) + _dot(lo, gmat)) * (1.0 / FOX_HD)
        return x * lax.rsqrt(ss + EPS) * gain

    qn = head_rms(fq_ref[...], gq_ref[...]) * ((FOX_HD ** -0.5) * (LOG2E if augment else 1.0))
    kn = head_rms(fk_ref[...], gk_ref[...])
    if augment:
        kn_ref[...] = kn.T
        vt_ref[...] = fv_ref[...].T
    else:
        kn_ref[...] = kn

    a = sm_ref[...] + bias_ref[...]
    lane = lax.broadcasted_iota(jnp.int32, a.shape, 1)
    pos = (lax.broadcasted_iota(jnp.int32, a.shape, 0) + c * tc) % seq_rows
    is_i = (lane >= LANE_MLI) & (lane < LANE_MLF)
    used = lane < LANE_MLF + ML_HEADS
    valid = pos < t_valid
    g = jnp.where(is_i, a, _log_sigmoid(a))
    g = jnp.where(used, g, 0.0)
    g = jnp.where(valid, g, jnp.where(is_i, -jnp.inf, 0.0))
    gc = jnp.where(is_i, 0.0, g)
    cs = _dot_exact_lhs(tril_ref[...], gc)
    carry = carry_ref[...]
    csg = cs + jnp.where(lane < FOX_HEADS, carry, 0.0)
    carry_ref[...] = carry + cs[tc - 1:tc, :]
    g_ref[...] = g
    cs_ref[...] = csg
    if augment:
        grow_ref[...] = g.T
        csrow_ref[...] = csg.T

    nx = FOX_HEADS * LANES
    lanex = lax.broadcasted_iota(jnp.int32, (1, nx), 1)
    keep = ((lanex % LANES) >= FOX_HD) == ((lanex // LANES) % 2 == 1)
    q_exp = jnp.concatenate([qn[:, (h // 2) * LANES:(h // 2 + 1) * LANES] for h in range(FOX_HEADS)], axis=1)
    if not augment:
        qx_ref[...] = jnp.where(keep, q_exp, 0.0).astype(qx_ref.dtype)
        return
    k_exp = jnp.concatenate([kn[:, (h // 2) * LANES:(h // 2 + 1) * LANES] for h in range(FOX_HEADS)], axis=1)
    hi, mid, lo = _split3(jnp.where(lane < FOX_HEADS, csg * LOG2E, 0.0))
    packed = (hi.astype(F32) + pltpu.roll(mid.astype(F32), FOX_HEADS, 1)
              + pltpu.roll(lo.astype(F32), 2 * FOX_HEADS, 1)).astype(BF16)
    aug = aug_ref[...] + _dot(packed, sel_ref[...])
    qx_ref[...] = jnp.where(keep, q_exp, aug[:, 0:nx]).astype(qx_ref.dtype)
    kx_ref[...] = jnp.where(keep, k_exp, aug[:, nx:2 * nx]).astype(kx_ref.dtype)
    vb_ref[...] = fv_ref[...].astype(vb_ref.dtype)


PREP_SHORT_ROWS = 128


def gate_prep(z, gq, gk, bias_row, nb, t, tc, t_valid, augment):
    m = nb * t
    gmat = jnp.asarray(np.kron(np.eye(FOX_HEADS), np.ones((FOX_HD, FOX_HD))), BF16)
    if augment:
        nc, tp, grid_rows = t // tc, tc, nb
        tril = np.tril(np.ones((tc, tc)))
    else:
        tc = tp = PREP_SHORT_ROWS
        nc, grid_rows = 1, m // tc
        tril = np.kron(np.eye(tc // t), np.tril(np.ones((t, t))))
    tril = jnp.asarray(tril, BF16)
    nx = FOX_HEADS * LANES
    sel = np.zeros((LANES, 2 * nx), np.float32)
    aug_const = np.zeros((1, 2 * nx), np.float32)
    for h in range(FOX_HEADS):
        o = h * LANES + (0 if h % 2 else FOX_HD)
        for p in range(3):
            sel[p * FOX_HEADS + h, o + p] = 1.0
            sel[p * FOX_HEADS + h, nx + o + 3 + p] = -1.0
            aug_const[0, o + 3 + p] = 1.0
            aug_const[0, nx + o + p] = 1.0
    sel, aug_const = jnp.asarray(sel, BF16), jnp.asarray(aug_const, F32)
    row_map = lambda b, c: (b * nc + c, 0)
    const = lambda b, c: (0, 0)
    feat_major = jax.ShapeDtypeStruct((nb * FOX_WIDTH, t), F32)
    feat_spec = pl.BlockSpec((FOX_WIDTH, tc), lambda b, c: (b, c))
    out_shape = [jax.ShapeDtypeStruct((m, FOX_HEADS * LANES), BF16 if augment else F32),
                 feat_major if augment else jax.ShapeDtypeStruct((m, FOX_WIDTH), F32),
                 jax.ShapeDtypeStruct((m, LANES), F32),
                 jax.ShapeDtypeStruct((m, LANES), F32)]
    out_specs = [pl.BlockSpec((tc, FOX_HEADS * LANES), row_map),
                 feat_spec if augment else pl.BlockSpec((tc, FOX_WIDTH), row_map),
                 pl.BlockSpec((tc, LANES), row_map),
                 pl.BlockSpec((tc, LANES), row_map)]
    if augment:
        out_shape += [jax.ShapeDtypeStruct((nb * LANES, t), F32), jax.ShapeDtypeStruct((nb * LANES, t), F32),
                      jax.ShapeDtypeStruct((m, FOX_HEADS * LANES), BF16), jax.ShapeDtypeStruct((m, FOX_WIDTH), BF16),
                      feat_major]
        out_specs += [pl.BlockSpec((LANES, tc), lambda b, c: (b, c)), pl.BlockSpec((LANES, tc), lambda b, c: (b, c)),
                      pl.BlockSpec((tc, FOX_HEADS * LANES), row_map), pl.BlockSpec((tc, FOX_WIDTH), row_map),
                      feat_spec]
    return pl.pallas_call(
        functools.partial(_prep_kernel, seq_rows=t, t_valid=t_valid, augment=augment),
        out_shape=tuple(out_shape),
        grid=(grid_rows, nc),
        in_specs=[pl.BlockSpec((tc, 512), lambda b, c: (b * nc + c, COL_FQ)),
                  pl.BlockSpec((tc, 512), lambda b, c: (b * nc + c, COL_FK)),
                  pl.BlockSpec((tc, 512), lambda b, c: (b * nc + c, COL_FV)),
                  pl.BlockSpec((tc, LANES), lambda b, c: (b * nc + c, COL_SMALL * 4)),
                  pl.BlockSpec((1, 512), const), pl.BlockSpec((1, 512), const),
                  pl.BlockSpec((1, LANES), const),
                  pl.BlockSpec((512, 512), const), pl.BlockSpec((tp, tp), const),
                  pl.BlockSpec((LANES, 2 * nx), const), pl.BlockSpec((1, 2 * nx), const)],
        out_specs=tuple(out_specs),
        scratch_shapes=[pltpu.VMEM((1, LANES), F32)],
        compiler_params=_cparams("parallel", "arbitrary"),
        name="gate_prep",
    )(z, z, z, z, gq, gk, bias_row, gmat, tril, sel, aug_const)


S5_SCAN_LANES = 512
S5_SHORT_SEQS = 16


def _cmul(ar, ai, br, bi):
    return ar * br - ai * bi, ar * bi + ai * br


def _s5_kernel(u_ref, h0r_ref, h0i_ref, perm_ref, permt_ref, bre_ref, bim_ref, lam_ref, pseg_ref, pk_ref,
               cre_ref, cim_ref, d_ref, wglu_ref, bglu_ref, y_ref, hlr_ref, hli_ref, hr_ref, hi_ref, cr_ref, ci_ref,
               *, t_last, seqs):
    c = pl.program_id(1)
    nc = pl.num_programs(1)
    tc = u_ref.shape[0]
    R = tc // SUBLANES if seqs == 1 else 1
    half = S5_WIDTH // 2
    hl = S5_LANES // 2

    if seqs == 1:
        @pl.when(c == 0)
        def _():
            cr_ref[...] = h0r_ref[0]
            ci_ref[...] = h0i_ref[0]

    u = u_ref[...]
    ub = _pad_rows(u, MIN_BF16_ROWS).astype(BF16)
    if R > 1:
        ub = _dot(perm_ref[...], ub).astype(BF16)
    for j in range(2):
        uj = ub[:, j * half:(j + 1) * half]
        hr_ref[:, j * hl:(j + 1) * hl] = _dot(uj, bre_ref[j])[0:tc]
        hi_ref[:, j * hl:(j + 1) * hl] = _dot(uj, bim_ref[j])[0:tc]

    sub = lax.broadcasted_iota(jnp.int32, (SUBLANES, S5_SCAN_LANES), 0)
    for lc in range(S5_LANES // S5_SCAN_LANES):
        ls = slice(lc * S5_SCAN_LANES, (lc + 1) * S5_SCAN_LANES)
        lam_r = jnp.broadcast_to(lam_ref[0:1, ls], sub.shape)
        lam_i = jnp.broadcast_to(lam_ref[1:2, ls], sub.shape)
        if seqs > 1:
            def seq_scan(g, _):
                rows = pl.ds(pl.multiple_of(g * SUBLANES, SUBLANES), SUBLANES)
                s_r = jnp.where(sub == 0, jnp.broadcast_to(h0r_ref[0, pl.ds(g, 1), ls], sub.shape), 0.0)
                s_i = jnp.where(sub == 0, jnp.broadcast_to(h0i_ref[0, pl.ds(g, 1), ls], sub.shape), 0.0)
                m_r, m_i = _cmul(lam_r, lam_i, s_r, s_i)
                x_r = hr_ref[rows, ls] + m_r
                x_i = hi_ref[rows, ls] + m_i
                for j, lag in enumerate((1, 2, 4)):
                    m_r, m_i = _cmul(pseg_ref[2 * j, :, ls], pseg_ref[2 * j + 1, :, ls],
                                     pltpu.roll(x_r, lag, 0), pltpu.roll(x_i, lag, 0))
                    x_r, x_i = x_r + m_r, x_i + m_i
                hr_ref[rows, ls] = x_r
                hi_ref[rows, ls] = x_i
                hlr_ref[0, pl.ds(g, 1), ls] = x_r[t_last:t_last + 1, :]
                hli_ref[0, pl.ds(g, 1), ls] = x_i[t_last:t_last + 1, :]
                return 0

            lax.fori_loop(0, seqs, seq_scan, 0, unroll=2)
            continue
        init_r = jnp.where(sub == 0, jnp.broadcast_to(cr_ref[:, ls], sub.shape), 0.0)
        init_i = jnp.where(sub == 0, jnp.broadcast_to(ci_ref[:, ls], sub.shape), 0.0)

        def local_step(k, carry):
            h_r, h_i = carry
            rows = pl.ds(pl.multiple_of(k * SUBLANES, SUBLANES), SUBLANES)
            m_r, m_i = _cmul(lam_r, lam_i, h_r, h_i)
            h_r = m_r + hr_ref[rows, ls]
            h_i = m_i + hi_ref[rows, ls]
            hr_ref[rows, ls] = h_r
            hi_ref[rows, ls] = h_i
            return h_r, h_i

        e_r, e_i = lax.fori_loop(0, R, local_step, (init_r, init_i), unroll=min(R, 4))
        for j, lag in enumerate((1, 2, 4)):
            m_r, m_i = _cmul(pseg_ref[2 * j, :, ls], pseg_ref[2 * j + 1, :, ls],
                             pltpu.roll(e_r, lag, 0), pltpu.roll(e_i, lag, 0))
            e_r, e_i = e_r + m_r, e_i + m_i
        cr_ref[:, ls] = e_r[SUBLANES - 1:SUBLANES, :]
        ci_ref[:, ls] = e_i[SUBLANES - 1:SUBLANES, :]
        in_r = jnp.where(sub == 0, 0.0, pltpu.roll(e_r, 1, 0))
        in_i = jnp.where(sub == 0, 0.0, pltpu.roll(e_i, 1, 0))

        def fix_step(k, _):
            rows = pl.ds(pl.multiple_of(k * SUBLANES, SUBLANES), SUBLANES)
            p_r = jnp.broadcast_to(pk_ref[0, pl.ds(k, 1), ls], sub.shape)
            p_i = jnp.broadcast_to(pk_ref[1, pl.ds(k, 1), ls], sub.shape)
            m_r, m_i = _cmul(p_r, p_i, in_r, in_i)
            hr_ref[rows, ls] += m_r
            hi_ref[rows, ls] += m_i
            return 0

        lax.fori_loop(0, R, fix_step, 0, unroll=min(R, 4))

    if seqs == 1:
        @pl.when(c == nc - 1)
        def _():
            tl = t_last % tc
            pos = (tl % R) * SUBLANES + tl // R
            hlr_ref[0] = hr_ref[pos:pos + 1, :]
            hli_ref[0] = hi_ref[pos:pos + 1, :]

    hrb = _pad_rows(hr_ref[...], MIN_BF16_ROWS).astype(BF16)
    hib = _pad_rows(hi_ref[...], MIN_BF16_ROWS).astype(BF16)
    ys = [(_dot(hrb[:, j * hl:(j + 1) * hl], cre_ref[j]) + _dot(hib[:, j * hl:(j + 1) * hl], cim_ref[j]))
          for j in range(2)]
    y = jnp.concatenate(ys, axis=1)
    if R > 1:
        y = _dot_exact_lhs(permt_ref[...], y)
    y = _gelu_tanh(y[0:tc] + d_ref[...] * u)
    gate = _dot(_pad_rows(y, MIN_BF16_ROWS).astype(BF16), wglu_ref[...])[0:tc]
    y_ref[...] = y * _sigmoid(gate + bglu_ref[...])


def s5_mixer(z, h0r, h0i, sp, nb, t, tc, t_last, seqs=1):
    assert seqs == 1 or (tc == SUBLANES and t == tc)
    nc = t // tc
    R = tc // SUBLANES
    tc = tc * seqs
    nbg = nb // seqs
    tperm = max(tc, MIN_BF16_ROWS) if seqs == 1 else MIN_BF16_ROWS
    perm = np.zeros((tperm, tperm), np.float32)
    if seqs == 1:
        for s in range(SUBLANES):
            for k in range(R):
                perm[k * SUBLANES + s, s * R + k] = 1.0
    permt = jnp.asarray(perm.T, BF16)
    perm = jnp.asarray(perm, BF16)
    const = lambda b, c: (0, 0)
    const3 = lambda b, c: (0, 0, 0)
    state_spec = pl.BlockSpec((1, seqs, S5_LANES), lambda b, c: (b, 0, 0))
    tabs = sp["tabs"][R]
    return pl.pallas_call(
        functools.partial(_s5_kernel, t_last=t_last, seqs=seqs),
        out_shape=(jax.ShapeDtypeStruct((nb * t, S5_WIDTH), F32),
                   jax.ShapeDtypeStruct((nbg, seqs, S5_LANES), F32),
                   jax.ShapeDtypeStruct((nbg, seqs, S5_LANES), F32)),
        grid=(nbg, nc),
        in_specs=[pl.BlockSpec((tc, 512), lambda b, c: (b * nc + c, COL_S5)),
                  state_spec, state_spec,
                  pl.BlockSpec((tperm, tperm), const), pl.BlockSpec((tperm, tperm), const),
                  pl.BlockSpec((2, S5_WIDTH // 2, S5_LANES // 2), const3),
                  pl.BlockSpec((2, S5_WIDTH // 2, S5_LANES // 2), const3),
                  pl.BlockSpec((2, S5_LANES), const),
                  pl.BlockSpec((6, SUBLANES, S5_LANES), const3),
                  pl.BlockSpec((2, R, S5_LANES), const3),
                  pl.BlockSpec((2, S5_LANES // 2, S5_WIDTH // 2), const3),
                  pl.BlockSpec((2, S5_LANES // 2, S5_WIDTH // 2), const3),
                  pl.BlockSpec((1, S5_WIDTH), const),
                  pl.BlockSpec((S5_WIDTH, S5_WIDTH), const), pl.BlockSpec((1, S5_WIDTH), const)],
        out_specs=(pl.BlockSpec((tc, S5_WIDTH), lambda b, c: (b * nc + c, 0)), state_spec, state_spec),
        scratch_shapes=[pltpu.VMEM((tc, S5_LANES), F32), pltpu.VMEM((tc, S5_LANES), F32),
                        pltpu.VMEM((1, S5_LANES), F32), pltpu.VMEM((1, S5_LANES), F32)],
        compiler_params=_cparams("parallel", "arbitrary"),
        name="s5_mixer",
    )(z, h0r.reshape(nbg, seqs, S5_LANES), h0i.reshape(nbg, seqs, S5_LANES), perm, permt, sp["bre"], sp["bim"],
      tabs["lam"],
      tabs["pseg"], tabs["pk"], sp["cre"], sp["cim"], sp["d"], sp["wglu"], sp["bglu"])


def s5_params(a_re, a_im, log_step, b_re, b_im, c_re, c_im, d_skip, w_glu, b_glu, seg_lens):
    dt = jnp.exp(log_step)[:, None]
    mag = jnp.exp(a_re * dt)
    lr = (mag * jnp.cos(a_im * dt)).reshape(1, S5_LANES)
    li = (mag * jnp.sin(a_im * dt)).reshape(1, S5_LANES)
    den = a_re * a_re + a_im * a_im
    xr, xi = lr.reshape(a_re.shape) - 1.0, li.reshape(a_re.shape)
    fr = (xr * a_re + xi * a_im) / den
    fi = (xi * a_re - xr * a_im) / den
    bbr = fr[..., None] * b_re - fi[..., None] * b_im
    bbi = fr[..., None] * b_im + fi[..., None] * b_re
    gh = S5_GROUPS // 2
    eye = jnp.eye(gh, dtype=F32)

    def in_mat(b):
        return jnp.einsum('jgpc,gh->jgchp', b.reshape(2, gh, S5_STATE, S5_GROUP), eye).reshape(
            2, S5_WIDTH // 2, S5_LANES // 2).astype(BF16)

    def out_mat(cm):
        return jnp.einsum('jgcp,gh->jgphc', cm.reshape(2, gh, S5_GROUP, S5_STATE), eye).reshape(
            2, S5_LANES // 2, S5_WIDTH // 2).astype(BF16)

    def powers(pr, pi, n):
        tr, ti, cnt = pr, pi, 1
        while cnt < n:
            lr_, li_ = tr[cnt - 1:cnt], ti[cnt - 1:cnt]
            nr, ni = _cmul(tr, ti, lr_, li_)
            tr, ti, cnt = jnp.concatenate([tr, nr], axis=0), jnp.concatenate([ti, ni], axis=0), 2 * cnt
        return tr, ti

    sub = jnp.arange(SUBLANES)[:, None]
    tabs = {}
    for R in seg_lens:
        kr, ki = powers(lr, li, R)
        sr, si = powers(kr[R - 1:R], ki[R - 1:R], 4)
        pseg = []
        for lag in (1, 2, 4):
            msk = (sub >= lag).astype(F32)
            pseg += [msk * sr[lag - 1:lag], msk * si[lag - 1:lag]]
        tabs[R] = dict(lam=jnp.concatenate([lr, li], axis=0), pseg=jnp.stack(pseg), pk=jnp.stack([kr, ki]))
    return dict(bre=in_mat(bbr), bim=in_mat(bbi), cre=out_mat(c_re), cim=out_mat(-c_im), tabs=tabs,
                d=d_skip.reshape(1, S5_WIDTH), wglu=w_glu.astype(BF16), bglu=b_glu.reshape(1, S5_WIDTH))


def _fox_flash_kernel(it_ref, jt_ref, qx_ref, kx_ref, v_ref, o_ref, m_ref, l_ref, acc_ref, s_ref, p_ref):
    i = it_ref[pl.program_id(1)]
    j = jt_ref[pl.program_id(1)]
    tq = qx_ref.shape[0]
    tk = kx_ref.shape[0]

    @pl.when(j == 0)
    def _():
        m_ref[...] = jnp.full_like(m_ref, -jnp.inf)
        l_ref[...] = jnp.zeros_like(l_ref)
        acc_ref[...] = jnp.zeros_like(acc_ref)

    def step(masked):
        nr, ncol = tq // FLASH_ROWS, tk // LANES
        if masked:
            diff = (lax.broadcasted_iota(jnp.int32, (FLASH_ROWS, LANES), 0) -
                    lax.broadcasted_iota(jnp.int32, (FLASH_ROWS, LANES), 1))
        for h in range(FOX_HEADS):
            hs = slice(h * LANES, (h + 1) * LANES)
            ps = slice((h // 2) * LANES, (h // 2 + 1) * LANES)
            s_buf, p_buf = s_ref.at[h % 2], p_ref.at[h % 2]
            if h == 0:
                s_buf[...] = _dot_nt(qx_ref[:, hs], kx_ref[:, hs])
            if h + 1 < FOX_HEADS:
                nhs = slice((h + 1) * LANES, (h + 2) * LANES)
                s_ref[(h + 1) % 2] = _dot_nt(qx_ref[:, nhs], kx_ref[:, nhs])
            for r in range(nr):
                rs = slice(r * FLASH_ROWS, (r + 1) * FLASH_ROWS)

                live = [cidx for cidx in range(ncol) if not (masked and cidx * LANES >= (r + 1) * FLASH_ROWS)]

                def piece(cidx):
                    sc = s_buf[rs, cidx * LANES:(cidx + 1) * LANES]
                    if masked and (cidx + 1) * LANES - 1 > r * FLASH_ROWS:
                        sc = jnp.where(diff >= (cidx * LANES - r * FLASH_ROWS), sc, -jnp.inf)
                    return sc

                mx = piece(live[0])
                for cidx in live[1:]:
                    mx = jnp.maximum(mx, piece(cidx))
                m_prev = m_ref[h, rs, :]
                m_new = jnp.maximum(m_prev, jnp.max(mx, axis=-1, keepdims=True))
                alpha = jnp.exp2(m_prev - m_new)
                lsum = jnp.zeros((FLASH_ROWS, LANES), F32)
                for cidx in range(ncol):
                    cs_ = slice(cidx * LANES, (cidx + 1) * LANES)
                    if cidx not in live:
                        p_buf[rs, cs_] = jnp.zeros((FLASH_ROWS, LANES), BF16)
                        continue
                    pc = jnp.exp2(piece(cidx) - m_new)
                    lsum = lsum + pc
                    p_buf[rs, cs_] = pc.astype(BF16)
                l_ref[h, rs, :] = alpha * l_ref[h, rs, :] + lsum
                acc_ref[h, rs, :] = alpha * acc_ref[h, rs, :]
                m_ref[h, rs, :] = m_new
            acc_ref[h] += _dot(p_buf[...], v_ref[:, ps])

    @pl.when(j < i)
    def _():
        step(False)

    @pl.when(j == i)
    def _():
        step(True)
        lane = lax.broadcasted_iota(jnp.int32, (1, LANES), 1)
        for p in range(FOX_HEADS // 2):
            lo = acc_ref[2 * p] / jnp.sum(l_ref[2 * p], axis=-1, keepdims=True)
            hi = acc_ref[2 * p + 1] / jnp.sum(l_ref[2 * p + 1], axis=-1, keepdims=True)
            o_ref[:, p * LANES:(p + 1) * LANES] = jnp.where(lane < FOX_HD, lo, hi)


def fox_flash(qx, kx, vb, nb, t, tile):
    assert tile % FLASH_ROWS == 0
    nt = t // tile
    pairs = [(i, j) for i in range(nt) for j in range(i + 1)]
    it = jnp.asarray([p[0] for p in pairs], jnp.int32)
    jt = jnp.asarray([p[1] for p in pairs], jnp.int32)
    return pl.pallas_call(
        _fox_flash_kernel,
        out_shape=jax.ShapeDtypeStruct((nb * t, FOX_WIDTH), F32),
        grid_spec=pltpu.PrefetchScalarGridSpec(
            num_scalar_prefetch=2,
            grid=(nb, len(pairs)),
            in_specs=[pl.BlockSpec((tile, FOX_HEADS * LANES), lambda b, p, it, jt: (b * nt + it[p], 0)),
                      pl.BlockSpec((tile, FOX_HEADS * LANES), lambda b, p, it, jt: (b * nt + jt[p], 0)),
                      pl.BlockSpec((tile, FOX_WIDTH), lambda b, p, it, jt: (b * nt + jt[p], 0))],
            out_specs=pl.BlockSpec((tile, FOX_WIDTH), lambda b, p, it, jt: (b * nt + it[p], 0)),
            scratch_shapes=[pltpu.VMEM((FOX_HEADS, tile, LANES), F32), pltpu.VMEM((FOX_HEADS, tile, LANES), F32),
                            pltpu.VMEM((FOX_HEADS, tile, LANES), F32),
                            pltpu.VMEM((2, tile, tile), F32), pltpu.VMEM((2, tile, tile), BF16)]),
        compiler_params=_cparams("parallel", "arbitrary"),
        name="fox_flash",
    )(it, jt, qx, kx, vb)


def _fox_decode_kernel(pt_ref, qb_ref, kn_ref, vn_ref, cs_ref, *rest, n_pages):
    k_refs = rest[0:n_pages]
    v_refs = rest[n_pages:2 * n_pages]
    f_refs = rest[2 * n_pages:3 * n_pages]
    triu_ref, o_ref, kpad_ref, vpad_ref = rest[3 * n_pages:]
    page = kpad_ref.shape[0]
    nrow = FOX_HEADS * SAMPLE_T
    qb = qb_ref[...].astype(BF16)

    def per_query(x):
        return jnp.concatenate([x] * SAMPLE_T, axis=0)

    fcat = jnp.concatenate([f_refs[i][0, 0] for i in range(n_pages)], axis=0)
    cum_in = _dot_exact_rhs(fcat, triu_ref[...])
    totals = jnp.broadcast_to(cum_in[:, page - 1:page], cum_in.shape)
    off = jnp.zeros((FOX_HEADS, page), F32)
    ss = []
    for i in range(n_pages):
        rows = slice(i * FOX_HEADS, (i + 1) * FOX_HEADS)
        kt = k_refs[i][0, 0].reshape(FOX_WIDTH, page).astype(BF16)
        ss.append(_dot(qb, kt) - per_query(cum_in[rows] + off))
        off = off + totals[rows]
    kpad_ref[...] = jnp.zeros_like(kpad_ref)
    vpad_ref[...] = jnp.zeros_like(vpad_ref)
    kpad_ref[0:SAMPLE_T, :] = kn_ref[...]
    vpad_ref[0:SAMPLE_T, :] = vn_ref[...]
    csrow = _pad_rows(cs_ref[...], page).T[LANE_FOXF:LANE_FOXF + FOX_HEADS]
    s_new = _dot_nt(qb, kpad_ref[...].astype(BF16)) - per_query(csrow + off)
    tq_idx = lax.broadcasted_iota(jnp.int32, (nrow, page), 0) // FOX_HEADS
    tk_idx = lax.broadcasted_iota(jnp.int32, (nrow, page), 1)
    ss.append(jnp.where(tk_idx <= tq_idx, s_new, -jnp.inf))

    m = ss[0]
    for s in ss[1:]:
        m = jnp.maximum(m, s)
    m = jnp.max(m, axis=-1, keepdims=True)
    lsum = jnp.zeros((nrow, page), F32)
    acc = jnp.zeros((nrow, FOX_WIDTH), F32)
    for i, s in enumerate(ss):
        p = jnp.exp(s - m)
        lsum = lsum + p
        if i < n_pages:
            acc = acc + _dot_nt(p.astype(BF16), v_refs[i][0, 0].reshape(FOX_WIDTH, page).astype(BF16))
        else:
            acc = acc + _dot(p.astype(BF16), vpad_ref[...].astype(BF16))
    acc = acc / jnp.sum(lsum, axis=-1, keepdims=True)
    lane = lax.broadcasted_iota(jnp.int32, (nrow, FOX_WIDTH), 1) // FOX_HD
    head = lax.broadcasted_iota(jnp.int32, (nrow, FOX_WIDTH), 0) % FOX_HEADS
    picked = jnp.where(lane == head, acc, 0.0)
    o_ref[...] = jnp.sum(picked.reshape(SAMPLE_T, FOX_HEADS, FOX_WIDTH), axis=1)


def fox_decode(page_table, qx, kn, z, cs, pool_kt, pool_vt, pool_ft, layer):
    nb, n_pages = page_table.shape
    page = pool_kt.shape[-1]
    assert page == LANES
    nrow = FOX_HEADS * SAMPLE_T
    pt = page_table.reshape(-1)
    triu = jnp.asarray(np.triu(np.ones((page, page))), BF16)
    pair_of_head = jnp.asarray(np.arange(FOX_HEADS)[:, None] // 2 == np.arange(FOX_HEADS // 2)[None, :], F32)
    qb = (qx.reshape(nb, SAMPLE_T, FOX_HEADS, 1, LANES) * pair_of_head[None, None, :, :, None]).reshape(
        nb * nrow, FOX_WIDTH)

    def pg5(i):
        return lambda b, pt: (layer, pt[b * n_pages + i], 0, 0, 0)

    def pg4(i):
        return lambda b, pt: (layer, pt[b * n_pages + i], 0, 0)

    row = lambda b, pt: (b, 0)
    in_specs = [pl.BlockSpec((nrow, FOX_WIDTH), row),
                pl.BlockSpec((SAMPLE_T, FOX_WIDTH), row),
                pl.BlockSpec((SAMPLE_T, FOX_WIDTH), lambda b, pt: (b, COL_FV)),
                pl.BlockSpec((SAMPLE_T, LANES), row)]
    in_specs += [pl.BlockSpec((1, 1, FOX_HEADS, FOX_HD, page), pg5(i)) for i in range(n_pages)]
    in_specs += [pl.BlockSpec((1, 1, FOX_HEADS, FOX_HD, page), pg5(i)) for i in range(n_pages)]
    in_specs += [pl.BlockSpec((1, 1, FOX_HEADS, page), pg4(i)) for i in range(n_pages)]
    in_specs += [pl.BlockSpec((page, page), lambda b, pt: (0, 0))]
    return pl.pallas_call(
        functools.partial(_fox_decode_kernel, n_pages=n_pages),
        out_shape=jax.ShapeDtypeStruct((nb * SAMPLE_T, FOX_WIDTH), F32),
        grid_spec=pltpu.PrefetchScalarGridSpec(
            num_scalar_prefetch=1,
            grid=(nb,),
            in_specs=in_specs,
            out_specs=pl.BlockSpec((SAMPLE_T, FOX_WIDTH), row),
            scratch_shapes=[pltpu.VMEM((page, FOX_WIDTH), F32), pltpu.VMEM((page, FOX_WIDTH), F32)]),
        compiler_params=_cparams("parallel"),
        name="fox_decode",
    )(pt, qb, kn, z, cs, *([pool_kt] * n_pages), *([pool_vt] * n_pages), *([pool_ft] * n_pages), triu)


def _mlstm_kernel(q_ref, k_ref, v_ref, o_ref, g_ref, cs_ref, *rest, short):
    if short:
        c0_ref, n0_ref, m0_ref, gn_ref, y_ref, cout_ref, nout_ref, mout_ref, c_ref, n_ref, m_ref = rest
    else:
        (grow_ref, csrow_ref, c0_ref, n0_ref, m0_ref, gn_ref, y_ref, cout_ref, nout_ref, mout_ref,
         c_ref, n_ref, m_ref) = rest
    c = pl.program_id(1)
    nc = pl.num_programs(1)
    L = q_ref.shape[0]

    @pl.when(c == 0)
    def _():
        c_ref[...] = c0_ref[0, 0]
        n_ref[...] = n0_ref[0]
        m_ref[...] = m0_ref[0]

    Lk = max(L, LANES)
    Lq = max(L, MIN_BF16_ROWS)
    g = g_ref[...]
    cs = cs_ref[...]
    if short:
        grow = _pad_rows(g, Lk).T
        csrow = _pad_rows(cs, Lk).T
    else:
        grow = grow_ref[...]
        csrow = csrow_ref[...]
    causal = (lax.broadcasted_iota(jnp.int32, (Lq, Lk), 0) >= lax.broadcasted_iota(jnp.int32, (Lq, Lk), 1))
    for h in range(ML_HEADS):
        hs = slice(h * ML_HD, (h + 1) * ML_HD)
        scale = ML_HD ** -0.5
        qb = _pad_rows(q_ref[:, hs], Lq).astype(BF16)
        kb = _pad_rows(k_ref[:, hs], Lk).astype(BF16)
        vb = _pad_rows(v_ref[:, hs], Lk).astype(BF16)
        qh, kh, vh = qb.astype(F32), kb.astype(F32), vb.astype(F32)
        bcol_k = _pad_rows(cs[:, LANE_MLF + h:LANE_MLF + h + 1], Lk)
        bcol = bcol_k[0:Lq]
        icol = _pad_rows(g[:, LANE_MLI + h:LANE_MLI + h + 1], Lk, -jnp.inf)
        brow = csrow[LANE_MLF + h:LANE_MLF + h + 1, :]
        irow = grow[LANE_MLI + h:LANE_MLI + h + 1, :]
        m_prev = m_ref[0:1, h:h + 1]
        log_d = jnp.where(causal, bcol - brow + irow, -jnp.inf)
        log_inter = bcol + m_prev
        m_t = jnp.maximum(log_inter, jnp.max(log_d, axis=-1, keepdims=True))
        d_w = jnp.exp(log_d - m_t)
        inter_w = jnp.exp(log_inter - m_t)
        s = _dot_nt(qb, kb) * (d_w * scale)
        ch = c_ref[h]
        n_row = n_ref[h:h + 1, :]
        num = _dot(s.astype(BF16), vb) + inter_w * _dot_nt(qb, ch.astype(BF16))
        den = jnp.sum(s, axis=-1, keepdims=True) + inter_w * jnp.sum(qh * n_row, axis=-1, keepdims=True)
        hh = num / jnp.maximum(jnp.abs(den), jnp.exp(-m_t))
        y_ref[:, hs] = _rms(hh[0:L], gn_ref[...]) * _sigmoid(o_ref[:, hs])
        m_end = m_t[L - 1:L, :]
        a_end = inter_w[L - 1:L, :]
        w_col = jnp.exp(bcol[L - 1:L, :] - bcol_k + icol - m_end) * scale
        c_ref[h] = a_end * ch + _dot_tn((vh * w_col).astype(BF16), kb)
        n_ref[h:h + 1, :] = a_end * n_row + jnp.sum(kh * w_col, axis=0, keepdims=True)
        m_ref[0:1, h:h + 1] = m_end

    @pl.when(c == nc - 1)
    def _():
        cout_ref[0] = c_ref[...]
        nout_ref[0] = n_ref[...]
        mout_ref[0] = m_ref[...]


def mlstm(za, zb, g, cs, grow, csrow, c0, n0, m0, gn, nb, t, L, c_layer):
    nc = t // L
    rows = lambda b, c: (b * nc + c, 0)
    rr = 2 * SUBLANES
    m0p = jnp.zeros((nb, 1, LANES), F32).at[:, 0, :ML_HEADS].set(m0)
    short = grow is None
    row_specs = [] if short else [pl.BlockSpec((rr, L), lambda b, c: (b * (LANES // rr), c))] * 2
    row_args = [] if short else [grow, csrow]
    outs = pl.pallas_call(
        functools.partial(_mlstm_kernel, short=short),
        out_shape=(jax.ShapeDtypeStruct((nb * t, ML_WIDTH), F32),
                   jax.ShapeDtypeStruct((nb, ML_HEADS, ML_HD, ML_HD), F32),
                   jax.ShapeDtypeStruct((nb, ML_HEADS, ML_HD), F32),
                   jax.ShapeDtypeStruct((nb, 1, LANES), F32)),
        grid=(nb, nc),
        in_specs=[pl.BlockSpec((L, 512), lambda b, c: (b * nc + c, COL_MQ)),
                  pl.BlockSpec((L, 512), lambda b, c: (b * nc + c, COL_MK)),
                  pl.BlockSpec((L, 512), lambda b, c: (b * nc + c, COL_MV)),
                  pl.BlockSpec((L, 512), lambda b, c: (b * nc + c, COL_MO)),
                  pl.BlockSpec((L, LANES), rows), pl.BlockSpec((L, LANES), rows), *row_specs,
                  pl.BlockSpec((1, 1, ML_HEADS, ML_HD, ML_HD), lambda b, c: (c_layer, b, 0, 0, 0)),
                  pl.BlockSpec((1, ML_HEADS, ML_HD), lambda b, c: (b, 0, 0)),
                  pl.BlockSpec((1, 1, LANES), lambda b, c: (b, 0, 0)),
                  pl.BlockSpec((1, ML_HD), lambda b, c: (0, 0))],
        out_specs=(pl.BlockSpec((L, ML_WIDTH), rows),
                   pl.BlockSpec((1, ML_HEADS, ML_HD, ML_HD), lambda b, c: (b, 0, 0, 0)),
                   pl.BlockSpec((1, ML_HEADS, ML_HD), lambda b, c: (b, 0, 0)),
                   pl.BlockSpec((1, 1, LANES), lambda b, c: (b, 0, 0))),
        scratch_shapes=[pltpu.VMEM((ML_HEADS, ML_HD, ML_HD), F32), pltpu.VMEM((ML_HEADS, ML_HD), F32),
                        pltpu.VMEM((1, LANES), F32)],
        compiler_params=_cparams("parallel", "arbitrary"),
        name="mlstm",
    )(zb, zb, zb, za, g, cs, *row_args, c0, n0, m0p, gn.reshape(1, ML_HD))
    y, c_new, n_new, m_new = outs
    return y, c_new, n_new, m_new[:, 0, :ML_HEADS]


def _merge_kernel(x_ref, g_ref, wg_ref, ys_ref, yf_ref, ym_ref, ws_ref, wf_ref, wm_ref, wo_ref, o_ref):
    x = x_ref[...]
    hn = _rms(x, g_ref[...]).astype(BF16)
    branches = (ys_ref, ws_ref), (yf_ref, wf_ref), (ym_ref, wm_ref)
    merged = None
    for b, (y_ref, w_ref) in enumerate(branches):
        gate = _sigmoid(_dot(hn, wg_ref[:, b * D_MODEL:(b + 1) * D_MODEL]))
        term = gate * _dot(y_ref[...].astype(BF16), w_ref[...])
        merged = term if merged is None else merged + term
    o_ref[...] = x + _dot(merged.astype(BF16), wo_ref[...])


def merge_out(x, g_mix, w_gates, ys, yf, ym, ws, wf, wm, wo, tm=256):
    m = x.shape[0]
    tm = min(tm, m)
    row = lambda i: (i, 0)
    const = lambda i: (0, 0)
    return pl.pallas_call(
        _merge_kernel,
        out_shape=jax.ShapeDtypeStruct((m, D_MODEL), F32),
        grid=(m // tm,),
        in_specs=[pl.BlockSpec((tm, D_MODEL), row), pl.BlockSpec((1, D_MODEL), const),
                  pl.BlockSpec((D_MODEL, 3 * D_MODEL), const),
                  pl.BlockSpec((tm, 512), row), pl.BlockSpec((tm, 512), row), pl.BlockSpec((tm, 512), row),
                  pl.BlockSpec((512, D_MODEL), const), pl.BlockSpec((512, D_MODEL), const),
                  pl.BlockSpec((512, D_MODEL), const), pl.BlockSpec((D_MODEL, D_MODEL), const)],
        out_specs=pl.BlockSpec((tm, D_MODEL), row),
        compiler_params=_cparams("parallel"),
        name="merge_out",
    )(x, g_mix.reshape(1, D_MODEL), w_gates, ys, yf, ym, ws, wf, wm, wo)


def _cross_block_kernel(x_ref, g_ref, wq_ref, gq_ref, k_ref, v_ref, wo_ref, o_ref):
    x = x_ref[...]
    q = _dot(_rms(x, g_ref[...]).astype(BF16), wq_ref[...])
    outs = []
    for h in range(MEM_HEADS):
        hs = slice(h * MEM_HD, (h + 1) * MEM_HD)
        qh = _rms(q[:, hs], gq_ref[...]).astype(BF16)
        s = _dot_nt(qh, k_ref[:, hs].astype(BF16)) * (MEM_HD ** -0.5)
        m = jnp.max(s, axis=-1, keepdims=True)
        p = jnp.exp(s - m)
        l = jnp.sum(p, axis=-1, keepdims=True)
        outs.append((_dot(p.astype(BF16), v_ref[:, hs].astype(BF16)) / l).astype(BF16))
    o_ref[...] = x + _dot(jnp.concatenate(outs, axis=1), wo_ref[...])


def cross_block(x, g, wq, gq, mem_k, mem_v, wo, nb, t, tq):
    nq = t // tq
    n_mem = mem_k.shape[0] // nb
    const = lambda b, i: (0, 0)
    row = lambda b, i: (b * nq + i, 0)
    return pl.pallas_call(
        _cross_block_kernel,
        out_shape=jax.ShapeDtypeStruct((nb * t, D_MODEL), F32),
        grid=(nb, nq),
        in_specs=[pl.BlockSpec((tq, D_MODEL), row), pl.BlockSpec((1, D_MODEL), const),
                  pl.BlockSpec((D_MODEL, MEM_WIDTH), const), pl.BlockSpec((1, MEM_HD), const),
                  pl.BlockSpec((n_mem, MEM_WIDTH), lambda b, i: (b, 0)),
                  pl.BlockSpec((n_mem, MEM_WIDTH), lambda b, i: (b, 0)),
                  pl.BlockSpec((MEM_WIDTH, D_MODEL), const)],
        out_specs=pl.BlockSpec((tq, D_MODEL), row),
        compiler_params=_cparams("parallel", "parallel"),
        name="cross_block",
    )(x, g.reshape(1, D_MODEL), wq, gq.reshape(1, MEM_HD), mem_k, mem_v, wo)


def _cross_cached_kernel(q_ref, k_ref, v_ref, o_ref):
    nseq = k_ref.shape[1]
    tq = q_ref.shape[0] // nseq
    for sq in range(nseq):
        q = q_ref[sq * tq:(sq + 1) * tq, :]
        qr = jnp.concatenate([q[:, h * MEM_HD:(h + 1) * MEM_HD] for h in range(MEM_HEADS)], axis=0)
        kf = k_ref[0, sq].astype(BF16)
        vf = v_ref[0, sq].astype(BF16)
        s = _dot_nt(qr.astype(BF16), kf) * (MEM_HD ** -0.5)
        row_head = lax.broadcasted_iota(jnp.int32, s.shape, 0) // tq
        col_head = lax.broadcasted_iota(jnp.int32, s.shape, 1) % MEM_HEADS
        s = jnp.where(row_head == col_head, s, -jnp.inf)
        m = jnp.max(s, axis=-1, keepdims=True)
        p = jnp.exp(s - m)
        l = jnp.sum(p, axis=-1, keepdims=True)
        o = _dot(p.astype(BF16), vf) / l
        for h in range(MEM_HEADS):
            o_ref[sq * tq:(sq + 1) * tq, h * MEM_HD:(h + 1) * MEM_HD] = o[h * tq:(h + 1) * tq]


CROSS_CACHED_SEQS = 4


def cross_attend_cached(q, mem_k, mem_v, nb, t, layer):
    ns = CROSS_CACHED_SEQS
    kv_spec = pl.BlockSpec((1, ns, mem_k.shape[2], MEM_HD), lambda b: (layer, b, 0, 0))
    return pl.pallas_call(
        _cross_cached_kernel,
        out_shape=jax.ShapeDtypeStruct((nb * t, MEM_WIDTH), F32),
        grid=(nb // ns,),
        in_specs=[pl.BlockSpec((ns * t, MEM_WIDTH), lambda b: (b, 0)), kv_spec, kv_spec],
        out_specs=pl.BlockSpec((ns * t, MEM_WIDTH), lambda b: (b, 0)),
        compiler_params=_cparams("parallel"),
        name="cross_attend_cached",
    )(q, mem_k, mem_v)


def _proj_residual_kernel(x_ref, a_ref, w_ref, o_ref):
    o_ref[...] = x_ref[...] + _dot(a_ref[...].astype(BF16), w_ref[...])


def proj_residual(x, a, w, tm=512):
    m, n = x.shape
    k = a.shape[1]
    tm = min(tm, m)
    return pl.pallas_call(
        _proj_residual_kernel,
        out_shape=jax.ShapeDtypeStruct((m, n), F32),
        grid=(m // tm,),
        in_specs=[pl.BlockSpec((tm, n), lambda i: (i, 0)), pl.BlockSpec((tm, k), lambda i: (i, 0)),
                  pl.BlockSpec((k, n), lambda i: (0, 0))],
        out_specs=pl.BlockSpec((tm, n), lambda i: (i, 0)),
        compiler_params=_cparams("parallel"),
        name="proj_residual",
    )(x, a, w)


def _mlp_kernel(x_ref, g_ref, wu_ref, wd_ref, o_ref, hn_ref, acc_ref):
    f = pl.program_id(1)

    @pl.when(f == 0)
    def _():
        hn_ref[...] = _rms(x_ref[...], g_ref[...]).astype(BF16)
        acc_ref[...] = jnp.zeros_like(acc_ref)

    a = jnp.maximum(_dot(hn_ref[...], wu_ref[...]), 0.0)
    acc_ref[...] += _dot((a * a).astype(BF16), wd_ref[...])

    @pl.when(f == pl.num_programs(1) - 1)
    def _():
        o_ref[...] = x_ref[...] + acc_ref[...]


def mlp(x, g, wu, wd, tm=1024, tf=512):
    m, d = x.shape
    dff = wu.shape[1]
    tm = min(tm, m)
    return pl.pallas_call(
        _mlp_kernel,
        out_shape=jax.ShapeDtypeStruct((m, d), F32),
        grid=(m // tm, dff // tf),
        in_specs=[pl.BlockSpec((tm, d), lambda i, f: (i, 0)), pl.BlockSpec((1, d), lambda i, f: (0, 0)),
                  pl.BlockSpec((d, tf), lambda i, f: (0, f)), pl.BlockSpec((tf, d), lambda i, f: (f, 0))],
        out_specs=pl.BlockSpec((tm, d), lambda i, f: (i, 0)),
        scratch_shapes=[pltpu.VMEM((tm, d), BF16), pltpu.VMEM((tm, d), F32)],
        compiler_params=_cparams("parallel", "arbitrary"),
        name="mlp",
    )(x, g.reshape(1, d), wu, wd)


def _pack_w_in(w_in):
    offs = np.concatenate([[0], np.cumsum(SPLITS)])
    col = lambda i: w_in[:, int(offs[i]):int(offs[i + 1])]
    s5, fq, fk, fv, ff, mq, mk, mv, mi, mf, mo, gates = [col(i) for i in range(12)]
    pad = jnp.zeros((w_in.shape[0], 512 - FOX_HEADS - 2 * ML_HEADS), w_in.dtype)
    w_a = jnp.concatenate([s5, fq, fk, fv, mo, ff, mi, mf, pad], axis=1).astype(BF16)
    w_b = jnp.concatenate([mq, mk, mv], axis=1).astype(BF16)
    return w_a, w_b, gates.astype(BF16)


def _layer_weights(l, g_mix, w_in, s5_a_re, s5_a_im, s5_log_step, s5_b_re, s5_b_im, s5_c_re, s5_c_im, s5_d,
                   s5_w_glu, s5_b_glu, fox_gq, fox_gk, fox_bf, ml_bi, ml_bf, ml_gn, w_br_s5, w_br_fox, w_br_ml,
                   w_out, g_cross, w_cq, cross_gq, g_mem, w_mk, w_mv, cross_gk, w_co, g_mlp, w_up, w_down):
    bias_row = jnp.zeros((1, LANES), F32)
    bias_row = bias_row.at[0, LANE_FOXF:LANE_FOXF + FOX_HEADS].set(fox_bf[l])
    bias_row = bias_row.at[0, LANE_MLI:LANE_MLI + ML_HEADS].set(ml_bi[l])
    bias_row = bias_row.at[0, LANE_MLF:LANE_MLF + ML_HEADS].set(ml_bf[l])
    w_a, w_b, w_gates = _pack_w_in(w_in[l])
    return dict(
        g_mix=g_mix[l], w_a=w_a, w_b=w_b, w_gates=w_gates,
        s5=s5_params(s5_a_re[l], s5_a_im[l], s5_log_step[l], s5_b_re[l], s5_b_im[l], s5_c_re[l], s5_c_im[l],
                     s5_d[l], s5_w_glu[l], s5_b_glu[l], (SEQ_TILE // SUBLANES, SAMPLE_T // SUBLANES)),
        gq=jnp.tile(fox_gq[l], FOX_HEADS).reshape(1, FOX_WIDTH),
        gk=jnp.tile(fox_gk[l], FOX_HEADS).reshape(1, FOX_WIDTH),
        bias_row=bias_row, ml_gn=ml_gn[l],
        w_br_s5=w_br_s5[l].astype(BF16), w_br_fox=w_br_fox[l].astype(BF16), w_br_ml=w_br_ml[l].astype(BF16),
        w_out=w_out[l].astype(BF16), g_cross=g_cross[l], w_cq=w_cq[l].astype(BF16), cross_gq=cross_gq[l],
        g_mem=g_mem[l], w_mk=w_mk[l].astype(BF16), w_mv=w_mv[l].astype(BF16), cross_gk=cross_gk[l],
        w_co=w_co[l].astype(BF16), g_mlp=g_mlp[l], w_up=w_up[l].astype(BF16), w_down=w_down[l].astype(BF16))


def _hybrid_layer(x, W, nb, t, seq_tile, t_valid, s5_state, ml_state, mem_k, mem_v, fox_attend, augment,
                  mem_layer=None):
    z = norm_matmul(x, W["g_mix"], W["w_a"], tm=min(1024, nb * t))
    zb = norm_matmul(x, W["g_mix"], W["w_b"], tm=min(1024, nb * t),
                     out_dtype=BF16 if seq_tile % MIN_BF16_ROWS == 0 else F32)
    prep = gate_prep(z, W["gq"], W["gk"], W["bias_row"], nb, t, seq_tile, t_valid, augment)
    kn, g, cs = prep[1:4]
    grow, csrow = prep[4:6] if augment else (None, None)
    y_s5, s5_re, s5_im = s5_mixer(z, s5_state[0], s5_state[1], W["s5"], nb, t, seq_tile, t_valid - 1,
                                  seqs=S5_SHORT_SEQS if t == SUBLANES else 1)
    y_fox = fox_attend(prep, z)
    y_ml, c_new, n_new, m_new = mlstm(z, zb, g, cs, grow, csrow, ml_state[0], ml_state[1], ml_state[2], W["ml_gn"],
                                      nb, t, seq_tile, 0 if mem_layer is None else mem_layer)
    x = merge_out(x, W["g_mix"], W["w_gates"], y_s5, y_fox, y_ml, W["w_br_s5"], W["w_br_fox"], W["w_br_ml"],
                  W["w_out"])
    if mem_layer is None:
        x = cross_block(x, W["g_cross"], W["w_cq"], W["cross_gq"], mem_k, mem_v, W["w_co"], nb, t, seq_tile)
    else:
        qc = norm_matmul(x, W["g_cross"], W["w_cq"], head_gain=W["cross_gq"])
        oc = cross_attend_cached(qc, mem_k, mem_v, nb, t, mem_layer)
        x = proj_residual(x, oc, W["w_co"])
    x = mlp(x, W["g_mlp"], W["w_up"], W["w_down"])
    if augment:
        fox_k = prep[1].reshape(nb, FOX_HEADS, FOX_HD, t).transpose(0, 3, 1, 2)
        fox_v = prep[8].reshape(nb, FOX_HEADS, FOX_HD, t).transpose(0, 3, 1, 2)
    else:
        fox_k = kn.reshape(nb, t, FOX_HEADS, FOX_HD)
        fox_v = z[:, COL_FV * 512:(COL_FV + 1) * 512].reshape(nb, t, FOX_HEADS, FOX_HD)
    return x, fox_k, fox_v, g, s5_re.reshape(nb, S5_GROUPS, S5_STATE), s5_im.reshape(nb, S5_GROUPS, S5_STATE), \
        c_new, n_new, m_new


def kernel(x_prompt, x_sample, mem_prompt, cache_fox_k, cache_fox_v, cache_fox_logf, page_table, state_s5_re, state_s5_im, state_mlstm_C, state_mlstm_n, state_mlstm_m, cache_mem_k, cache_mem_v, g_mix, w_in, s5_a_re, s5_a_im, s5_log_step, s5_b_re, s5_b_im, s5_c_re, s5_c_im, s5_d, s5_w_glu, s5_b_glu, fox_gq, fox_gk, fox_bf, ml_bi, ml_bf, ml_gn, w_br_s5, w_br_fox, w_br_ml, w_out, g_cross, w_cq, cross_gq, g_mem, w_mk, w_mv, cross_gk, w_co, g_mlp, w_up, w_down):
    depth = w_in.shape[0]
    bp, tp, _ = x_prompt.shape
    bs, ts, _ = x_sample.shape
    n_mem = mem_prompt.shape[1]

    xp = x_prompt.reshape(bp * tp, D_MODEL)
    xs = jnp.pad(x_sample, ((0, 0), (0, SAMPLE_T - ts), (0, 0))).reshape(bs * SAMPLE_T, D_MODEL)
    mem = mem_prompt.reshape(bp * n_mem, D_MODEL)
    zeros_p = (jnp.zeros((bp, S5_LANES), F32), jnp.zeros((bp, S5_LANES), F32))
    zeros_ml = (jnp.zeros((1, bp, ML_HEADS, ML_HD, ML_HD), F32), jnp.zeros((bp, ML_HEADS, ML_HD), F32),
                jnp.zeros((bp, ML_HEADS), F32))
    pool_kt = jnp.transpose(cache_fox_k, (0, 1, 3, 4, 2))
    pool_vt = jnp.transpose(cache_fox_v, (0, 1, 3, 4, 2))
    pool_ft = jnp.transpose(cache_fox_logf, (0, 1, 3, 2))
    st_p, st_s = [], []
    for l in range(depth):
        W = _layer_weights(l, g_mix, w_in, s5_a_re, s5_a_im, s5_log_step, s5_b_re, s5_b_im, s5_c_re, s5_c_im, s5_d,
                           s5_w_glu, s5_b_glu, fox_gq, fox_gk, fox_bf, ml_bi, ml_bf, ml_gn, w_br_s5, w_br_fox,
                           w_br_ml, w_out, g_cross, w_cq, cross_gq, g_mem, w_mk, w_mv, cross_gk, w_co, g_mlp, w_up,
                           w_down)
        mk_p = norm_matmul(mem, W["g_mem"], W["w_mk"], head_gain=W["cross_gk"])
        mv_p = norm_matmul(mem, W["g_mem"], W["w_mv"])

        def flash(prep, z):
            return fox_flash(prep[0], prep[6], prep[7], bp, tp, FLASH_TILE)

        xp, fox_k, fox_v, g, s5r, s5i, c_new, n_new, m_new = _hybrid_layer(
            xp, W, bp, tp, SEQ_TILE, tp, zeros_p, zeros_ml, mk_p, mv_p, flash, True)
        st_p.append((fox_k, fox_v,
                     g[:, LANE_FOXF:LANE_FOXF + FOX_HEADS].reshape(bp, tp, FOX_HEADS),
                     s5r, s5i, c_new, n_new, m_new,
                     mk_p.reshape(bp, n_mem, MEM_HEADS, MEM_HD), mv_p.reshape(bp, n_mem, MEM_HEADS, MEM_HD)))

        def decode(prep, z, layer=l):
            return fox_decode(page_table, prep[0], prep[1], z, prep[3], pool_kt, pool_vt, pool_ft, layer)

        xs, fox_k, fox_v, g, s5r, s5i, c_new, n_new, m_new = _hybrid_layer(
            xs, W, bs, SAMPLE_T, SAMPLE_T, ts,
            (state_s5_re[l].reshape(bs, S5_LANES), state_s5_im[l].reshape(bs, S5_LANES)),
            (state_mlstm_C, state_mlstm_n[l], state_mlstm_m[l]),
            cache_mem_k.reshape(depth, bs, n_mem * MEM_HEADS, MEM_HD),
            cache_mem_v.reshape(depth, bs, n_mem * MEM_HEADS, MEM_HD), decode, False, mem_layer=l)
        st_s.append((fox_k[:, :ts], fox_v[:, :ts],
                     g[:, LANE_FOXF:LANE_FOXF + FOX_HEADS].reshape(bs, SAMPLE_T, FOX_HEADS)[:, :ts],
                     s5r, s5i, c_new, n_new, m_new))
    outs_p = [jnp.stack(a) for a in zip(*st_p)]
    outs_s = [jnp.stack(a) for a in zip(*st_s)]
    yp = xp.reshape(bp, tp, D_MODEL)
    ys = xs.reshape(bs, SAMPLE_T, D_MODEL)[:, :ts]
    return (yp, ys, *outs_p, *outs_s)
```

```python
import functools
import math

import jax
import jax.numpy as jnp
import numpy as np
from jax import lax
from jax.experimental import pallas as pl
from jax.experimental.pallas import tpu as pltpu

F32 = jnp.float32
BF16 = jnp.bfloat16

LANES = 128
SUBLANES = 8
MIN_BF16_ROWS = 16
MXU_DIM = 256
VMEM_LIMIT_BYTES = 48 * 1024 * 1024

D_MODEL = 1024
S5_WIDTH = 512
S5_GROUP = 16
S5_GROUPS = 32
S5_STATE = 64
S5_LANES = S5_GROUPS * S5_STATE
FOX_HEADS = 8
FOX_HD = 64
FOX_WIDTH = 512
ML_HEADS = 4
ML_HD = 128
ML_WIDTH = 512
MEM_HEADS = 4
MEM_HD = 128
MEM_WIDTH = 512
D_FF = 4096
EPS = 1e-6
LOG2E = math.log2(math.e)
SPLITS =(S5_WIDTH, FOX_WIDTH, FOX_WIDTH, FOX_WIDTH, FOX_HEADS, ML_WIDTH, ML_WIDTH, ML_WIDTH,
          ML_HEADS, ML_HEADS, ML_WIDTH, 3 * D_MODEL)

COL_S5, COL_FQ, COL_FK, COL_FV, COL_MO, COL_SMALL = range(6)
COL_MQ, COL_MK, COL_MV = range(3)
LANE_FOXF = 0
LANE_MLI = 8
LANE_MLF = 12
SEQ_TILE = 256
FLASH_TILE = 512
FLASH_ROWS = 128
SAMPLE_T = 8


def _cparams(*sem):
    return pltpu.CompilerParams(dimension_semantics=sem, vmem_limit_bytes=VMEM_LIMIT_BYTES)


def _dot(a, b):
    return jnp.dot(a, b, preferred_element_type=F32)


def _dot_nt(a, b):
    return lax.dot_general(a, b, (((1,), (1,)), ((), ())), preferred_element_type=F32)


def _dot_tn(a, b):
    return lax.dot_general(a, b, (((0,), (0,)), ((), ())), preferred_element_type=F32)


def _split3(x):
    hi = x.astype(BF16)
    r1 = x - hi.astype(F32)
    mid = r1.astype(BF16)
    lo = (r1 - mid.astype(F32)).astype(BF16)
    return hi, mid, lo


def _dot_exact_rhs(x, ones_rhs):
    hi, mid, lo = _split3(x)
    return _dot(hi, ones_rhs) + _dot(mid, ones_rhs) + _dot(lo, ones_rhs)


def _dot_exact_lhs(ones_lhs, x):
    hi, mid, lo = _split3(x)
    return _dot(ones_lhs, hi) + _dot(ones_lhs, mid) + _dot(ones_lhs, lo)


def _pad_rows(x, n, fill=0.0):
    if x.shape[0] >= n:
        return x
    return jnp.concatenate([x, jnp.full((n - x.shape[0], x.shape[1]), fill, x.dtype)], axis=0)


def _lane_tile(x, n):
    return x if n == 1 else jnp.concatenate([x] * n, axis=1)


def _log_sigmoid(a):
    return jnp.minimum(a, 0.0) - jnp.log1p(jnp.exp(-jnp.abs(a)))


def _sigmoid(a):
    return 1.0 / (1.0 + jnp.exp(-a))


def _gelu_tanh(x):
    c = math.sqrt(2.0 / math.pi)
    return 0.5 * x * (1.0 + jnp.tanh(c * (x + 0.044715 * (x * x * x))))


def _rms(x, g):
    ms = jnp.mean(x * x, axis=-1, keepdims=True)
    return x * lax.rsqrt(ms + EPS) * g


def _norm_matmul_kernel(x_ref, g_ref, w_ref, *rest, head_norm):
    if head_norm:
        hg_ref, o_ref, hn_ref = rest
    else:
        o_ref, hn_ref = rest

    @pl.when(pl.program_id(1) == 0)
    def _():
        hn_ref[...] = _rms(x_ref[...], g_ref[...]).astype(BF16)

    y = _dot(hn_ref[...], w_ref[...])
    if head_norm:
        tn = y.shape[1]
        for s in range(tn // LANES):
            sl = slice(s * LANES, (s + 1) * LANES)
            o_ref[:, sl] = _rms(y[:, sl], hg_ref[...]).astype(o_ref.dtype)
    else:
        o_ref[...] = y.astype(o_ref.dtype)


def norm_matmul(x, g, w, head_gain=None, out_dtype=F32, tm=512, tn=512):
    m, k = x.shape
    n = w.shape[1]
    tm = min(tm, m)
    tn = min(tn, n)
    head_norm = head_gain is not None
    in_specs = [pl.BlockSpec((tm, k), lambda i, j: (i, 0)),
                pl.BlockSpec((1, k), lambda i, j: (0, 0)),
                pl.BlockSpec((k, tn), lambda i, j: (0, j))]
    args = [x, g.reshape(1, k), w]
    if head_norm:
        in_specs.append(pl.BlockSpec((1, LANES), lambda i, j: (0, 0)))
        args.append(head_gain.reshape(1, LANES))
    return pl.pallas_call(
        functools.partial(_norm_matmul_kernel, head_norm=head_norm),
        out_shape=jax.ShapeDtypeStruct((m, n), out_dtype),
        grid=(m // tm, n // tn),
        in_specs=in_specs,
        out_specs=pl.BlockSpec((tm, tn), lambda i, j: (i, j)),
        scratch_shapes=[pltpu.VMEM((tm, k), BF16)],
        compiler_params=_cparams("parallel", "arbitrary"),
        name="norm_matmul",
    )(*args)


def _prep_kernel(fq_ref, fk_ref, fv_ref, sm_ref, gq_ref, gk_ref, bias_ref, gmat_ref, tril_ref, sel_ref, aug_ref,
                 *rest, seq_rows, t_valid, augment):
    if augment:
        qx_ref, kn_ref, g_ref, cs_ref, grow_ref, csrow_ref, kx_ref, vb_ref, vt_ref, carry_ref = rest
    else:
        qx_ref, kn_ref, g_ref, cs_ref, carry_ref = rest
    c = pl.program_id(1)
    tc = sm_ref.shape[0]

    @pl.when(c == 0)
    def _():
        carry_ref[...] = jnp.zeros_like(carry_ref)

    gmat = gmat_ref[...]

    def head_rms(x, gain):
        x2 = x * x
        hi = x2.astype(BF16)
        lo = (x2 - hi.astype(F32)).astype(BF16)
        ss = (_dot(hi, gmat) + _dot(lo, gmat)) * (1.0 / FOX_HD)
        return x * lax.rsqrt(ss + EPS) * gain

    qn = head_rms(fq_ref[...], gq_ref[...]) * ((FOX_HD ** -0.5) * (LOG2E if augment else 1.0))
    kn = head_rms(fk_ref[...], gk_ref[...])
    if augment:
        kn_ref[...] = kn.T
        vt_ref[...] = fv_ref[...].T
    else:
        kn_ref[...] = kn

    a = sm_ref[...] + bias_ref[...]
    lane = lax.broadcasted_iota(jnp.int32, a.shape, 1)
    pos = (lax.broadcasted_iota(jnp.int32, a.shape, 0) + c * tc) % seq_rows
    is_i = (lane >= LANE_MLI) & (lane < LANE_MLF)
    used = lane < LANE_MLF + ML_HEADS
    valid = pos < t_valid
    g = jnp.where(is_i, a, _log_sigmoid(a))
    g = jnp.where(used, g, 0.0)
    g = jnp.where(valid, g, jnp.where(is_i, -jnp.inf, 0.0))
    gc = jnp.where(is_i, 0.0, g)
    cs = _dot_exact_lhs(tril_ref[...], gc)
    carry = carry_ref[...]
    csg = cs + jnp.where(lane < FOX_HEADS, carry, 0.0)
    carry_ref[...] = carry + cs[tc - 1:tc, :]
    g_ref[...] = g
    cs_ref[...] = csg
    if augment:
        grow_ref[...] = g.T
        csrow_ref[...] = csg.T

    nx = FOX_HEADS * LANES
    lanex = lax.broadcasted_iota(jnp.int32, (1, nx), 1)
    keep = ((lanex % LANES) >= FOX_HD) == ((lanex // LANES) % 2 == 1)
    q_exp = jnp.concatenate([qn[:, (h // 2) * LANES:(h // 2 + 1) * LANES] for h in range(FOX_HEADS)], axis=1)
    if not augment:
        qx_ref[...] = jnp.where(keep, q_exp, 0.0).astype(qx_ref.dtype)
        return
    k_exp = jnp.concatenate([kn[:, (h // 2) * LANES:(h // 2 + 1) * LANES] for h in range(FOX_HEADS)], axis=1)
    hi, mid, lo = _split3(jnp.where(lane < FOX_HEADS, csg * LOG2E, 0.0))
    packed = (hi.astype(F32) + pltpu.roll(mid.astype(F32), FOX_HEADS, 1)
              + pltpu.roll(lo.astype(F32), 2 * FOX_HEADS, 1)).astype(BF16)
    aug = aug_ref[...] + _dot(packed, sel_ref[...])
    qx_ref[...] = jnp.where(keep, q_exp, aug[:, 0:nx]).astype(qx_ref.dtype)
    kx_ref[...] = jnp.where(keep, k_exp, aug[:, nx:2 * nx]).astype(kx_ref.dtype)
    vb_ref[...] = fv_ref[...].astype(vb_ref.dtype)


PREP_SHORT_ROWS = 128


def gate_prep(z, gq, gk, bias_row, nb, t, tc, t_valid, augment):
    m = nb * t
    gmat = jnp.asarray(np.kron(np.eye(FOX_HEADS), np.ones((FOX_HD, FOX_HD))), BF16)
    if augment:
        nc, tp, grid_rows = t // tc, tc, nb
        tril = np.tril(np.ones((tc, tc)))
    else:
        tc = tp = PREP_SHORT_ROWS
        nc, grid_rows = 1, m // tc
        tril = np.kron(np.eye(tc // t), np.tril(np.ones((t, t))))
    tril = jnp.asarray(tril, BF16)
    nx = FOX_HEADS * LANES
    sel = np.zeros((LANES, 2 * nx), np.float32)
    aug_const = np.zeros((1, 2 * nx), np.float32)
    for h in range(FOX_HEADS):
        o = h * LANES + (0 if h % 2 else FOX_HD)
        for p in range(3):
            sel[p * FOX_HEADS + h, o + p] = 1.0
            sel[p * FOX_HEADS + h, nx + o + 3 + p] = -1.0
            aug_const[0, o + 3 + p] = 1.0
            aug_const[0, nx + o + p] = 1.0
    sel, aug_const = jnp.asarray(sel, BF16), jnp.asarray(aug_const, F32)
    row_map = lambda b, c: (b * nc + c, 0)
    const = lambda b, c: (0, 0)
    feat_major = jax.ShapeDtypeStruct((nb * FOX_WIDTH, t), F32)
    feat_spec = pl.BlockSpec((FOX_WIDTH, tc), lambda b, c: (b, c))
    out_shape = [jax.ShapeDtypeStruct((m, FOX_HEADS * LANES), BF16 if augment else F32),
                 feat_major if augment else jax.ShapeDtypeStruct((m, FOX_WIDTH), F32),
                 jax.ShapeDtypeStruct((m, LANES), F32),
                 jax.ShapeDtypeStruct((m, LANES), F32)]
    out_specs = [pl.BlockSpec((tc, FOX_HEADS * LANES), row_map),
                 feat_spec if augment else pl.BlockSpec((tc, FOX_WIDTH), row_map),
                 pl.BlockSpec((tc, LANES), row_map),
                 pl.BlockSpec((tc, LANES), row_map)]
    if augment:
        out_shape += [jax.ShapeDtypeStruct((nb * LANES, t), F32), jax.ShapeDtypeStruct((nb * LANES, t), F32),
                      jax.ShapeDtypeStruct((m, FOX_HEADS * LANES), BF16), jax.ShapeDtypeStruct((m, FOX_WIDTH), BF16),
                      feat_major]
        out_specs += [pl.BlockSpec((LANES, tc), lambda b, c: (b, c)), pl.BlockSpec((LANES, tc), lambda b, c: (b, c)),
                      pl.BlockSpec((tc, FOX_HEADS * LANES), row_map), pl.BlockSpec((tc, FOX_WIDTH), row_map),
                      feat_spec]
    return pl.pallas_call(
        functools.partial(_prep_kernel, seq_rows=t, t_valid=t_valid, augment=augment),
        out_shape=tuple(out_shape),
        grid=(grid_rows, nc),
        in_specs=[pl.BlockSpec((tc, 512), lambda b, c: (b * nc + c, COL_FQ)),
                  pl.BlockSpec((tc, 512), lambda b, c: (b * nc + c, COL_FK)),
                  pl.BlockSpec((tc, 512), lambda b, c: (b * nc + c, COL_FV)),
                  pl.BlockSpec((tc, LANES), lambda b, c: (b * nc + c, COL_SMALL * 4)),
                  pl.BlockSpec((1, 512), const), pl.BlockSpec((1, 512), const),
                  pl.BlockSpec((1, LANES), const),
                  pl.BlockSpec((512, 512), const), pl.BlockSpec((tp, tp), const),
                  pl.BlockSpec((LANES, 2 * nx), const), pl.BlockSpec((1, 2 * nx), const)],
        out_specs=tuple(out_specs),
        scratch_shapes=[pltpu.VMEM((1, LANES), F32)],
        compiler_params=_cparams("parallel", "arbitrary"),
        name="gate_prep",
    )(z, z, z, z, gq, gk, bias_row, gmat, tril, sel, aug_const)


S5_SCAN_LANES = 512
S5_SHORT_SEQS = 16


def _cmul(ar, ai, br, bi):
    return ar * br - ai * bi, ar * bi + ai * br


def _s5_kernel(u_ref, h0r_ref, h0i_ref, perm_ref, permt_ref, bre_ref, bim_ref, lam_ref, pseg_ref, pk_ref,
               cre_ref, cim_ref, d_ref, wglu_ref, bglu_ref, y_ref, hlr_ref, hli_ref, hr_ref, hi_ref, cr_ref, ci_ref,
               *, t_last, seqs):
    c = pl.program_id(1)
    nc = pl.num_programs(1)
    tc = u_ref.shape[0]
    R = tc // SUBLANES if seqs == 1 else 1
    half = S5_WIDTH // 2
    hl = S5_LANES // 2

    if seqs == 1:
        @pl.when(c == 0)
        def _():
            cr_ref[...] = h0r_ref[0]
            ci_ref[...] = h0i_ref[0]

    u = u_ref[...]
    ub = _pad_rows(u, MIN_BF16_ROWS).astype(BF16)
    if R > 1:
        ub = _dot(perm_ref[...], ub).astype(BF16)
    for j in range(2):
        uj = ub[:, j * half:(j + 1) * half]
        hr_ref[:, j * hl:(j + 1) * hl] = _dot(uj, bre_ref[j])[0:tc]
        hi_ref[:, j * hl:(j + 1) * hl] = _dot(uj, bim_ref[j])[0:tc]

    sub = lax.broadcasted_iota(jnp.int32, (SUBLANES, S5_SCAN_LANES), 0)
    for lc in range(S5_LANES // S5_SCAN_LANES):
        ls = slice(lc * S5_SCAN_LANES, (lc + 1) * S5_SCAN_LANES)
        lam_r = jnp.broadcast_to(lam_ref[0:1, ls], sub.shape)
        lam_i = jnp.broadcast_to(lam_ref[1:2, ls], sub.shape)
        if seqs > 1:
            def seq_scan(g, _):
                rows = pl.ds(pl.multiple_of(g * SUBLANES, SUBLANES), SUBLANES)
                s_r = jnp.where(sub == 0, jnp.broadcast_to(h0r_ref[0, pl.ds(g, 1), ls], sub.shape), 0.0)
                s_i = jnp.where(sub == 0, jnp.broadcast_to(h0i_ref[0, pl.ds(g, 1), ls], sub.shape), 0.0)
                m_r, m_i = _cmul(lam_r, lam_i, s_r, s_i)
                x_r = hr_ref[rows, ls] + m_r
                x_i = hi_ref[rows, ls] + m_i
                for j, lag in enumerate((1, 2, 4)):
                    m_r, m_i = _cmul(pseg_ref[2 * j, :, ls], pseg_ref[2 * j + 1, :, ls],
                                     pltpu.roll(x_r, lag, 0), pltpu.roll(x_i, lag, 0))
                    x_r, x_i = x_r + m_r, x_i + m_i
                hr_ref[rows, ls] = x_r
                hi_ref[rows, ls] = x_i
                hlr_ref[0, pl.ds(g, 1), ls] = x_r[t_last:t_last + 1, :]
                hli_ref[0, pl.ds(g, 1), ls] = x_i[t_last:t_last + 1, :]
                return 0

            lax.fori_loop(0, seqs, seq_scan, 0, unroll=2)
            continue
        init_r = jnp.where(sub == 0, jnp.broadcast_to(cr_ref[:, ls], sub.shape), 0.0)
        init_i = jnp.where(sub == 0, jnp.broadcast_to(ci_ref[:, ls], sub.shape), 0.0)

        def local_step(k, carry):
            h_r, h_i = carry
            rows = pl.ds(pl.multiple_of(k * SUBLANES, SUBLANES), SUBLANES)
            m_r, m_i = _cmul(lam_r, lam_i, h_r, h_i)
            h_r = m_r + hr_ref[rows, ls]
            h_i = m_i + hi_ref[rows, ls]
            hr_ref[rows, ls] = h_r
            hi_ref[rows, ls] = h_i
            return h_r, h_i

        e_r, e_i = lax.fori_loop(0, R, local_step, (init_r, init_i), unroll=min(R, 4))
        for j, lag in enumerate((1, 2, 4)):
            m_r, m_i = _cmul(pseg_ref[2 * j, :, ls], pseg_ref[2 * j + 1, :, ls],
                             pltpu.roll(e_r, lag, 0), pltpu.roll(e_i, lag, 0))
            e_r, e_i = e_r + m_r, e_i + m_i
        cr_ref[:, ls] = e_r[SUBLANES - 1:SUBLANES, :]
        ci_ref[:, ls] = e_i[SUBLANES - 1:SUBLANES, :]
        in_r = jnp.where(sub == 0, 0.0, pltpu.roll(e_r, 1, 0))
        in_i = jnp.where(sub == 0, 0.0, pltpu.roll(e_i, 1, 0))

        def fix_step(k, _):
            rows = pl.ds(pl.multiple_of(k * SUBLANES, SUBLANES), SUBLANES)
            p_r = jnp.broadcast_to(pk_ref[0, pl.ds(k, 1), ls], sub.shape)
            p_i = jnp.broadcast_to(pk_ref[1, pl.ds(k, 1), ls], sub.shape)
            m_r, m_i = _cmul(p_r, p_i, in_r, in_i)
            hr_ref[rows, ls] += m_r
            hi_ref[rows, ls] += m_i
            return 0

        lax.fori_loop(0, R, fix_step, 0, unroll=min(R, 4))

    if seqs == 1:
        @pl.when(c == nc - 1)
        def _():
            tl = t_last % tc
            pos = (tl % R) * SUBLANES + tl // R
            hlr_ref[0] = hr_ref[pos:pos + 1, :]
            hli_ref[0] = hi_ref[pos:pos + 1, :]

    hrb = _pad_rows(hr_ref[...], MIN_BF16_ROWS).astype(BF16)
    hib = _pad_rows(hi_ref[...], MIN_BF16_ROWS).astype(BF16)
    ys = [(_dot(hrb[:, j * hl:(j + 1) * hl], cre_ref[j]) + _dot(hib[:, j * hl:(j + 1) * hl], cim_ref[j]))
          for j in range(2)]
    y = jnp.concatenate(ys, axis=1)
    if R > 1:
        y = _dot_exact_lhs(permt_ref[...], y)
    y = _gelu_tanh(y[0:tc] + d_ref[...] * u)
    gate = _dot(_pad_rows(y, MIN_BF16_ROWS).astype(BF16), wglu_ref[...])[0:tc]
    y_ref[...] = y * _sigmoid(gate + bglu_ref[...])


def s5_mixer(z, h0r, h0i, sp, nb, t, tc, t_last, seqs=1):
    assert seqs == 1 or (tc == SUBLANES and t == tc)
    nc = t // tc
    R = tc // SUBLANES
    tc = tc * seqs
    nbg = nb // seqs
    tperm = max(tc, MIN_BF16_ROWS) if seqs == 1 else MIN_BF16_ROWS
    perm = np.zeros((tperm, tperm), np.float32)
    if seqs == 1:
        for s in range(SUBLANES):
            for k in range(R):
                perm[k * SUBLANES + s, s * R + k] = 1.0
    permt = jnp.asarray(perm.T, BF16)
    perm = jnp.asarray(perm, BF16)
    const = lambda b, c: (0, 0)
    const3 = lambda b, c: (0, 0, 0)
    state_spec = pl.BlockSpec((1, seqs, S5_LANES), lambda b, c: (b, 0, 0))
    tabs = sp["tabs"][R]
    return pl.pallas_call(
        functools.partial(_s5_kernel, t_last=t_last, seqs=seqs),
        out_shape=(jax.ShapeDtypeStruct((nb * t, S5_WIDTH), F32),
                   jax.ShapeDtypeStruct((nbg, seqs, S5_LANES), F32),
                   jax.ShapeDtypeStruct((nbg, seqs, S5_LANES), F32)),
        grid=(nbg, nc),
        in_specs=[pl.BlockSpec((tc, 512), lambda b, c: (b * nc + c, COL_S5)),
                  state_spec, state_spec,
                  pl.BlockSpec((tperm, tperm), const), pl.BlockSpec((tperm, tperm), const),
                  pl.BlockSpec((2, S5_WIDTH // 2, S5_LANES // 2), const3),
                  pl.BlockSpec((2, S5_WIDTH // 2, S5_LANES // 2), const3),
                  pl.BlockSpec((2, S5_LANES), const),
                  pl.BlockSpec((6, SUBLANES, S5_LANES), const3),
                  pl.BlockSpec((2, R, S5_LANES), const3),
                  pl.BlockSpec((2, S5_LANES // 2, S5_WIDTH // 2), const3),
                  pl.BlockSpec((2, S5_LANES // 2, S5_WIDTH // 2), const3),
                  pl.BlockSpec((1, S5_WIDTH), const),
                  pl.BlockSpec((S5_WIDTH, S5_WIDTH), const), pl.BlockSpec((1, S5_WIDTH), const)],
        out_specs=(pl.BlockSpec((tc, S5_WIDTH), lambda b, c: (b * nc + c, 0)), state_spec, state_spec),
        scratch_shapes=[pltpu.VMEM((tc, S5_LANES), F32), pltpu.VMEM((tc, S5_LANES), F32),
                        pltpu.VMEM((1, S5_LANES), F32), pltpu.VMEM((1, S5_LANES), F32)],
        compiler_params=_cparams("parallel", "arbitrary"),
        name="s5_mixer",
    )(z, h0r.reshape(nbg, seqs, S5_LANES), h0i.reshape(nbg, seqs, S5_LANES), perm, permt, sp["bre"], sp["bim"],
      tabs["lam"],
      tabs["pseg"], tabs["pk"], sp["cre"], sp["cim"], sp["d"], sp["wglu"], sp["bglu"])


def s5_params(a_re, a_im, log_step, b_re, b_im, c_re, c_im, d_skip, w_glu, b_glu, seg_lens):
    dt = jnp.exp(log_step)[:, None]
    mag = jnp.exp(a_re * dt)
    lr = (mag * jnp.cos(a_im * dt)).reshape(1, S5_LANES)
    li = (mag * jnp.sin(a_im * dt)).reshape(1, S5_LANES)
    den = a_re * a_re + a_im * a_im
    xr, xi = lr.reshape(a_re.shape) - 1.0, li.reshape(a_re.shape)
    fr = (xr * a_re + xi * a_im) / den
    fi = (xi * a_re - xr * a_im) / den
    bbr = fr[..., None] * b_re - fi[..., None] * b_im
    bbi = fr[..., None] * b_im + fi[..., None] * b_re
    gh = S5_GROUPS // 2
    eye = jnp.eye(gh, dtype=F32)

    def in_mat(b):
        return jnp.einsum('jgpc,gh->jgchp', b.reshape(2, gh, S5_STATE, S5_GROUP), eye).reshape(
            2, S5_WIDTH // 2, S5_LANES // 2).astype(BF16)

    def out_mat(cm):
        return jnp.einsum('jgcp,gh->jgphc', cm.reshape(2, gh, S5_GROUP, S5_STATE), eye).reshape(
            2, S5_LANES // 2, S5_WIDTH // 2).astype(BF16)

    def powers(pr, pi, n):
        tr, ti, cnt = pr, pi, 1
        while cnt < n:
            lr_, li_ = tr[cnt - 1:cnt], ti[cnt - 1:cnt]
            nr, ni = _cmul(tr, ti, lr_, li_)
            tr, ti, cnt = jnp.concatenate([tr, nr], axis=0), jnp.concatenate([ti, ni], axis=0), 2 * cnt
        return tr, ti

    sub = jnp.arange(SUBLANES)[:, None]
    tabs = {}
    for R in seg_lens:
        kr, ki = powers(lr, li, R)
        sr, si = powers(kr[R - 1:R], ki[R - 1:R], 4)
        pseg = []
        for lag in (1, 2, 4):
            msk = (sub >= lag).astype(F32)
            pseg += [msk * sr[lag - 1:lag], msk * si[lag - 1:lag]]
        tabs[R] = dict(lam=jnp.concatenate([lr, li], axis=0), pseg=jnp.stack(pseg), pk=jnp.stack([kr, ki]))
    return dict(bre=in_mat(bbr), bim=in_mat(bbi), cre=out_mat(c_re), cim=out_mat(-c_im), tabs=tabs,
                d=d_skip.reshape(1, S5_WIDTH), wglu=w_glu.astype(BF16), bglu=b_glu.reshape(1, S5_WIDTH))


def _fox_flash_kernel(it_ref, jt_ref, qx_ref, kx_ref, v_ref, o_ref, m_ref, l_ref, acc_ref, s_ref, p_ref):
    i = it_ref[pl.program_id(1)]
    j = jt_ref[pl.program_id(1)]
    tq = qx_ref.shape[0]
    tk = kx_ref.shape[0]

    @pl.when(j == 0)
    def _():
        m_ref[...] = jnp.full_like(m_ref, -jnp.inf)
        l_ref[...] = jnp.zeros_like(l_ref)
        acc_ref[...] = jnp.zeros_like(acc_ref)

    def step(masked):
        nr, ncol = tq // FLASH_ROWS, tk // LANES
        if masked:
            diff = (lax.broadcasted_iota(jnp.int32, (FLASH_ROWS, LANES), 0) -
                    lax.broadcasted_iota(jnp.int32, (FLASH_ROWS, LANES), 1))
        for h in range(FOX_HEADS):
            hs = slice(h * LANES, (h + 1) * LANES)
            ps = slice((h // 2) * LANES, (h // 2 + 1) * LANES)
            s_buf, p_buf = s_ref.at[h % 2], p_ref.at[h % 2]
            if h == 0:
                s_buf[...] = _dot_nt(qx_ref[:, hs], kx_ref[:, hs])
            if h + 1 < FOX_HEADS:
                nhs = slice((h + 1) * LANES, (h + 2) * LANES)
                s_ref[(h + 1) % 2] = _dot_nt(qx_ref[:, nhs], kx_ref[:, nhs])
            for r in range(nr):
                rs = slice(r * FLASH_ROWS, (r + 1) * FLASH_ROWS)

                live = [cidx for cidx in range(ncol) if not (masked and cidx * LANES >= (r + 1) * FLASH_ROWS)]

                def piece(cidx):
                    sc = s_buf[rs, cidx * LANES:(cidx + 1) * LANES]
                    if masked and (cidx + 1) * LANES - 1 > r * FLASH_ROWS:
                        sc = jnp.where(diff >= (cidx * LANES - r * FLASH_ROWS), sc, -jnp.inf)
                    return sc

                mx = piece(live[0])
                for cidx in live[1:]:
                    mx = jnp.maximum(mx, piece(cidx))
                m_prev = m_ref[h, rs, :]
                m_new = jnp.maximum(m_prev, jnp.max(mx, axis=-1, keepdims=True))
                alpha = jnp.exp2(m_prev - m_new)
                lsum = jnp.zeros((FLASH_ROWS, LANES), F32)
                for cidx in range(ncol):
                    cs_ = slice(cidx * LANES, (cidx + 1) * LANES)
                    if cidx not in live:
                        p_buf[rs, cs_] = jnp.zeros((FLASH_ROWS, LANES), BF16)
                        continue
                    pc = jnp.exp2(piece(cidx) - m_new)
                    lsum = lsum + pc
                    p_buf[rs, cs_] = pc.astype(BF16)
                l_ref[h, rs, :] = alpha * l_ref[h, rs, :] + lsum
                acc_ref[h, rs, :] = alpha * acc_ref[h, rs, :]
                m_ref[h, rs, :] = m_new
            acc_ref[h] += _dot(p_buf[...], v_ref[:, ps])

    @pl.when(j < i)
    def _():
        step(False)

    @pl.when(j == i)
    def _():
        step(True)
        lane = lax.broadcasted_iota(jnp.int32, (1, LANES), 1)
        for p in range(FOX_HEADS // 2):
            lo = acc_ref[2 * p] / jnp.sum(l_ref[2 * p], axis=-1, keepdims=True)
            hi = acc_ref[2 * p + 1] / jnp.sum(l_ref[2 * p + 1], axis=-1, keepdims=True)
            o_ref[:, p * LANES:(p + 1) * LANES] = jnp.where(lane < FOX_HD, lo, hi)


def fox_flash(qx, kx, vb, nb, t, tile):
    assert tile % FLASH_ROWS == 0
    nt = t // tile
    pairs = [(i, j) for i in range(nt) for j in range(i + 1)]
    it = jnp.asarray([p[0] for p in pairs], jnp.int32)
    jt = jnp.asarray([p[1] for p in pairs], jnp.int32)
    return pl.pallas_call(
        _fox_flash_kernel,
        out_shape=jax.ShapeDtypeStruct((nb * t, FOX_WIDTH), F32),
        grid_spec=pltpu.PrefetchScalarGridSpec(
            num_scalar_prefetch=2,
            grid=(nb, len(pairs)),
            in_specs=[pl.BlockSpec((tile, FOX_HEADS * LANES), lambda b, p, it, jt: (b * nt + it[p], 0)),
                      pl.BlockSpec((tile, FOX_HEADS * LANES), lambda b, p, it, jt: (b * nt + jt[p], 0)),
                      pl.BlockSpec((tile, FOX_WIDTH), lambda b, p, it, jt: (b * nt + jt[p], 0))],
            out_specs=pl.BlockSpec((tile, FOX_WIDTH), lambda b, p, it, jt: (b * nt + it[p], 0)),
            scratch_shapes=[pltpu.VMEM((FOX_HEADS, tile, LANES), F32), pltpu.VMEM((FOX_HEADS, tile, LANES), F32),
                            pltpu.VMEM((FOX_HEADS, tile, LANES), F32),
                            pltpu.VMEM((2, tile, tile), F32), pltpu.VMEM((2, tile, tile), BF16)]),
        compiler_params=_cparams("parallel", "arbitrary"),
        name="fox_flash",
    )(it, jt, qx, kx, vb)


def _fox_decode_kernel(pt_ref, qb_ref, kn_ref, vn_ref, cs_ref, *rest, n_pages):
    k_refs = rest[0:n_pages]
    v_refs = rest[n_pages:2 * n_pages]
    f_refs = rest[2 * n_pages:3 * n_pages]
    triu_ref, o_ref, kpad_ref, vpad_ref = rest[3 * n_pages:]
    page = kpad_ref.shape[0]
    nrow = FOX_HEADS * SAMPLE_T
    qb = qb_ref[...].astype(BF16)

    def per_query(x):
        return jnp.concatenate([x] * SAMPLE_T, axis=0)

    fcat = jnp.concatenate([f_refs[i][0, 0] for i in range(n_pages)], axis=0)
    cum_in = _dot_exact_rhs(fcat, triu_ref[...])
    totals = jnp.broadcast_to(cum_in[:, page - 1:page], cum_in.shape)
    off = jnp.zeros((FOX_HEADS, page), F32)
    ss = []
    for i in range(n_pages):
        rows = slice(i * FOX_HEADS, (i + 1) * FOX_HEADS)
        kt = k_refs[i][0, 0].reshape(FOX_WIDTH, page).astype(BF16)
        ss.append(_dot(qb, kt) - per_query(cum_in[rows] + off))
        off = off + totals[rows]
    kpad_ref[...] = jnp.zeros_like(kpad_ref)
    vpad_ref[...] = jnp.zeros_like(vpad_ref)
    kpad_ref[0:SAMPLE_T, :] = kn_ref[...]
    vpad_ref[0:SAMPLE_T, :] = vn_ref[...]
    csrow = _pad_rows(cs_ref[...], page).T[LANE_FOXF:LANE_FOXF + FOX_HEADS]
    s_new = _dot_nt(qb, kpad_ref[...].astype(BF16)) - per_query(csrow + off)
    tq_idx = lax.broadcasted_iota(jnp.int32, (nrow, page), 0) // FOX_HEADS
    tk_idx = lax.broadcasted_iota(jnp.int32, (nrow, page), 1)
    ss.append(jnp.where(tk_idx <= tq_idx, s_new, -jnp.inf))

    m = ss[0]
    for s in ss[1:]:
        m = jnp.maximum(m, s)
    m = jnp.max(m, axis=-1, keepdims=True)
    lsum = jnp.zeros((nrow, page), F32)
    acc = jnp.zeros((nrow, FOX_WIDTH), F32)
    for i, s in enumerate(ss):
        p = jnp.exp(s - m)
        lsum = lsum + p
        if i < n_pages:
            acc = acc + _dot_nt(p.astype(BF16), v_refs[i][0, 0].reshape(FOX_WIDTH, page).astype(BF16))
        else:
            acc = acc + _dot(p.astype(BF16), vpad_ref[...].astype(BF16))
    acc = acc / jnp.sum(lsum, axis=-1, keepdims=True)
    lane = lax.broadcasted_iota(jnp.int32, (nrow, FOX_WIDTH), 1) // FOX_HD
    head = lax.broadcasted_iota(jnp.int32, (nrow, FOX_WIDTH), 0) % FOX_HEADS
    picked = jnp.where(lane == head, acc, 0.0)
    o_ref[...] = jnp.sum(picked.reshape(SAMPLE_T, FOX_HEADS, FOX_WIDTH), axis=1)


def fox_decode(page_table, qx, kn, z, cs, pool_kt, pool_vt, pool_ft, layer):
    nb, n_pages = page_table.shape
    page = pool_kt.shape[-1]
    assert page == LANES
    nrow = FOX_HEADS * SAMPLE_T
    pt = page_table.reshape(-1)
    triu = jnp.asarray(np.triu(np.ones((page, page))), BF16)
    pair_of_head = jnp.asarray(np.arange(FOX_HEADS)[:, None] // 2 == np.arange(FOX_HEADS // 2)[None, :], F32)
    qb = (qx.reshape(nb, SAMPLE_T, FOX_HEADS, 1, LANES) * pair_of_head[None, None, :, :, None]).reshape(
        nb * nrow, FOX_WIDTH)

    def pg5(i):
        return lambda b, pt: (layer, pt[b * n_pages + i], 0, 0, 0)

    def pg4(i):
        return lambda b, pt: (layer, pt[b * n_pages + i], 0, 0)

    row = lambda b, pt: (b, 0)
    in_specs = [pl.BlockSpec((nrow, FOX_WIDTH), row),
                pl.BlockSpec((SAMPLE_T, FOX_WIDTH), row),
                pl.BlockSpec((SAMPLE_T, FOX_WIDTH), lambda b, pt: (b, COL_FV)),
                pl.BlockSpec((SAMPLE_T, LANES), row)]
    in_specs += [pl.BlockSpec((1, 1, FOX_HEADS, FOX_HD, page), pg5(i)) for i in range(n_pages)]
    in_specs += [pl.BlockSpec((1, 1, FOX_HEADS, FOX_HD, page), pg5(i)) for i in range(n_pages)]
    in_specs += [pl.BlockSpec((1, 1, FOX_HEADS, page), pg4(i)) for i in range(n_pages)]
    in_specs += [pl.BlockSpec((page, page), lambda b, pt: (0, 0))]
    return pl.pallas_call(
        functools.partial(_fox_decode_kernel, n_pages=n_pages),
        out_shape=jax.ShapeDtypeStruct((nb * SAMPLE_T, FOX_WIDTH), F32),
        grid_spec=pltpu.PrefetchScalarGridSpec(
            num_scalar_prefetch=1,
            grid=(nb,),
            in_specs=in_specs,
            out_specs=pl.BlockSpec((SAMPLE_T, FOX_WIDTH), row),
            scratch_shapes=[pltpu.VMEM((page, FOX_WIDTH), F32), pltpu.VMEM((page, FOX_WIDTH), F32)]),
        compiler_params=_cparams("parallel"),
        name="fox_decode",
    )(pt, qb, kn, z, cs, *([pool_kt] * n_pages), *([pool_vt] * n_pages), *([pool_ft] * n_pages), triu)


def _mlstm_kernel(q_ref, k_ref, v_ref, o_ref, g_ref, cs_ref, *rest, short):
    if short:
        c0_ref, n0_ref, m0_ref, gn_ref, y_ref, cout_ref, nout_ref, mout_ref, c_ref, n_ref, m_ref = rest
    else:
        (grow_ref, csrow_ref, c0_ref, n0_ref, m0_ref, gn_ref, y_ref, cout_ref, nout_ref, mout_ref,
         c_ref, n_ref, m_ref) = rest
    c = pl.program_id(1)
    nc = pl.num_programs(1)
    L = q_ref.shape[0]

    @pl.when(c == 0)
    def _():
        c_ref[...] = c0_ref[0, 0]
        n_ref[...] = n0_ref[0]
        m_ref[...] = m0_ref[0]

    Lk = max(L, LANES)
    Lq = max(L, MIN_BF16_ROWS)
    g = g_ref[...]
    cs = cs_ref[...]
    if short:
        grow = _pad_rows(g, Lk).T
        csrow = _pad_rows(cs, Lk).T
    else:
        grow = grow_ref[...]
        csrow = csrow_ref[...]
    causal = (lax.broadcasted_iota(jnp.int32, (Lq, Lk), 0) >= lax.broadcasted_iota(jnp.int32, (Lq, Lk), 1))
    scale = ML_HD ** -0.5
    heads = range(ML_HEADS)
    hsl = [slice(h * ML_HD, (h + 1) * ML_HD) for h in heads]
    qb = [_pad_rows(q_ref[:, hsl[h]], Lq).astype(BF16) for h in heads]
    kb = [_pad_rows(k_ref[:, hsl[h]], Lk).astype(BF16) for h in heads]
    vb = [_pad_rows(v_ref[:, hsl[h]], Lk).astype(BF16) for h in heads]
    bcol_k = [_pad_rows(cs[:, LANE_MLF + h:LANE_MLF + h + 1], Lk) for h in heads]
    bcol = [b[0:Lq] for b in bcol_k]
    icol = [_pad_rows(g[:, LANE_MLI + h:LANE_MLI + h + 1], Lk, -jnp.inf) for h in heads]
    log_d = [jnp.where(causal, bcol[h] - csrow[LANE_MLF + h:LANE_MLF + h + 1, :]
                       + grow[LANE_MLI + h:LANE_MLI + h + 1, :], -jnp.inf) for h in heads]
    log_inter = [bcol[h] + m_ref[0:1, h:h + 1] for h in heads]
    m_t = [jnp.maximum(log_inter[h], jnp.max(log_d[h], axis=-1, keepdims=True)) for h in heads]
    inter_w = [jnp.exp(log_inter[h] - m_t[h]) for h in heads]
    qk = [_dot_nt(qb[h], kb[h]) for h in heads]
    ch = [c_ref[h] for h in heads]
    n_row = [n_ref[h:h + 1, :] for h in heads]
    qc = [_dot_nt(qb[h], ch[h].astype(BF16)) for h in heads]
    s = [qk[h] * (jnp.exp(log_d[h] - m_t[h]) * scale) for h in heads]
    sv = [_dot(s[h].astype(BF16), vb[h]) for h in heads]
    for h in heads:
        num = sv[h] + inter_w[h] * qc[h]
        den = (jnp.sum(s[h], axis=-1, keepdims=True)
               + inter_w[h] * jnp.sum(qb[h].astype(F32) * n_row[h], axis=-1, keepdims=True))
        hh = num / jnp.maximum(jnp.abs(den), jnp.exp(-m_t[h]))
        y_ref[:, hsl[h]] = _rms(hh[0:L], gn_ref[...]) * _sigmoid(o_ref[:, hsl[h]])
    m_end = [m_t[h][L - 1:L, :] for h in heads]
    a_end = [inter_w[h][L - 1:L, :] for h in heads]
    w_col = [jnp.exp(bcol[h][L - 1:L, :] - bcol_k[h] + icol[h] - m_end[h]) * scale for h in heads]
    upd = [_dot_tn((vb[h].astype(F32) * w_col[h]).astype(BF16), kb[h]) for h in heads]
    for h in heads:
        c_ref[h] = a_end[h] * ch[h] + upd[h]
        n_ref[h:h + 1, :] = a_end[h] * n_row[h] + jnp.sum(kb[h].astype(F32) * w_col[h], axis=0, keepdims=True)
        m_ref[0:1, h:h + 1] = m_end[h]

    @pl.when(c == nc - 1)
    def _():
        cout_ref[0] = c_ref[...]
        nout_ref[0] = n_ref[...]
        mout_ref[0] = m_ref[...]


def mlstm(za, zb, g, cs, grow, csrow, c0, n0, m0, gn, nb, t, L, c_layer):
    nc = t // L
    rows = lambda b, c: (b * nc + c, 0)
    rr = 2 * SUBLANES
    m0p = jnp.zeros((nb, 1, LANES), F32).at[:, 0, :ML_HEADS].set(m0)
    short = grow is None
    row_specs = [] if short else [pl.BlockSpec((rr, L), lambda b, c: (b * (LANES // rr), c))] * 2
    row_args = [] if short else [grow, csrow]
    outs = pl.pallas_call(
        functools.partial(_mlstm_kernel, short=short),
        out_shape=(jax.ShapeDtypeStruct((nb * t, ML_WIDTH), F32),
                   jax.ShapeDtypeStruct((nb, ML_HEADS, ML_HD, ML_HD), F32),
                   jax.ShapeDtypeStruct((nb, ML_HEADS, ML_HD), F32),
                   jax.ShapeDtypeStruct((nb, 1, LANES), F32)),
        grid=(nb, nc),
        in_specs=[pl.BlockSpec((L, 512), lambda b, c: (b * nc + c, COL_MQ)),
                  pl.BlockSpec((L, 512), lambda b, c: (b * nc + c, COL_MK)),
                  pl.BlockSpec((L, 512), lambda b, c: (b * nc + c, COL_MV)),
                  pl.BlockSpec((L, 512), lambda b, c: (b * nc + c, COL_MO)),
                  pl.BlockSpec((L, LANES), rows), pl.BlockSpec((L, LANES), rows), *row_specs,
                  pl.BlockSpec((1, 1, ML_HEADS, ML_HD, ML_HD), lambda b, c: (c_layer, b, 0, 0, 0)),
                  pl.BlockSpec((1, ML_HEADS, ML_HD), lambda b, c: (b, 0, 0)),
                  pl.BlockSpec((1, 1, LANES), lambda b, c: (b, 0, 0)),
                  pl.BlockSpec((1, ML_HD), lambda b, c: (0, 0))],
        out_specs=(pl.BlockSpec((L, ML_WIDTH), rows),
                   pl.BlockSpec((1, ML_HEADS, ML_HD, ML_HD), lambda b, c: (b, 0, 0, 0)),
                   pl.BlockSpec((1, ML_HEADS, ML_HD), lambda b, c: (b, 0, 0)),
                   pl.BlockSpec((1, 1, LANES), lambda b, c: (b, 0, 0))),
        scratch_shapes=[pltpu.VMEM((ML_HEADS, ML_HD, ML_HD), F32), pltpu.VMEM((ML_HEADS, ML_HD), F32),
                        pltpu.VMEM((1, LANES), F32)],
        compiler_params=_cparams("parallel", "arbitrary"),
        name="mlstm",
    )(zb, zb, zb, za, g, cs, *row_args, c0, n0, m0p, gn.reshape(1, ML_HD))
    y, c_new, n_new, m_new = outs
    return y, c_new, n_new, m_new[:, 0, :ML_HEADS]


def _merge_kernel(x_ref, g_ref, wg_ref, ys_ref, yf_ref, ym_ref, ws_ref, wf_ref, wm_ref, wo_ref, o_ref):
    x = x_ref[...]
    hn = _rms(x, g_ref[...]).astype(BF16)
    branches = (ys_ref, ws_ref), (yf_ref, wf_ref), (ym_ref, wm_ref)
    merged = None
    for b, (y_ref, w_ref) in enumerate(branches):
        gate = _sigmoid(_dot(hn, wg_ref[:, b * D_MODEL:(b + 1) * D_MODEL]))
        term = gate * _dot(y_ref[...].astype(BF16), w_ref[...])
        merged = term if merged is None else merged + term
    o_ref[...] = x + _dot(merged.astype(BF16), wo_ref[...])


def merge_out(x, g_mix, w_gates, ys, yf, ym, ws, wf, wm, wo, tm=256):
    m = x.shape[0]
    tm = min(tm, m)
    row = lambda i: (i, 0)
    const = lambda i: (0, 0)
    return pl.pallas_call(
        _merge_kernel,
        out_shape=jax.ShapeDtypeStruct((m, D_MODEL), F32),
        grid=(m // tm,),
        in_specs=[pl.BlockSpec((tm, D_MODEL), row), pl.BlockSpec((1, D_MODEL), const),
                  pl.BlockSpec((D_MODEL, 3 * D_MODEL), const),
                  pl.BlockSpec((tm, 512), row), pl.BlockSpec((tm, 512), row), pl.BlockSpec((tm, 512), row),
                  pl.BlockSpec((512, D_MODEL), const), pl.BlockSpec((512, D_MODEL), const),
                  pl.BlockSpec((512, D_MODEL), const), pl.BlockSpec((D_MODEL, D_MODEL), const)],
        out_specs=pl.BlockSpec((tm, D_MODEL), row),
        compiler_params=_cparams("parallel"),
        name="merge_out",
    )(x, g_mix.reshape(1, D_MODEL), w_gates, ys, yf, ym, ws, wf, wm, wo)


def _cross_block_kernel(x_ref, g_ref, wq_ref, gq_ref, k_ref, v_ref, wo_ref, o_ref):
    x = x_ref[...]
    q = _dot(_rms(x, g_ref[...]).astype(BF16), wq_ref[...])
    outs = []
    for h in range(MEM_HEADS):
        hs = slice(h * MEM_HD, (h + 1) * MEM_HD)
        qh = _rms(q[:, hs], gq_ref[...]).astype(BF16)
        s = _dot_nt(qh, k_ref[:, hs].astype(BF16)) * (MEM_HD ** -0.5)
        m = jnp.max(s, axis=-1, keepdims=True)
        p = jnp.exp(s - m)
        l = jnp.sum(p, axis=-1, keepdims=True)
        outs.append((_dot(p.astype(BF16), v_ref[:, hs].astype(BF16)) / l).astype(BF16))
    o_ref[...] = x + _dot(jnp.concatenate(outs, axis=1), wo_ref[...])


def cross_block(x, g, wq, gq, mem_k, mem_v, wo, nb, t, tq):
    nq = t // tq
    n_mem = mem_k.shape[0] // nb
    const = lambda b, i: (0, 0)
    row = lambda b, i: (b * nq + i, 0)
    return pl.pallas_call(
        _cross_block_kernel,
        out_shape=jax.ShapeDtypeStruct((nb * t, D_MODEL), F32),
        grid=(nb, nq),
        in_specs=[pl.BlockSpec((tq, D_MODEL), row), pl.BlockSpec((1, D_MODEL), const),
                  pl.BlockSpec((D_MODEL, MEM_WIDTH), const), pl.BlockSpec((1, MEM_HD), const),
                  pl.BlockSpec((n_mem, MEM_WIDTH), lambda b, i: (b, 0)),
                  pl.BlockSpec((n_mem, MEM_WIDTH), lambda b, i: (b, 0)),
                  pl.BlockSpec((MEM_WIDTH, D_MODEL), const)],
        out_specs=pl.BlockSpec((tq, D_MODEL), row),
        compiler_params=_cparams("parallel", "parallel"),
        name="cross_block",
    )(x, g.reshape(1, D_MODEL), wq, gq.reshape(1, MEM_HD), mem_k, mem_v, wo)


def _cross_cached_kernel(q_ref, k_ref, v_ref, o_ref):
    nseq = k_ref.shape[1]
    tq = q_ref.shape[0] // nseq
    for sq in range(nseq):
        q = q_ref[sq * tq:(sq + 1) * tq, :]
        qr = jnp.concatenate([q[:, h * MEM_HD:(h + 1) * MEM_HD] for h in range(MEM_HEADS)], axis=0)
        kf = k_ref[0, sq].astype(BF16)
        vf = v_ref[0, sq].astype(BF16)
        s = _dot_nt(qr.astype(BF16), kf) * (MEM_HD ** -0.5)
        row_head = lax.broadcasted_iota(jnp.int32, s.shape, 0) // tq
        col_head = lax.broadcasted_iota(jnp.int32, s.shape, 1) % MEM_HEADS
        s = jnp.where(row_head == col_head, s, -jnp.inf)
        m = jnp.max(s, axis=-1, keepdims=True)
        p = jnp.exp(s - m)
        l = jnp.sum(p, axis=-1, keepdims=True)
        o = _dot(p.astype(BF16), vf) / l
        for h in range(MEM_HEADS):
            o_ref[sq * tq:(sq + 1) * tq, h * MEM_HD:(h + 1) * MEM_HD] = o[h * tq:(h + 1) * tq]


CROSS_CACHED_SEQS = 4


def cross_attend_cached(q, mem_k, mem_v, nb, t, layer):
    ns = CROSS_CACHED_SEQS
    kv_spec = pl.BlockSpec((1, ns, mem_k.shape[2], MEM_HD), lambda b: (layer, b, 0, 0))
    return pl.pallas_call(
        _cross_cached_kernel,
        out_shape=jax.ShapeDtypeStruct((nb * t, MEM_WIDTH), F32),
        grid=(nb // ns,),
        in_specs=[pl.BlockSpec((ns * t, MEM_WIDTH), lambda b: (b, 0)), kv_spec, kv_spec],
        out_specs=pl.BlockSpec((ns * t, MEM_WIDTH), lambda b: (b, 0)),
        compiler_params=_cparams("parallel"),
        name="cross_attend_cached",
    )(q, mem_k, mem_v)


def _proj_residual_kernel(x_ref, a_ref, w_ref, o_ref):
    o_ref[...] = x_ref[...] + _dot(a_ref[...].astype(BF16), w_ref[...])


def proj_residual(x, a, w, tm=512):
    m, n = x.shape
    k = a.shape[1]
    tm = min(tm, m)
    return pl.pallas_call(
        _proj_residual_kernel,
        out_shape=jax.ShapeDtypeStruct((m, n), F32),
        grid=(m // tm,),
        in_specs=[pl.BlockSpec((tm, n), lambda i: (i, 0)), pl.BlockSpec((tm, k), lambda i: (i, 0)),
                  pl.BlockSpec((k, n), lambda i: (0, 0))],
        out_specs=pl.BlockSpec((tm, n), lambda i: (i, 0)),
        compiler_params=_cparams("parallel"),
        name="proj_residual",
    )(x, a, w)


def _mlp_kernel(x_ref, g_ref, wu_ref, wd_ref, o_ref, hn_ref, acc_ref):
    f = pl.program_id(1)

    @pl.when(f == 0)
    def _():
        hn_ref[...] = _rms(x_ref[...], g_ref[...]).astype(BF16)
        acc_ref[...] = jnp.zeros_like(acc_ref)

    a = jnp.maximum(_dot(hn_ref[...], wu_ref[...]), 0.0)
    acc_ref[...] += _dot((a * a).astype(BF16), wd_ref[...])

    @pl.when(f == pl.num_programs(1) - 1)
    def _():
        o_ref[...] = x_ref[...] + acc_ref[...]


def mlp(x, g, wu, wd, tm=1024, tf=512):
    m, d = x.shape
    dff = wu.shape[1]
    tm = min(tm, m)
    return pl.pallas_call(
        _mlp_kernel,
        out_shape=jax.ShapeDtypeStruct((m, d), F32),
        grid=(m // tm, dff // tf),
        in_specs=[pl.BlockSpec((tm, d), lambda i, f: (i, 0)), pl.BlockSpec((1, d), lambda i, f: (0, 0)),
                  pl.BlockSpec((d, tf), lambda i, f: (0, f)), pl.BlockSpec((tf, d), lambda i, f: (f, 0))],
        out_specs=pl.BlockSpec((tm, d), lambda i, f: (i, 0)),
        scratch_shapes=[pltpu.VMEM((tm, d), BF16), pltpu.VMEM((tm, d), F32)],
        compiler_params=_cparams("parallel", "arbitrary"),
        name="mlp",
    )(x, g.reshape(1, d), wu, wd)


def _pack_w_in(w_in):
    offs = np.concatenate([[0], np.cumsum(SPLITS)])
    col = lambda i: w_in[:, int(offs[i]):int(offs[i + 1])]
    s5, fq, fk, fv, ff, mq, mk, mv, mi, mf, mo, gates = [col(i) for i in range(12)]
    pad = jnp.zeros((w_in.shape[0], 512 - FOX_HEADS - 2 * ML_HEADS), w_in.dtype)
    w_a = jnp.concatenate([s5, fq, fk, fv, mo, ff, mi, mf, pad], axis=1).astype(BF16)
    w_b = jnp.concatenate([mq, mk, mv], axis=1).astype(BF16)
    return w_a, w_b, gates.astype(BF16)


def _layer_weights(l, g_mix, w_in, s5_a_re, s5_a_im, s5_log_step, s5_b_re, s5_b_im, s5_c_re, s5_c_im, s5_d,
                   s5_w_glu, s5_b_glu, fox_gq, fox_gk, fox_bf, ml_bi, ml_bf, ml_gn, w_br_s5, w_br_fox, w_br_ml,
                   w_out, g_cross, w_cq, cross_gq, g_mem, w_mk, w_mv, cross_gk, w_co, g_mlp, w_up, w_down):
    bias_row = jnp.zeros((1, LANES), F32)
    bias_row = bias_row.at[0, LANE_FOXF:LANE_FOXF + FOX_HEADS].set(fox_bf[l])
    bias_row = bias_row.at[0, LANE_MLI:LANE_MLI + ML_HEADS].set(ml_bi[l])
    bias_row = bias_row.at[0, LANE_MLF:LANE_MLF + ML_HEADS].set(ml_bf[l])
    w_a, w_b, w_gates = _pack_w_in(w_in[l])
    return dict(
        g_mix=g_mix[l], w_a=w_a, w_b=w_b, w_gates=w_gates,
        s5=s5_params(s5_a_re[l], s5_a_im[l], s5_log_step[l], s5_b_re[l], s5_b_im[l], s5_c_re[l], s5_c_im[l],
                     s5_d[l], s5_w_glu[l], s5_b_glu[l], (SEQ_TILE // SUBLANES, SAMPLE_T // SUBLANES)),
        gq=jnp.tile(fox_gq[l], FOX_HEADS).reshape(1, FOX_WIDTH),
        gk=jnp.tile(fox_gk[l], FOX_HEADS).reshape(1, FOX_WIDTH),
        bias_row=bias_row, ml_gn=ml_gn[l],
        w_br_s5=w_br_s5[l].astype(BF16), w_br_fox=w_br_fox[l].astype(BF16), w_br_ml=w_br_ml[l].astype(BF16),
        w_out=w_out[l].astype(BF16), g_cross=g_cross[l], w_cq=w_cq[l].astype(BF16), cross_gq=cross_gq[l],
        g_mem=g_mem[l], w_mk=w_mk[l].astype(BF16), w_mv=w_mv[l].astype(BF16), cross_gk=cross_gk[l],
        w_co=w_co[l].astype(BF16), g_mlp=g_mlp[l], w_up=w_up[l].astype(BF16), w_down=w_down[l].astype(BF16))


def _hybrid_layer(x, W, nb, t, seq_tile, t_valid, s5_state, ml_state, mem_k, mem_v, fox_attend, augment,
                  mem_layer=None):
    z = norm_matmul(x, W["g_mix"], W["w_a"], tm=min(1024, nb * t))
    zb = norm_matmul(x, W["g_mix"], W["w_b"], tm=min(1024, nb * t), tn=W["w_b"].shape[1],
                     out_dtype=BF16 if seq_tile % MIN_BF16_ROWS == 0 else F32)
    prep = gate_prep(z, W["gq"], W["gk"], W["bias_row"], nb, t, seq_tile, t_valid, augment)
    kn, g, cs = prep[1:4]
    grow, csrow = prep[4:6] if augment else (None, None)
    y_s5, s5_re, s5_im = s5_mixer(z, s5_state[0], s5_state[1], W["s5"], nb, t, seq_tile, t_valid - 1,
                                  seqs=S5_SHORT_SEQS if t == SUBLANES else 1)
    y_fox = fox_attend(prep, z)
    y_ml, c_new, n_new, m_new = mlstm(z, zb, g, cs, grow, csrow, ml_state[0], ml_state[1], ml_state[2], W["ml_gn"],
                                      nb, t, seq_tile, 0 if mem_layer is None else mem_layer)
    x = merge_out(x, W["g_mix"], W["w_gates"], y_s5, y_fox, y_ml, W["w_br_s5"], W["w_br_fox"], W["w_br_ml"],
                  W["w_out"])
    if mem_layer is None:
        x = cross_block(x, W["g_cross"], W["w_cq"], W["cross_gq"], mem_k, mem_v, W["w_co"], nb, t, seq_tile)
    else:
        qc = norm_matmul(x, W["g_cross"], W["w_cq"], head_gain=W["cross_gq"])
        oc = cross_attend_cached(qc, mem_k, mem_v, nb, t, mem_layer)
        x = proj_residual(x, oc, W["w_co"])
    x = mlp(x, W["g_mlp"], W["w_up"], W["w_down"])
    if augment:
        fox_k = prep[1].reshape(nb, FOX_HEADS, FOX_HD, t).transpose(0, 3, 1, 2)
        fox_v = prep[8].reshape(nb, FOX_HEADS, FOX_HD, t).transpose(0, 3, 1, 2)
    else:
        fox_k = kn.reshape(nb, t, FOX_HEADS, FOX_HD)
        fox_v = z[:, COL_FV * 512:(COL_FV + 1) * 512].reshape(nb, t, FOX_HEADS, FOX_HD)
    return x, fox_k, fox_v, g, s5_re.reshape(nb, S5_GROUPS, S5_STATE), s5_im.reshape(nb, S5_GROUPS, S5_STATE), \
        c_new, n_new, m_new


def kernel(x_prompt, x_sample, mem_prompt, cache_fox_k, cache_fox_v, cache_fox_logf, page_table, state_s5_re, state_s5_im, state_mlstm_C, state_mlstm_n, state_mlstm_m, cache_mem_k, cache_mem_v, g_mix, w_in, s5_a_re, s5_a_im, s5_log_step, s5_b_re, s5_b_im, s5_c_re, s5_c_im, s5_d, s5_w_glu, s5_b_glu, fox_gq, fox_gk, fox_bf, ml_bi, ml_bf, ml_gn, w_br_s5, w_br_fox, w_br_ml, w_out, g_cross, w_cq, cross_gq, g_mem, w_mk, w_mv, cross_gk, w_co, g_mlp, w_up, w_down):
    depth = w_in.shape[0]
    bp, tp, _ = x_prompt.shape
    bs, ts, _ = x_sample.shape
    n_mem = mem_prompt.shape[1]

    xp = x_prompt.reshape(bp * tp, D_MODEL)
    xs = jnp.pad(x_sample, ((0, 0), (0, SAMPLE_T - ts), (0, 0))).reshape(bs * SAMPLE_T, D_MODEL)
    mem = mem_prompt.reshape(bp * n_mem, D_MODEL)
    zeros_p = (jnp.zeros((bp, S5_LANES), F32), jnp.zeros((bp, S5_LANES), F32))
    zeros_ml = (jnp.zeros((1, bp, ML_HEADS, ML_HD, ML_HD), F32), jnp.zeros((bp, ML_HEADS, ML_HD), F32),
                jnp.zeros((bp, ML_HEADS), F32))
    pool_kt = jnp.transpose(cache_fox_k, (0, 1, 3, 4, 2))
    pool_vt = jnp.transpose(cache_fox_v, (0, 1, 3, 4, 2))
    pool_ft = jnp.transpose(cache_fox_logf, (0, 1, 3, 2))
    st_p, st_s = [], []
    for l in range(depth):
        W = _layer_weights(l, g_mix, w_in, s5_a_re, s5_a_im, s5_log_step, s5_b_re, s5_b_im, s5_c_re, s5_c_im, s5_d,
                           s5_w_glu, s5_b_glu, fox_gq, fox_gk, fox_bf, ml_bi, ml_bf, ml_gn, w_br_s5, w_br_fox,
                           w_br_ml, w_out, g_cross, w_cq, cross_gq, g_mem, w_mk, w_mv, cross_gk, w_co, g_mlp, w_up,
                           w_down)
        mk_p = norm_matmul(mem, W["g_mem"], W["w_mk"], head_gain=W["cross_gk"])
        mv_p = norm_matmul(mem, W["g_mem"], W["w_mv"])

        def flash(prep, z):
            return fox_flash(prep[0], prep[6], prep[7], bp, tp, FLASH_TILE)

        xp, fox_k, fox_v, g, s5r, s5i, c_new, n_new, m_new = _hybrid_layer(
            xp, W, bp, tp, SEQ_TILE, tp, zeros_p, zeros_ml, mk_p, mv_p, flash, True)
        st_p.append((fox_k, fox_v,
                     g[:, LANE_FOXF:LANE_FOXF + FOX_HEADS].reshape(bp, tp, FOX_HEADS),
                     s5r, s5i, c_new, n_new, m_new,
                     mk_p.reshape(bp, n_mem, MEM_HEADS, MEM_HD), mv_p.reshape(bp, n_mem, MEM_HEADS, MEM_HD)))

        def decode(prep, z, layer=l):
            return fox_decode(page_table, prep[0], prep[1], z, prep[3], pool_kt, pool_vt, pool_ft, layer)

        xs, fox_k, fox_v, g, s5r, s5i, c_new, n_new, m_new = _hybrid_layer(
            xs, W, bs, SAMPLE_T, SAMPLE_T, ts,
            (state_s5_re[l].reshape(bs, S5_LANES), state_s5_im[l].reshape(bs, S5_LANES)),
            (state_mlstm_C, state_mlstm_n[l], state_mlstm_m[l]),
            cache_mem_k.reshape(depth, bs, n_mem * MEM_HEADS, MEM_HD),
            cache_mem_v.reshape(depth, bs, n_mem * MEM_HEADS, MEM_HD), decode, False, mem_layer=l)
        st_s.append((fox_k[:, :ts], fox_v[:, :ts],
                     g[:, LANE_FOXF:LANE_FOXF + FOX_HEADS].reshape(bs, SAMPLE_T, FOX_HEADS)[:, :ts],
                     s5r, s5i, c_new, n_new, m_new))
    outs_p = [jnp.stack(a) for a in zip(*st_p)]
    outs_s = [jnp.stack(a) for a in zip(*st_s)]
    yp = xp.reshape(bp, tp, D_MODEL)
    ys = xs.reshape(bs, SAMPLE_T, D_MODEL)[:, :ts]
    return (yp, ys, *outs_p, *outs_s)
```

```python
import functools
import math

import jax
import jax.numpy as jnp
import numpy as np
from jax import lax
from jax.experimental import pallas as pl
from jax.experimental.pallas import tpu as pltpu

F32 = jnp.float32
BF16 = jnp.bfloat16

LANES = 128
SUBLANES = 8
MIN_BF16_ROWS = 16
MXU_DIM = 256
VMEM_LIMIT_BYTES = 48 * 1024 * 1024

D_MODEL = 1024
S5_WIDTH = 512
S5_GROUP = 16
S5_GROUPS = 32
S5_STATE = 64
S5_LANES = S5_GROUPS * S5_STATE
FOX_HEADS = 8
FOX_HD = 64
FOX_WIDTH = 512
ML_HEADS = 4
ML_HD = 128
ML_WIDTH = 512
MEM_HEADS = 4
MEM_HD = 128
MEM_WIDTH = 512
D_FF = 4096
EPS = 1e-6
LOG2E = math.log2(math.e)
SPLITS =(S5_WIDTH, FOX_WIDTH, FOX_WIDTH, FOX_WIDTH, FOX_HEADS, ML_WIDTH, ML_WIDTH, ML_WIDTH,
          ML_HEADS, ML_HEADS, ML_WIDTH, 3 * D_MODEL)

COL_S5, COL_FQ, COL_FK, COL_FV, COL_MO, COL_SMALL = range(6)
COL_MQ, COL_MK, COL_MV = range(3)
LANE_FOXF = 0
LANE_MLI = 8
LANE_MLF = 12
SEQ_TILE = 256
FLASH_TILE = 512
FLASH_ROWS = 128
SAMPLE_T = 8


def _cparams(*sem):
    return pltpu.CompilerParams(dimension_semantics=sem, vmem_limit_bytes=VMEM_LIMIT_BYTES)


def _dot(a, b):
    return jnp.dot(a, b, preferred_element_type=F32)


def _dot_nt(a, b):
    return lax.dot_general(a, b, (((1,), (1,)), ((), ())), preferred_element_type=F32)


def _dot_tn(a, b):
    return lax.dot_general(a, b, (((0,), (0,)), ((), ())), preferred_element_type=F32)


def _split3(x):
    hi = x.astype(BF16)
    r1 = x - hi.astype(F32)
    mid = r1.astype(BF16)
    lo = (r1 - mid.astype(F32)).astype(BF16)
    return hi, mid, lo


def _dot_exact_rhs(x, ones_rhs):
    hi, mid, lo = _split3(x)
    return _dot(hi, ones_rhs) + _dot(mid, ones_rhs) + _dot(lo, ones_rhs)


def _dot_exact_lhs(ones_lhs, x):
    hi, mid, lo = _split3(x)
    return _dot(ones_lhs, hi) + _dot(ones_lhs, mid) + _dot(ones_lhs, lo)


def _pad_rows(x, n, fill=0.0):
    if x.shape[0] >= n:
        return x
    return jnp.concatenate([x, jnp.full((n - x.shape[0], x.shape[1]), fill, x.dtype)], axis=0)


def _lane_tile(x, n):
    return x if n == 1 else jnp.concatenate([x] * n, axis=1)


def _log_sigmoid(a):
    return jnp.minimum(a, 0.0) - jnp.log1p(jnp.exp(-jnp.abs(a)))


def _sigmoid(a):
    return 1.0 / (1.0 + jnp.exp(-a))


def _gelu_tanh(x):
    c = math.sqrt(2.0 / math.pi)
    return 0.5 * x * (1.0 + jnp.tanh(c * (x + 0.044715 * (x * x * x))))


def _rms(x, g):
    ms = jnp.mean(x * x, axis=-1, keepdims=True)
    return x * lax.rsqrt(ms + EPS) * g


def _norm_matmul_kernel(x_ref, g_ref, w_ref, *rest, head_norm):
    if head_norm:
        hg_ref, o_ref, hn_ref = rest
    else:
        o_ref, hn_ref = rest

    @pl.when(pl.program_id(1) == 0)
    def _():
        hn_ref[...] = _rms(x_ref[...], g_ref[...]).astype(BF16)

    y = _dot(hn_ref[...], w_ref[...])
    if head_norm:
        tn = y.shape[1]
        for s in range(tn // LANES):
            sl = slice(s * LANES, (s + 1) * LANES)
            o_ref[:, sl] = _rms(y[:, sl], hg_ref[...]).astype(o_ref.dtype)
    else:
        o_ref[...] = y.astype(o_ref.dtype)


def norm_matmul(x, g, w, head_gain=None, out_dtype=F32, tm=512, tn=512):
    m, k = x.shape
    n = w.shape[1]
    tm = min(tm, m)
    tn = min(tn, n)
    head_norm = head_gain is not None
    in_specs = [pl.BlockSpec((tm, k), lambda i, j: (i, 0)),
                pl.BlockSpec((1, k), lambda i, j: (0, 0)),
                pl.BlockSpec((k, tn), lambda i, j: (0, j))]
    args = [x, g.reshape(1, k), w]
    if head_norm:
        in_specs.append(pl.BlockSpec((1, LANES), lambda i, j: (0, 0)))
        args.append(head_gain.reshape(1, LANES))
    return pl.pallas_call(
        functools.partial(_norm_matmul_kernel, head_norm=head_norm),
        out_shape=jax.ShapeDtypeStruct((m, n), out_dtype),
        grid=(m // tm, n // tn),
        in_specs=in_specs,
        out_specs=pl.BlockSpec((tm, tn), lambda i, j: (i, j)),
        scratch_shapes=[pltpu.VMEM((tm, k), BF16)],
        compiler_params=_cparams("parallel", "arbitrary"),
        name="norm_matmul",
    )(*args)


def _prep_kernel(fq_ref, fk_ref, fv_ref, sm_ref, gq_ref, gk_ref, bias_ref, gmat_ref, tril_ref, sel_ref, aug_ref,
                 *rest, seq_rows, t_valid, augment):
    if augment:
        qx_ref, kn_ref, g_ref, cs_ref, grow_ref, csrow_ref, kx_ref, vb_ref, vt_ref, carry_ref = rest
    else:
        qx_ref, kn_ref, g_ref, cs_ref, carry_ref = rest
    c = pl.program_id(1)
    tc = sm_ref.shape[0]

    @pl.when(c == 0)
    def _():
        carry_ref[...] = jnp.zeros_like(carry_ref)

    gmat = gmat_ref[...]

    def head_rms(x, gain):
        x2 = x * x
        hi = x2.astype(BF16)
        lo = (x2 - hi.astype(F32)).astype(BF16)
        ss = (_dot(hi, gmat) + _dot(lo, gmat)) * (1.0 / FOX_HD)
        return x * lax.rsqrt(ss + EPS) * gain

    qn = head_rms(fq_ref[...], gq_ref[...]) * ((FOX_HD ** -0.5) * (LOG2E if augment else 1.0))
    kn = head_rms(fk_ref[...], gk_ref[...])
    if augment:
        kn_ref[...] = kn.T
        vt_ref[...] = fv_ref[...].T
    else:
        kn_ref[...] = kn

    a = sm_ref[...] + bias_ref[...]
    lane = lax.broadcasted_iota(jnp.int32, a.shape, 1)
    pos = (lax.broadcasted_iota(jnp.int32, a.shape, 0) + c * tc) % seq_rows
    is_i = (lane >= LANE_MLI) & (lane < LANE_MLF)
    used = lane < LANE_MLF + ML_HEADS
    valid = pos < t_valid
    g = jnp.where(is_i, a, _log_sigmoid(a))
    g = jnp.where(used, g, 0.0)
    g = jnp.where(valid, g, jnp.where(is_i, -jnp.inf, 0.0))
    gc = jnp.where(is_i, 0.0, g)
    cs = _dot_exact_lhs(tril_ref[...], gc)
    carry = carry_ref[...]
    csg = cs + jnp.where(lane < FOX_HEADS, carry, 0.0)
    carry_ref[...] = carry + cs[tc - 1:tc, :]
    g_ref[...] = g
    cs_ref[...] = csg
    if augment:
        grow_ref[...] = g.T
        csrow_ref[...] = csg.T

    nx = FOX_HEADS * LANES
    lanex = lax.broadcasted_iota(jnp.int32, (1, nx), 1)
    keep = ((lanex % LANES) >= FOX_HD) == ((lanex // LANES) % 2 == 1)
    q_exp = jnp.concatenate([qn[:, (h // 2) * LANES:(h // 2 + 1) * LANES] for h in range(FOX_HEADS)], axis=1)
    if not augment:
        qx_ref[...] = jnp.where(keep, q_exp, 0.0).astype(qx_ref.dtype)
        return
    k_exp = jnp.concatenate([kn[:, (h // 2) * LANES:(h // 2 + 1) * LANES] for h in range(FOX_HEADS)], axis=1)
    hi, mid, lo = _split3(jnp.where(lane < FOX_HEADS, csg * LOG2E, 0.0))
    packed = (hi.astype(F32) + pltpu.roll(mid.astype(F32), FOX_HEADS, 1)
              + pltpu.roll(lo.astype(F32), 2 * FOX_HEADS, 1)).astype(BF16)
    aug = aug_ref[...] + _dot(packed, sel_ref[...])
    qx_ref[...] = jnp.where(keep, q_exp, aug[:, 0:nx]).astype(qx_ref.dtype)
    kx_ref[...] = jnp.where(keep, k_exp, aug[:, nx:2 * nx]).astype(kx_ref.dtype)
    vb_ref[...] = fv_ref[...].astype(vb_ref.dtype)


PREP_SHORT_ROWS = 128


def gate_prep(z, gq, gk, bias_row, nb, t, tc, t_valid, augment):
    m = nb * t
    gmat = jnp.asarray(np.kron(np.eye(FOX_HEADS), np.ones((FOX_HD, FOX_HD))), BF16)
    if augment:
        nc, tp, grid_rows = t // tc, tc, nb
        tril = np.tril(np.ones((tc, tc)))
    else:
        tc = tp = PREP_SHORT_ROWS
        nc, grid_rows = 1, m // tc
        tril = np.kron(np.eye(tc // t), np.tril(np.ones((t, t))))
    tril = jnp.asarray(tril, BF16)
    nx = FOX_HEADS * LANES
    sel = np.zeros((LANES, 2 * nx), np.float32)
    aug_const = np.zeros((1, 2 * nx), np.float32)
    for h in range(FOX_HEADS):
        o = h * LANES + (0 if h % 2 else FOX_HD)
        for p in range(3):
            sel[p * FOX_HEADS + h, o + p] = 1.0
            sel[p * FOX_HEADS + h, nx + o + 3 + p] = -1.0
            aug_const[0, o + 3 + p] = 1.0
            aug_const[0, nx + o + p] = 1.0
    sel, aug_const = jnp.asarray(sel, BF16), jnp.asarray(aug_const, F32)
    row_map = lambda b, c: (b * nc + c, 0)
    const = lambda b, c: (0, 0)
    feat_major = jax.ShapeDtypeStruct((nb * FOX_WIDTH, t), F32)
    feat_spec = pl.BlockSpec((FOX_WIDTH, tc), lambda b, c: (b, c))
    out_shape = [jax.ShapeDtypeStruct((m, FOX_HEADS * LANES), BF16 if augment else F32),
                 feat_major if augment else jax.ShapeDtypeStruct((m, FOX_WIDTH), F32),
                 jax.ShapeDtypeStruct((m, LANES), F32),
                 jax.ShapeDtypeStruct((m, LANES), F32)]
    out_specs = [pl.BlockSpec((tc, FOX_HEADS * LANES), row_map),
                 feat_spec if augment else pl.BlockSpec((tc, FOX_WIDTH), row_map),
                 pl.BlockSpec((tc, LANES), row_map),
                 pl.BlockSpec((tc, LANES), row_map)]
    if augment:
        out_shape += [jax.ShapeDtypeStruct((nb * LANES, t), F32), jax.ShapeDtypeStruct((nb * LANES, t), F32),
                      jax.ShapeDtypeStruct((m, FOX_HEADS * LANES), BF16), jax.ShapeDtypeStruct((m, FOX_WIDTH), BF16),
                      feat_major]
        out_specs += [pl.BlockSpec((LANES, tc), lambda b, c: (b, c)), pl.BlockSpec((LANES, tc), lambda b, c: (b, c)),
                      pl.BlockSpec((tc, FOX_HEADS * LANES), row_map), pl.BlockSpec((tc, FOX_WIDTH), row_map),
                      feat_spec]
    return pl.pallas_call(
        functools.partial(_prep_kernel, seq_rows=t, t_valid=t_valid, augment=augment),
        out_shape=tuple(out_shape),
        grid=(grid_rows, nc),
        in_specs=[pl.BlockSpec((tc, 512), lambda b, c: (b * nc + c, COL_FQ)),
                  pl.BlockSpec((tc, 512), lambda b, c: (b * nc + c, COL_FK)),
                  pl.BlockSpec((tc, 512), lambda b, c: (b * nc + c, COL_FV)),
                  pl.BlockSpec((tc, LANES), lambda b, c: (b * nc + c, COL_SMALL * 4)),
                  pl.BlockSpec((1, 512), const), pl.BlockSpec((1, 512), const),
                  pl.BlockSpec((1, LANES), const),
                  pl.BlockSpec((512, 512), const), pl.BlockSpec((tp, tp), const),
                  pl.BlockSpec((LANES, 2 * nx), const), pl.BlockSpec((1, 2 * nx), const)],
        out_specs=tuple(out_specs),
        scratch_shapes=[pltpu.VMEM((1, LANES), F32)],
        compiler_params=_cparams("parallel", "arbitrary"),
        name="gate_prep",
    )(z, z, z, z, gq, gk, bias_row, gmat, tril, sel, aug_const)


S5_SCAN_LANES = 512
S5_SHORT_SEQS = 16


def _cmul(ar, ai, br, bi):
    return ar * br - ai * bi, ar * bi + ai * br


def _s5_kernel(u_ref, h0r_ref, h0i_ref, perm_ref, permt_ref, bre_ref, bim_ref, lam_ref, pseg_ref, pk_ref,
               cre_ref, cim_ref, d_ref, wglu_ref, bglu_ref, y_ref, hlr_ref, hli_ref, hr_ref, hi_ref, cr_ref, ci_ref,
               *, t_last, seqs):
    c = pl.program_id(1)
    nc = pl.num_programs(1)
    tc = u_ref.shape[0]
    R = tc // SUBLANES if seqs == 1 else 1
    half = S5_WIDTH // 2
    hl = S5_LANES // 2

    if seqs == 1:
        @pl.when(c == 0)
        def _():
            cr_ref[...] = h0r_ref[0]
            ci_ref[...] = h0i_ref[0]

    u = u_ref[...]
    ub = _pad_rows(u, MIN_BF16_ROWS).astype(BF16)
    if R > 1:
        ub = _dot(perm_ref[...], ub).astype(BF16)
    for j in range(2):
        uj = ub[:, j * half:(j + 1) * half]
        hr_ref[:, j * hl:(j + 1) * hl] = _dot(uj, bre_ref[j])[0:tc]
        hi_ref[:, j * hl:(j + 1) * hl] = _dot(uj, bim_ref[j])[0:tc]

    sub = lax.broadcasted_iota(jnp.int32, (SUBLANES, S5_SCAN_LANES), 0)
    for lc in range(S5_LANES // S5_SCAN_LANES):
        ls = slice(lc * S5_SCAN_LANES, (lc + 1) * S5_SCAN_LANES)
        lam_r = jnp.broadcast_to(lam_ref[0:1, ls], sub.shape)
        lam_i = jnp.broadcast_to(lam_ref[1:2, ls], sub.shape)
        if seqs > 1:
            def seq_scan(g, _):
                rows = pl.ds(pl.multiple_of(g * SUBLANES, SUBLANES), SUBLANES)
                s_r = jnp.where(sub == 0, jnp.broadcast_to(h0r_ref[0, pl.ds(g, 1), ls], sub.shape), 0.0)
                s_i = jnp.where(sub == 0, jnp.broadcast_to(h0i_ref[0, pl.ds(g, 1), ls], sub.shape), 0.0)
                m_r, m_i = _cmul(lam_r, lam_i, s_r, s_i)
                x_r = hr_ref[rows, ls] + m_r
                x_i = hi_ref[rows, ls] + m_i
                for j, lag in enumerate((1, 2, 4)):
                    m_r, m_i = _cmul(pseg_ref[2 * j, :, ls], pseg_ref[2 * j + 1, :, ls],
                                     pltpu.roll(x_r, lag, 0), pltpu.roll(x_i, lag, 0))
                    x_r, x_i = x_r + m_r, x_i + m_i
                hr_ref[rows, ls] = x_r
                hi_ref[rows, ls] = x_i
                hlr_ref[0, pl.ds(g, 1), ls] = x_r[t_last:t_last + 1, :]
                hli_ref[0, pl.ds(g, 1), ls] = x_i[t_last:t_last + 1, :]
                return 0

            lax.fori_loop(0, seqs, seq_scan, 0, unroll=2)
            continue
        init_r = jnp.where(sub == 0, jnp.broadcast_to(cr_ref[:, ls], sub.shape), 0.0)
        init_i = jnp.where(sub == 0, jnp.broadcast_to(ci_ref[:, ls], sub.shape), 0.0)

        def local_step(k, carry):
            h_r, h_i = carry
            rows = pl.ds(pl.multiple_of(k * SUBLANES, SUBLANES), SUBLANES)
            m_r, m_i = _cmul(lam_r, lam_i, h_r, h_i)
            h_r = m_r + hr_ref[rows, ls]
            h_i = m_i + hi_ref[rows, ls]
            hr_ref[rows, ls] = h_r
            hi_ref[rows, ls] = h_i
            return h_r, h_i

        e_r, e_i = lax.fori_loop(0, R, local_step, (init_r, init_i), unroll=min(R, 8))
        for j, lag in enumerate((1, 2, 4)):
            m_r, m_i = _cmul(pseg_ref[2 * j, :, ls], pseg_ref[2 * j + 1, :, ls],
                             pltpu.roll(e_r, lag, 0), pltpu.roll(e_i, lag, 0))
            e_r, e_i = e_r + m_r, e_i + m_i
        cr_ref[:, ls] = e_r[SUBLANES - 1:SUBLANES, :]
        ci_ref[:, ls] = e_i[SUBLANES - 1:SUBLANES, :]
        in_r = jnp.where(sub == 0, 0.0, pltpu.roll(e_r, 1, 0))
        in_i = jnp.where(sub == 0, 0.0, pltpu.roll(e_i, 1, 0))

        def fix_step(k, _):
            rows = pl.ds(pl.multiple_of(k * SUBLANES, SUBLANES), SUBLANES)
            p_r = jnp.broadcast_to(pk_ref[0, pl.ds(k, 1), ls], sub.shape)
            p_i = jnp.broadcast_to(pk_ref[1, pl.ds(k, 1), ls], sub.shape)
            m_r, m_i = _cmul(p_r, p_i, in_r, in_i)
            hr_ref[rows, ls] += m_r
            hi_ref[rows, ls] += m_i
            return 0

        lax.fori_loop(0, R, fix_step, 0, unroll=min(R, 8))

    if seqs == 1:
        @pl.when(c == nc - 1)
        def _():
            tl = t_last % tc
            pos = (tl % R) * SUBLANES + tl // R
            hlr_ref[0] = hr_ref[pos:pos + 1, :]
            hli_ref[0] = hi_ref[pos:pos + 1, :]

    hrb = _pad_rows(hr_ref[...], MIN_BF16_ROWS).astype(BF16)
    hib = _pad_rows(hi_ref[...], MIN_BF16_ROWS).astype(BF16)
    ys = [(_dot(hrb[:, j * hl:(j + 1) * hl], cre_ref[j]) + _dot(hib[:, j * hl:(j + 1) * hl], cim_ref[j]))
          for j in range(2)]
    y = jnp.concatenate(ys, axis=1)
    if R > 1:
        y = _dot_exact_lhs(permt_ref[...], y)
    y = _gelu_tanh(y[0:tc] + d_ref[...] * u)
    gate = _dot(_pad_rows(y, MIN_BF16_ROWS).astype(BF16), wglu_ref[...])[0:tc]
    y_ref[...] = y * _sigmoid(gate + bglu_ref[...])


def s5_mixer(z, h0r, h0i, sp, nb, t, tc, t_last, seqs=1):
    assert seqs == 1 or (tc == SUBLANES and t == tc)
    nc = t // tc
    R = tc // SUBLANES
    tc = tc * seqs
    nbg = nb // seqs
    tperm = max(tc, MIN_BF16_ROWS) if seqs == 1 else MIN_BF16_ROWS
    perm = np.zeros((tperm, tperm), np.float32)
    if seqs == 1:
        for s in range(SUBLANES):
            for k in range(R):
                perm[k * SUBLANES + s, s * R + k] = 1.0
    permt = jnp.asarray(perm.T, BF16)
    perm = jnp.asarray(perm, BF16)
    const = lambda b, c: (0, 0)
    const3 = lambda b, c: (0, 0, 0)
    state_spec = pl.BlockSpec((1, seqs, S5_LANES), lambda b, c: (b, 0, 0))
    tabs = sp["tabs"][R]
    return pl.pallas_call(
        functools.partial(_s5_kernel, t_last=t_last, seqs=seqs),
        out_shape=(jax.ShapeDtypeStruct((nb * t, S5_WIDTH), F32),
                   jax.ShapeDtypeStruct((nbg, seqs, S5_LANES), F32),
                   jax.ShapeDtypeStruct((nbg, seqs, S5_LANES), F32)),
        grid=(nbg, nc),
        in_specs=[pl.BlockSpec((tc, 512), lambda b, c: (b * nc + c, COL_S5)),
                  state_spec, state_spec,
                  pl.BlockSpec((tperm, tperm), const), pl.BlockSpec((tperm, tperm), const),
                  pl.BlockSpec((2, S5_WIDTH // 2, S5_LANES // 2), const3),
                  pl.BlockSpec((2, S5_WIDTH // 2, S5_LANES // 2), const3),
                  pl.BlockSpec((2, S5_LANES), const),
                  pl.BlockSpec((6, SUBLANES, S5_LANES), const3),
                  pl.BlockSpec((2, R, S5_LANES), const3),
                  pl.BlockSpec((2, S5_LANES // 2, S5_WIDTH // 2), const3),
                  pl.BlockSpec((2, S5_LANES // 2, S5_WIDTH // 2), const3),
                  pl.BlockSpec((1, S5_WIDTH), const),
                  pl.BlockSpec((S5_WIDTH, S5_WIDTH), const), pl.BlockSpec((1, S5_WIDTH), const)],
        out_specs=(pl.BlockSpec((tc, S5_WIDTH), lambda b, c: (b * nc + c, 0)), state_spec, state_spec),
        scratch_shapes=[pltpu.VMEM((tc, S5_LANES), F32), pltpu.VMEM((tc, S5_LANES), F32),
                        pltpu.VMEM((1, S5_LANES), F32), pltpu.VMEM((1, S5_LANES), F32)],
        compiler_params=_cparams("parallel", "arbitrary"),
        name="s5_mixer",
    )(z, h0r.reshape(nbg, seqs, S5_LANES), h0i.reshape(nbg, seqs, S5_LANES), perm, permt, sp["bre"], sp["bim"],
      tabs["lam"],
      tabs["pseg"], tabs["pk"], sp["cre"], sp["cim"], sp["d"], sp["wglu"], sp["bglu"])


def s5_params(a_re, a_im, log_step, b_re, b_im, c_re, c_im, d_skip, w_glu, b_glu, seg_lens):
    dt = jnp.exp(log_step)[:, None]
    mag = jnp.exp(a_re * dt)
    lr = (mag * jnp.cos(a_im * dt)).reshape(1, S5_LANES)
    li = (mag * jnp.sin(a_im * dt)).reshape(1, S5_LANES)
    den = a_re * a_re + a_im * a_im
    xr, xi = lr.reshape(a_re.shape) - 1.0, li.reshape(a_re.shape)
    fr = (xr * a_re + xi * a_im) / den
    fi = (xi * a_re - xr * a_im) / den
    bbr = fr[..., None] * b_re - fi[..., None] * b_im
    bbi = fr[..., None] * b_im + fi[..., None] * b_re
    gh = S5_GROUPS // 2
    eye = jnp.eye(gh, dtype=F32)

    def in_mat(b):
        return jnp.einsum('jgpc,gh->jgchp', b.reshape(2, gh, S5_STATE, S5_GROUP), eye).reshape(
            2, S5_WIDTH // 2, S5_LANES // 2).astype(BF16)

    def out_mat(cm):
        return jnp.einsum('jgcp,gh->jgphc', cm.reshape(2, gh, S5_GROUP, S5_STATE), eye).reshape(
            2, S5_LANES // 2, S5_WIDTH // 2).astype(BF16)

    def powers(pr, pi, n):
        tr, ti, cnt = pr, pi, 1
        while cnt < n:
            lr_, li_ = tr[cnt - 1:cnt], ti[cnt - 1:cnt]
            nr, ni = _cmul(tr, ti, lr_, li_)
            tr, ti, cnt = jnp.concatenate([tr, nr], axis=0), jnp.concatenate([ti, ni], axis=0), 2 * cnt
        return tr, ti

    sub = jnp.arange(SUBLANES)[:, None]
    tabs = {}
    for R in seg_lens:
        kr, ki = powers(lr, li, R)
        sr, si = powers(kr[R - 1:R], ki[R - 1:R], 4)
        pseg = []
        for lag in (1, 2, 4):
            msk = (sub >= lag).astype(F32)
            pseg += [msk * sr[lag - 1:lag], msk * si[lag - 1:lag]]
        tabs[R] = dict(lam=jnp.concatenate([lr, li], axis=0), pseg=jnp.stack(pseg), pk=jnp.stack([kr, ki]))
    return dict(bre=in_mat(bbr), bim=in_mat(bbi), cre=out_mat(c_re), cim=out_mat(-c_im), tabs=tabs,
                d=d_skip.reshape(1, S5_WIDTH), wglu=w_glu.astype(BF16), bglu=b_glu.reshape(1, S5_WIDTH))


def _fox_flash_kernel(it_ref, jt_ref, qx_ref, kx_ref, v_ref, o_ref, m_ref, l_ref, acc_ref, s_ref, p_ref):
    i = it_ref[pl.program_id(1)]
    j = jt_ref[pl.program_id(1)]
    tq = qx_ref.shape[0]
    tk = kx_ref.shape[0]

    @pl.when(j == 0)
    def _():
        m_ref[...] = jnp.full_like(m_ref, -jnp.inf)
        l_ref[...] = jnp.zeros_like(l_ref)
        acc_ref[...] = jnp.zeros_like(acc_ref)

    def step(masked):
        nr, ncol = tq // FLASH_ROWS, tk // LANES
        if masked:
            diff = (lax.broadcasted_iota(jnp.int32, (FLASH_ROWS, LANES), 0) -
                    lax.broadcasted_iota(jnp.int32, (FLASH_ROWS, LANES), 1))
        for h in range(FOX_HEADS):
            hs = slice(h * LANES, (h + 1) * LANES)
            ps = slice((h // 2) * LANES, (h // 2 + 1) * LANES)
            s_buf, p_buf = s_ref.at[h % 2], p_ref.at[h % 2]
            if h == 0:
                s_buf[...] = _dot_nt(qx_ref[:, hs], kx_ref[:, hs])
            if h + 1 < FOX_HEADS:
                nhs = slice((h + 1) * LANES, (h + 2) * LANES)
                s_ref[(h + 1) % 2] = _dot_nt(qx_ref[:, nhs], kx_ref[:, nhs])
            for r in range(nr):
                rs = slice(r * FLASH_ROWS, (r + 1) * FLASH_ROWS)

                live = [cidx for cidx in range(ncol) if not (masked and cidx * LANES >= (r + 1) * FLASH_ROWS)]

                def piece(cidx):
                    sc = s_buf[rs, cidx * LANES:(cidx + 1) * LANES]
                    if masked and (cidx + 1) * LANES - 1 > r * FLASH_ROWS:
                        sc = jnp.where(diff >= (cidx * LANES - r * FLASH_ROWS), sc, -jnp.inf)
                    return sc

                mx = piece(live[0])
                for cidx in live[1:]:
                    mx = jnp.maximum(mx, piece(cidx))
                m_prev = m_ref[h, rs, :]
                m_new = jnp.maximum(m_prev, jnp.max(mx, axis=-1, keepdims=True))
                alpha = jnp.exp2(m_prev - m_new)
                lsum = jnp.zeros((FLASH_ROWS, LANES), F32)
                for cidx in range(ncol):
                    cs_ = slice(cidx * LANES, (cidx + 1) * LANES)
                    if cidx not in live:
                        p_buf[rs, cs_] = jnp.zeros((FLASH_ROWS, LANES), BF16)
                        continue
                    pc = jnp.exp2(piece(cidx) - m_new)
                    lsum = lsum + pc
                    p_buf[rs, cs_] = pc.astype(BF16)
                l_ref[h, rs, :] = alpha * l_ref[h, rs, :] + lsum
                acc_ref[h, rs, :] = alpha * acc_ref[h, rs, :]
                m_ref[h, rs, :] = m_new
            acc_ref[h] += _dot(p_buf[...], v_ref[:, ps])

    @pl.when(j < i)
    def _():
        step(False)

    @pl.when(j == i)
    def _():
        step(True)
        lane = lax.broadcasted_iota(jnp.int32, (1, LANES), 1)
        for p in range(FOX_HEADS // 2):
            lo = acc_ref[2 * p] / jnp.sum(l_ref[2 * p], axis=-1, keepdims=True)
            hi = acc_ref[2 * p + 1] / jnp.sum(l_ref[2 * p + 1], axis=-1, keepdims=True)
            o_ref[:, p * LANES:(p + 1) * LANES] = jnp.where(lane < FOX_HD, lo, hi)


def fox_flash(qx, kx, vb, nb, t, tile):
    assert tile % FLASH_ROWS == 0
    nt = t // tile
    pairs = [(i, j) for i in range(nt) for j in range(i + 1)]
    it = jnp.asarray([p[0] for p in pairs], jnp.int32)
    jt = jnp.asarray([p[1] for p in pairs], jnp.int32)
    return pl.pallas_call(
        _fox_flash_kernel,
        out_shape=jax.ShapeDtypeStruct((nb * t, FOX_WIDTH), F32),
        grid_spec=pltpu.PrefetchScalarGridSpec(
            num_scalar_prefetch=2,
            grid=(nb, len(pairs)),
            in_specs=[pl.BlockSpec((tile, FOX_HEADS * LANES), lambda b, p, it, jt: (b * nt + it[p], 0)),
                      pl.BlockSpec((tile, FOX_HEADS * LANES), lambda b, p, it, jt: (b * nt + jt[p], 0)),
                      pl.BlockSpec((tile, FOX_WIDTH), lambda b, p, it, jt: (b * nt + jt[p], 0))],
            out_specs=pl.BlockSpec((tile, FOX_WIDTH), lambda b, p, it, jt: (b * nt + it[p], 0)),
            scratch_shapes=[pltpu.VMEM((FOX_HEADS, tile, LANES), F32), pltpu.VMEM((FOX_HEADS, tile, LANES), F32),
                            pltpu.VMEM((FOX_HEADS, tile, LANES), F32),
                            pltpu.VMEM((2, tile, tile), F32), pltpu.VMEM((2, tile, tile), BF16)]),
        compiler_params=_cparams("parallel", "arbitrary"),
        name="fox_flash",
    )(it, jt, qx, kx, vb)


def _fox_decode_kernel(pt_ref, qb_ref, kn_ref, vn_ref, cs_ref, *rest, n_pages):
    k_refs = rest[0:n_pages]
    v_refs = rest[n_pages:2 * n_pages]
    f_refs = rest[2 * n_pages:3 * n_pages]
    triu_ref, o_ref, kpad_ref, vpad_ref = rest[3 * n_pages:]
    page = kpad_ref.shape[0]
    nrow = FOX_HEADS * SAMPLE_T
    qb = qb_ref[...].astype(BF16)

    def per_query(x):
        return jnp.concatenate([x] * SAMPLE_T, axis=0)

    fcat = jnp.concatenate([f_refs[i][0, 0] for i in range(n_pages)], axis=0)
    cum_in = _dot_exact_rhs(fcat, triu_ref[...])
    totals = jnp.broadcast_to(cum_in[:, page - 1:page], cum_in.shape)
    off = jnp.zeros((FOX_HEADS, page), F32)
    ss = []
    for i in range(n_pages):
        rows = slice(i * FOX_HEADS, (i + 1) * FOX_HEADS)
        kt = k_refs[i][0, 0].reshape(FOX_WIDTH, page).astype(BF16)
        ss.append(_dot(qb, kt) - per_query(cum_in[rows] + off))
        off = off + totals[rows]
    kpad_ref[...] = jnp.zeros_like(kpad_ref)
    vpad_ref[...] = jnp.zeros_like(vpad_ref)
    kpad_ref[0:SAMPLE_T, :] = kn_ref[...]
    vpad_ref[0:SAMPLE_T, :] = vn_ref[...]
    csrow = _pad_rows(cs_ref[...], page).T[LANE_FOXF:LANE_FOXF + FOX_HEADS]
    s_new = _dot_nt(qb, kpad_ref[...].astype(BF16)) - per_query(csrow + off)
    tq_idx = lax.broadcasted_iota(jnp.int32, (nrow, page), 0) // FOX_HEADS
    tk_idx = lax.broadcasted_iota(jnp.int32, (nrow, page), 1)
    ss.append(jnp.where(tk_idx <= tq_idx, s_new, -jnp.inf))

    m = ss[0]
    for s in ss[1:]:
        m = jnp.maximum(m, s)
    m = jnp.max(m, axis=-1, keepdims=True)
    lsum = jnp.zeros((nrow, page), F32)
    acc = jnp.zeros((nrow, FOX_WIDTH), F32)
    for i, s in enumerate(ss):
        p = jnp.exp(s - m)
        lsum = lsum + p
        if i < n_pages:
            acc = acc + _dot_nt(p.astype(BF16), v_refs[i][0, 0].reshape(FOX_WIDTH, page).astype(BF16))
        else:
            acc = acc + _dot(p.astype(BF16), vpad_ref[...].astype(BF16))
    acc = acc / jnp.sum(lsum, axis=-1, keepdims=True)
    lane = lax.broadcasted_iota(jnp.int32, (nrow, FOX_WIDTH), 1) // FOX_HD
    head = lax.broadcasted_iota(jnp.int32, (nrow, FOX_WIDTH), 0) % FOX_HEADS
    picked = jnp.where(lane == head, acc, 0.0)
    o_ref[...] = jnp.sum(picked.reshape(SAMPLE_T, FOX_HEADS, FOX_WIDTH), axis=1)


def fox_decode(page_table, qx, kn, z, cs, pool_kt, pool_vt, pool_ft, layer):
    nb, n_pages = page_table.shape
    page = pool_kt.shape[-1]
    assert page == LANES
    nrow = FOX_HEADS * SAMPLE_T
    pt = page_table.reshape(-1)
    triu = jnp.asarray(np.triu(np.ones((page, page))), BF16)
    pair_of_head = jnp.asarray(np.arange(FOX_HEADS)[:, None] // 2 == np.arange(FOX_HEADS // 2)[None, :], F32)
    qb = (qx.reshape(nb, SAMPLE_T, FOX_HEADS, 1, LANES) * pair_of_head[None, None, :, :, None]).reshape(
        nb * nrow, FOX_WIDTH)

    def pg5(i):
        return lambda b, pt: (layer, pt[b * n_pages + i], 0, 0, 0)

    def pg4(i):
        return lambda b, pt: (layer, pt[b * n_pages + i], 0, 0)

    row = lambda b, pt: (b, 0)
    in_specs = [pl.BlockSpec((nrow, FOX_WIDTH), row),
                pl.BlockSpec((SAMPLE_T, FOX_WIDTH), row),
                pl.BlockSpec((SAMPLE_T, FOX_WIDTH), lambda b, pt: (b, COL_FV)),
                pl.BlockSpec((SAMPLE_T, LANES), row)]
    in_specs += [pl.BlockSpec((1, 1, FOX_HEADS, FOX_HD, page), pg5(i)) for i in range(n_pages)]
    in_specs += [pl.BlockSpec((1, 1, FOX_HEADS, FOX_HD, page), pg5(i)) for i in range(n_pages)]
    in_specs += [pl.BlockSpec((1, 1, FOX_HEADS, page), pg4(i)) for i in range(n_pages)]
    in_specs += [pl.BlockSpec((page, page), lambda b, pt: (0, 0))]
    return pl.pallas_call(
        functools.partial(_fox_decode_kernel, n_pages=n_pages),
        out_shape=jax.ShapeDtypeStruct((nb * SAMPLE_T, FOX_WIDTH), F32),
        grid_spec=pltpu.PrefetchScalarGridSpec(
            num_scalar_prefetch=1,
            grid=(nb,),
            in_specs=in_specs,
            out_specs=pl.BlockSpec((SAMPLE_T, FOX_WIDTH), row),
            scratch_shapes=[pltpu.VMEM((page, FOX_WIDTH), F32), pltpu.VMEM((page, FOX_WIDTH), F32)]),
        compiler_params=_cparams("parallel"),
        name="fox_decode",
    )(pt, qb, kn, z, cs, *([pool_kt] * n_pages), *([pool_vt] * n_pages), *([pool_ft] * n_pages), triu)


def _mlstm_kernel(q_ref, k_ref, v_ref, o_ref, g_ref, cs_ref, *rest, short):
    if short:
        c0_ref, n0_ref, m0_ref, gn_ref, y_ref, cout_ref, nout_ref, mout_ref, c_ref, n_ref, m_ref = rest
    else:
        (grow_ref, csrow_ref, c0_ref, n0_ref, m0_ref, gn_ref, y_ref, cout_ref, nout_ref, mout_ref,
         c_ref, n_ref, m_ref) = rest
    c = pl.program_id(1)
    nc = pl.num_programs(1)
    L = q_ref.shape[0]

    @pl.when(c == 0)
    def _():
        c_ref[...] = c0_ref[0, 0]
        n_ref[...] = n0_ref[0]
        m_ref[...] = m0_ref[0]

    Lk = max(L, LANES)
    Lq = max(L, MIN_BF16_ROWS)
    g = g_ref[...]
    cs = cs_ref[...]
    if short:
        grow = _pad_rows(g, Lk).T
        csrow = _pad_rows(cs, Lk).T
    else:
        grow = grow_ref[...]
        csrow = csrow_ref[...]
    causal = (lax.broadcasted_iota(jnp.int32, (Lq, Lk), 0) >= lax.broadcasted_iota(jnp.int32, (Lq, Lk), 1))
    scale = ML_HD ** -0.5
    heads = range(ML_HEADS)
    hsl = [slice(h * ML_HD, (h + 1) * ML_HD) for h in heads]
    qb = [_pad_rows(q_ref[:, hsl[h]], Lq).astype(BF16) for h in heads]
    kb = [_pad_rows(k_ref[:, hsl[h]], Lk).astype(BF16) for h in heads]
    vb = [_pad_rows(v_ref[:, hsl[h]], Lk).astype(BF16) for h in heads]
    bcol_k = [_pad_rows(cs[:, LANE_MLF + h:LANE_MLF + h + 1], Lk) for h in heads]
    bcol = [b[0:Lq] for b in bcol_k]
    icol = [_pad_rows(g[:, LANE_MLI + h:LANE_MLI + h + 1], Lk, -jnp.inf) for h in heads]
    log_d = [jnp.where(causal, bcol[h] - csrow[LANE_MLF + h:LANE_MLF + h + 1, :]
                       + grow[LANE_MLI + h:LANE_MLI + h + 1, :], -jnp.inf) for h in heads]
    log_inter = [bcol[h] + m_ref[0:1, h:h + 1] for h in heads]
    m_t = [jnp.maximum(log_inter[h], jnp.max(log_d[h], axis=-1, keepdims=True)) for h in heads]
    inter_w = [jnp.exp(log_inter[h] - m_t[h]) for h in heads]
    qk = [_dot_nt(qb[h], kb[h]) for h in heads]
    ch = [c_ref[h] for h in heads]
    n_row = [n_ref[h:h + 1, :] for h in heads]
    qc = [_dot_nt(qb[h], ch[h].astype(BF16)) for h in heads]
    s = [qk[h] * (jnp.exp(log_d[h] - m_t[h]) * scale) for h in heads]
    sv = [_dot(s[h].astype(BF16), vb[h]) for h in heads]
    for h in heads:
        num = sv[h] + inter_w[h] * qc[h]
        den = (jnp.sum(s[h], axis=-1, keepdims=True)
               + inter_w[h] * jnp.sum(qb[h].astype(F32) * n_row[h], axis=-1, keepdims=True))
        hh = num / jnp.maximum(jnp.abs(den), jnp.exp(-m_t[h]))
        y_ref[:, hsl[h]] = _rms(hh[0:L], gn_ref[...]) * _sigmoid(o_ref[:, hsl[h]])
    m_end = [m_t[h][L - 1:L, :] for h in heads]
    a_end = [inter_w[h][L - 1:L, :] for h in heads]
    w_col = [jnp.exp(bcol[h][L - 1:L, :] - bcol_k[h] + icol[h] - m_end[h]) * scale for h in heads]
    upd = [_dot_tn((vb[h].astype(F32) * w_col[h]).astype(BF16), kb[h]) for h in heads]
    for h in heads:
        c_ref[h] = a_end[h] * ch[h] + upd[h]
        n_ref[h:h + 1, :] = a_end[h] * n_row[h] + jnp.sum(kb[h].astype(F32) * w_col[h], axis=0, keepdims=True)
        m_ref[0:1, h:h + 1] = m_end[h]

    @pl.when(c == nc - 1)
    def _():
        cout_ref[0] = c_ref[...]
        nout_ref[0] = n_ref[...]
        mout_ref[0] = m_ref[...]


def mlstm(za, zb, g, cs, grow, csrow, c0, n0, m0, gn, nb, t, L, c_layer):
    nc = t // L
    rows = lambda b, c: (b * nc + c, 0)
    rr = 2 * SUBLANES
    m0p = jnp.zeros((nb, 1, LANES), F32).at[:, 0, :ML_HEADS].set(m0)
    short = grow is None
    row_specs = [] if short else [pl.BlockSpec((rr, L), lambda b, c: (b * (LANES // rr), c))] * 2
    row_args = [] if short else [grow, csrow]
    outs = pl.pallas_call(
        functools.partial(_mlstm_kernel, short=short),
        out_shape=(jax.ShapeDtypeStruct((nb * t, ML_WIDTH), F32),
                   jax.ShapeDtypeStruct((nb, ML_HEADS, ML_HD, ML_HD), F32),
                   jax.ShapeDtypeStruct((nb, ML_HEADS, ML_HD), F32),
                   jax.ShapeDtypeStruct((nb, 1, LANES), F32)),
        grid=(nb, nc),
        in_specs=[pl.BlockSpec((L, 512), lambda b, c: (b * nc + c, COL_MQ)),
                  pl.BlockSpec((L, 512), lambda b, c: (b * nc + c, COL_MK)),
                  pl.BlockSpec((L, 512), lambda b, c: (b * nc + c, COL_MV)),
                  pl.BlockSpec((L, 512), lambda b, c: (b * nc + c, COL_MO)),
                  pl.BlockSpec((L, LANES), rows), pl.BlockSpec((L, LANES), rows), *row_specs,
                  pl.BlockSpec((1, 1, ML_HEADS, ML_HD, ML_HD), lambda b, c: (c_layer, b, 0, 0, 0)),
                  pl.BlockSpec((1, ML_HEADS, ML_HD), lambda b, c: (b, 0, 0)),
                  pl.BlockSpec((1, 1, LANES), lambda b, c: (b, 0, 0)),
                  pl.BlockSpec((1, ML_HD), lambda b, c: (0, 0))],
        out_specs=(pl.BlockSpec((L, ML_WIDTH), rows),
                   pl.BlockSpec((1, ML_HEADS, ML_HD, ML_HD), lambda b, c: (b, 0, 0, 0)),
                   pl.BlockSpec((1, ML_HEADS, ML_HD), lambda b, c: (b, 0, 0)),
                   pl.BlockSpec((1, 1, LANES), lambda b, c: (b, 0, 0))),
        scratch_shapes=[pltpu.VMEM((ML_HEADS, ML_HD, ML_HD), F32), pltpu.VMEM((ML_HEADS, ML_HD), F32),
                        pltpu.VMEM((1, LANES), F32)],
        compiler_params=_cparams("parallel", "arbitrary"),
        name="mlstm",
    )(zb, zb, zb, za, g, cs, *row_args, c0, n0, m0p, gn.reshape(1, ML_HD))
    y, c_new, n_new, m_new = outs
    return y, c_new, n_new, m_new[:, 0, :ML_HEADS]


def _merge_kernel(x_ref, g_ref, wg_ref, ys_ref, yf_ref, ym_ref, ws_ref, wf_ref, wm_ref, wo_ref, o_ref):
    x = x_ref[...]
    hn = _rms(x, g_ref[...]).astype(BF16)
    branches = (ys_ref, ws_ref), (yf_ref, wf_ref), (ym_ref, wm_ref)
    merged = None
    for b, (y_ref, w_ref) in enumerate(branches):
        gate = _sigmoid(_dot(hn, wg_ref[:, b * D_MODEL:(b + 1) * D_MODEL]))
        term = gate * _dot(y_ref[...].astype(BF16), w_ref[...])
        merged = term if merged is None else merged + term
    o_ref[...] = x + _dot(merged.astype(BF16), wo_ref[...])


def merge_out(x, g_mix, w_gates, ys, yf, ym, ws, wf, wm, wo, tm=256):
    m = x.shape[0]
    tm = min(tm, m)
    row = lambda i: (i, 0)
    const = lambda i: (0, 0)
    return pl.pallas_call(
        _merge_kernel,
        out_shape=jax.ShapeDtypeStruct((m, D_MODEL), F32),
        grid=(m // tm,),
        in_specs=[pl.BlockSpec((tm, D_MODEL), row), pl.BlockSpec((1, D_MODEL), const),
                  pl.BlockSpec((D_MODEL, 3 * D_MODEL), const),
                  pl.BlockSpec((tm, 512), row), pl.BlockSpec((tm, 512), row), pl.BlockSpec((tm, 512), row),
                  pl.BlockSpec((512, D_MODEL), const), pl.BlockSpec((512, D_MODEL), const),
                  pl.BlockSpec((512, D_MODEL), const), pl.BlockSpec((D_MODEL, D_MODEL), const)],
        out_specs=pl.BlockSpec((tm, D_MODEL), row),
        compiler_params=_cparams("parallel"),
        name="merge_out",
    )(x, g_mix.reshape(1, D_MODEL), w_gates, ys, yf, ym, ws, wf, wm, wo)


def _cross_block_kernel(x_ref, g_ref, wq_ref, gq_ref, k_ref, v_ref, wo_ref, o_ref):
    x = x_ref[...]
    q = _dot(_rms(x, g_ref[...]).astype(BF16), wq_ref[...])
    heads = range(MEM_HEADS)
    hsl = [slice(h * MEM_HD, (h + 1) * MEM_HD) for h in heads]
    qh = [_rms(q[:, hsl[h]], gq_ref[...]).astype(BF16) for h in heads]
    s = [_dot_nt(qh[h], k_ref[:, hsl[h]].astype(BF16)) * (MEM_HD ** -0.5) for h in heads]
    m = [jnp.max(s[h], axis=-1, keepdims=True) for h in heads]
    p = [jnp.exp(s[h] - m[h]) for h in heads]
    l = [jnp.sum(p[h], axis=-1, keepdims=True) for h in heads]
    pv = [_dot(p[h].astype(BF16), v_ref[:, hsl[h]].astype(BF16)) for h in heads]
    outs = [(pv[h] / l[h]).astype(BF16) for h in heads]
    o_ref[...] = x + _dot(jnp.concatenate(outs, axis=1), wo_ref[...])


def cross_block(x, g, wq, gq, mem_k, mem_v, wo, nb, t, tq):
    nq = t // tq
    n_mem = mem_k.shape[0] // nb
    const = lambda b, i: (0, 0)
    row = lambda b, i: (b * nq + i, 0)
    return pl.pallas_call(
        _cross_block_kernel,
        out_shape=jax.ShapeDtypeStruct((nb * t, D_MODEL), F32),
        grid=(nb, nq),
        in_specs=[pl.BlockSpec((tq, D_MODEL), row), pl.BlockSpec((1, D_MODEL), const),
                  pl.BlockSpec((D_MODEL, MEM_WIDTH), const), pl.BlockSpec((1, MEM_HD), const),
                  pl.BlockSpec((n_mem, MEM_WIDTH), lambda b, i: (b, 0)),
                  pl.BlockSpec((n_mem, MEM_WIDTH), lambda b, i: (b, 0)),
                  pl.BlockSpec((MEM_WIDTH, D_MODEL), const)],
        out_specs=pl.BlockSpec((tq, D_MODEL), row),
        compiler_params=_cparams("parallel", "parallel"),
        name="cross_block",
    )(x, g.reshape(1, D_MODEL), wq, gq.reshape(1, MEM_HD), mem_k, mem_v, wo)


def _cross_cached_kernel(q_ref, k_ref, v_ref, o_ref):
    nseq = k_ref.shape[1]
    tq = q_ref.shape[0] // nseq
    seqs = range(nseq)
    n_cols = k_ref.shape[2]
    same_head = (lax.broadcasted_iota(jnp.int32, (MEM_HEADS * tq, n_cols), 0) // tq ==
                 lax.broadcasted_iota(jnp.int32, (MEM_HEADS * tq, n_cols), 1) % MEM_HEADS)
    qr = [jnp.concatenate([q_ref[sq * tq:(sq + 1) * tq, h * MEM_HD:(h + 1) * MEM_HD] for h in range(MEM_HEADS)],
                          axis=0).astype(BF16) for sq in seqs]
    s = [jnp.where(same_head, _dot_nt(qr[sq], k_ref[0, sq].astype(BF16)) * (MEM_HD ** -0.5), -jnp.inf)
         for sq in seqs]
    m = [jnp.max(s[sq], axis=-1, keepdims=True) for sq in seqs]
    p = [jnp.exp(s[sq] - m[sq]) for sq in seqs]
    l = [jnp.sum(p[sq], axis=-1, keepdims=True) for sq in seqs]
    o = [_dot(p[sq].astype(BF16), v_ref[0, sq].astype(BF16)) / l[sq] for sq in seqs]
    for sq in seqs:
        for h in range(MEM_HEADS):
            o_ref[sq * tq:(sq + 1) * tq, h * MEM_HD:(h + 1) * MEM_HD] = o[sq][h * tq:(h + 1) * tq]


CROSS_CACHED_SEQS = 4


def cross_attend_cached(q, mem_k, mem_v, nb, t, layer):
    ns = CROSS_CACHED_SEQS
    kv_spec = pl.BlockSpec((1, ns, mem_k.shape[2], MEM_HD), lambda b: (layer, b, 0, 0))
    return pl.pallas_call(
        _cross_cached_kernel,
        out_shape=jax.ShapeDtypeStruct((nb * t, MEM_WIDTH), F32),
        grid=(nb // ns,),
        in_specs=[pl.BlockSpec((ns * t, MEM_WIDTH), lambda b: (b, 0)), kv_spec, kv_spec],
        out_specs=pl.BlockSpec((ns * t, MEM_WIDTH), lambda b: (b, 0)),
        compiler_params=_cparams("parallel"),
        name="cross_attend_cached",
    )(q, mem_k, mem_v)


def _proj_residual_kernel(x_ref, a_ref, w_ref, o_ref):
    o_ref[...] = x_ref[...] + _dot(a_ref[...].astype(BF16), w_ref[...])


def proj_residual(x, a, w, tm=512):
    m, n = x.shape
    k = a.shape[1]
    tm = min(tm, m)
    return pl.pallas_call(
        _proj_residual_kernel,
        out_shape=jax.ShapeDtypeStruct((m, n), F32),
        grid=(m // tm,),
        in_specs=[pl.BlockSpec((tm, n), lambda i: (i, 0)), pl.BlockSpec((tm, k), lambda i: (i, 0)),
                  pl.BlockSpec((k, n), lambda i: (0, 0))],
        out_specs=pl.BlockSpec((tm, n), lambda i: (i, 0)),
        compiler_params=_cparams("parallel"),
        name="proj_residual",
    )(x, a, w)


def _mlp_kernel(x_ref, g_ref, wu_ref, wd_ref, o_ref, hn_ref, acc_ref):
    f = pl.program_id(1)

    @pl.when(f == 0)
    def _():
        hn_ref[...] = _rms(x_ref[...], g_ref[...]).astype(BF16)
        acc_ref[...] = jnp.zeros_like(acc_ref)

    a = jnp.maximum(_dot(hn_ref[...], wu_ref[...]), 0.0)
    acc_ref[...] += _dot((a * a).astype(BF16), wd_ref[...])

    @pl.when(f == pl.num_programs(1) - 1)
    def _():
        o_ref[...] = x_ref[...] + acc_ref[...]


def mlp(x, g, wu, wd, tm=1024, tf=512):
    m, d = x.shape
    dff = wu.shape[1]
    tm = min(tm, m)
    return pl.pallas_call(
        _mlp_kernel,
        out_shape=jax.ShapeDtypeStruct((m, d), F32),
        grid=(m // tm, dff // tf),
        in_specs=[pl.BlockSpec((tm, d), lambda i, f: (i, 0)), pl.BlockSpec((1, d), lambda i, f: (0, 0)),
                  pl.BlockSpec((d, tf), lambda i, f: (0, f)), pl.BlockSpec((tf, d), lambda i, f: (f, 0))],
        out_specs=pl.BlockSpec((tm, d), lambda i, f: (i, 0)),
        scratch_shapes=[pltpu.VMEM((tm, d), BF16), pltpu.VMEM((tm, d), F32)],
        compiler_params=_cparams("parallel", "arbitrary"),
        name="mlp",
    )(x, g.reshape(1, d), wu, wd)


def _pack_w_in(w_in):
    offs = np.concatenate([[0], np.cumsum(SPLITS)])
    col = lambda i: w_in[:, int(offs[i]):int(offs[i + 1])]
    s5, fq, fk, fv, ff, mq, mk, mv, mi, mf, mo, gates = [col(i) for i in range(12)]
    pad = jnp.zeros((w_in.shape[0], 512 - FOX_HEADS - 2 * ML_HEADS), w_in.dtype)
    w_a = jnp.concatenate([s5, fq, fk, fv, mo, ff, mi, mf, pad], axis=1).astype(BF16)
    w_b = jnp.concatenate([mq, mk, mv], axis=1).astype(BF16)
    return w_a, w_b, gates.astype(BF16)


def _layer_weights(l, g_mix, w_in, s5_a_re, s5_a_im, s5_log_step, s5_b_re, s5_b_im, s5_c_re, s5_c_im, s5_d,
                   s5_w_glu, s5_b_glu, fox_gq, fox_gk, fox_bf, ml_bi, ml_bf, ml_gn, w_br_s5, w_br_fox, w_br_ml,
                   w_out, g_cross, w_cq, cross_gq, g_mem, w_mk, w_mv, cross_gk, w_co, g_mlp, w_up, w_down):
    bias_row = jnp.zeros((1, LANES), F32)
    bias_row = bias_row.at[0, LANE_FOXF:LANE_FOXF + FOX_HEADS].set(fox_bf[l])
    bias_row = bias_row.at[0, LANE_MLI:LANE_MLI + ML_HEADS].set(ml_bi[l])
    bias_row = bias_row.at[0, LANE_MLF:LANE_MLF + ML_HEADS].set(ml_bf[l])
    w_a, w_b, w_gates = _pack_w_in(w_in[l])
    return dict(
        g_mix=g_mix[l], w_a=w_a, w_b=w_b, w_gates=w_gates,
        s5=s5_params(s5_a_re[l], s5_a_im[l], s5_log_step[l], s5_b_re[l], s5_b_im[l], s5_c_re[l], s5_c_im[l],
                     s5_d[l], s5_w_glu[l], s5_b_glu[l], (SEQ_TILE // SUBLANES, SAMPLE_T // SUBLANES)),
        gq=jnp.tile(fox_gq[l], FOX_HEADS).reshape(1, FOX_WIDTH),
        gk=jnp.tile(fox_gk[l], FOX_HEADS).reshape(1, FOX_WIDTH),
        bias_row=bias_row, ml_gn=ml_gn[l],
        w_br_s5=w_br_s5[l].astype(BF16), w_br_fox=w_br_fox[l].astype(BF16), w_br_ml=w_br_ml[l].astype(BF16),
        w_out=w_out[l].astype(BF16), g_cross=g_cross[l], w_cq=w_cq[l].astype(BF16), cross_gq=cross_gq[l],
        g_mem=g_mem[l], w_mk=w_mk[l].astype(BF16), w_mv=w_mv[l].astype(BF16), cross_gk=cross_gk[l],
        w_co=w_co[l].astype(BF16), g_mlp=g_mlp[l], w_up=w_up[l].astype(BF16), w_down=w_down[l].astype(BF16))


def _hybrid_layer(x, W, nb, t, seq_tile, t_valid, s5_state, ml_state, mem_k, mem_v, fox_attend, augment,
                  mem_layer=None):
    z = norm_matmul(x, W["g_mix"], W["w_a"], tm=min(1024, nb * t))
    zb = norm_matmul(x, W["g_mix"], W["w_b"], tm=min(1024, nb * t), tn=W["w_b"].shape[1],
                     out_dtype=BF16 if seq_tile % MIN_BF16_ROWS == 0 else F32)
    prep = gate_prep(z, W["gq"], W["gk"], W["bias_row"], nb, t, seq_tile, t_valid, augment)
    kn, g, cs = prep[1:4]
    grow, csrow = prep[4:6] if augment else (None, None)
    y_s5, s5_re, s5_im = s5_mixer(z, s5_state[0], s5_state[1], W["s5"], nb, t, seq_tile, t_valid - 1,
                                  seqs=S5_SHORT_SEQS if t == SUBLANES else 1)
    y_fox = fox_attend(prep, z)
    y_ml, c_new, n_new, m_new = mlstm(z, zb, g, cs, grow, csrow, ml_state[0], ml_state[1], ml_state[2], W["ml_gn"],
                                      nb, t, seq_tile, 0 if mem_layer is None else mem_layer)
    x = merge_out(x, W["g_mix"], W["w_gates"], y_s5, y_fox, y_ml, W["w_br_s5"], W["w_br_fox"], W["w_br_ml"],
                  W["w_out"])
    if mem_layer is None:
        x = cross_block(x, W["g_cross"], W["w_cq"], W["cross_gq"], mem_k, mem_v, W["w_co"], nb, t, seq_tile)
    else:
        qc = norm_matmul(x, W["g_cross"], W["w_cq"], head_gain=W["cross_gq"])
        oc = cross_attend_cached(qc, mem_k, mem_v, nb, t, mem_layer)
        x = proj_residual(x, oc, W["w_co"])
    x = mlp(x, W["g_mlp"], W["w_up"], W["w_down"])
    if augment:
        fox_k = prep[1].reshape(nb, FOX_HEADS, FOX_HD, t).transpose(0, 3, 1, 2)
        fox_v = prep[8].reshape(nb, FOX_HEADS, FOX_HD, t).transpose(0, 3, 1, 2)
    else:
        fox_k = kn.reshape(nb, t, FOX_HEADS, FOX_HD)
        fox_v = z[:, COL_FV * 512:(COL_FV + 1) * 512].reshape(nb, t, FOX_HEADS, FOX_HD)
    return x, fox_k, fox_v, g, s5_re.reshape(nb, S5_GROUPS, S5_STATE), s5_im.reshape(nb, S5_GROUPS, S5_STATE), \
        c_new, n_new, m_new


def kernel(x_prompt, x_sample, mem_prompt, cache_fox_k, cache_fox_v, cache_fox_logf, page_table, state_s5_re, state_s5_im, state_mlstm_C, state_mlstm_n, state_mlstm_m, cache_mem_k, cache_mem_v, g_mix, w_in, s5_a_re, s5_a_im, s5_log_step, s5_b_re, s5_b_im, s5_c_re, s5_c_im, s5_d, s5_w_glu, s5_b_glu, fox_gq, fox_gk, fox_bf, ml_bi, ml_bf, ml_gn, w_br_s5, w_br_fox, w_br_ml, w_out, g_cross, w_cq, cross_gq, g_mem, w_mk, w_mv, cross_gk, w_co, g_mlp, w_up, w_down):
    depth = w_in.shape[0]
    bp, tp, _ = x_prompt.shape
    bs, ts, _ = x_sample.shape
    n_mem = mem_prompt.shape[1]

    xp = x_prompt.reshape(bp * tp, D_MODEL)
    xs = jnp.pad(x_sample, ((0, 0), (0, SAMPLE_T - ts), (0, 0))).reshape(bs * SAMPLE_T, D_MODEL)
    mem = mem_prompt.reshape(bp * n_mem, D_MODEL)
    zeros_p = (jnp.zeros((bp, S5_LANES), F32), jnp.zeros((bp, S5_LANES), F32))
    zeros_ml = (jnp.zeros((1, bp, ML_HEADS, ML_HD, ML_HD), F32), jnp.zeros((bp, ML_HEADS, ML_HD), F32),
                jnp.zeros((bp, ML_HEADS), F32))
    pool_kt = jnp.transpose(cache_fox_k, (0, 1, 3, 4, 2))
    pool_vt = jnp.transpose(cache_fox_v, (0, 1, 3, 4, 2))
    pool_ft = jnp.transpose(cache_fox_logf, (0, 1, 3, 2))
    st_p, st_s = [], []
    for l in range(depth):
        W = _layer_weights(l, g_mix, w_in, s5_a_re, s5_a_im, s5_log_step, s5_b_re, s5_b_im, s5_c_re, s5_c_im, s5_d,
                           s5_w_glu, s5_b_glu, fox_gq, fox_gk, fox_bf, ml_bi, ml_bf, ml_gn, w_br_s5, w_br_fox,
                           w_br_ml, w_out, g_cross, w_cq, cross_gq, g_mem, w_mk, w_mv, cross_gk, w_co, g_mlp, w_up,
                           w_down)
        mk_p = norm_matmul(mem, W["g_mem"], W["w_mk"], head_gain=W["cross_gk"])
        mv_p = norm_matmul(mem, W["g_mem"], W["w_mv"])

        def flash(prep, z):
            return fox_flash(prep[0], prep[6], prep[7], bp, tp, FLASH_TILE)

        xp, fox_k, fox_v, g, s5r, s5i, c_new, n_new, m_new = _hybrid_layer(
            xp, W, bp, tp, SEQ_TILE, tp, zeros_p, zeros_ml, mk_p, mv_p, flash, True)
        st_p.append((fox_k, fox_v,
                     g[:, LANE_FOXF:LANE_FOXF + FOX_HEADS].reshape(bp, tp, FOX_HEADS),
                     s5r, s5i, c_new, n_new, m_new,
                     mk_p.reshape(bp, n_mem, MEM_HEADS, MEM_HD), mv_p.reshape(bp, n_mem, MEM_HEADS, MEM_HD)))

        def decode(prep, z, layer=l):
            return fox_decode(page_table, prep[0], prep[1], z, prep[3], pool_kt, pool_vt, pool_ft, layer)

        xs, fox_k, fox_v, g, s5r, s5i, c_new, n_new, m_new = _hybrid_layer(
            xs, W, bs, SAMPLE_T, SAMPLE_T, ts,
            (state_s5_re[l].reshape(bs, S5_LANES), state_s5_im[l].reshape(bs, S5_LANES)),
            (state_mlstm_C, state_mlstm_n[l], state_mlstm_m[l]),
            cache_mem_k.reshape(depth, bs, n_mem * MEM_HEADS, MEM_HD),
            cache_mem_v.reshape(depth, bs, n_mem * MEM_HEADS, MEM_HD), decode, False, mem_layer=l)
        st_s.append((fox_k[:, :ts], fox_v[:, :ts],
                     g[:, LANE_FOXF:LANE_FOXF + FOX_HEADS].reshape(bs, SAMPLE_T, FOX_HEADS)[:, :ts],
                     s5r, s5i, c_new, n_new, m_new))
    outs_p = [jnp.stack(a) for a in zip(*st_p)]
    outs_s = [jnp.stack(a) for a in zip(*st_s)]
    yp = xp.reshape(bp, tp, D_MODEL)
    ys = xs.reshape(bs, SAMPLE_T, D_MODEL)[:, :ts]
    return (yp, ys, *outs_p, *outs_s)
```

```python
import functools
import math

import jax
import jax.numpy as jnp
import numpy as np
from jax import lax
from jax.experimental import pallas as pl
from jax.experimental.pallas import tpu as pltpu

F32 = jnp.float32
BF16 = jnp.bfloat16

LANES = 128
SUBLANES = 8
MIN_BF16_ROWS = 16
MXU_DIM = 256
VMEM_LIMIT_BYTES = 48 * 1024 * 1024

D_MODEL = 1024
S5_WIDTH = 512
S5_GROUP = 16
S5_GROUPS = 32
S5_STATE = 64
S5_LANES = S5_GROUPS * S5_STATE
FOX_HEADS = 8
FOX_HD = 64
FOX_WIDTH = 512
ML_HEADS = 4
ML_HD = 128
ML_WIDTH = 512
MEM_HEADS = 4
MEM_HD = 128
MEM_WIDTH = 512
D_FF = 4096
EPS = 1e-6
LOG2E = math.log2(math.e)
SPLITS =(S5_WIDTH, FOX_WIDTH, FOX_WIDTH, FOX_WIDTH, FOX_HEADS, ML_WIDTH, ML_WIDTH, ML_WIDTH,
          ML_HEADS, ML_HEADS, ML_WIDTH, 3 * D_MODEL)

COL_S5, COL_FQ, COL_FK, COL_FV, COL_MO, COL_SMALL = range(6)
COL_MQ, COL_MK, COL_MV = range(3)
LANE_FOXF = 0
LANE_MLI = 8
LANE_MLF = 12
SEQ_TILE = 256
FLASH_TILE = 512
FLASH_ROWS = 128
SAMPLE_T = 8


def _cparams(*sem):
    return pltpu.CompilerParams(dimension_semantics=sem, vmem_limit_bytes=VMEM_LIMIT_BYTES)


def _dot(a, b):
    return jnp.dot(a, b, preferred_element_type=F32)


def _dot_nt(a, b):
    return lax.dot_general(a, b, (((1,), (1,)), ((), ())), preferred_element_type=F32)


def _dot_tn(a, b):
    return lax.dot_general(a, b, (((0,), (0,)), ((), ())), preferred_element_type=F32)


def _split3(x):
    hi = x.astype(BF16)
    r1 = x - hi.astype(F32)
    mid = r1.astype(BF16)
    lo = (r1 - mid.astype(F32)).astype(BF16)
    return hi, mid, lo


def _dot_exact_rhs(x, ones_rhs):
    hi, mid, lo = _split3(x)
    return _dot(hi, ones_rhs) + _dot(mid, ones_rhs) + _dot(lo, ones_rhs)


def _dot_exact_lhs(ones_lhs, x):
    hi, mid, lo = _split3(x)
    return _dot(ones_lhs, hi) + _dot(ones_lhs, mid) + _dot(ones_lhs, lo)


def _pad_rows(x, n, fill=0.0):
    if x.shape[0] >= n:
        return x
    return jnp.concatenate([x, jnp.full((n - x.shape[0], x.shape[1]), fill, x.dtype)], axis=0)


def _lane_tile(x, n):
    return x if n == 1 else jnp.concatenate([x] * n, axis=1)


def _log_sigmoid(a):
    return jnp.minimum(a, 0.0) - jnp.log1p(jnp.exp(-jnp.abs(a)))


def _sigmoid(a):
    return 1.0 / (1.0 + jnp.exp(-a))


def _gelu_tanh(x):
    c = math.sqrt(2.0 / math.pi)
    return 0.5 * x * (1.0 + jnp.tanh(c * (x + 0.044715 * (x * x * x))))


def _rms(x, g):
    ms = jnp.mean(x * x, axis=-1, keepdims=True)
    return x * lax.rsqrt(ms + EPS) * g


def _norm_matmul_kernel(x_ref, g_ref, w_ref, *rest, head_norm):
    if head_norm:
        hg_ref, o_ref, hn_ref = rest
    else:
        o_ref, hn_ref = rest

    @pl.when(pl.program_id(1) == 0)
    def _():
        hn_ref[...] = _rms(x_ref[...], g_ref[...]).astype(BF16)

    y = _dot(hn_ref[...], w_ref[...])
    if head_norm:
        tn = y.shape[1]
        for s in range(tn // LANES):
            sl = slice(s * LANES, (s + 1) * LANES)
            o_ref[:, sl] = _rms(y[:, sl], hg_ref[...]).astype(o_ref.dtype)
    else:
        o_ref[...] = y.astype(o_ref.dtype)


def norm_matmul(x, g, w, head_gain=None, out_dtype=F32, tm=512, tn=512):
    m, k = x.shape
    n = w.shape[1]
    tm = min(tm, m)
    tn = min(tn, n)
    head_norm = head_gain is not None
    in_specs = [pl.BlockSpec((tm, k), lambda i, j: (i, 0)),
                pl.BlockSpec((1, k), lambda i, j: (0, 0)),
                pl.BlockSpec((k, tn), lambda i, j: (0, j))]
    args = [x, g.reshape(1, k), w]
    if head_norm:
        in_specs.append(pl.BlockSpec((1, LANES), lambda i, j: (0, 0)))
        args.append(head_gain.reshape(1, LANES))
    return pl.pallas_call(
        functools.partial(_norm_matmul_kernel, head_norm=head_norm),
        out_shape=jax.ShapeDtypeStruct((m, n), out_dtype),
        grid=(m // tm, n // tn),
        in_specs=in_specs,
        out_specs=pl.BlockSpec((tm, tn), lambda i, j: (i, j)),
        scratch_shapes=[pltpu.VMEM((tm, k), BF16)],
        compiler_params=_cparams("parallel", "arbitrary"),
        name="norm_matmul",
    )(*args)


def _prep_kernel(fq_ref, fk_ref, fv_ref, sm_ref, gq_ref, gk_ref, bias_ref, gmat_ref, tril_ref, sel_ref, aug_ref,
                 *rest, seq_rows, t_valid, augment):
    if augment:
        qx_ref, kn_ref, g_ref, cs_ref, grow_ref, csrow_ref, kx_ref, vb_ref, vt_ref, carry_ref = rest
    else:
        qx_ref, kn_ref, g_ref, cs_ref, carry_ref = rest
    c = pl.program_id(1)
    tc = sm_ref.shape[0]

    @pl.when(c == 0)
    def _():
        carry_ref[...] = jnp.zeros_like(carry_ref)

    gmat = gmat_ref[...]

    def head_rms(x, gain):
        x2 = x * x
        hi = x2.astype(BF16)
        lo = (x2 - hi.astype(F32)).astype(BF16)
        ss = (_dot(hi, gmat) + _dot(lo, gmat)) * (1.0 / FOX_HD)
        return x * lax.rsqrt(ss + EPS) * gain

    qn = head_rms(fq_ref[...], gq_ref[...]) * ((FOX_HD ** -0.5) * (LOG2E if augment else 1.0))
    kn = head_rms(fk_ref[...], gk_ref[...])
    if augment:
        kn_ref[...] = kn.T
        vt_ref[...] = fv_ref[...].T
    else:
        kn_ref[...] = kn

    a = sm_ref[...] + bias_ref[...]
    lane = lax.broadcasted_iota(jnp.int32, a.shape, 1)
    pos = (lax.broadcasted_iota(jnp.int32, a.shape, 0) + c * tc) % seq_rows
    is_i = (lane >= LANE_MLI) & (lane < LANE_MLF)
    used = lane < LANE_MLF + ML_HEADS
    valid = pos < t_valid
    g = jnp.where(is_i, a, _log_sigmoid(a))
    g = jnp.where(used, g, 0.0)
    g = jnp.where(valid, g, jnp.where(is_i, -jnp.inf, 0.0))
    gc = jnp.where(is_i, 0.0, g)
    cs = _dot_exact_lhs(tril_ref[...], gc)
    carry = carry_ref[...]
    csg = cs + jnp.where(lane < FOX_HEADS, carry, 0.0)
    carry_ref[...] = carry + cs[tc - 1:tc, :]
    g_ref[...] = g
    cs_ref[...] = csg
    if augment:
        grow_ref[...] = g.T
        csrow_ref[...] = csg.T

    nx = FOX_HEADS * LANES
    lanex = lax.broadcasted_iota(jnp.int32, (1, nx), 1)
    keep = ((lanex % LANES) >= FOX_HD) == ((lanex // LANES) % 2 == 1)
    q_exp = jnp.concatenate([qn[:, (h // 2) * LANES:(h // 2 + 1) * LANES] for h in range(FOX_HEADS)], axis=1)
    if not augment:
        qx_ref[...] = jnp.where(keep, q_exp, 0.0).astype(qx_ref.dtype)
        return
    k_exp = jnp.concatenate([kn[:, (h // 2) * LANES:(h // 2 + 1) * LANES] for h in range(FOX_HEADS)], axis=1)
    hi, mid, lo = _split3(jnp.where(lane < FOX_HEADS, csg * LOG2E, 0.0))
    packed = (hi.astype(F32) + pltpu.roll(mid.astype(F32), FOX_HEADS, 1)
              + pltpu.roll(lo.astype(F32), 2 * FOX_HEADS, 1)).astype(BF16)
    aug = aug_ref[...] + _dot(packed, sel_ref[...])
    qx_ref[...] = jnp.where(keep, q_exp, aug[:, 0:nx]).astype(qx_ref.dtype)
    kx_ref[...] = jnp.where(keep, k_exp, aug[:, nx:2 * nx]).astype(kx_ref.dtype)
    vb_ref[...] = fv_ref[...].astype(vb_ref.dtype)


PREP_SHORT_ROWS = 128


def gate_prep(z, gq, gk, bias_row, nb, t, tc, t_valid, augment):
    m = nb * t
    gmat = jnp.asarray(np.kron(np.eye(FOX_HEADS), np.ones((FOX_HD, FOX_HD))), BF16)
    if augment:
        nc, tp, grid_rows = t // tc, tc, nb
        tril = np.tril(np.ones((tc, tc)))
    else:
        tc = tp = PREP_SHORT_ROWS
        nc, grid_rows = 1, m // tc
        tril = np.kron(np.eye(tc // t), np.tril(np.ones((t, t))))
    tril = jnp.asarray(tril, BF16)
    nx = FOX_HEADS * LANES
    sel = np.zeros((LANES, 2 * nx), np.float32)
    aug_const = np.zeros((1, 2 * nx), np.float32)
    for h in range(FOX_HEADS):
        o = h * LANES + (0 if h % 2 else FOX_HD)
        for p in range(3):
            sel[p * FOX_HEADS + h, o + p] = 1.0
            sel[p * FOX_HEADS + h, nx + o + 3 + p] = -1.0
            aug_const[0, o + 3 + p] = 1.0
            aug_const[0, nx + o + p] = 1.0
    sel, aug_const = jnp.asarray(sel, BF16), jnp.asarray(aug_const, F32)
    row_map = lambda b, c: (b * nc + c, 0)
    const = lambda b, c: (0, 0)
    feat_major = jax.ShapeDtypeStruct((nb * FOX_WIDTH, t), F32)
    feat_spec = pl.BlockSpec((FOX_WIDTH, tc), lambda b, c: (b, c))
    out_shape = [jax.ShapeDtypeStruct((m, FOX_HEADS * LANES), BF16 if augment else F32),
                 feat_major if augment else jax.ShapeDtypeStruct((m, FOX_WIDTH), F32),
                 jax.ShapeDtypeStruct((m, LANES), F32),
                 jax.ShapeDtypeStruct((m, LANES), F32)]
    out_specs = [pl.BlockSpec((tc, FOX_HEADS * LANES), row_map),
                 feat_spec if augment else pl.BlockSpec((tc, FOX_WIDTH), row_map),
                 pl.BlockSpec((tc, LANES), row_map),
                 pl.BlockSpec((tc, LANES), row_map)]
    if augment:
        out_shape += [jax.ShapeDtypeStruct((nb * LANES, t), F32), jax.ShapeDtypeStruct((nb * LANES, t), F32),
                      jax.ShapeDtypeStruct((m, FOX_HEADS * LANES), BF16), jax.ShapeDtypeStruct((m, FOX_WIDTH), BF16),
                      feat_major]
        out_specs += [pl.BlockSpec((LANES, tc), lambda b, c: (b, c)), pl.BlockSpec((LANES, tc), lambda b, c: (b, c)),
                      pl.BlockSpec((tc, FOX_HEADS * LANES), row_map), pl.BlockSpec((tc, FOX_WIDTH), row_map),
                      feat_spec]
    return pl.pallas_call(
        functools.partial(_prep_kernel, seq_rows=t, t_valid=t_valid, augment=augment),
        out_shape=tuple(out_shape),
        grid=(grid_rows, nc),
        in_specs=[pl.BlockSpec((tc, 512), lambda b, c: (b * nc + c, COL_FQ)),
                  pl.BlockSpec((tc, 512), lambda b, c: (b * nc + c, COL_FK)),
                  pl.BlockSpec((tc, 512), lambda b, c: (b * nc + c, COL_FV)),
                  pl.BlockSpec((tc, LANES), lambda b, c: (b * nc + c, COL_SMALL * 4)),
                  pl.BlockSpec((1, 512), const), pl.BlockSpec((1, 512), const),
                  pl.BlockSpec((1, LANES), const),
                  pl.BlockSpec((512, 512), const), pl.BlockSpec((tp, tp), const),
                  pl.BlockSpec((LANES, 2 * nx), const), pl.BlockSpec((1, 2 * nx), const)],
        out_specs=tuple(out_specs),
        scratch_shapes=[pltpu.VMEM((1, LANES), F32)],
        compiler_params=_cparams("parallel", "arbitrary"),
        name="gate_prep",
    )(z, z, z, z, gq, gk, bias_row, gmat, tril, sel, aug_const)


S5_SCAN_LANES = 512
S5_SHORT_SEQS = 16


def _cmul(ar, ai, br, bi):
    return ar * br - ai * bi, ar * bi + ai * br


def _s5_kernel(u_ref, h0r_ref, h0i_ref, perm_ref, permt_ref, bre_ref, bim_ref, lam_ref, pseg_ref, pk_ref,
               cre_ref, cim_ref, d_ref, wglu_ref, bglu_ref, y_ref, hlr_ref, hli_ref, hr_ref, hi_ref, cr_ref, ci_ref,
               *, t_last, seqs):
    c = pl.program_id(1)
    nc = pl.num_programs(1)
    tc = u_ref.shape[0]
    R = tc // SUBLANES if seqs == 1 else 1
    half = S5_WIDTH // 2
    hl = S5_LANES // 2

    if seqs == 1:
        @pl.when(c == 0)
        def _():
            cr_ref[...] = h0r_ref[0]
            ci_ref[...] = h0i_ref[0]

    u = u_ref[...]
    ub = _pad_rows(u, MIN_BF16_ROWS).astype(BF16)
    if R > 1:
        ub = _dot(perm_ref[...], ub).astype(BF16)
    for j in range(2):
        uj = ub[:, j * half:(j + 1) * half]
        hr_ref[:, j * hl:(j + 1) * hl] = _dot(uj, bre_ref[j])[0:tc]
        hi_ref[:, j * hl:(j + 1) * hl] = _dot(uj, bim_ref[j])[0:tc]

    sub = lax.broadcasted_iota(jnp.int32, (SUBLANES, S5_SCAN_LANES), 0)
    for lc in range(S5_LANES // S5_SCAN_LANES):
        ls = slice(lc * S5_SCAN_LANES, (lc + 1) * S5_SCAN_LANES)
        lam_r = jnp.broadcast_to(lam_ref[0:1, ls], sub.shape)
        lam_i = jnp.broadcast_to(lam_ref[1:2, ls], sub.shape)
        if seqs > 1:
            def seq_scan(g, _):
                rows = pl.ds(pl.multiple_of(g * SUBLANES, SUBLANES), SUBLANES)
                s_r = jnp.where(sub == 0, jnp.broadcast_to(h0r_ref[0, pl.ds(g, 1), ls], sub.shape), 0.0)
                s_i = jnp.where(sub == 0, jnp.broadcast_to(h0i_ref[0, pl.ds(g, 1), ls], sub.shape), 0.0)
                m_r, m_i = _cmul(lam_r, lam_i, s_r, s_i)
                x_r = hr_ref[rows, ls] + m_r
                x_i = hi_ref[rows, ls] + m_i
                for j, lag in enumerate((1, 2, 4)):
                    m_r, m_i = _cmul(pseg_ref[2 * j, :, ls], pseg_ref[2 * j + 1, :, ls],
                                     pltpu.roll(x_r, lag, 0), pltpu.roll(x_i, lag, 0))
                    x_r, x_i = x_r + m_r, x_i + m_i
                hr_ref[rows, ls] = x_r
                hi_ref[rows, ls] = x_i
                hlr_ref[0, pl.ds(g, 1), ls] = x_r[t_last:t_last + 1, :]
                hli_ref[0, pl.ds(g, 1), ls] = x_i[t_last:t_last + 1, :]
                return 0

            lax.fori_loop(0, seqs, seq_scan, 0, unroll=2)
            continue
        init_r = jnp.where(sub == 0, jnp.broadcast_to(cr_ref[:, ls], sub.shape), 0.0)
        init_i = jnp.where(sub == 0, jnp.broadcast_to(ci_ref[:, ls], sub.shape), 0.0)

        def local_step(k, carry):
            h_r, h_i = carry
            rows = pl.ds(pl.multiple_of(k * SUBLANES, SUBLANES), SUBLANES)
            m_r, m_i = _cmul(lam_r, lam_i, h_r, h_i)
            h_r = m_r + hr_ref[rows, ls]
            h_i = m_i + hi_ref[rows, ls]
            hr_ref[rows, ls] = h_r
            hi_ref[rows, ls] = h_i
            return h_r, h_i

        e_r, e_i = lax.fori_loop(0, R, local_step, (init_r, init_i), unroll=min(R, 8))
        for j, lag in enumerate((1, 2, 4)):
            m_r, m_i = _cmul(pseg_ref[2 * j, :, ls], pseg_ref[2 * j + 1, :, ls],
                             pltpu.roll(e_r, lag, 0), pltpu.roll(e_i, lag, 0))
            e_r, e_i = e_r + m_r, e_i + m_i
        cr_ref[:, ls] = e_r[SUBLANES - 1:SUBLANES, :]
        ci_ref[:, ls] = e_i[SUBLANES - 1:SUBLANES, :]
        in_r = jnp.where(sub == 0, 0.0, pltpu.roll(e_r, 1, 0))
        in_i = jnp.where(sub == 0, 0.0, pltpu.roll(e_i, 1, 0))

        def fix_step(k, _):
            rows = pl.ds(pl.multiple_of(k * SUBLANES, SUBLANES), SUBLANES)
            p_r = jnp.broadcast_to(pk_ref[0, pl.ds(k, 1), ls], sub.shape)
            p_i = jnp.broadcast_to(pk_ref[1, pl.ds(k, 1), ls], sub.shape)
            m_r, m_i = _cmul(p_r, p_i, in_r, in_i)
            hr_ref[rows, ls] += m_r
            hi_ref[rows, ls] += m_i
            return 0

        lax.fori_loop(0, R, fix_step, 0, unroll=min(R, 8))

    if seqs == 1:
        @pl.when(c == nc - 1)
        def _():
            tl = t_last % tc
            pos = (tl % R) * SUBLANES + tl // R
            hlr_ref[0] = hr_ref[pos:pos + 1, :]
            hli_ref[0] = hi_ref[pos:pos + 1, :]

    hrb = _pad_rows(hr_ref[...], MIN_BF16_ROWS).astype(BF16)
    hib = _pad_rows(hi_ref[...], MIN_BF16_ROWS).astype(BF16)
    ys = [(_dot(hrb[:, j * hl:(j + 1) * hl], cre_ref[j]) + _dot(hib[:, j * hl:(j + 1) * hl], cim_ref[j]))
          for j in range(2)]
    y = jnp.concatenate(ys, axis=1)
    if R > 1:
        y = _dot_exact_lhs(permt_ref[...], y)
    y = _gelu_tanh(y[0:tc] + d_ref[...] * u)
    gate = _dot(_pad_rows(y, MIN_BF16_ROWS).astype(BF16), wglu_ref[...])[0:tc]
    y_ref[...] = y * _sigmoid(gate + bglu_ref[...])


def s5_mixer(z, h0r, h0i, sp, nb, t, tc, t_last, seqs=1):
    assert seqs == 1 or (tc == SUBLANES and t == tc)
    nc = t // tc
    R = tc // SUBLANES
    tc = tc * seqs
    nbg = nb // seqs
    tperm = max(tc, MIN_BF16_ROWS) if seqs == 1 else MIN_BF16_ROWS
    perm = np.zeros((tperm, tperm), np.float32)
    if seqs == 1:
        for s in range(SUBLANES):
            for k in range(R):
                perm[k * SUBLANES + s, s * R + k] = 1.0
    permt = jnp.asarray(perm.T, BF16)
    perm = jnp.asarray(perm, BF16)
    const = lambda b, c: (0, 0)
    const3 = lambda b, c: (0, 0, 0)
    state_spec = pl.BlockSpec((1, seqs, S5_LANES), lambda b, c: (b, 0, 0))
    tabs = sp["tabs"][R]
    return pl.pallas_call(
        functools.partial(_s5_kernel, t_last=t_last, seqs=seqs),
        out_shape=(jax.ShapeDtypeStruct((nb * t, S5_WIDTH), F32),
                   jax.ShapeDtypeStruct((nbg, seqs, S5_LANES), F32),
                   jax.ShapeDtypeStruct((nbg, seqs, S5_LANES), F32)),
        grid=(nbg, nc),
        in_specs=[pl.BlockSpec((tc, 512), lambda b, c: (b * nc + c, COL_S5)),
                  state_spec, state_spec,
                  pl.BlockSpec((tperm, tperm), const), pl.BlockSpec((tperm, tperm), const),
                  pl.BlockSpec((2, S5_WIDTH // 2, S5_LANES // 2), const3),
                  pl.BlockSpec((2, S5_WIDTH // 2, S5_LANES // 2), const3),
                  pl.BlockSpec((2, S5_LANES), const),
                  pl.BlockSpec((6, SUBLANES, S5_LANES), const3),
                  pl.BlockSpec((2, R, S5_LANES), const3),
                  pl.BlockSpec((2, S5_LANES // 2, S5_WIDTH // 2), const3),
                  pl.BlockSpec((2, S5_LANES // 2, S5_WIDTH // 2), const3),
                  pl.BlockSpec((1, S5_WIDTH), const),
                  pl.BlockSpec((S5_WIDTH, S5_WIDTH), const), pl.BlockSpec((1, S5_WIDTH), const)],
        out_specs=(pl.BlockSpec((tc, S5_WIDTH), lambda b, c: (b * nc + c, 0)), state_spec, state_spec),
        scratch_shapes=[pltpu.VMEM((tc, S5_LANES), F32), pltpu.VMEM((tc, S5_LANES), F32),
                        pltpu.VMEM((1, S5_LANES), F32), pltpu.VMEM((1, S5_LANES), F32)],
        compiler_params=_cparams("parallel", "arbitrary"),
        name="s5_mixer",
    )(z, h0r.reshape(nbg, seqs, S5_LANES), h0i.reshape(nbg, seqs, S5_LANES), perm, permt, sp["bre"], sp["bim"],
      tabs["lam"],
      tabs["pseg"], tabs["pk"], sp["cre"], sp["cim"], sp["d"], sp["wglu"], sp["bglu"])


def s5_params(a_re, a_im, log_step, b_re, b_im, c_re, c_im, d_skip, w_glu, b_glu, seg_lens):
    dt = jnp.exp(log_step)[:, None]
    mag = jnp.exp(a_re * dt)
    lr = (mag * jnp.cos(a_im * dt)).reshape(1, S5_LANES)
    li = (mag * jnp.sin(a_im * dt)).reshape(1, S5_LANES)
    den = a_re * a_re + a_im * a_im
    xr, xi = lr.reshape(a_re.shape) - 1.0, li.reshape(a_re.shape)
    fr = (xr * a_re + xi * a_im) / den
    fi = (xi * a_re - xr * a_im) / den
    bbr = fr[..., None] * b_re - fi[..., None] * b_im
    bbi = fr[..., None] * b_im + fi[..., None] * b_re
    gh = S5_GROUPS // 2
    eye = jnp.eye(gh, dtype=F32)

    def in_mat(b):
        return jnp.einsum('jgpc,gh->jgchp', b.reshape(2, gh, S5_STATE, S5_GROUP), eye).reshape(
            2, S5_WIDTH // 2, S5_LANES // 2).astype(BF16)

    def out_mat(cm):
        return jnp.einsum('jgcp,gh->jgphc', cm.reshape(2, gh, S5_GROUP, S5_STATE), eye).reshape(
            2, S5_LANES // 2, S5_WIDTH // 2).astype(BF16)

    def powers(pr, pi, n):
        tr, ti, cnt = pr, pi, 1
        while cnt < n:
            lr_, li_ = tr[cnt - 1:cnt], ti[cnt - 1:cnt]
            nr, ni = _cmul(tr, ti, lr_, li_)
            tr, ti, cnt = jnp.concatenate([tr, nr], axis=0), jnp.concatenate([ti, ni], axis=0), 2 * cnt
        return tr, ti

    sub = jnp.arange(SUBLANES)[:, None]
    tabs = {}
    for R in seg_lens:
        kr, ki = powers(lr, li, R)
        sr, si = powers(kr[R - 1:R], ki[R - 1:R], 4)
        pseg = []
        for lag in (1, 2, 4):
            msk = (sub >= lag).astype(F32)
            pseg += [msk * sr[lag - 1:lag], msk * si[lag - 1:lag]]
        tabs[R] = dict(lam=jnp.concatenate([lr, li], axis=0), pseg=jnp.stack(pseg), pk=jnp.stack([kr, ki]))
    return dict(bre=in_mat(bbr), bim=in_mat(bbi), cre=out_mat(c_re), cim=out_mat(-c_im), tabs=tabs,
                d=d_skip.reshape(1, S5_WIDTH), wglu=w_glu.astype(BF16), bglu=b_glu.reshape(1, S5_WIDTH))


def _fox_flash_kernel(it_ref, jt_ref, qx_ref, kx_ref, v_ref, o_ref, m_ref, l_ref, acc_ref, s_ref, p_ref):
    i = it_ref[pl.program_id(1)]
    j = jt_ref[pl.program_id(1)]
    tq = qx_ref.shape[0]
    tk = kx_ref.shape[0]

    @pl.when(j == 0)
    def _():
        m_ref[...] = jnp.full_like(m_ref, -jnp.inf)
        l_ref[...] = jnp.zeros_like(l_ref)
        acc_ref[...] = jnp.zeros_like(acc_ref)

    def step(masked):
        nr, ncol = tq // FLASH_ROWS, tk // LANES
        if masked:
            diff = (lax.broadcasted_iota(jnp.int32, (FLASH_ROWS, LANES), 0) -
                    lax.broadcasted_iota(jnp.int32, (FLASH_ROWS, LANES), 1))
        for h in range(FOX_HEADS):
            hs = slice(h * LANES, (h + 1) * LANES)
            ps = slice((h // 2) * LANES, (h // 2 + 1) * LANES)
            s_buf, p_buf = s_ref.at[h % 2], p_ref.at[h % 2]
            if h == 0:
                s_buf[...] = _dot_nt(qx_ref[:, hs], kx_ref[:, hs])
            if h + 1 < FOX_HEADS:
                nhs = slice((h + 1) * LANES, (h + 2) * LANES)
                s_ref[(h + 1) % 2] = _dot_nt(qx_ref[:, nhs], kx_ref[:, nhs])
            for r in range(nr):
                rs = slice(r * FLASH_ROWS, (r + 1) * FLASH_ROWS)

                live = [cidx for cidx in range(ncol) if not (masked and cidx * LANES >= (r + 1) * FLASH_ROWS)]

                def piece(cidx):
                    sc = s_buf[rs, cidx * LANES:(cidx + 1) * LANES]
                    if masked and (cidx + 1) * LANES - 1 > r * FLASH_ROWS:
                        sc = jnp.where(diff >= (cidx * LANES - r * FLASH_ROWS), sc, -jnp.inf)
                    return sc

                mx = piece(live[0])
                for cidx in live[1:]:
                    mx = jnp.maximum(mx, piece(cidx))
                m_prev = m_ref[h, rs, :]
                m_new = jnp.maximum(m_prev, jnp.max(mx, axis=-1, keepdims=True))
                alpha = jnp.exp2(m_prev - m_new)
                lsum = jnp.zeros((FLASH_ROWS, LANES), F32)
                for cidx in range(ncol):
                    cs_ = slice(cidx * LANES, (cidx + 1) * LANES)
                    if cidx not in live:
                        p_buf[rs, cs_] = jnp.zeros((FLASH_ROWS, LANES), BF16)
                        continue
                    pc = jnp.exp2(piece(cidx) - m_new)
                    lsum = lsum + pc
                    p_buf[rs, cs_] = pc.astype(BF16)
                l_ref[h, rs, :] = alpha * l_ref[h, rs, :] + lsum
                acc_ref[h, rs, :] = alpha * acc_ref[h, rs, :]
                m_ref[h, rs, :] = m_new
            acc_ref[h] += _dot(p_buf[...], v_ref[:, ps])

    @pl.when(j < i)
    def _():
        step(False)

    @pl.when(j == i)
    def _():
        step(True)
        lane = lax.broadcasted_iota(jnp.int32, (1, LANES), 1)
        for p in range(FOX_HEADS // 2):
            lo = acc_ref[2 * p] / jnp.sum(l_ref[2 * p], axis=-1, keepdims=True)
            hi = acc_ref[2 * p + 1] / jnp.sum(l_ref[2 * p + 1], axis=-1, keepdims=True)
            o_ref[:, p * LANES:(p + 1) * LANES] = jnp.where(lane < FOX_HD, lo, hi)


def fox_flash(qx, kx, vb, nb, t, tile):
    assert tile % FLASH_ROWS == 0
    nt = t // tile
    pairs = [(i, j) for i in range(nt) for j in range(i + 1)]
    it = jnp.asarray([p[0] for p in pairs], jnp.int32)
    jt = jnp.asarray([p[1] for p in pairs], jnp.int32)
    return pl.pallas_call(
        _fox_flash_kernel,
        out_shape=jax.ShapeDtypeStruct((nb * t, FOX_WIDTH), F32),
        grid_spec=pltpu.PrefetchScalarGridSpec(
            num_scalar_prefetch=2,
            grid=(nb, len(pairs)),
            in_specs=[pl.BlockSpec((tile, FOX_HEADS * LANES), lambda b, p, it, jt: (b * nt + it[p], 0)),
                      pl.BlockSpec((tile, FOX_HEADS * LANES), lambda b, p, it, jt: (b * nt + jt[p], 0)),
                      pl.BlockSpec((tile, FOX_WIDTH), lambda b, p, it, jt: (b * nt + jt[p], 0))],
            out_specs=pl.BlockSpec((tile, FOX_WIDTH), lambda b, p, it, jt: (b * nt + it[p], 0)),
            scratch_shapes=[pltpu.VMEM((FOX_HEADS, tile, LANES), F32), pltpu.VMEM((FOX_HEADS, tile, LANES), F32),
                            pltpu.VMEM((FOX_HEADS, tile, LANES), F32),
                            pltpu.VMEM((2, tile, tile), F32), pltpu.VMEM((2, tile, tile), BF16)]),
        compiler_params=_cparams("parallel", "arbitrary"),
        name="fox_flash",
    )(it, jt, qx, kx, vb)


def _fox_decode_kernel(pt_ref, qb_ref, kn_ref, vn_ref, cs_ref, *rest, n_pages):
    k_refs = rest[0:n_pages]
    v_refs = rest[n_pages:2 * n_pages]
    f_refs = rest[2 * n_pages:3 * n_pages]
    triu_ref, o_ref, kpad_ref, vpad_ref = rest[3 * n_pages:]
    page = kpad_ref.shape[0]
    nrow = FOX_HEADS * SAMPLE_T
    qb = qb_ref[...].astype(BF16)

    def per_query(x):
        return jnp.concatenate([x] * SAMPLE_T, axis=0)

    fcat = jnp.concatenate([f_refs[i][0, 0] for i in range(n_pages)], axis=0)
    cum_in = _dot_exact_rhs(fcat, triu_ref[...])
    totals = jnp.broadcast_to(cum_in[:, page - 1:page], cum_in.shape)
    off = jnp.zeros((FOX_HEADS, page), F32)
    ss = []
    for i in range(n_pages):
        rows = slice(i * FOX_HEADS, (i + 1) * FOX_HEADS)
        kt = k_refs[i][0, 0].reshape(FOX_WIDTH, page).astype(BF16)
        ss.append(_dot(qb, kt) - per_query(cum_in[rows] + off))
        off = off + totals[rows]
    kpad_ref[...] = jnp.zeros_like(kpad_ref)
    vpad_ref[...] = jnp.zeros_like(vpad_ref)
    kpad_ref[0:SAMPLE_T, :] = kn_ref[...]
    vpad_ref[0:SAMPLE_T, :] = vn_ref[...]
    csrow = _pad_rows(cs_ref[...], page).T[LANE_FOXF:LANE_FOXF + FOX_HEADS]
    s_new = _dot_nt(qb, kpad_ref[...].astype(BF16)) - per_query(csrow + off)
    tq_idx = lax.broadcasted_iota(jnp.int32, (nrow, page), 0) // FOX_HEADS
    tk_idx = lax.broadcasted_iota(jnp.int32, (nrow, page), 1)
    ss.append(jnp.where(tk_idx <= tq_idx, s_new, -jnp.inf))

    m = ss[0]
    for s in ss[1:]:
        m = jnp.maximum(m, s)
    m = jnp.max(m, axis=-1, keepdims=True)
    lsum = jnp.zeros((nrow, page), F32)
    acc = jnp.zeros((nrow, FOX_WIDTH), F32)
    for i, s in enumerate(ss):
        p = jnp.exp(s - m)
        lsum = lsum + p
        if i < n_pages:
            acc = acc + _dot_nt(p.astype(BF16), v_refs[i][0, 0].reshape(FOX_WIDTH, page).astype(BF16))
        else:
            acc = acc + _dot(p.astype(BF16), vpad_ref[...].astype(BF16))
    acc = acc / jnp.sum(lsum, axis=-1, keepdims=True)
    lane = lax.broadcasted_iota(jnp.int32, (nrow, FOX_WIDTH), 1) // FOX_HD
    head = lax.broadcasted_iota(jnp.int32, (nrow, FOX_WIDTH), 0) % FOX_HEADS
    picked = jnp.where(lane == head, acc, 0.0)
    o_ref[...] = jnp.sum(picked.reshape(SAMPLE_T, FOX_HEADS, FOX_WIDTH), axis=1)


def fox_decode(page_table, qx, kn, z, cs, pool_kt, pool_vt, pool_ft, layer):
    nb, n_pages = page_table.shape
    page = pool_kt.shape[-1]
    assert page == LANES
    nrow = FOX_HEADS * SAMPLE_T
    pt = page_table.reshape(-1)
    triu = jnp.asarray(np.triu(np.ones((page, page))), BF16)
    pair_of_head = jnp.asarray(np.arange(FOX_HEADS)[:, None] // 2 == np.arange(FOX_HEADS // 2)[None, :], F32)
    qb = (qx.reshape(nb, SAMPLE_T, FOX_HEADS, 1, LANES) * pair_of_head[None, None, :, :, None]).reshape(
        nb * nrow, FOX_WIDTH)

    def pg5(i):
        return lambda b, pt: (layer, pt[b * n_pages + i], 0, 0, 0)

    def pg4(i):
        return lambda b, pt: (layer, pt[b * n_pages + i], 0, 0)

    row = lambda b, pt: (b, 0)
    in_specs = [pl.BlockSpec((nrow, FOX_WIDTH), row),
                pl.BlockSpec((SAMPLE_T, FOX_WIDTH), row),
                pl.BlockSpec((SAMPLE_T, FOX_WIDTH), lambda b, pt: (b, COL_FV)),
                pl.BlockSpec((SAMPLE_T, LANES), row)]
    in_specs += [pl.BlockSpec((1, 1, FOX_HEADS, FOX_HD, page), pg5(i)) for i in range(n_pages)]
    in_specs += [pl.BlockSpec((1, 1, FOX_HEADS, FOX_HD, page), pg5(i)) for i in range(n_pages)]
    in_specs += [pl.BlockSpec((1, 1, FOX_HEADS, page), pg4(i)) for i in range(n_pages)]
    in_specs += [pl.BlockSpec((page, page), lambda b, pt: (0, 0))]
    return pl.pallas_call(
        functools.partial(_fox_decode_kernel, n_pages=n_pages),
        out_shape=jax.ShapeDtypeStruct((nb * SAMPLE_T, FOX_WIDTH), F32),
        grid_spec=pltpu.PrefetchScalarGridSpec(
            num_scalar_prefetch=1,
            grid=(nb,),
            in_specs=in_specs,
            out_specs=pl.BlockSpec((SAMPLE_T, FOX_WIDTH), row),
            scratch_shapes=[pltpu.VMEM((page, FOX_WIDTH), F32), pltpu.VMEM((page, FOX_WIDTH), F32)]),
        compiler_params=_cparams("parallel"),
        name="fox_decode",
    )(pt, qb, kn, z, cs, *([pool_kt] * n_pages), *([pool_vt] * n_pages), *([pool_ft] * n_pages), triu)


def _mlstm_kernel(q_ref, k_ref, v_ref, o_ref, g_ref, cs_ref, *rest, short):
    if short:
        c0_ref, n0_ref, m0_ref, gn_ref, y_ref, cout_ref, nout_ref, mout_ref, c_ref, n_ref, m_ref = rest
    else:
        (grow_ref, csrow_ref, c0_ref, n0_ref, m0_ref, gn_ref, y_ref, cout_ref, nout_ref, mout_ref,
         c_ref, n_ref, m_ref) = rest
    c = pl.program_id(1)
    nc = pl.num_programs(1)
    L = q_ref.shape[0]

    @pl.when(c == 0)
    def _():
        c_ref[...] = c0_ref[0, 0]
        n_ref[...] = n0_ref[0]
        m_ref[...] = m0_ref[0]

    Lk = max(L, LANES)
    Lq = max(L, MIN_BF16_ROWS)
    g = g_ref[...]
    cs = cs_ref[...]
    if short:
        grow = _pad_rows(g, Lk).T
        csrow = _pad_rows(cs, Lk).T
    else:
        grow = grow_ref[...]
        csrow = csrow_ref[...]
    causal = (lax.broadcasted_iota(jnp.int32, (Lq, Lk), 0) >= lax.broadcasted_iota(jnp.int32, (Lq, Lk), 1))
    scale = ML_HD ** -0.5
    heads = range(ML_HEADS)
    hsl = [slice(h * ML_HD, (h + 1) * ML_HD) for h in heads]
    qb = [_pad_rows(q_ref[:, hsl[h]], Lq).astype(BF16) for h in heads]
    kb = [_pad_rows(k_ref[:, hsl[h]], Lk).astype(BF16) for h in heads]
    vb = [_pad_rows(v_ref[:, hsl[h]], Lk).astype(BF16) for h in heads]
    bcol_k = [_pad_rows(cs[:, LANE_MLF + h:LANE_MLF + h + 1], Lk) for h in heads]
    bcol = [b[0:Lq] for b in bcol_k]
    icol = [_pad_rows(g[:, LANE_MLI + h:LANE_MLI + h + 1], Lk, -jnp.inf) for h in heads]
    log_d = [jnp.where(causal, bcol[h] - csrow[LANE_MLF + h:LANE_MLF + h + 1, :]
                       + grow[LANE_MLI + h:LANE_MLI + h + 1, :], -jnp.inf) for h in heads]
    log_inter = [bcol[h] + m_ref[0:1, h:h + 1] for h in heads]
    m_t = [jnp.maximum(log_inter[h], jnp.max(log_d[h], axis=-1, keepdims=True)) for h in heads]
    inter_w = [jnp.exp(log_inter[h] - m_t[h]) for h in heads]
    qk = [_dot_nt(qb[h], kb[h]) for h in heads]
    ch = [c_ref[h] for h in heads]
    n_row = [n_ref[h:h + 1, :] for h in heads]
    qc = [_dot_nt(qb[h], ch[h].astype(BF16)) for h in heads]
    s = [qk[h] * (jnp.exp(log_d[h] - m_t[h]) * scale) for h in heads]
    sv = [_dot(s[h].astype(BF16), vb[h]) for h in heads]
    for h in heads:
        num = sv[h] + inter_w[h] * qc[h]
        den = (jnp.sum(s[h], axis=-1, keepdims=True)
               + inter_w[h] * jnp.sum(qb[h].astype(F32) * n_row[h], axis=-1, keepdims=True))
        hh = num / jnp.maximum(jnp.abs(den), jnp.exp(-m_t[h]))
        y_ref[:, hsl[h]] = _rms(hh[0:L], gn_ref[...]) * _sigmoid(o_ref[:, hsl[h]])
    m_end = [m_t[h][L - 1:L, :] for h in heads]
    a_end = [inter_w[h][L - 1:L, :] for h in heads]
    w_col = [jnp.exp(bcol[h][L - 1:L, :] - bcol_k[h] + icol[h] - m_end[h]) * scale for h in heads]
    upd = [_dot_tn((vb[h].astype(F32) * w_col[h]).astype(BF16), kb[h]) for h in heads]
    for h in heads:
        c_ref[h] = a_end[h] * ch[h] + upd[h]
        n_ref[h:h + 1, :] = a_end[h] * n_row[h] + jnp.sum(kb[h].astype(F32) * w_col[h], axis=0, keepdims=True)
        m_ref[0:1, h:h + 1] = m_end[h]

    @pl.when(c == nc - 1)
    def _():
        cout_ref[0] = c_ref[...]
        nout_ref[0] = n_ref[...]
        mout_ref[0] = m_ref[...]


def mlstm(za, zb, g, cs, grow, csrow, c0, n0, m0, gn, nb, t, L, c_layer):
    nc = t // L
    rows = lambda b, c: (b * nc + c, 0)
    rr = 2 * SUBLANES
    m0p = jnp.zeros((nb, 1, LANES), F32).at[:, 0, :ML_HEADS].set(m0)
    short = grow is None
    row_specs = [] if short else [pl.BlockSpec((rr, L), lambda b, c: (b * (LANES // rr), c))] * 2
    row_args = [] if short else [grow, csrow]
    outs = pl.pallas_call(
        functools.partial(_mlstm_kernel, short=short),
        out_shape=(jax.ShapeDtypeStruct((nb * t, ML_WIDTH), F32),
                   jax.ShapeDtypeStruct((nb, ML_HEADS, ML_HD, ML_HD), F32),
                   jax.ShapeDtypeStruct((nb, ML_HEADS, ML_HD), F32),
                   jax.ShapeDtypeStruct((nb, 1, LANES), F32)),
        grid=(nb, nc),
        in_specs=[pl.BlockSpec((L, 512), lambda b, c: (b * nc + c, COL_MQ)),
                  pl.BlockSpec((L, 512), lambda b, c: (b * nc + c, COL_MK)),
                  pl.BlockSpec((L, 512), lambda b, c: (b * nc + c, COL_MV)),
                  pl.BlockSpec((L, 512), lambda b, c: (b * nc + c, COL_MO)),
                  pl.BlockSpec((L, LANES), rows), pl.BlockSpec((L, LANES), rows), *row_specs,
                  pl.BlockSpec((1, 1, ML_HEADS, ML_HD, ML_HD), lambda b, c: (c_layer, b, 0, 0, 0)),
                  pl.BlockSpec((1, ML_HEADS, ML_HD), lambda b, c: (b, 0, 0)),
                  pl.BlockSpec((1, 1, LANES), lambda b, c: (b, 0, 0)),
                  pl.BlockSpec((1, ML_HD), lambda b, c: (0, 0))],
        out_specs=(pl.BlockSpec((L, ML_WIDTH), rows),
                   pl.BlockSpec((1, ML_HEADS, ML_HD, ML_HD), lambda b, c: (b, 0, 0, 0)),
                   pl.BlockSpec((1, ML_HEADS, ML_HD), lambda b, c: (b, 0, 0)),
                   pl.BlockSpec((1, 1, LANES), lambda b, c: (b, 0, 0))),
        scratch_shapes=[pltpu.VMEM((ML_HEADS, ML_HD, ML_HD), F32), pltpu.VMEM((ML_HEADS, ML_HD), F32),
                        pltpu.VMEM((1, LANES), F32)],
        compiler_params=_cparams("parallel", "arbitrary"),
        name="mlstm",
    )(zb, zb, zb, za, g, cs, *row_args, c0, n0, m0p, gn.reshape(1, ML_HD))
    y, c_new, n_new, m_new = outs
    return y, c_new, n_new, m_new[:, 0, :ML_HEADS]


def _merge_kernel(x_ref, g_ref, wg_ref, ys_ref, yf_ref, ym_ref, ws_ref, wf_ref, wm_ref, wo_ref, o_ref):
    x = x_ref[...]
    hn = _rms(x, g_ref[...]).astype(BF16)
    branches = (ys_ref, ws_ref), (yf_ref, wf_ref), (ym_ref, wm_ref)
    merged = None
    for b, (y_ref, w_ref) in enumerate(branches):
        gate = _sigmoid(_dot(hn, wg_ref[:, b * D_MODEL:(b + 1) * D_MODEL]))
        term = gate * _dot(y_ref[...].astype(BF16), w_ref[...])
        merged = term if merged is None else merged + term
    o_ref[...] = x + _dot(merged.astype(BF16), wo_ref[...])


def merge_out(x, g_mix, w_gates, ys, yf, ym, ws, wf, wm, wo, tm=512):
    m = x.shape[0]
    tm = min(tm, m)
    row = lambda i: (i, 0)
    const = lambda i: (0, 0)
    return pl.pallas_call(
        _merge_kernel,
        out_shape=jax.ShapeDtypeStruct((m, D_MODEL), F32),
        grid=(m // tm,),
        in_specs=[pl.BlockSpec((tm, D_MODEL), row), pl.BlockSpec((1, D_MODEL), const),
                  pl.BlockSpec((D_MODEL, 3 * D_MODEL), const),
                  pl.BlockSpec((tm, 512), row), pl.BlockSpec((tm, 512), row), pl.BlockSpec((tm, 512), row),
                  pl.BlockSpec((512, D_MODEL), const), pl.BlockSpec((512, D_MODEL), const),
                  pl.BlockSpec((512, D_MODEL), const), pl.BlockSpec((D_MODEL, D_MODEL), const)],
        out_specs=pl.BlockSpec((tm, D_MODEL), row),
        compiler_params=_cparams("parallel"),
        name="merge_out",
    )(x, g_mix.reshape(1, D_MODEL), w_gates, ys, yf, ym, ws, wf, wm, wo)


def _cross_block_kernel(x_ref, g_ref, wq_ref, gq_ref, k_ref, v_ref, wo_ref, o_ref):
    x = x_ref[...]
    q = _dot(_rms(x, g_ref[...]).astype(BF16), wq_ref[...])
    heads = range(MEM_HEADS)
    hsl = [slice(h * MEM_HD, (h + 1) * MEM_HD) for h in heads]
    qh = [_rms(q[:, hsl[h]], gq_ref[...]).astype(BF16) for h in heads]
    s = [_dot_nt(qh[h], k_ref[:, hsl[h]].astype(BF16)) * (MEM_HD ** -0.5) for h in heads]
    m = [jnp.max(s[h], axis=-1, keepdims=True) for h in heads]
    p = [jnp.exp(s[h] - m[h]) for h in heads]
    l = [jnp.sum(p[h], axis=-1, keepdims=True) for h in heads]
    pv = [_dot(p[h].astype(BF16), v_ref[:, hsl[h]].astype(BF16)) for h in heads]
    outs = [(pv[h] / l[h]).astype(BF16) for h in heads]
    o_ref[...] = x + _dot(jnp.concatenate(outs, axis=1), wo_ref[...])


def cross_block(x, g, wq, gq, mem_k, mem_v, wo, nb, t, tq):
    nq = t // tq
    n_mem = mem_k.shape[0] // nb
    const = lambda b, i: (0, 0)
    row = lambda b, i: (b * nq + i, 0)
    return pl.pallas_call(
        _cross_block_kernel,
        out_shape=jax.ShapeDtypeStruct((nb * t, D_MODEL), F32),
        grid=(nb, nq),
        in_specs=[pl.BlockSpec((tq, D_MODEL), row), pl.BlockSpec((1, D_MODEL), const),
                  pl.BlockSpec((D_MODEL, MEM_WIDTH), const), pl.BlockSpec((1, MEM_HD), const),
                  pl.BlockSpec((n_mem, MEM_WIDTH), lambda b, i: (b, 0)),
                  pl.BlockSpec((n_mem, MEM_WIDTH), lambda b, i: (b, 0)),
                  pl.BlockSpec((MEM_WIDTH, D_MODEL), const)],
        out_specs=pl.BlockSpec((tq, D_MODEL), row),
        compiler_params=_cparams("parallel", "parallel"),
        name="cross_block",
    )(x, g.reshape(1, D_MODEL), wq, gq.reshape(1, MEM_HD), mem_k, mem_v, wo)


def _cross_cached_kernel(q_ref, k_ref, v_ref, o_ref):
    nseq = k_ref.shape[1]
    tq = q_ref.shape[0] // nseq
    seqs = range(nseq)
    n_cols = k_ref.shape[2]
    same_head = (lax.broadcasted_iota(jnp.int32, (MEM_HEADS * tq, n_cols), 0) // tq ==
                 lax.broadcasted_iota(jnp.int32, (MEM_HEADS * tq, n_cols), 1) % MEM_HEADS)
    qr = [jnp.concatenate([q_ref[sq * tq:(sq + 1) * tq, h * MEM_HD:(h + 1) * MEM_HD] for h in range(MEM_HEADS)],
                          axis=0).astype(BF16) for sq in seqs]
    s = [jnp.where(same_head, _dot_nt(qr[sq], k_ref[0, sq].astype(BF16)) * (MEM_HD ** -0.5), -jnp.inf)
         for sq in seqs]
    m = [jnp.max(s[sq], axis=-1, keepdims=True) for sq in seqs]
    p = [jnp.exp(s[sq] - m[sq]) for sq in seqs]
    l = [jnp.sum(p[sq], axis=-1, keepdims=True) for sq in seqs]
    o = [_dot(p[sq].astype(BF16), v_ref[0, sq].astype(BF16)) / l[sq] for sq in seqs]
    for sq in seqs:
        for h in range(MEM_HEADS):
            o_ref[sq * tq:(sq + 1) * tq, h * MEM_HD:(h + 1) * MEM_HD] = o[sq][h * tq:(h + 1) * tq]


CROSS_CACHED_SEQS = 4


def cross_attend_cached(q, mem_k, mem_v, nb, t, layer):
    ns = CROSS_CACHED_SEQS
    kv_spec = pl.BlockSpec((1, ns, mem_k.shape[2], MEM_HD), lambda b: (layer, b, 0, 0))
    return pl.pallas_call(
        _cross_cached_kernel,
        out_shape=jax.ShapeDtypeStruct((nb * t, MEM_WIDTH), F32),
        grid=(nb // ns,),
        in_specs=[pl.BlockSpec((ns * t, MEM_WIDTH), lambda b: (b, 0)), kv_spec, kv_spec],
        out_specs=pl.BlockSpec((ns * t, MEM_WIDTH), lambda b: (b, 0)),
        compiler_params=_cparams("parallel"),
        name="cross_attend_cached",
    )(q, mem_k, mem_v)


def _proj_residual_kernel(x_ref, a_ref, w_ref, o_ref):
    o_ref[...] = x_ref[...] + _dot(a_ref[...].astype(BF16), w_ref[...])


def proj_residual(x, a, w, tm=512):
    m, n = x.shape
    k = a.shape[1]
    tm = min(tm, m)
    return pl.pallas_call(
        _proj_residual_kernel,
        out_shape=jax.ShapeDtypeStruct((m, n), F32),
        grid=(m // tm,),
        in_specs=[pl.BlockSpec((tm, n), lambda i: (i, 0)), pl.BlockSpec((tm, k), lambda i: (i, 0)),
                  pl.BlockSpec((k, n), lambda i: (0, 0))],
        out_specs=pl.BlockSpec((tm, n), lambda i: (i, 0)),
        compiler_params=_cparams("parallel"),
        name="proj_residual",
    )(x, a, w)


def _mlp_kernel(x_ref, g_ref, wu_ref, wd_ref, o_ref, hn_ref, acc_ref):
    f = pl.program_id(1)

    @pl.when(f == 0)
    def _():
        hn_ref[...] = _rms(x_ref[...], g_ref[...]).astype(BF16)
        acc_ref[...] = jnp.zeros_like(acc_ref)

    a = jnp.maximum(_dot(hn_ref[...], wu_ref[...]), 0.0)
    acc_ref[...] += _dot((a * a).astype(BF16), wd_ref[...])

    @pl.when(f == pl.num_programs(1) - 1)
    def _():
        o_ref[...] = x_ref[...] + acc_ref[...]


def mlp(x, g, wu, wd, tm=1024, tf=1024):
    m, d = x.shape
    dff = wu.shape[1]
    tm = min(tm, m)
    return pl.pallas_call(
        _mlp_kernel,
        out_shape=jax.ShapeDtypeStruct((m, d), F32),
        grid=(m // tm, dff // tf),
        in_specs=[pl.BlockSpec((tm, d), lambda i, f: (i, 0)), pl.BlockSpec((1, d), lambda i, f: (0, 0)),
                  pl.BlockSpec((d, tf), lambda i, f: (0, f)), pl.BlockSpec((tf, d), lambda i, f: (f, 0))],
        out_specs=pl.BlockSpec((tm, d), lambda i, f: (i, 0)),
        scratch_shapes=[pltpu.VMEM((tm, d), BF16), pltpu.VMEM((tm, d), F32)],
        compiler_params=_cparams("parallel", "arbitrary"),
        name="mlp",
    )(x, g.reshape(1, d), wu, wd)


def _pack_w_in(w_in):
    offs = np.concatenate([[0], np.cumsum(SPLITS)])
    col = lambda i: w_in[:, int(offs[i]):int(offs[i + 1])]
    s5, fq, fk, fv, ff, mq, mk, mv, mi, mf, mo, gates = [col(i) for i in range(12)]
    pad = jnp.zeros((w_in.shape[0], 512 - FOX_HEADS - 2 * ML_HEADS), w_in.dtype)
    w_a = jnp.concatenate([s5, fq, fk, fv, mo, ff, mi, mf, pad], axis=1).astype(BF16)
    w_b = jnp.concatenate([mq, mk, mv], axis=1).astype(BF16)
    return w_a, w_b, gates.astype(BF16)


def _layer_weights(l, g_mix, w_in, s5_a_re, s5_a_im, s5_log_step, s5_b_re, s5_b_im, s5_c_re, s5_c_im, s5_d,
                   s5_w_glu, s5_b_glu, fox_gq, fox_gk, fox_bf, ml_bi, ml_bf, ml_gn, w_br_s5, w_br_fox, w_br_ml,
                   w_out, g_cross, w_cq, cross_gq, g_mem, w_mk, w_mv, cross_gk, w_co, g_mlp, w_up, w_down):
    bias_row = jnp.zeros((1, LANES), F32)
    bias_row = bias_row.at[0, LANE_FOXF:LANE_FOXF + FOX_HEADS].set(fox_bf[l])
    bias_row = bias_row.at[0, LANE_MLI:LANE_MLI + ML_HEADS].set(ml_bi[l])
    bias_row = bias_row.at[0, LANE_MLF:LANE_MLF + ML_HEADS].set(ml_bf[l])
    w_a, w_b, w_gates = _pack_w_in(w_in[l])
    return dict(
        g_mix=g_mix[l], w_a=w_a, w_b=w_b, w_gates=w_gates,
        s5=s5_params(s5_a_re[l], s5_a_im[l], s5_log_step[l], s5_b_re[l], s5_b_im[l], s5_c_re[l], s5_c_im[l],
                     s5_d[l], s5_w_glu[l], s5_b_glu[l], (SEQ_TILE // SUBLANES, SAMPLE_T // SUBLANES)),
        gq=jnp.tile(fox_gq[l], FOX_HEADS).reshape(1, FOX_WIDTH),
        gk=jnp.tile(fox_gk[l], FOX_HEADS).reshape(1, FOX_WIDTH),
        bias_row=bias_row, ml_gn=ml_gn[l],
        w_br_s5=w_br_s5[l].astype(BF16), w_br_fox=w_br_fox[l].astype(BF16), w_br_ml=w_br_ml[l].astype(BF16),
        w_out=w_out[l].astype(BF16), g_cross=g_cross[l], w_cq=w_cq[l].astype(BF16), cross_gq=cross_gq[l],
        g_mem=g_mem[l], w_mk=w_mk[l].astype(BF16), w_mv=w_mv[l].astype(BF16), cross_gk=cross_gk[l],
        w_co=w_co[l].astype(BF16), g_mlp=g_mlp[l], w_up=w_up[l].astype(BF16), w_down=w_down[l].astype(BF16))


def _hybrid_layer(x, W, nb, t, seq_tile, t_valid, s5_state, ml_state, mem_k, mem_v, fox_attend, augment,
                  mem_layer=None):
    z = norm_matmul(x, W["g_mix"], W["w_a"], tm=min(1024, nb * t))
    zb = norm_matmul(x, W["g_mix"], W["w_b"], tm=min(1024, nb * t), tn=W["w_b"].shape[1],
                     out_dtype=BF16 if seq_tile % MIN_BF16_ROWS == 0 else F32)
    prep = gate_prep(z, W["gq"], W["gk"], W["bias_row"], nb, t, seq_tile, t_valid, augment)
    kn, g, cs = prep[1:4]
    grow, csrow = prep[4:6] if augment else (None, None)
    y_s5, s5_re, s5_im = s5_mixer(z, s5_state[0], s5_state[1], W["s5"], nb, t, seq_tile, t_valid - 1,
                                  seqs=S5_SHORT_SEQS if t == SUBLANES else 1)
    y_fox = fox_attend(prep, z)
    y_ml, c_new, n_new, m_new = mlstm(z, zb, g, cs, grow, csrow, ml_state[0], ml_state[1], ml_state[2], W["ml_gn"],
                                      nb, t, seq_tile, 0 if mem_layer is None else mem_layer)
    x = merge_out(x, W["g_mix"], W["w_gates"], y_s5, y_fox, y_ml, W["w_br_s5"], W["w_br_fox"], W["w_br_ml"],
                  W["w_out"])
    if mem_layer is None:
        x = cross_block(x, W["g_cross"], W["w_cq"], W["cross_gq"], mem_k, mem_v, W["w_co"], nb, t, seq_tile)
    else:
        qc = norm_matmul(x, W["g_cross"], W["w_cq"], head_gain=W["cross_gq"])
        oc = cross_attend_cached(qc, mem_k, mem_v, nb, t, mem_layer)
        x = proj_residual(x, oc, W["w_co"])
    x = mlp(x, W["g_mlp"], W["w_up"], W["w_down"])
    if augment:
        fox_k = prep[1].reshape(nb, FOX_HEADS, FOX_HD, t).transpose(0, 3, 1, 2)
        fox_v = prep[8].reshape(nb, FOX_HEADS, FOX_HD, t).transpose(0, 3, 1, 2)
    else:
        fox_k = kn.reshape(nb, t, FOX_HEADS, FOX_HD)
        fox_v = z[:, COL_FV * 512:(COL_FV + 1) * 512].reshape(nb, t, FOX_HEADS, FOX_HD)
    return x, fox_k, fox_v, g, s5_re.reshape(nb, S5_GROUPS, S5_STATE), s5_im.reshape(nb, S5_GROUPS, S5_STATE), \
        c_new, n_new, m_new


def kernel(x_prompt, x_sample, mem_prompt, cache_fox_k, cache_fox_v, cache_fox_logf, page_table, state_s5_re, state_s5_im, state_mlstm_C, state_mlstm_n, state_mlstm_m, cache_mem_k, cache_mem_v, g_mix, w_in, s5_a_re, s5_a_im, s5_log_step, s5_b_re, s5_b_im, s5_c_re, s5_c_im, s5_d, s5_w_glu, s5_b_glu, fox_gq, fox_gk, fox_bf, ml_bi, ml_bf, ml_gn, w_br_s5, w_br_fox, w_br_ml, w_out, g_cross, w_cq, cross_gq, g_mem, w_mk, w_mv, cross_gk, w_co, g_mlp, w_up, w_down):
    depth = w_in.shape[0]
    bp, tp, _ = x_prompt.shape
    bs, ts, _ = x_sample.shape
    n_mem = mem_prompt.shape[1]

    xp = x_prompt.reshape(bp * tp, D_MODEL)
    xs = jnp.pad(x_sample, ((0, 0), (0, SAMPLE_T - ts), (0, 0))).reshape(bs * SAMPLE_T, D_MODEL)
    mem = mem_prompt.reshape(bp * n_mem, D_MODEL)
    zeros_p = (jnp.zeros((bp, S5_LANES), F32), jnp.zeros((bp, S5_LANES), F32))
    zeros_ml = (jnp.zeros((1, bp, ML_HEADS, ML_HD, ML_HD), F32), jnp.zeros((bp, ML_HEADS, ML_HD), F32),
                jnp.zeros((bp, ML_HEADS), F32))
    pool_kt = jnp.transpose(cache_fox_k, (0, 1, 3, 4, 2))
    pool_vt = jnp.transpose(cache_fox_v, (0, 1, 3, 4, 2))
    pool_ft = jnp.transpose(cache_fox_logf, (0, 1, 3, 2))
    st_p, st_s = [], []
    for l in range(depth):
        W = _layer_weights(l, g_mix, w_in, s5_a_re, s5_a_im, s5_log_step, s5_b_re, s5_b_im, s5_c_re, s5_c_im, s5_d,
                           s5_w_glu, s5_b_glu, fox_gq, fox_gk, fox_bf, ml_bi, ml_bf, ml_gn, w_br_s5, w_br_fox,
                           w_br_ml, w_out, g_cross, w_cq, cross_gq, g_mem, w_mk, w_mv, cross_gk, w_co, g_mlp, w_up,
                           w_down)
        mk_p = norm_matmul(mem, W["g_mem"], W["w_mk"], head_gain=W["cross_gk"])
        mv_p = norm_matmul(mem, W["g_mem"], W["w_mv"])

        def flash(prep, z):
            return fox_flash(prep[0], prep[6], prep[7], bp, tp, FLASH_TILE)

        xp, fox_k, fox_v, g, s5r, s5i, c_new, n_new, m_new = _hybrid_layer(
            xp, W, bp, tp, SEQ_TILE, tp, zeros_p, zeros_ml, mk_p, mv_p, flash, True)
        st_p.append((fox_k, fox_v,
                     g[:, LANE_FOXF:LANE_FOXF + FOX_HEADS].reshape(bp, tp, FOX_HEADS),
                     s5r, s5i, c_new, n_new, m_new,
                     mk_p.reshape(bp, n_mem, MEM_HEADS, MEM_HD), mv_p.reshape(bp, n_mem, MEM_HEADS, MEM_HD)))

        def decode(prep, z, layer=l):
            return fox_decode(page_table, prep[0], prep[1], z, prep[3], pool_kt, pool_vt, pool_ft, layer)

        xs, fox_k, fox_v, g, s5r, s5i, c_new, n_new, m_new = _hybrid_layer(
            xs, W, bs, SAMPLE_T, SAMPLE_T, ts,
            (state_s5_re[l].reshape(bs, S5_LANES), state_s5_im[l].reshape(bs, S5_LANES)),
            (state_mlstm_C, state_mlstm_n[l], state_mlstm_m[l]),
            cache_mem_k.reshape(depth, bs, n_mem * MEM_HEADS, MEM_HD),
            cache_mem_v.reshape(depth, bs, n_mem * MEM_HEADS, MEM_HD), decode, False, mem_layer=l)
        st_s.append((fox_k[:, :ts], fox_v[:, :ts],
                     g[:, LANE_FOXF:LANE_FOXF + FOX_HEADS].reshape(bs, SAMPLE_T, FOX_HEADS)[:, :ts],
                     s5r, s5i, c_new, n_new, m_new))
    outs_p = [jnp.stack(a) for a in zip(*st_p)]
    outs_s = [jnp.stack(a) for a in zip(*st_s)]
    yp = xp.reshape(bp, tp, D_MODEL)
    ys = xs.reshape(bs, SAMPLE_T, D_MODEL)[:, :ts]
    return (yp, ys, *outs_p, *outs_s)
```

```python
import functools
import math

import jax
import jax.numpy as jnp
import numpy as np
from jax import lax
from jax.experimental import pallas as pl
from jax.experimental.pallas import tpu as pltpu

F32 = jnp.float32
BF16 = jnp.bfloat16

LANES = 128
SUBLANES = 8
MIN_BF16_ROWS = 16
MXU_DIM = 256
VMEM_LIMIT_BYTES = 48 * 1024 * 1024

D_MODEL = 1024
S5_WIDTH = 512
S5_GROUP = 16
S5_GROUPS = 32
S5_STATE = 64
S5_LANES = S5_GROUPS * S5_STATE
FOX_HEADS = 8
FOX_HD = 64
FOX_WIDTH = 512
ML_HEADS = 4
ML_HD = 128
ML_WIDTH = 512
MEM_HEADS = 4
MEM_HD = 128
MEM_WIDTH = 512
D_FF = 4096
EPS = 1e-6
LOG2E = math.log2(math.e)
SPLITS =(S5_WIDTH, FOX_WIDTH, FOX_WIDTH, FOX_WIDTH, FOX_HEADS, ML_WIDTH, ML_WIDTH, ML_WIDTH,
          ML_HEADS, ML_HEADS, ML_WIDTH, 3 * D_MODEL)

COL_S5, COL_FQ, COL_FK, COL_FV, COL_MO, COL_SMALL = range(6)
COL_MQ, COL_MK, COL_MV = range(3)
LANE_FOXF = 0
LANE_MLI = 8
LANE_MLF = 12
SEQ_TILE = 256
FLASH_TILE = 512
FLASH_ROWS = 128
SAMPLE_T = 8


def _cparams(*sem):
    return pltpu.CompilerParams(dimension_semantics=sem, vmem_limit_bytes=VMEM_LIMIT_BYTES)


def _dot(a, b):
    return jnp.dot(a, b, preferred_element_type=F32)


def _dot_nt(a, b):
    return lax.dot_general(a, b, (((1,), (1,)), ((), ())), preferred_element_type=F32)


def _dot_tn(a, b):
    return lax.dot_general(a, b, (((0,), (0,)), ((), ())), preferred_element_type=F32)


def _split3(x):
    hi = x.astype(BF16)
    r1 = x - hi.astype(F32)
    mid = r1.astype(BF16)
    lo = (r1 - mid.astype(F32)).astype(BF16)
    return hi, mid, lo


def _dot_exact_rhs(x, ones_rhs):
    hi, mid, lo = _split3(x)
    return _dot(hi, ones_rhs) + _dot(mid, ones_rhs) + _dot(lo, ones_rhs)


def _dot_exact_lhs(ones_lhs, x):
    hi, mid, lo = _split3(x)
    return _dot(ones_lhs, hi) + _dot(ones_lhs, mid) + _dot(ones_lhs, lo)


def _pad_rows(x, n, fill=0.0):
    if x.shape[0] >= n:
        return x
    return jnp.concatenate([x, jnp.full((n - x.shape[0], x.shape[1]), fill, x.dtype)], axis=0)


def _lane_tile(x, n):
    return x if n == 1 else jnp.concatenate([x] * n, axis=1)


def _log_sigmoid(a):
    return jnp.minimum(a, 0.0) - jnp.log1p(jnp.exp(-jnp.abs(a)))


def _sigmoid(a):
    return 1.0 / (1.0 + jnp.exp(-a))


def _gelu_tanh(x):
    c = math.sqrt(2.0 / math.pi)
    return 0.5 * x * (1.0 + jnp.tanh(c * (x + 0.044715 * (x * x * x))))


def _rms(x, g):
    ms = jnp.mean(x * x, axis=-1, keepdims=True)
    return x * lax.rsqrt(ms + EPS) * g


def _norm_matmul_kernel(x_ref, g_ref, w_ref, *rest, head_norm):
    if head_norm:
        hg_ref, o_ref, hn_ref = rest
    else:
        o_ref, hn_ref = rest

    @pl.when(pl.program_id(1) == 0)
    def _():
        hn_ref[...] = _rms(x_ref[...], g_ref[...]).astype(BF16)

    y = _dot(hn_ref[...], w_ref[...])
    if head_norm:
        tn = y.shape[1]
        for s in range(tn // LANES):
            sl = slice(s * LANES, (s + 1) * LANES)
            o_ref[:, sl] = _rms(y[:, sl], hg_ref[...]).astype(o_ref.dtype)
    else:
        o_ref[...] = y.astype(o_ref.dtype)


def norm_matmul(x, g, w, head_gain=None, out_dtype=F32, tm=512, tn=512):
    m, k = x.shape
    n = w.shape[1]
    tm = min(tm, m)
    tn = min(tn, n)
    head_norm = head_gain is not None
    in_specs = [pl.BlockSpec((tm, k), lambda i, j: (i, 0)),
                pl.BlockSpec((1, k), lambda i, j: (0, 0)),
                pl.BlockSpec((k, tn), lambda i, j: (0, j))]
    args = [x, g.reshape(1, k), w]
    if head_norm:
        in_specs.append(pl.BlockSpec((1, LANES), lambda i, j: (0, 0)))
        args.append(head_gain.reshape(1, LANES))
    return pl.pallas_call(
        functools.partial(_norm_matmul_kernel, head_norm=head_norm),
        out_shape=jax.ShapeDtypeStruct((m, n), out_dtype),
        grid=(m // tm, n // tn),
        in_specs=in_specs,
        out_specs=pl.BlockSpec((tm, tn), lambda i, j: (i, j)),
        scratch_shapes=[pltpu.VMEM((tm, k), BF16)],
        compiler_params=_cparams("parallel", "arbitrary"),
        name="norm_matmul",
    )(*args)


def _prep_kernel(fq_ref, fk_ref, fv_ref, sm_ref, gq_ref, gk_ref, bias_ref, gmat_ref, tril_ref, sel_ref, aug_ref,
                 *rest, seq_rows, t_valid, augment):
    if augment:
        qx_ref, kn_ref, g_ref, cs_ref, grow_ref, csrow_ref, kx_ref, vb_ref, vt_ref, carry_ref = rest
    else:
        qx_ref, kn_ref, g_ref, cs_ref, carry_ref = rest
    c = pl.program_id(1)
    tc = sm_ref.shape[0]

    @pl.when(c == 0)
    def _():
        carry_ref[...] = jnp.zeros_like(carry_ref)

    gmat = gmat_ref[...]

    def head_rms(x, gain):
        x2 = x * x
        hi = x2.astype(BF16)
        lo = (x2 - hi.astype(F32)).astype(BF16)
        ss = (_dot(hi, gmat) + _dot(lo, gmat)) * (1.0 / FOX_HD)
        return x * lax.rsqrt(ss + EPS) * gain

    qn = head_rms(fq_ref[...], gq_ref[...]) * ((FOX_HD ** -0.5) * (LOG2E if augment else 1.0))
    kn = head_rms(fk_ref[...], gk_ref[...])
    if augment:
        kn_ref[...] = kn.T
        vt_ref[...] = fv_ref[...].T
    else:
        kn_ref[...] = kn

    a = sm_ref[...] + bias_ref[...]
    lane = lax.broadcasted_iota(jnp.int32, a.shape, 1)
    pos = (lax.broadcasted_iota(jnp.int32, a.shape, 0) + c * tc) % seq_rows
    is_i = (lane >= LANE_MLI) & (lane < LANE_MLF)
    used = lane < LANE_MLF + ML_HEADS
    valid = pos < t_valid
    g = jnp.where(is_i, a, _log_sigmoid(a))
    g = jnp.where(used, g, 0.0)
    g = jnp.where(valid, g, jnp.where(is_i, -jnp.inf, 0.0))
    gc = jnp.where(is_i, 0.0, g)
    cs = _dot_exact_lhs(tril_ref[...], gc)
    carry = carry_ref[...]
    csg = cs + jnp.where(lane < FOX_HEADS, carry, 0.0)
    carry_ref[...] = carry + cs[tc - 1:tc, :]
    g_ref[...] = g
    cs_ref[...] = csg
    if augment:
        grow_ref[...] = g.T
        csrow_ref[...] = csg.T

    nx = FOX_HEADS * LANES
    lanex = lax.broadcasted_iota(jnp.int32, (1, nx), 1)
    keep = ((lanex % LANES) >= FOX_HD) == ((lanex // LANES) % 2 == 1)
    q_exp = jnp.concatenate([qn[:, (h // 2) * LANES:(h // 2 + 1) * LANES] for h in range(FOX_HEADS)], axis=1)
    if not augment:
        qx_ref[...] = jnp.where(keep, q_exp, 0.0).astype(qx_ref.dtype)
        return
    k_exp = jnp.concatenate([kn[:, (h // 2) * LANES:(h // 2 + 1) * LANES] for h in range(FOX_HEADS)], axis=1)
    hi, mid, lo = _split3(jnp.where(lane < FOX_HEADS, csg * LOG2E, 0.0))
    packed = (hi.astype(F32) + pltpu.roll(mid.astype(F32), FOX_HEADS, 1)
              + pltpu.roll(lo.astype(F32), 2 * FOX_HEADS, 1)).astype(BF16)
    aug = aug_ref[...] + _dot(packed, sel_ref[...])
    qx_ref[...] = jnp.where(keep, q_exp, aug[:, 0:nx]).astype(qx_ref.dtype)
    kx_ref[...] = jnp.where(keep, k_exp, aug[:, nx:2 * nx]).astype(kx_ref.dtype)
    vb_ref[...] = fv_ref[...].astype(vb_ref.dtype)


PREP_SHORT_ROWS = 128


def gate_prep(z, gq, gk, bias_row, nb, t, tc, t_valid, augment):
    m = nb * t
    gmat = jnp.asarray(np.kron(np.eye(FOX_HEADS), np.ones((FOX_HD, FOX_HD))), BF16)
    if augment:
        nc, tp, grid_rows = t // tc, tc, nb
        tril = np.tril(np.ones((tc, tc)))
    else:
        tc = tp = PREP_SHORT_ROWS
        nc, grid_rows = 1, m // tc
        tril = np.kron(np.eye(tc // t), np.tril(np.ones((t, t))))
    tril = jnp.asarray(tril, BF16)
    nx = FOX_HEADS * LANES
    sel = np.zeros((LANES, 2 * nx), np.float32)
    aug_const = np.zeros((1, 2 * nx), np.float32)
    for h in range(FOX_HEADS):
        o = h * LANES + (0 if h % 2 else FOX_HD)
        for p in range(3):
            sel[p * FOX_HEADS + h, o + p] = 1.0
            sel[p * FOX_HEADS + h, nx + o + 3 + p] = -1.0
            aug_const[0, o + 3 + p] = 1.0
            aug_const[0, nx + o + p] = 1.0
    sel, aug_const = jnp.asarray(sel, BF16), jnp.asarray(aug_const, F32)
    row_map = lambda b, c: (b * nc + c, 0)
    const = lambda b, c: (0, 0)
    feat_major = jax.ShapeDtypeStruct((nb * FOX_WIDTH, t), F32)
    feat_spec = pl.BlockSpec((FOX_WIDTH, tc), lambda b, c: (b, c))
    out_shape = [jax.ShapeDtypeStruct((m, FOX_HEADS * LANES), BF16 if augment else F32),
                 feat_major if augment else jax.ShapeDtypeStruct((m, FOX_WIDTH), F32),
                 jax.ShapeDtypeStruct((m, LANES), F32),
                 jax.ShapeDtypeStruct((m, LANES), F32)]
    out_specs = [pl.BlockSpec((tc, FOX_HEADS * LANES), row_map),
                 feat_spec if augment else pl.BlockSpec((tc, FOX_WIDTH), row_map),
                 pl.BlockSpec((tc, LANES), row_map),
                 pl.BlockSpec((tc, LANES), row_map)]
    if augment:
        out_shape += [jax.ShapeDtypeStruct((nb * LANES, t), F32), jax.ShapeDtypeStruct((nb * LANES, t), F32),
                      jax.ShapeDtypeStruct((m, FOX_HEADS * LANES), BF16), jax.ShapeDtypeStruct((m, FOX_WIDTH), BF16),
                      feat_major]
        out_specs += [pl.BlockSpec((LANES, tc), lambda b, c: (b, c)), pl.BlockSpec((LANES, tc), lambda b, c: (b, c)),
                      pl.BlockSpec((tc, FOX_HEADS * LANES), row_map), pl.BlockSpec((tc, FOX_WIDTH), row_map),
                      feat_spec]
    return pl.pallas_call(
        functools.partial(_prep_kernel, seq_rows=t, t_valid=t_valid, augment=augment),
        out_shape=tuple(out_shape),
        grid=(grid_rows, nc),
        in_specs=[pl.BlockSpec((tc, 512), lambda b, c: (b * nc + c, COL_FQ)),
                  pl.BlockSpec((tc, 512), lambda b, c: (b * nc + c, COL_FK)),
                  pl.BlockSpec((tc, 512), lambda b, c: (b * nc + c, COL_FV)),
                  pl.BlockSpec((tc, LANES), lambda b, c: (b * nc + c, COL_SMALL * 4)),
                  pl.BlockSpec((1, 512), const), pl.BlockSpec((1, 512), const),
                  pl.BlockSpec((1, LANES), const),
                  pl.BlockSpec((512, 512), const), pl.BlockSpec((tp, tp), const),
                  pl.BlockSpec((LANES, 2 * nx), const), pl.BlockSpec((1, 2 * nx), const)],
        out_specs=tuple(out_specs),
        scratch_shapes=[pltpu.VMEM((1, LANES), F32)],
        compiler_params=_cparams("parallel", "arbitrary"),
        name="gate_prep",
    )(z, z, z, z, gq, gk, bias_row, gmat, tril, sel, aug_const)


S5_SCAN_LANES = 1024
S5_SHORT_SEQS = 16


def _cmul(ar, ai, br, bi):
    return ar * br - ai * bi, ar * bi + ai * br


def _s5_kernel(u_ref, h0r_ref, h0i_ref, perm_ref, permt_ref, bre_ref, bim_ref, lam_ref, pseg_ref, pk_ref,
               cre_ref, cim_ref, d_ref, wglu_ref, bglu_ref, y_ref, hlr_ref, hli_ref, hr_ref, hi_ref, cr_ref, ci_ref,
               *, t_last, seqs):
    c = pl.program_id(1)
    nc = pl.num_programs(1)
    tc = u_ref.shape[0]
    R = tc // SUBLANES if seqs == 1 else 1
    half = S5_WIDTH // 2
    hl = S5_LANES // 2

    if seqs == 1:
        @pl.when(c == 0)
        def _():
            cr_ref[...] = h0r_ref[0]
            ci_ref[...] = h0i_ref[0]

    u = u_ref[...]
    ub = _pad_rows(u, MIN_BF16_ROWS).astype(BF16)
    if R > 1:
        ub = _dot(perm_ref[...], ub).astype(BF16)
    for j in range(2):
        uj = ub[:, j * half:(j + 1) * half]
        hr_ref[:, j * hl:(j + 1) * hl] = _dot(uj, bre_ref[j])[0:tc]
        hi_ref[:, j * hl:(j + 1) * hl] = _dot(uj, bim_ref[j])[0:tc]

    sub = lax.broadcasted_iota(jnp.int32, (SUBLANES, S5_SCAN_LANES), 0)
    for lc in range(S5_LANES // S5_SCAN_LANES):
        ls = slice(lc * S5_SCAN_LANES, (lc + 1) * S5_SCAN_LANES)
        lam_r = jnp.broadcast_to(lam_ref[0:1, ls], sub.shape)
        lam_i = jnp.broadcast_to(lam_ref[1:2, ls], sub.shape)
        if seqs > 1:
            def seq_scan(g, _):
                rows = pl.ds(pl.multiple_of(g * SUBLANES, SUBLANES), SUBLANES)
                s_r = jnp.where(sub == 0, jnp.broadcast_to(h0r_ref[0, pl.ds(g, 1), ls], sub.shape), 0.0)
                s_i = jnp.where(sub == 0, jnp.broadcast_to(h0i_ref[0, pl.ds(g, 1), ls], sub.shape), 0.0)
                m_r, m_i = _cmul(lam_r, lam_i, s_r, s_i)
                x_r = hr_ref[rows, ls] + m_r
                x_i = hi_ref[rows, ls] + m_i
                for j, lag in enumerate((1, 2, 4)):
                    m_r, m_i = _cmul(pseg_ref[2 * j, :, ls], pseg_ref[2 * j + 1, :, ls],
                                     pltpu.roll(x_r, lag, 0), pltpu.roll(x_i, lag, 0))
                    x_r, x_i = x_r + m_r, x_i + m_i
                hr_ref[rows, ls] = x_r
                hi_ref[rows, ls] = x_i
                hlr_ref[0, pl.ds(g, 1), ls] = x_r[t_last:t_last + 1, :]
                hli_ref[0, pl.ds(g, 1), ls] = x_i[t_last:t_last + 1, :]
                return 0

            lax.fori_loop(0, seqs, seq_scan, 0, unroll=2)
            continue
        init_r = jnp.where(sub == 0, jnp.broadcast_to(cr_ref[:, ls], sub.shape), 0.0)
        init_i = jnp.where(sub == 0, jnp.broadcast_to(ci_ref[:, ls], sub.shape), 0.0)

        def local_step(k, carry):
            h_r, h_i = carry
            rows = pl.ds(pl.multiple_of(k * SUBLANES, SUBLANES), SUBLANES)
            m_r, m_i = _cmul(lam_r, lam_i, h_r, h_i)
            h_r = m_r + hr_ref[rows, ls]
            h_i = m_i + hi_ref[rows, ls]
            hr_ref[rows, ls] = h_r
            hi_ref[rows, ls] = h_i
            return h_r, h_i

        e_r, e_i = lax.fori_loop(0, R, local_step, (init_r, init_i), unroll=min(R, 8))
        for j, lag in enumerate((1, 2, 4)):
            m_r, m_i = _cmul(pseg_ref[2 * j, :, ls], pseg_ref[2 * j + 1, :, ls],
                             pltpu.roll(e_r, lag, 0), pltpu.roll(e_i, lag, 0))
            e_r, e_i = e_r + m_r, e_i + m_i
        cr_ref[:, ls] = e_r[SUBLANES - 1:SUBLANES, :]
        ci_ref[:, ls] = e_i[SUBLANES - 1:SUBLANES, :]
        in_r = jnp.where(sub == 0, 0.0, pltpu.roll(e_r, 1, 0))
        in_i = jnp.where(sub == 0, 0.0, pltpu.roll(e_i, 1, 0))

        def fix_step(k, _):
            rows = pl.ds(pl.multiple_of(k * SUBLANES, SUBLANES), SUBLANES)
            p_r = jnp.broadcast_to(pk_ref[0, pl.ds(k, 1), ls], sub.shape)
            p_i = jnp.broadcast_to(pk_ref[1, pl.ds(k, 1), ls], sub.shape)
            m_r, m_i = _cmul(p_r, p_i, in_r, in_i)
            hr_ref[rows, ls] += m_r
            hi_ref[rows, ls] += m_i
            return 0

        lax.fori_loop(0, R, fix_step, 0, unroll=min(R, 8))

    if seqs == 1:
        @pl.when(c == nc - 1)
        def _():
            tl = t_last % tc
            pos = (tl % R) * SUBLANES + tl // R
            hlr_ref[0] = hr_ref[pos:pos + 1, :]
            hli_ref[0] = hi_ref[pos:pos + 1, :]

    hrb = _pad_rows(hr_ref[...], MIN_BF16_ROWS).astype(BF16)
    hib = _pad_rows(hi_ref[...], MIN_BF16_ROWS).astype(BF16)
    ys = [(_dot(hrb[:, j * hl:(j + 1) * hl], cre_ref[j]) + _dot(hib[:, j * hl:(j + 1) * hl], cim_ref[j]))
          for j in range(2)]
    y = jnp.concatenate(ys, axis=1)
    if R > 1:
        y = _dot_exact_lhs(permt_ref[...], y)
    y = _gelu_tanh(y[0:tc] + d_ref[...] * u)
    gate = _dot(_pad_rows(y, MIN_BF16_ROWS).astype(BF16), wglu_ref[...])[0:tc]
    y_ref[...] = y * _sigmoid(gate + bglu_ref[...])


def s5_mixer(z, h0r, h0i, sp, nb, t, tc, t_last, seqs=1):
    assert seqs == 1 or (tc == SUBLANES and t == tc)
    nc = t // tc
    R = tc // SUBLANES
    tc = tc * seqs
    nbg = nb // seqs
    tperm = max(tc, MIN_BF16_ROWS) if seqs == 1 else MIN_BF16_ROWS
    perm = np.zeros((tperm, tperm), np.float32)
    if seqs == 1:
        for s in range(SUBLANES):
            for k in range(R):
                perm[k * SUBLANES + s, s * R + k] = 1.0
    permt = jnp.asarray(perm.T, BF16)
    perm = jnp.asarray(perm, BF16)
    const = lambda b, c: (0, 0)
    const3 = lambda b, c: (0, 0, 0)
    state_spec = pl.BlockSpec((1, seqs, S5_LANES), lambda b, c: (b, 0, 0))
    tabs = sp["tabs"][R]
    return pl.pallas_call(
        functools.partial(_s5_kernel, t_last=t_last, seqs=seqs),
        out_shape=(jax.ShapeDtypeStruct((nb * t, S5_WIDTH), F32),
                   jax.ShapeDtypeStruct((nbg, seqs, S5_LANES), F32),
                   jax.ShapeDtypeStruct((nbg, seqs, S5_LANES), F32)),
        grid=(nbg, nc),
        in_specs=[pl.BlockSpec((tc, 512), lambda b, c: (b * nc + c, COL_S5)),
                  state_spec, state_spec,
                  pl.BlockSpec((tperm, tperm), const), pl.BlockSpec((tperm, tperm), const),
                  pl.BlockSpec((2, S5_WIDTH // 2, S5_LANES // 2), const3),
                  pl.BlockSpec((2, S5_WIDTH // 2, S5_LANES // 2), const3),
                  pl.BlockSpec((2, S5_LANES), const),
                  pl.BlockSpec((6, SUBLANES, S5_LANES), const3),
                  pl.BlockSpec((2, R, S5_LANES), const3),
                  pl.BlockSpec((2, S5_LANES // 2, S5_WIDTH // 2), const3),
                  pl.BlockSpec((2, S5_LANES // 2, S5_WIDTH // 2), const3),
                  pl.BlockSpec((1, S5_WIDTH), const),
                  pl.BlockSpec((S5_WIDTH, S5_WIDTH), const), pl.BlockSpec((1, S5_WIDTH), const)],
        out_specs=(pl.BlockSpec((tc, S5_WIDTH), lambda b, c: (b * nc + c, 0)), state_spec, state_spec),
        scratch_shapes=[pltpu.VMEM((tc, S5_LANES), F32), pltpu.VMEM((tc, S5_LANES), F32),
                        pltpu.VMEM((1, S5_LANES), F32), pltpu.VMEM((1, S5_LANES), F32)],
        compiler_params=_cparams("parallel", "arbitrary"),
        name="s5_mixer",
    )(z, h0r.reshape(nbg, seqs, S5_LANES), h0i.reshape(nbg, seqs, S5_LANES), perm, permt, sp["bre"], sp["bim"],
      tabs["lam"],
      tabs["pseg"], tabs["pk"], sp["cre"], sp["cim"], sp["d"], sp["wglu"], sp["bglu"])


def s5_params(a_re, a_im, log_step, b_re, b_im, c_re, c_im, d_skip, w_glu, b_glu, seg_lens):
    dt = jnp.exp(log_step)[:, None]
    mag = jnp.exp(a_re * dt)
    lr = (mag * jnp.cos(a_im * dt)).reshape(1, S5_LANES)
    li = (mag * jnp.sin(a_im * dt)).reshape(1, S5_LANES)
    den = a_re * a_re + a_im * a_im
    xr, xi = lr.reshape(a_re.shape) - 1.0, li.reshape(a_re.shape)
    fr = (xr * a_re + xi * a_im) / den
    fi = (xi * a_re - xr * a_im) / den
    bbr = fr[..., None] * b_re - fi[..., None] * b_im
    bbi = fr[..., None] * b_im + fi[..., None] * b_re
    gh = S5_GROUPS // 2
    eye = jnp.eye(gh, dtype=F32)

    def in_mat(b):
        return jnp.einsum('jgpc,gh->jgchp', b.reshape(2, gh, S5_STATE, S5_GROUP), eye).reshape(
            2, S5_WIDTH // 2, S5_LANES // 2).astype(BF16)

    def out_mat(cm):
        return jnp.einsum('jgcp,gh->jgphc', cm.reshape(2, gh, S5_GROUP, S5_STATE), eye).reshape(
            2, S5_LANES // 2, S5_WIDTH // 2).astype(BF16)

    def powers(pr, pi, n):
        tr, ti, cnt = pr, pi, 1
        while cnt < n:
            lr_, li_ = tr[cnt - 1:cnt], ti[cnt - 1:cnt]
            nr, ni = _cmul(tr, ti, lr_, li_)
            tr, ti, cnt = jnp.concatenate([tr, nr], axis=0), jnp.concatenate([ti, ni], axis=0), 2 * cnt
        return tr, ti

    sub = jnp.arange(SUBLANES)[:, None]
    tabs = {}
    for R in seg_lens:
        kr, ki = powers(lr, li, R)
        sr, si = powers(kr[R - 1:R], ki[R - 1:R], 4)
        pseg = []
        for lag in (1, 2, 4):
            msk = (sub >= lag).astype(F32)
            pseg += [msk * sr[lag - 1:lag], msk * si[lag - 1:lag]]
        tabs[R] = dict(lam=jnp.concatenate([lr, li], axis=0), pseg=jnp.stack(pseg), pk=jnp.stack([kr, ki]))
    return dict(bre=in_mat(bbr), bim=in_mat(bbi), cre=out_mat(c_re), cim=out_mat(-c_im), tabs=tabs,
                d=d_skip.reshape(1, S5_WIDTH), wglu=w_glu.astype(BF16), bglu=b_glu.reshape(1, S5_WIDTH))


def _fox_flash_kernel(it_ref, jt_ref, qx_ref, kx_ref, v_ref, o_ref, m_ref, l_ref, acc_ref, s_ref, p_ref):
    i = it_ref[pl.program_id(1)]
    j = jt_ref[pl.program_id(1)]
    tq = qx_ref.shape[0]
    tk = kx_ref.shape[0]

    @pl.when(j == 0)
    def _():
        m_ref[...] = jnp.full_like(m_ref, -jnp.inf)
        l_ref[...] = jnp.zeros_like(l_ref)
        acc_ref[...] = jnp.zeros_like(acc_ref)

    def step(masked):
        nr, ncol = tq // FLASH_ROWS, tk // LANES
        if masked:
            diff = (lax.broadcasted_iota(jnp.int32, (FLASH_ROWS, LANES), 0) -
                    lax.broadcasted_iota(jnp.int32, (FLASH_ROWS, LANES), 1))
        for h in range(FOX_HEADS):
            hs = slice(h * LANES, (h + 1) * LANES)
            ps = slice((h // 2) * LANES, (h // 2 + 1) * LANES)
            s_buf, p_buf = s_ref.at[h % 2], p_ref.at[h % 2]
            if h == 0:
                s_buf[...] = _dot_nt(qx_ref[:, hs], kx_ref[:, hs])
            if h + 1 < FOX_HEADS:
                nhs = slice((h + 1) * LANES, (h + 2) * LANES)
                s_ref[(h + 1) % 2] = _dot_nt(qx_ref[:, nhs], kx_ref[:, nhs])
            for r in range(nr):
                rs = slice(r * FLASH_ROWS, (r + 1) * FLASH_ROWS)

                live = [cidx for cidx in range(ncol) if not (masked and cidx * LANES >= (r + 1) * FLASH_ROWS)]

                def piece(cidx):
                    sc = s_buf[rs, cidx * LANES:(cidx + 1) * LANES]
                    if masked and (cidx + 1) * LANES - 1 > r * FLASH_ROWS:
                        sc = jnp.where(diff >= (cidx * LANES - r * FLASH_ROWS), sc, -jnp.inf)
                    return sc

                mx = piece(live[0])
                for cidx in live[1:]:
                    mx = jnp.maximum(mx, piece(cidx))
                m_prev = m_ref[h, rs, :]
                m_new = jnp.maximum(m_prev, jnp.max(mx, axis=-1, keepdims=True))
                alpha = jnp.exp2(m_prev - m_new)
                lsum = jnp.zeros((FLASH_ROWS, LANES), F32)
                for cidx in range(ncol):
                    cs_ = slice(cidx * LANES, (cidx + 1) * LANES)
                    if cidx not in live:
                        p_buf[rs, cs_] = jnp.zeros((FLASH_ROWS, LANES), BF16)
                        continue
                    pc = jnp.exp2(piece(cidx) - m_new)
                    lsum = lsum + pc
                    p_buf[rs, cs_] = pc.astype(BF16)
                l_ref[h, rs, :] = alpha * l_ref[h, rs, :] + lsum
                acc_ref[h, rs, :] = alpha * acc_ref[h, rs, :]
                m_ref[h, rs, :] = m_new
            acc_ref[h] += _dot(p_buf[...], v_ref[:, ps])

    @pl.when(j < i)
    def _():
        step(False)

    @pl.when(j == i)
    def _():
        step(True)
        lane = lax.broadcasted_iota(jnp.int32, (1, LANES), 1)
        for p in range(FOX_HEADS // 2):
            lo = acc_ref[2 * p] / jnp.sum(l_ref[2 * p], axis=-1, keepdims=True)
            hi = acc_ref[2 * p + 1] / jnp.sum(l_ref[2 * p + 1], axis=-1, keepdims=True)
            o_ref[:, p * LANES:(p + 1) * LANES] = jnp.where(lane < FOX_HD, lo, hi)


def fox_flash(qx, kx, vb, nb, t, tile):
    assert tile % FLASH_ROWS == 0
    nt = t // tile
    pairs = [(i, j) for i in range(nt) for j in range(i + 1)]
    it = jnp.asarray([p[0] for p in pairs], jnp.int32)
    jt = jnp.asarray([p[1] for p in pairs], jnp.int32)
    return pl.pallas_call(
        _fox_flash_kernel,
        out_shape=jax.ShapeDtypeStruct((nb * t, FOX_WIDTH), F32),
        grid_spec=pltpu.PrefetchScalarGridSpec(
            num_scalar_prefetch=2,
            grid=(nb, len(pairs)),
            in_specs=[pl.BlockSpec((tile, FOX_HEADS * LANES), lambda b, p, it, jt: (b * nt + it[p], 0)),
                      pl.BlockSpec((tile, FOX_HEADS * LANES), lambda b, p, it, jt: (b * nt + jt[p], 0)),
                      pl.BlockSpec((tile, FOX_WIDTH), lambda b, p, it, jt: (b * nt + jt[p], 0))],
            out_specs=pl.BlockSpec((tile, FOX_WIDTH), lambda b, p, it, jt: (b * nt + it[p], 0)),
            scratch_shapes=[pltpu.VMEM((FOX_HEADS, tile, LANES), F32), pltpu.VMEM((FOX_HEADS, tile, LANES), F32),
                            pltpu.VMEM((FOX_HEADS, tile, LANES), F32),
                            pltpu.VMEM((2, tile, tile), F32), pltpu.VMEM((2, tile, tile), BF16)]),
        compiler_params=_cparams("parallel", "arbitrary"),
        name="fox_flash",
    )(it, jt, qx, kx, vb)


def _fox_decode_kernel(pt_ref, qb_ref, kn_ref, vn_ref, cs_ref, *rest, n_pages):
    k_refs = rest[0:n_pages]
    v_refs = rest[n_pages:2 * n_pages]
    f_refs = rest[2 * n_pages:3 * n_pages]
    triu_ref, o_ref, kpad_ref, vpad_ref = rest[3 * n_pages:]
    page = kpad_ref.shape[0]
    nrow = FOX_HEADS * SAMPLE_T
    qb = qb_ref[...].astype(BF16)

    def per_query(x):
        return jnp.concatenate([x] * SAMPLE_T, axis=0)

    fcat = jnp.concatenate([f_refs[i][0, 0] for i in range(n_pages)], axis=0)
    cum_in = _dot_exact_rhs(fcat, triu_ref[...])
    totals = jnp.broadcast_to(cum_in[:, page - 1:page], cum_in.shape)
    off = jnp.zeros((FOX_HEADS, page), F32)
    ss = []
    for i in range(n_pages):
        rows = slice(i * FOX_HEADS, (i + 1) * FOX_HEADS)
        kt = k_refs[i][0, 0].reshape(FOX_WIDTH, page).astype(BF16)
        ss.append(_dot(qb, kt) - per_query(cum_in[rows] + off))
        off = off + totals[rows]
    kpad_ref[...] = jnp.zeros_like(kpad_ref)
    vpad_ref[...] = jnp.zeros_like(vpad_ref)
    kpad_ref[0:SAMPLE_T, :] = kn_ref[...]
    vpad_ref[0:SAMPLE_T, :] = vn_ref[...]
    csrow = _pad_rows(cs_ref[...], page).T[LANE_FOXF:LANE_FOXF + FOX_HEADS]
    s_new = _dot_nt(qb, kpad_ref[...].astype(BF16)) - per_query(csrow + off)
    tq_idx = lax.broadcasted_iota(jnp.int32, (nrow, page), 0) // FOX_HEADS
    tk_idx = lax.broadcasted_iota(jnp.int32, (nrow, page), 1)
    ss.append(jnp.where(tk_idx <= tq_idx, s_new, -jnp.inf))

    m = ss[0]
    for s in ss[1:]:
        m = jnp.maximum(m, s)
    m = jnp.max(m, axis=-1, keepdims=True)
    lsum = jnp.zeros((nrow, page), F32)
    acc = jnp.zeros((nrow, FOX_WIDTH), F32)
    for i, s in enumerate(ss):
        p = jnp.exp(s - m)
        lsum = lsum + p
        if i < n_pages:
            acc = acc + _dot_nt(p.astype(BF16), v_refs[i][0, 0].reshape(FOX_WIDTH, page).astype(BF16))
        else:
            acc = acc + _dot(p.astype(BF16), vpad_ref[...].astype(BF16))
    acc = acc / jnp.sum(lsum, axis=-1, keepdims=True)
    lane = lax.broadcasted_iota(jnp.int32, (nrow, FOX_WIDTH), 1) // FOX_HD
    head = lax.broadcasted_iota(jnp.int32, (nrow, FOX_WIDTH), 0) % FOX_HEADS
    picked = jnp.where(lane == head, acc, 0.0)
    o_ref[...] = jnp.sum(picked.reshape(SAMPLE_T, FOX_HEADS, FOX_WIDTH), axis=1)


def fox_decode(page_table, qx, kn, z, cs, pool_kt, pool_vt, pool_ft, layer):
    nb, n_pages = page_table.shape
    page = pool_kt.shape[-1]
    assert page == LANES
    nrow = FOX_HEADS * SAMPLE_T
    pt = page_table.reshape(-1)
    triu = jnp.asarray(np.triu(np.ones((page, page))), BF16)
    pair_of_head = jnp.asarray(np.arange(FOX_HEADS)[:, None] // 2 == np.arange(FOX_HEADS // 2)[None, :], F32)
    qb = (qx.reshape(nb, SAMPLE_T, FOX_HEADS, 1, LANES) * pair_of_head[None, None, :, :, None]).reshape(
        nb * nrow, FOX_WIDTH)

    def pg5(i):
        return lambda b, pt: (layer, pt[b * n_pages + i], 0, 0, 0)

    def pg4(i):
        return lambda b, pt: (layer, pt[b * n_pages + i], 0, 0)

    row = lambda b, pt: (b, 0)
    in_specs = [pl.BlockSpec((nrow, FOX_WIDTH), row),
                pl.BlockSpec((SAMPLE_T, FOX_WIDTH), row),
                pl.BlockSpec((SAMPLE_T, FOX_WIDTH), lambda b, pt: (b, COL_FV)),
                pl.BlockSpec((SAMPLE_T, LANES), row)]
    in_specs += [pl.BlockSpec((1, 1, FOX_HEADS, FOX_HD, page), pg5(i)) for i in range(n_pages)]
    in_specs += [pl.BlockSpec((1, 1, FOX_HEADS, FOX_HD, page), pg5(i)) for i in range(n_pages)]
    in_specs += [pl.BlockSpec((1, 1, FOX_HEADS, page), pg4(i)) for i in range(n_pages)]
    in_specs += [pl.BlockSpec((page, page), lambda b, pt: (0, 0))]
    return pl.pallas_call(
        functools.partial(_fox_decode_kernel, n_pages=n_pages),
        out_shape=jax.ShapeDtypeStruct((nb * SAMPLE_T, FOX_WIDTH), F32),
        grid_spec=pltpu.PrefetchScalarGridSpec(
            num_scalar_prefetch=1,
            grid=(nb,),
            in_specs=in_specs,
            out_specs=pl.BlockSpec((SAMPLE_T, FOX_WIDTH), row),
            scratch_shapes=[pltpu.VMEM((page, FOX_WIDTH), F32), pltpu.VMEM((page, FOX_WIDTH), F32)]),
        compiler_params=_cparams("parallel"),
        name="fox_decode",
    )(pt, qb, kn, z, cs, *([pool_kt] * n_pages), *([pool_vt] * n_pages), *([pool_ft] * n_pages), triu)


def _mlstm_kernel(q_ref, k_ref, v_ref, o_ref, g_ref, cs_ref, *rest, short):
    if short:
        c0_ref, n0_ref, m0_ref, gn_ref, y_ref, cout_ref, nout_ref, mout_ref, c_ref, n_ref, m_ref = rest
    else:
        (grow_ref, csrow_ref, c0_ref, n0_ref, m0_ref, gn_ref, y_ref, cout_ref, nout_ref, mout_ref,
         c_ref, n_ref, m_ref) = rest
    c = pl.program_id(1)
    nc = pl.num_programs(1)
    L = q_ref.shape[0]

    @pl.when(c == 0)
    def _():
        c_ref[...] = c0_ref[0, 0]
        n_ref[...] = n0_ref[0]
        m_ref[...] = m0_ref[0]

    Lk = max(L, LANES)
    Lq = max(L, MIN_BF16_ROWS)
    g = g_ref[...]
    cs = cs_ref[...]
    if short:
        grow = _pad_rows(g, Lk).T
        csrow = _pad_rows(cs, Lk).T
    else:
        grow = grow_ref[...]
        csrow = csrow_ref[...]
    causal = (lax.broadcasted_iota(jnp.int32, (Lq, Lk), 0) >= lax.broadcasted_iota(jnp.int32, (Lq, Lk), 1))
    scale = ML_HD ** -0.5
    heads = range(ML_HEADS)
    hsl = [slice(h * ML_HD, (h + 1) * ML_HD) for h in heads]
    qb = [_pad_rows(q_ref[:, hsl[h]], Lq).astype(BF16) for h in heads]
    kb = [_pad_rows(k_ref[:, hsl[h]], Lk).astype(BF16) for h in heads]
    vb = [_pad_rows(v_ref[:, hsl[h]], Lk).astype(BF16) for h in heads]
    bcol_k = [_pad_rows(cs[:, LANE_MLF + h:LANE_MLF + h + 1], Lk) for h in heads]
    bcol = [b[0:Lq] for b in bcol_k]
    icol = [_pad_rows(g[:, LANE_MLI + h:LANE_MLI + h + 1], Lk, -jnp.inf) for h in heads]
    log_d = [jnp.where(causal, bcol[h] - csrow[LANE_MLF + h:LANE_MLF + h + 1, :]
                       + grow[LANE_MLI + h:LANE_MLI + h + 1, :], -jnp.inf) for h in heads]
    log_inter = [bcol[h] + m_ref[0:1, h:h + 1] for h in heads]
    m_t = [jnp.maximum(log_inter[h], jnp.max(log_d[h], axis=-1, keepdims=True)) for h in heads]
    inter_w = [jnp.exp(log_inter[h] - m_t[h]) for h in heads]
    qk = [_dot_nt(qb[h], kb[h]) for h in heads]
    ch = [c_ref[h] for h in heads]
    n_row = [n_ref[h:h + 1, :] for h in heads]
    qc = [_dot_nt(qb[h], ch[h].astype(BF16)) for h in heads]
    s = [qk[h] * (jnp.exp(log_d[h] - m_t[h]) * scale) for h in heads]
    sv = [_dot(s[h].astype(BF16), vb[h]) for h in heads]
    for h in heads:
        num = sv[h] + inter_w[h] * qc[h]
        den = (jnp.sum(s[h], axis=-1, keepdims=True)
               + inter_w[h] * jnp.sum(qb[h].astype(F32) * n_row[h], axis=-1, keepdims=True))
        hh = num / jnp.maximum(jnp.abs(den), jnp.exp(-m_t[h]))
        y_ref[:, hsl[h]] = _rms(hh[0:L], gn_ref[...]) * _sigmoid(o_ref[:, hsl[h]])
    m_end = [m_t[h][L - 1:L, :] for h in heads]
    a_end = [inter_w[h][L - 1:L, :] for h in heads]
    w_col = [jnp.exp(bcol[h][L - 1:L, :] - bcol_k[h] + icol[h] - m_end[h]) * scale for h in heads]
    upd = [_dot_tn((vb[h].astype(F32) * w_col[h]).astype(BF16), kb[h]) for h in heads]
    for h in heads:
        c_ref[h] = a_end[h] * ch[h] + upd[h]
        n_ref[h:h + 1, :] = a_end[h] * n_row[h] + jnp.sum(kb[h].astype(F32) * w_col[h], axis=0, keepdims=True)
        m_ref[0:1, h:h + 1] = m_end[h]

    @pl.when(c == nc - 1)
    def _():
        cout_ref[0] = c_ref[...]
        nout_ref[0] = n_ref[...]
        mout_ref[0] = m_ref[...]


def mlstm(za, zb, g, cs, grow, csrow, c0, n0, m0, gn, nb, t, L, c_layer):
    nc = t // L
    rows = lambda b, c: (b * nc + c, 0)
    rr = 2 * SUBLANES
    m0p = jnp.zeros((nb, 1, LANES), F32).at[:, 0, :ML_HEADS].set(m0)
    short = grow is None
    row_specs = [] if short else [pl.BlockSpec((rr, L), lambda b, c: (b * (LANES // rr), c))] * 2
    row_args = [] if short else [grow, csrow]
    outs = pl.pallas_call(
        functools.partial(_mlstm_kernel, short=short),
        out_shape=(jax.ShapeDtypeStruct((nb * t, ML_WIDTH), F32),
                   jax.ShapeDtypeStruct((nb, ML_HEADS, ML_HD, ML_HD), F32),
                   jax.ShapeDtypeStruct((nb, ML_HEADS, ML_HD), F32),
                   jax.ShapeDtypeStruct((nb, 1, LANES), F32)),
        grid=(nb, nc),
        in_specs=[pl.BlockSpec((L, 512), lambda b, c: (b * nc + c, COL_MQ)),
                  pl.BlockSpec((L, 512), lambda b, c: (b * nc + c, COL_MK)),
                  pl.BlockSpec((L, 512), lambda b, c: (b * nc + c, COL_MV)),
                  pl.BlockSpec((L, 512), lambda b, c: (b * nc + c, COL_MO)),
                  pl.BlockSpec((L, LANES), rows), pl.BlockSpec((L, LANES), rows), *row_specs,
                  pl.BlockSpec((1, 1, ML_HEADS, ML_HD, ML_HD), lambda b, c: (c_layer, b, 0, 0, 0)),
                  pl.BlockSpec((1, ML_HEADS, ML_HD), lambda b, c: (b, 0, 0)),
                  pl.BlockSpec((1, 1, LANES), lambda b, c: (b, 0, 0)),
                  pl.BlockSpec((1, ML_HD), lambda b, c: (0, 0))],
        out_specs=(pl.BlockSpec((L, ML_WIDTH), rows),
                   pl.BlockSpec((1, ML_HEADS, ML_HD, ML_HD), lambda b, c: (b, 0, 0, 0)),
                   pl.BlockSpec((1, ML_HEADS, ML_HD), lambda b, c: (b, 0, 0)),
                   pl.BlockSpec((1, 1, LANES), lambda b, c: (b, 0, 0))),
        scratch_shapes=[pltpu.VMEM((ML_HEADS, ML_HD, ML_HD), F32), pltpu.VMEM((ML_HEADS, ML_HD), F32),
                        pltpu.VMEM((1, LANES), F32)],
        compiler_params=_cparams("parallel", "arbitrary"),
        name="mlstm",
    )(zb, zb, zb, za, g, cs, *row_args, c0, n0, m0p, gn.reshape(1, ML_HD))
    y, c_new, n_new, m_new = outs
    return y, c_new, n_new, m_new[:, 0, :ML_HEADS]


def _merge_kernel(x_ref, g_ref, wg_ref, ys_ref, yf_ref, ym_ref, ws_ref, wf_ref, wm_ref, wo_ref, o_ref):
    x = x_ref[...]
    hn = _rms(x, g_ref[...]).astype(BF16)
    branches = (ys_ref, ws_ref), (yf_ref, wf_ref), (ym_ref, wm_ref)
    merged = None
    for b, (y_ref, w_ref) in enumerate(branches):
        gate = _sigmoid(_dot(hn, wg_ref[:, b * D_MODEL:(b + 1) * D_MODEL]))
        term = gate * _dot(y_ref[...].astype(BF16), w_ref[...])
        merged = term if merged is None else merged + term
    o_ref[...] = x + _dot(merged.astype(BF16), wo_ref[...])


def merge_out(x, g_mix, w_gates, ys, yf, ym, ws, wf, wm, wo, tm=512):
    m = x.shape[0]
    tm = min(tm, m)
    row = lambda i: (i, 0)
    const = lambda i: (0, 0)
    return pl.pallas_call(
        _merge_kernel,
        out_shape=jax.ShapeDtypeStruct((m, D_MODEL), F32),
        grid=(m // tm,),
        in_specs=[pl.BlockSpec((tm, D_MODEL), row), pl.BlockSpec((1, D_MODEL), const),
                  pl.BlockSpec((D_MODEL, 3 * D_MODEL), const),
                  pl.BlockSpec((tm, 512), row), pl.BlockSpec((tm, 512), row), pl.BlockSpec((tm, 512), row),
                  pl.BlockSpec((512, D_MODEL), const), pl.BlockSpec((512, D_MODEL), const),
                  pl.BlockSpec((512, D_MODEL), const), pl.BlockSpec((D_MODEL, D_MODEL), const)],
        out_specs=pl.BlockSpec((tm, D_MODEL), row),
        compiler_params=_cparams("parallel"),
        name="merge_out",
    )(x, g_mix.reshape(1, D_MODEL), w_gates, ys, yf, ym, ws, wf, wm, wo)


def _cross_block_kernel(x_ref, g_ref, wq_ref, gq_ref, k_ref, v_ref, wo_ref, o_ref):
    x = x_ref[...]
    q = _dot(_rms(x, g_ref[...]).astype(BF16), wq_ref[...])
    heads = range(MEM_HEADS)
    hsl = [slice(h * MEM_HD, (h + 1) * MEM_HD) for h in heads]
    qh = [_rms(q[:, hsl[h]], gq_ref[...]).astype(BF16) for h in heads]
    s = [_dot_nt(qh[h], k_ref[:, hsl[h]].astype(BF16)) * (MEM_HD ** -0.5) for h in heads]
    m = [jnp.max(s[h], axis=-1, keepdims=True) for h in heads]
    p = [jnp.exp(s[h] - m[h]) for h in heads]
    l = [jnp.sum(p[h], axis=-1, keepdims=True) for h in heads]
    pv = [_dot(p[h].astype(BF16), v_ref[:, hsl[h]].astype(BF16)) for h in heads]
    outs = [(pv[h] / l[h]).astype(BF16) for h in heads]
    o_ref[...] = x + _dot(jnp.concatenate(outs, axis=1), wo_ref[...])


def cross_block(x, g, wq, gq, mem_k, mem_v, wo, nb, t, tq):
    nq = t // tq
    n_mem = mem_k.shape[0] // nb
    const = lambda b, i: (0, 0)
    row = lambda b, i: (b * nq + i, 0)
    return pl.pallas_call(
        _cross_block_kernel,
        out_shape=jax.ShapeDtypeStruct((nb * t, D_MODEL), F32),
        grid=(nb, nq),
        in_specs=[pl.BlockSpec((tq, D_MODEL), row), pl.BlockSpec((1, D_MODEL), const),
                  pl.BlockSpec((D_MODEL, MEM_WIDTH), const), pl.BlockSpec((1, MEM_HD), const),
                  pl.BlockSpec((n_mem, MEM_WIDTH), lambda b, i: (b, 0)),
                  pl.BlockSpec((n_mem, MEM_WIDTH), lambda b, i: (b, 0)),
                  pl.BlockSpec((MEM_WIDTH, D_MODEL), const)],
        out_specs=pl.BlockSpec((tq, D_MODEL), row),
        compiler_params=_cparams("parallel", "parallel"),
        name="cross_block",
    )(x, g.reshape(1, D_MODEL), wq, gq.reshape(1, MEM_HD), mem_k, mem_v, wo)


def _cross_cached_kernel(q_ref, k_ref, v_ref, o_ref):
    nseq = k_ref.shape[1]
    tq = q_ref.shape[0] // nseq
    seqs = range(nseq)
    n_cols = k_ref.shape[2]
    same_head = (lax.broadcasted_iota(jnp.int32, (MEM_HEADS * tq, n_cols), 0) // tq ==
                 lax.broadcasted_iota(jnp.int32, (MEM_HEADS * tq, n_cols), 1) % MEM_HEADS)
    qr = [jnp.concatenate([q_ref[sq * tq:(sq + 1) * tq, h * MEM_HD:(h + 1) * MEM_HD] for h in range(MEM_HEADS)],
                          axis=0).astype(BF16) for sq in seqs]
    s = [jnp.where(same_head, _dot_nt(qr[sq], k_ref[0, sq].astype(BF16)) * (MEM_HD ** -0.5), -jnp.inf)
         for sq in seqs]
    m = [jnp.max(s[sq], axis=-1, keepdims=True) for sq in seqs]
    p = [jnp.exp(s[sq] - m[sq]) for sq in seqs]
    l = [jnp.sum(p[sq], axis=-1, keepdims=True) for sq in seqs]
    o = [_dot(p[sq].astype(BF16), v_ref[0, sq].astype(BF16)) / l[sq] for sq in seqs]
    for sq in seqs:
        for h in range(MEM_HEADS):
            o_ref[sq * tq:(sq + 1) * tq, h * MEM_HD:(h + 1) * MEM_HD] = o[sq][h * tq:(h + 1) * tq]


CROSS_CACHED_SEQS = 4


def cross_attend_cached(q, mem_k, mem_v, nb, t, layer):
    ns = CROSS_CACHED_SEQS
    kv_spec = pl.BlockSpec((1, ns, mem_k.shape[2], MEM_HD), lambda b: (layer, b, 0, 0))
    return pl.pallas_call(
        _cross_cached_kernel,
        out_shape=jax.ShapeDtypeStruct((nb * t, MEM_WIDTH), F32),
        grid=(nb // ns,),
        in_specs=[pl.BlockSpec((ns * t, MEM_WIDTH), lambda b: (b, 0)), kv_spec, kv_spec],
        out_specs=pl.BlockSpec((ns * t, MEM_WIDTH), lambda b: (b, 0)),
        compiler_params=_cparams("parallel"),
        name="cross_attend_cached",
    )(q, mem_k, mem_v)


def _proj_residual_kernel(x_ref, a_ref, w_ref, o_ref):
    o_ref[...] = x_ref[...] + _dot(a_ref[...].astype(BF16), w_ref[...])


def proj_residual(x, a, w, tm=512):
    m, n = x.shape
    k = a.shape[1]
    tm = min(tm, m)
    return pl.pallas_call(
        _proj_residual_kernel,
        out_shape=jax.ShapeDtypeStruct((m, n), F32),
        grid=(m // tm,),
        in_specs=[pl.BlockSpec((tm, n), lambda i: (i, 0)), pl.BlockSpec((tm, k), lambda i: (i, 0)),
                  pl.BlockSpec((k, n), lambda i: (0, 0))],
        out_specs=pl.BlockSpec((tm, n), lambda i: (i, 0)),
        compiler_params=_cparams("parallel"),
        name="proj_residual",
    )(x, a, w)


def _mlp_kernel(x_ref, g_ref, wu_ref, wd_ref, o_ref, hn_ref, acc_ref):
    f = pl.program_id(1)

    @pl.when(f == 0)
    def _():
        hn_ref[...] = _rms(x_ref[...], g_ref[...]).astype(BF16)
        acc_ref[...] = jnp.zeros_like(acc_ref)

    a = jnp.maximum(_dot(hn_ref[...], wu_ref[...]), 0.0)
    acc_ref[...] += _dot((a * a).astype(BF16), wd_ref[...])

    @pl.when(f == pl.num_programs(1) - 1)
    def _():
        o_ref[...] = x_ref[...] + acc_ref[...]


def mlp(x, g, wu, wd, tm=1024, tf=1024):
    m, d = x.shape
    dff = wu.shape[1]
    tm = min(tm, m)
    return pl.pallas_call(
        _mlp_kernel,
        out_shape=jax.ShapeDtypeStruct((m, d), F32),
        grid=(m // tm, dff // tf),
        in_specs=[pl.BlockSpec((tm, d), lambda i, f: (i, 0)), pl.BlockSpec((1, d), lambda i, f: (0, 0)),
                  pl.BlockSpec((d, tf), lambda i, f: (0, f)), pl.BlockSpec((tf, d), lambda i, f: (f, 0))],
        out_specs=pl.BlockSpec((tm, d), lambda i, f: (i, 0)),
        scratch_shapes=[pltpu.VMEM((tm, d), BF16), pltpu.VMEM((tm, d), F32)],
        compiler_params=_cparams("parallel", "arbitrary"),
        name="mlp",
    )(x, g.reshape(1, d), wu, wd)


def _pack_w_in(w_in):
    offs = np.concatenate([[0], np.cumsum(SPLITS)])
    col = lambda i: w_in[:, int(offs[i]):int(offs[i + 1])]
    s5, fq, fk, fv, ff, mq, mk, mv, mi, mf, mo, gates = [col(i) for i in range(12)]
    pad = jnp.zeros((w_in.shape[0], 512 - FOX_HEADS - 2 * ML_HEADS), w_in.dtype)
    w_a = jnp.concatenate([s5, fq, fk, fv, mo, ff, mi, mf, pad], axis=1).astype(BF16)
    w_b = jnp.concatenate([mq, mk, mv], axis=1).astype(BF16)
    return w_a, w_b, gates.astype(BF16)


def _layer_weights(l, g_mix, w_in, s5_a_re, s5_a_im, s5_log_step, s5_b_re, s5_b_im, s5_c_re, s5_c_im, s5_d,
                   s5_w_glu, s5_b_glu, fox_gq, fox_gk, fox_bf, ml_bi, ml_bf, ml_gn, w_br_s5, w_br_fox, w_br_ml,
                   w_out, g_cross, w_cq, cross_gq, g_mem, w_mk, w_mv, cross_gk, w_co, g_mlp, w_up, w_down):
    bias_row = jnp.zeros((1, LANES), F32)
    bias_row = bias_row.at[0, LANE_FOXF:LANE_FOXF + FOX_HEADS].set(fox_bf[l])
    bias_row = bias_row.at[0, LANE_MLI:LANE_MLI + ML_HEADS].set(ml_bi[l])
    bias_row = bias_row.at[0, LANE_MLF:LANE_MLF + ML_HEADS].set(ml_bf[l])
    w_a, w_b, w_gates = _pack_w_in(w_in[l])
    return dict(
        g_mix=g_mix[l], w_a=w_a, w_b=w_b, w_gates=w_gates,
        s5=s5_params(s5_a_re[l], s5_a_im[l], s5_log_step[l], s5_b_re[l], s5_b_im[l], s5_c_re[l], s5_c_im[l],
                     s5_d[l], s5_w_glu[l], s5_b_glu[l], (SEQ_TILE // SUBLANES, SAMPLE_T // SUBLANES)),
        gq=jnp.tile(fox_gq[l], FOX_HEADS).reshape(1, FOX_WIDTH),
        gk=jnp.tile(fox_gk[l], FOX_HEADS).reshape(1, FOX_WIDTH),
        bias_row=bias_row, ml_gn=ml_gn[l],
        w_br_s5=w_br_s5[l].astype(BF16), w_br_fox=w_br_fox[l].astype(BF16), w_br_ml=w_br_ml[l].astype(BF16),
        w_out=w_out[l].astype(BF16), g_cross=g_cross[l], w_cq=w_cq[l].astype(BF16), cross_gq=cross_gq[l],
        g_mem=g_mem[l], w_mk=w_mk[l].astype(BF16), w_mv=w_mv[l].astype(BF16), cross_gk=cross_gk[l],
        w_co=w_co[l].astype(BF16), g_mlp=g_mlp[l], w_up=w_up[l].astype(BF16), w_down=w_down[l].astype(BF16))


def _hybrid_layer(x, W, nb, t, seq_tile, t_valid, s5_state, ml_state, mem_k, mem_v, fox_attend, augment,
                  mem_layer=None):
    z = norm_matmul(x, W["g_mix"], W["w_a"], tm=min(1024, nb * t), tn=1024)
    zb = norm_matmul(x, W["g_mix"], W["w_b"], tm=min(1024, nb * t), tn=W["w_b"].shape[1],
                     out_dtype=BF16 if seq_tile % MIN_BF16_ROWS == 0 else F32)
    prep = gate_prep(z, W["gq"], W["gk"], W["bias_row"], nb, t, seq_tile, t_valid, augment)
    kn, g, cs = prep[1:4]
    grow, csrow = prep[4:6] if augment else (None, None)
    y_s5, s5_re, s5_im = s5_mixer(z, s5_state[0], s5_state[1], W["s5"], nb, t, seq_tile, t_valid - 1,
                                  seqs=S5_SHORT_SEQS if t == SUBLANES else 1)
    y_fox = fox_attend(prep, z)
    y_ml, c_new, n_new, m_new = mlstm(z, zb, g, cs, grow, csrow, ml_state[0], ml_state[1], ml_state[2], W["ml_gn"],
                                      nb, t, seq_tile, 0 if mem_layer is None else mem_layer)
    x = merge_out(x, W["g_mix"], W["w_gates"], y_s5, y_fox, y_ml, W["w_br_s5"], W["w_br_fox"], W["w_br_ml"],
                  W["w_out"])
    if mem_layer is None:
        x = cross_block(x, W["g_cross"], W["w_cq"], W["cross_gq"], mem_k, mem_v, W["w_co"], nb, t, seq_tile)
    else:
        qc = norm_matmul(x, W["g_cross"], W["w_cq"], head_gain=W["cross_gq"])
        oc = cross_attend_cached(qc, mem_k, mem_v, nb, t, mem_layer)
        x = proj_residual(x, oc, W["w_co"])
    x = mlp(x, W["g_mlp"], W["w_up"], W["w_down"])
    if augment:
        fox_k = prep[1].reshape(nb, FOX_HEADS, FOX_HD, t).transpose(0, 3, 1, 2)
        fox_v = prep[8].reshape(nb, FOX_HEADS, FOX_HD, t).transpose(0, 3, 1, 2)
    else:
        fox_k = kn.reshape(nb, t, FOX_HEADS, FOX_HD)
        fox_v = z[:, COL_FV * 512:(COL_FV + 1) * 512].reshape(nb, t, FOX_HEADS, FOX_HD)
    return x, fox_k, fox_v, g, s5_re.reshape(nb, S5_GROUPS, S5_STATE), s5_im.reshape(nb, S5_GROUPS, S5_STATE), \
        c_new, n_new, m_new


def kernel(x_prompt, x_sample, mem_prompt, cache_fox_k, cache_fox_v, cache_fox_logf, page_table, state_s5_re, state_s5_im, state_mlstm_C, state_mlstm_n, state_mlstm_m, cache_mem_k, cache_mem_v, g_mix, w_in, s5_a_re, s5_a_im, s5_log_step, s5_b_re, s5_b_im, s5_c_re, s5_c_im, s5_d, s5_w_glu, s5_b_glu, fox_gq, fox_gk, fox_bf, ml_bi, ml_bf, ml_gn, w_br_s5, w_br_fox, w_br_ml, w_out, g_cross, w_cq, cross_gq, g_mem, w_mk, w_mv, cross_gk, w_co, g_mlp, w_up, w_down):
    depth = w_in.shape[0]
    bp, tp, _ = x_prompt.shape
    bs, ts, _ = x_sample.shape
    n_mem = mem_prompt.shape[1]

    xp = x_prompt.reshape(bp * tp, D_MODEL)
    xs = jnp.pad(x_sample, ((0, 0), (0, SAMPLE_T - ts), (0, 0))).reshape(bs * SAMPLE_T, D_MODEL)
    mem = mem_prompt.reshape(bp * n_mem, D_MODEL)
    zeros_p = (jnp.zeros((bp, S5_LANES), F32), jnp.zeros((bp, S5_LANES), F32))
    zeros_ml = (jnp.zeros((1, bp, ML_HEADS, ML_HD, ML_HD), F32), jnp.zeros((bp, ML_HEADS, ML_HD), F32),
                jnp.zeros((bp, ML_HEADS), F32))
    pool_kt = jnp.transpose(cache_fox_k, (0, 1, 3, 4, 2))
    pool_vt = jnp.transpose(cache_fox_v, (0, 1, 3, 4, 2))
    pool_ft = jnp.transpose(cache_fox_logf, (0, 1, 3, 2))
    st_p, st_s = [], []
    for l in range(depth):
        W = _layer_weights(l, g_mix, w_in, s5_a_re, s5_a_im, s5_log_step, s5_b_re, s5_b_im, s5_c_re, s5_c_im, s5_d,
                           s5_w_glu, s5_b_glu, fox_gq, fox_gk, fox_bf, ml_bi, ml_bf, ml_gn, w_br_s5, w_br_fox,
                           w_br_ml, w_out, g_cross, w_cq, cross_gq, g_mem, w_mk, w_mv, cross_gk, w_co, g_mlp, w_up,
                           w_down)
        mk_p = norm_matmul(mem, W["g_mem"], W["w_mk"], head_gain=W["cross_gk"])
        mv_p = norm_matmul(mem, W["g_mem"], W["w_mv"])

        def flash(prep, z):
            return fox_flash(prep[0], prep[6], prep[7], bp, tp, FLASH_TILE)

        xp, fox_k, fox_v, g, s5r, s5i, c_new, n_new, m_new = _hybrid_layer(
            xp, W, bp, tp, SEQ_TILE, tp, zeros_p, zeros_ml, mk_p, mv_p, flash, True)
        st_p.append((fox_k, fox_v,
                     g[:, LANE_FOXF:LANE_FOXF + FOX_HEADS].reshape(bp, tp, FOX_HEADS),
                     s5r, s5i, c_new, n_new, m_new,
                     mk_p.reshape(bp, n_mem, MEM_HEADS, MEM_HD), mv_p.reshape(bp, n_mem, MEM_HEADS, MEM_HD)))

        def decode(prep, z, layer=l):
            return fox_decode(page_table, prep[0], prep[1], z, prep[3], pool_kt, pool_vt, pool_ft, layer)

        xs, fox_k, fox_v, g, s5r, s5i, c_new, n_new, m_new = _hybrid_layer(
            xs, W, bs, SAMPLE_T, SAMPLE_T, ts,
            (state_s5_re[l].reshape(bs, S5_LANES), state_s5_im[l].reshape(bs, S5_LANES)),
            (state_mlstm_C, state_mlstm_n[l], state_mlstm_m[l]),
            cache_mem_k.reshape(depth, bs, n_mem * MEM_HEADS, MEM_HD),
            cache_mem_v.reshape(depth, bs, n_mem * MEM_HEADS, MEM_HD), decode, False, mem_layer=l)
        st_s.append((fox_k[:, :ts], fox_v[:, :ts],
                     g[:, LANE_FOXF:LANE_FOXF + FOX_HEADS].reshape(bs, SAMPLE_T, FOX_HEADS)[:, :ts],
                     s5r, s5i, c_new, n_new, m_new))
    outs_p = [jnp.stack(a) for a in zip(*st_p)]
    outs_s = [jnp.stack(a) for a in zip(*st_s)]
    yp = xp.reshape(bp, tp, D_MODEL)
    ys = xs.reshape(bs, SAMPLE_T, D_MODEL)[:, :ts]
    return (yp, ys, *outs_p, *outs_s)
```

```python
import functools
import math

import jax
import jax.numpy as jnp
import numpy as np
from jax import lax
from jax.experimental import pallas as pl
from jax.experimental.pallas import tpu as pltpu

F32 = jnp.float32
BF16 = jnp.bfloat16

LANES = 128
SUBLANES = 8
MIN_BF16_ROWS = 16
MXU_DIM = 256
VMEM_LIMIT_BYTES = 48 * 1024 * 1024

D_MODEL = 1024
S5_WIDTH = 512
S5_GROUP = 16
S5_GROUPS = 32
S5_STATE = 64
S5_LANES = S5_GROUPS * S5_STATE
FOX_HEADS = 8
FOX_HD = 64
FOX_WIDTH = 512
ML_HEADS = 4
ML_HD = 128
ML_WIDTH = 512
MEM_HEADS = 4
MEM_HD = 128
MEM_WIDTH = 512
D_FF = 4096
EPS = 1e-6
LOG2E = math.log2(math.e)
SPLITS =(S5_WIDTH, FOX_WIDTH, FOX_WIDTH, FOX_WIDTH, FOX_HEADS, ML_WIDTH, ML_WIDTH, ML_WIDTH,
          ML_HEADS, ML_HEADS, ML_WIDTH, 3 * D_MODEL)

COL_S5, COL_FQ, COL_FK, COL_FV, COL_MO, COL_SMALL = range(6)
COL_MQ, COL_MK, COL_MV = range(3)
LANE_FOXF = 0
LANE_MLI = 8
LANE_MLF = 12
SEQ_TILE = 256
FLASH_TILE = 512
FLASH_ROWS = 128
CROSS_TILE = 512
SAMPLE_T = 8


def _cparams(*sem):
    return pltpu.CompilerParams(dimension_semantics=sem, vmem_limit_bytes=VMEM_LIMIT_BYTES)


def _dot(a, b):
    return jnp.dot(a, b, preferred_element_type=F32)


def _dot_nt(a, b):
    return lax.dot_general(a, b, (((1,), (1,)), ((), ())), preferred_element_type=F32)


def _dot_tn(a, b):
    return lax.dot_general(a, b, (((0,), (0,)), ((), ())), preferred_element_type=F32)


def _split3(x):
    hi = x.astype(BF16)
    r1 = x - hi.astype(F32)
    mid = r1.astype(BF16)
    lo = (r1 - mid.astype(F32)).astype(BF16)
    return hi, mid, lo


def _dot_exact_rhs(x, ones_rhs):
    hi, mid, lo = _split3(x)
    return _dot(hi, ones_rhs) + _dot(mid, ones_rhs) + _dot(lo, ones_rhs)


def _dot_exact_lhs(ones_lhs, x):
    hi, mid, lo = _split3(x)
    return _dot(ones_lhs, hi) + _dot(ones_lhs, mid) + _dot(ones_lhs, lo)


def _pad_rows(x, n, fill=0.0):
    if x.shape[0] >= n:
        return x
    return jnp.concatenate([x, jnp.full((n - x.shape[0], x.shape[1]), fill, x.dtype)], axis=0)


def _lane_tile(x, n):
    return x if n == 1 else jnp.concatenate([x] * n, axis=1)


def _log_sigmoid(a):
    return jnp.minimum(a, 0.0) - jnp.log1p(jnp.exp(-jnp.abs(a)))


def _sigmoid(a):
    return 1.0 / (1.0 + jnp.exp(-a))


def _gelu_tanh(x):
    c = math.sqrt(2.0 / math.pi)
    return 0.5 * x * (1.0 + jnp.tanh(c * (x + 0.044715 * (x * x * x))))


def _rms(x, g):
    ms = jnp.mean(x * x, axis=-1, keepdims=True)
    return x * lax.rsqrt(ms + EPS) * g


def _norm_matmul_kernel(x_ref, g_ref, w_ref, *rest, head_norm):
    if head_norm:
        hg_ref, o_ref, hn_ref = rest
    else:
        o_ref, hn_ref = rest

    @pl.when(pl.program_id(1) == 0)
    def _():
        hn_ref[...] = _rms(x_ref[...], g_ref[...]).astype(BF16)

    y = _dot(hn_ref[...], w_ref[...])
    if head_norm:
        tn = y.shape[1]
        for s in range(tn // LANES):
            sl = slice(s * LANES, (s + 1) * LANES)
            o_ref[:, sl] = _rms(y[:, sl], hg_ref[...]).astype(o_ref.dtype)
    else:
        o_ref[...] = y.astype(o_ref.dtype)


def norm_matmul(x, g, w, head_gain=None, out_dtype=F32, tm=512, tn=512):
    m, k = x.shape
    n = w.shape[1]
    tm = min(tm, m)
    tn = min(tn, n)
    head_norm = head_gain is not None
    in_specs = [pl.BlockSpec((tm, k), lambda i, j: (i, 0)),
                pl.BlockSpec((1, k), lambda i, j: (0, 0)),
                pl.BlockSpec((k, tn), lambda i, j: (0, j))]
    args = [x, g.reshape(1, k), w]
    if head_norm:
        in_specs.append(pl.BlockSpec((1, LANES), lambda i, j: (0, 0)))
        args.append(head_gain.reshape(1, LANES))
    return pl.pallas_call(
        functools.partial(_norm_matmul_kernel, head_norm=head_norm),
        out_shape=jax.ShapeDtypeStruct((m, n), out_dtype),
        grid=(m // tm, n // tn),
        in_specs=in_specs,
        out_specs=pl.BlockSpec((tm, tn), lambda i, j: (i, j)),
        scratch_shapes=[pltpu.VMEM((tm, k), BF16)],
        compiler_params=_cparams("parallel", "arbitrary"),
        name="norm_matmul",
    )(*args)


def _prep_kernel(fq_ref, fk_ref, fv_ref, sm_ref, gq_ref, gk_ref, bias_ref, gmat_ref, tril_ref, sel_ref, aug_ref,
                 *rest, seq_rows, t_valid, augment):
    if augment:
        qx_ref, kn_ref, g_ref, cs_ref, grow_ref, csrow_ref, kx_ref, vb_ref, vt_ref, carry_ref = rest
    else:
        qx_ref, kn_ref, g_ref, cs_ref, carry_ref = rest
    c = pl.program_id(1)
    tc = sm_ref.shape[0]

    @pl.when(c == 0)
    def _():
        carry_ref[...] = jnp.zeros_like(carry_ref)

    gmat = gmat_ref[...]

    def head_rms(x, gain):
        x2 = x * x
        hi = x2.astype(BF16)
        lo = (x2 - hi.astype(F32)).astype(BF16)
        ss = (_dot(hi, gmat) + _dot(lo, gmat)) * (1.0 / FOX_HD)
        return x * lax.rsqrt(ss + EPS) * gain

    qn = head_rms(fq_ref[...], gq_ref[...]) * ((FOX_HD ** -0.5) * (LOG2E if augment else 1.0))
    kn = head_rms(fk_ref[...], gk_ref[...])
    if augment:
        kn_ref[...] = kn.T
        vt_ref[...] = fv_ref[...].T
    else:
        kn_ref[...] = kn

    a = sm_ref[...] + bias_ref[...]
    lane = lax.broadcasted_iota(jnp.int32, a.shape, 1)
    pos = (lax.broadcasted_iota(jnp.int32, a.shape, 0) + c * tc) % seq_rows
    is_i = (lane >= LANE_MLI) & (lane < LANE_MLF)
    used = lane < LANE_MLF + ML_HEADS
    valid = pos < t_valid
    g = jnp.where(is_i, a, _log_sigmoid(a))
    g = jnp.where(used, g, 0.0)
    g = jnp.where(valid, g, jnp.where(is_i, -jnp.inf, 0.0))
    gc = jnp.where(is_i, 0.0, g)
    cs = _dot_exact_lhs(tril_ref[...], gc)
    carry = carry_ref[...]
    csg = cs + jnp.where(lane < FOX_HEADS, carry, 0.0)
    carry_ref[...] = carry + cs[tc - 1:tc, :]
    g_ref[...] = g
    cs_ref[...] = csg
    if augment:
        grow_ref[...] = g.T
        csrow_ref[...] = csg.T

    nx = FOX_HEADS * LANES
    lanex = lax.broadcasted_iota(jnp.int32, (1, nx), 1)
    keep = ((lanex % LANES) >= FOX_HD) == ((lanex // LANES) % 2 == 1)
    q_exp = jnp.concatenate([qn[:, (h // 2) * LANES:(h // 2 + 1) * LANES] for h in range(FOX_HEADS)], axis=1)
    if not augment:
        qx_ref[...] = jnp.where(keep, q_exp, 0.0).astype(qx_ref.dtype)
        return
    k_exp = jnp.concatenate([kn[:, (h // 2) * LANES:(h // 2 + 1) * LANES] for h in range(FOX_HEADS)], axis=1)
    hi, mid, lo = _split3(jnp.where(lane < FOX_HEADS, csg * LOG2E, 0.0))
    packed = (hi.astype(F32) + pltpu.roll(mid.astype(F32), FOX_HEADS, 1)
              + pltpu.roll(lo.astype(F32), 2 * FOX_HEADS, 1)).astype(BF16)
    aug = aug_ref[...] + _dot(packed, sel_ref[...])
    qx_ref[...] = jnp.where(keep, q_exp, aug[:, 0:nx]).astype(qx_ref.dtype)
    kx_ref[...] = jnp.where(keep, k_exp, aug[:, nx:2 * nx]).astype(kx_ref.dtype)
    vb_ref[...] = fv_ref[...].astype(vb_ref.dtype)


PREP_SHORT_ROWS = 128


def gate_prep(z, gq, gk, bias_row, nb, t, tc, t_valid, augment):
    m = nb * t
    gmat = jnp.asarray(np.kron(np.eye(FOX_HEADS), np.ones((FOX_HD, FOX_HD))), BF16)
    if augment:
        nc, tp, grid_rows = t // tc, tc, nb
        tril = np.tril(np.ones((tc, tc)))
    else:
        tc = tp = PREP_SHORT_ROWS
        nc, grid_rows = 1, m // tc
        tril = np.kron(np.eye(tc // t), np.tril(np.ones((t, t))))
    tril = jnp.asarray(tril, BF16)
    nx = FOX_HEADS * LANES
    sel = np.zeros((LANES, 2 * nx), np.float32)
    aug_const = np.zeros((1, 2 * nx), np.float32)
    for h in range(FOX_HEADS):
        o = h * LANES + (0 if h % 2 else FOX_HD)
        for p in range(3):
            sel[p * FOX_HEADS + h, o + p] = 1.0
            sel[p * FOX_HEADS + h, nx + o + 3 + p] = -1.0
            aug_const[0, o + 3 + p] = 1.0
            aug_const[0, nx + o + p] = 1.0
    sel, aug_const = jnp.asarray(sel, BF16), jnp.asarray(aug_const, F32)
    row_map = lambda b, c: (b * nc + c, 0)
    const = lambda b, c: (0, 0)
    feat_major = jax.ShapeDtypeStruct((nb * FOX_WIDTH, t), F32)
    feat_spec = pl.BlockSpec((FOX_WIDTH, tc), lambda b, c: (b, c))
    out_shape = [jax.ShapeDtypeStruct((m, FOX_HEADS * LANES), BF16 if augment else F32),
                 feat_major if augment else jax.ShapeDtypeStruct((m, FOX_WIDTH), F32),
                 jax.ShapeDtypeStruct((m, LANES), F32),
                 jax.ShapeDtypeStruct((m, LANES), F32)]
    out_specs = [pl.BlockSpec((tc, FOX_HEADS * LANES), row_map),
                 feat_spec if augment else pl.BlockSpec((tc, FOX_WIDTH), row_map),
                 pl.BlockSpec((tc, LANES), row_map),
                 pl.BlockSpec((tc, LANES), row_map)]
    if augment:
        out_shape += [jax.ShapeDtypeStruct((nb * LANES, t), F32), jax.ShapeDtypeStruct((nb * LANES, t), F32),
                      jax.ShapeDtypeStruct((m, FOX_HEADS * LANES), BF16), jax.ShapeDtypeStruct((m, FOX_WIDTH), BF16),
                      feat_major]
        out_specs += [pl.BlockSpec((LANES, tc), lambda b, c: (b, c)), pl.BlockSpec((LANES, tc), lambda b, c: (b, c)),
                      pl.BlockSpec((tc, FOX_HEADS * LANES), row_map), pl.BlockSpec((tc, FOX_WIDTH), row_map),
                      feat_spec]
    return pl.pallas_call(
        functools.partial(_prep_kernel, seq_rows=t, t_valid=t_valid, augment=augment),
        out_shape=tuple(out_shape),
        grid=(grid_rows, nc),
        in_specs=[pl.BlockSpec((tc, 512), lambda b, c: (b * nc + c, COL_FQ)),
                  pl.BlockSpec((tc, 512), lambda b, c: (b * nc + c, COL_FK)),
                  pl.BlockSpec((tc, 512), lambda b, c: (b * nc + c, COL_FV)),
                  pl.BlockSpec((tc, LANES), lambda b, c: (b * nc + c, COL_SMALL * 4)),
                  pl.BlockSpec((1, 512), const), pl.BlockSpec((1, 512), const),
                  pl.BlockSpec((1, LANES), const),
                  pl.BlockSpec((512, 512), const), pl.BlockSpec((tp, tp), const),
                  pl.BlockSpec((LANES, 2 * nx), const), pl.BlockSpec((1, 2 * nx), const)],
        out_specs=tuple(out_specs),
        scratch_shapes=[pltpu.VMEM((1, LANES), F32)],
        compiler_params=_cparams("parallel", "arbitrary"),
        name="gate_prep",
    )(z, z, z, z, gq, gk, bias_row, gmat, tril, sel, aug_const)


S5_SCAN_LANES = 1024
S5_SHORT_SEQS = 16


def _cmul(ar, ai, br, bi):
    return ar * br - ai * bi, ar * bi + ai * br


def _s5_kernel(u_ref, h0r_ref, h0i_ref, perm_ref, permt_ref, bre_ref, bim_ref, lam_ref, pseg_ref, pk_ref,
               cre_ref, cim_ref, d_ref, wglu_ref, bglu_ref, y_ref, hlr_ref, hli_ref, hr_ref, hi_ref, cr_ref, ci_ref,
               *, t_last, seqs):
    c = pl.program_id(1)
    nc = pl.num_programs(1)
    tc = u_ref.shape[0]
    R = tc // SUBLANES if seqs == 1 else 1
    half = S5_WIDTH // 2
    hl = S5_LANES // 2

    if seqs == 1:
        @pl.when(c == 0)
        def _():
            cr_ref[...] = h0r_ref[0]
            ci_ref[...] = h0i_ref[0]

    u = u_ref[...]
    ub = _pad_rows(u, MIN_BF16_ROWS).astype(BF16)
    if R > 1:
        ub = _dot(perm_ref[...], ub).astype(BF16)
    for j in range(2):
        uj = ub[:, j * half:(j + 1) * half]
        hr_ref[:, j * hl:(j + 1) * hl] = _dot(uj, bre_ref[j])[0:tc]
        hi_ref[:, j * hl:(j + 1) * hl] = _dot(uj, bim_ref[j])[0:tc]

    sub = lax.broadcasted_iota(jnp.int32, (SUBLANES, S5_SCAN_LANES), 0)
    for lc in range(S5_LANES // S5_SCAN_LANES):
        ls = slice(lc * S5_SCAN_LANES, (lc + 1) * S5_SCAN_LANES)
        lam_r = jnp.broadcast_to(lam_ref[0:1, ls], sub.shape)
        lam_i = jnp.broadcast_to(lam_ref[1:2, ls], sub.shape)
        if seqs > 1:
            def seq_scan(g, _):
                rows = pl.ds(pl.multiple_of(g * SUBLANES, SUBLANES), SUBLANES)
                s_r = jnp.where(sub == 0, jnp.broadcast_to(h0r_ref[0, pl.ds(g, 1), ls], sub.shape), 0.0)
                s_i = jnp.where(sub == 0, jnp.broadcast_to(h0i_ref[0, pl.ds(g, 1), ls], sub.shape), 0.0)
                m_r, m_i = _cmul(lam_r, lam_i, s_r, s_i)
                x_r = hr_ref[rows, ls] + m_r
                x_i = hi_ref[rows, ls] + m_i
                for j, lag in enumerate((1, 2, 4)):
                    m_r, m_i = _cmul(pseg_ref[2 * j, :, ls], pseg_ref[2 * j + 1, :, ls],
                                     pltpu.roll(x_r, lag, 0), pltpu.roll(x_i, lag, 0))
                    x_r, x_i = x_r + m_r, x_i + m_i
                hr_ref[rows, ls] = x_r
                hi_ref[rows, ls] = x_i
                hlr_ref[0, pl.ds(g, 1), ls] = x_r[t_last:t_last + 1, :]
                hli_ref[0, pl.ds(g, 1), ls] = x_i[t_last:t_last + 1, :]
                return 0

            lax.fori_loop(0, seqs, seq_scan, 0, unroll=2)
            continue
        init_r = jnp.where(sub == 0, jnp.broadcast_to(cr_ref[:, ls], sub.shape), 0.0)
        init_i = jnp.where(sub == 0, jnp.broadcast_to(ci_ref[:, ls], sub.shape), 0.0)

        def local_step(k, carry):
            h_r, h_i = carry
            rows = pl.ds(pl.multiple_of(k * SUBLANES, SUBLANES), SUBLANES)
            m_r, m_i = _cmul(lam_r, lam_i, h_r, h_i)
            h_r = m_r + hr_ref[rows, ls]
            h_i = m_i + hi_ref[rows, ls]
            hr_ref[rows, ls] = h_r
            hi_ref[rows, ls] = h_i
            return h_r, h_i

        e_r, e_i = lax.fori_loop(0, R, local_step, (init_r, init_i), unroll=min(R, 8))
        for j, lag in enumerate((1, 2, 4)):
            m_r, m_i = _cmul(pseg_ref[2 * j, :, ls], pseg_ref[2 * j + 1, :, ls],
                             pltpu.roll(e_r, lag, 0), pltpu.roll(e_i, lag, 0))
            e_r, e_i = e_r + m_r, e_i + m_i
        cr_ref[:, ls] = e_r[SUBLANES - 1:SUBLANES, :]
        ci_ref[:, ls] = e_i[SUBLANES - 1:SUBLANES, :]
        in_r = jnp.where(sub == 0, 0.0, pltpu.roll(e_r, 1, 0))
        in_i = jnp.where(sub == 0, 0.0, pltpu.roll(e_i, 1, 0))

        def fix_step(k, _):
            rows = pl.ds(pl.multiple_of(k * SUBLANES, SUBLANES), SUBLANES)
            p_r = jnp.broadcast_to(pk_ref[0, pl.ds(k, 1), ls], sub.shape)
            p_i = jnp.broadcast_to(pk_ref[1, pl.ds(k, 1), ls], sub.shape)
            m_r, m_i = _cmul(p_r, p_i, in_r, in_i)
            hr_ref[rows, ls] += m_r
            hi_ref[rows, ls] += m_i
            return 0

        lax.fori_loop(0, R, fix_step, 0, unroll=min(R, 8))

    if seqs == 1:
        @pl.when(c == nc - 1)
        def _():
            tl = t_last % tc
            pos = (tl % R) * SUBLANES + tl // R
            hlr_ref[0] = hr_ref[pos:pos + 1, :]
            hli_ref[0] = hi_ref[pos:pos + 1, :]

    hrb = _pad_rows(hr_ref[...], MIN_BF16_ROWS).astype(BF16)
    hib = _pad_rows(hi_ref[...], MIN_BF16_ROWS).astype(BF16)
    ys = [(_dot(hrb[:, j * hl:(j + 1) * hl], cre_ref[j]) + _dot(hib[:, j * hl:(j + 1) * hl], cim_ref[j]))
          for j in range(2)]
    y = jnp.concatenate(ys, axis=1)
    if R > 1:
        y = _dot_exact_lhs(permt_ref[...], y)
    y = _gelu_tanh(y[0:tc] + d_ref[...] * u)
    gate = _dot(_pad_rows(y, MIN_BF16_ROWS).astype(BF16), wglu_ref[...])[0:tc]
    y_ref[...] = y * _sigmoid(gate + bglu_ref[...])


def s5_mixer(z, h0r, h0i, sp, nb, t, tc, t_last, seqs=1):
    assert seqs == 1 or (tc == SUBLANES and t == tc)
    nc = t // tc
    R = tc // SUBLANES
    tc = tc * seqs
    nbg = nb // seqs
    tperm = max(tc, MIN_BF16_ROWS) if seqs == 1 else MIN_BF16_ROWS
    perm = np.zeros((tperm, tperm), np.float32)
    if seqs == 1:
        for s in range(SUBLANES):
            for k in range(R):
                perm[k * SUBLANES + s, s * R + k] = 1.0
    permt = jnp.asarray(perm.T, BF16)
    perm = jnp.asarray(perm, BF16)
    const = lambda b, c: (0, 0)
    const3 = lambda b, c: (0, 0, 0)
    state_spec = pl.BlockSpec((1, seqs, S5_LANES), lambda b, c: (b, 0, 0))
    tabs = sp["tabs"][R]
    return pl.pallas_call(
        functools.partial(_s5_kernel, t_last=t_last, seqs=seqs),
        out_shape=(jax.ShapeDtypeStruct((nb * t, S5_WIDTH), F32),
                   jax.ShapeDtypeStruct((nbg, seqs, S5_LANES), F32),
                   jax.ShapeDtypeStruct((nbg, seqs, S5_LANES), F32)),
        grid=(nbg, nc),
        in_specs=[pl.BlockSpec((tc, 512), lambda b, c: (b * nc + c, COL_S5)),
                  state_spec, state_spec,
                  pl.BlockSpec((tperm, tperm), const), pl.BlockSpec((tperm, tperm), const),
                  pl.BlockSpec((2, S5_WIDTH // 2, S5_LANES // 2), const3),
                  pl.BlockSpec((2, S5_WIDTH // 2, S5_LANES // 2), const3),
                  pl.BlockSpec((2, S5_LANES), const),
                  pl.BlockSpec((6, SUBLANES, S5_LANES), const3),
                  pl.BlockSpec((2, R, S5_LANES), const3),
                  pl.BlockSpec((2, S5_LANES // 2, S5_WIDTH // 2), const3),
                  pl.BlockSpec((2, S5_LANES // 2, S5_WIDTH // 2), const3),
                  pl.BlockSpec((1, S5_WIDTH), const),
                  pl.BlockSpec((S5_WIDTH, S5_WIDTH), const), pl.BlockSpec((1, S5_WIDTH), const)],
        out_specs=(pl.BlockSpec((tc, S5_WIDTH), lambda b, c: (b * nc + c, 0)), state_spec, state_spec),
        scratch_shapes=[pltpu.VMEM((tc, S5_LANES), F32), pltpu.VMEM((tc, S5_LANES), F32),
                        pltpu.VMEM((1, S5_LANES), F32), pltpu.VMEM((1, S5_LANES), F32)],
        compiler_params=_cparams("parallel", "arbitrary"),
        name="s5_mixer",
    )(z, h0r.reshape(nbg, seqs, S5_LANES), h0i.reshape(nbg, seqs, S5_LANES), perm, permt, sp["bre"], sp["bim"],
      tabs["lam"],
      tabs["pseg"], tabs["pk"], sp["cre"], sp["cim"], sp["d"], sp["wglu"], sp["bglu"])


def s5_params(a_re, a_im, log_step, b_re, b_im, c_re, c_im, d_skip, w_glu, b_glu, seg_lens):
    dt = jnp.exp(log_step)[:, None]
    mag = jnp.exp(a_re * dt)
    lr = (mag * jnp.cos(a_im * dt)).reshape(1, S5_LANES)
    li = (mag * jnp.sin(a_im * dt)).reshape(1, S5_LANES)
    den = a_re * a_re + a_im * a_im
    xr, xi = lr.reshape(a_re.shape) - 1.0, li.reshape(a_re.shape)
    fr = (xr * a_re + xi * a_im) / den
    fi = (xi * a_re - xr * a_im) / den
    bbr = fr[..., None] * b_re - fi[..., None] * b_im
    bbi = fr[..., None] * b_im + fi[..., None] * b_re
    gh = S5_GROUPS // 2
    eye = jnp.eye(gh, dtype=F32)

    def in_mat(b):
        return jnp.einsum('jgpc,gh->jgchp', b.reshape(2, gh, S5_STATE, S5_GROUP), eye).reshape(
            2, S5_WIDTH // 2, S5_LANES // 2).astype(BF16)

    def out_mat(cm):
        return jnp.einsum('jgcp,gh->jgphc', cm.reshape(2, gh, S5_GROUP, S5_STATE), eye).reshape(
            2, S5_LANES // 2, S5_WIDTH // 2).astype(BF16)

    def powers(pr, pi, n):
        tr, ti, cnt = pr, pi, 1
        while cnt < n:
            lr_, li_ = tr[cnt - 1:cnt], ti[cnt - 1:cnt]
            nr, ni = _cmul(tr, ti, lr_, li_)
            tr, ti, cnt = jnp.concatenate([tr, nr], axis=0), jnp.concatenate([ti, ni], axis=0), 2 * cnt
        return tr, ti

    sub = jnp.arange(SUBLANES)[:, None]
    tabs = {}
    for R in seg_lens:
        kr, ki = powers(lr, li, R)
        sr, si = powers(kr[R - 1:R], ki[R - 1:R], 4)
        pseg = []
        for lag in (1, 2, 4):
            msk = (sub >= lag).astype(F32)
            pseg += [msk * sr[lag - 1:lag], msk * si[lag - 1:lag]]
        tabs[R] = dict(lam=jnp.concatenate([lr, li], axis=0), pseg=jnp.stack(pseg), pk=jnp.stack([kr, ki]))
    return dict(bre=in_mat(bbr), bim=in_mat(bbi), cre=out_mat(c_re), cim=out_mat(-c_im), tabs=tabs,
                d=d_skip.reshape(1, S5_WIDTH), wglu=w_glu.astype(BF16), bglu=b_glu.reshape(1, S5_WIDTH))


def _fox_flash_kernel(it_ref, jt_ref, qx_ref, kx_ref, v_ref, o_ref, m_ref, l_ref, acc_ref, s_ref, p_ref):
    i = it_ref[pl.program_id(1)]
    j = jt_ref[pl.program_id(1)]
    tq = qx_ref.shape[0]
    tk = kx_ref.shape[0]

    @pl.when(j == 0)
    def _():
        m_ref[...] = jnp.full_like(m_ref, -jnp.inf)
        l_ref[...] = jnp.zeros_like(l_ref)
        acc_ref[...] = jnp.zeros_like(acc_ref)

    def step(masked):
        nr, ncol = tq // FLASH_ROWS, tk // LANES
        if masked:
            diff = (lax.broadcasted_iota(jnp.int32, (FLASH_ROWS, LANES), 0) -
                    lax.broadcasted_iota(jnp.int32, (FLASH_ROWS, LANES), 1))
        for h in range(FOX_HEADS):
            hs = slice(h * LANES, (h + 1) * LANES)
            ps = slice((h // 2) * LANES, (h // 2 + 1) * LANES)
            s_buf, p_buf = s_ref.at[h % 2], p_ref.at[h % 2]
            if h == 0:
                s_buf[...] = _dot_nt(qx_ref[:, hs], kx_ref[:, hs])
            if h + 1 < FOX_HEADS:
                nhs = slice((h + 1) * LANES, (h + 2) * LANES)
                s_ref[(h + 1) % 2] = _dot_nt(qx_ref[:, nhs], kx_ref[:, nhs])
            for r in range(nr):
                rs = slice(r * FLASH_ROWS, (r + 1) * FLASH_ROWS)

                live = [cidx for cidx in range(ncol) if not (masked and cidx * LANES >= (r + 1) * FLASH_ROWS)]

                def piece(cidx):
                    sc = s_buf[rs, cidx * LANES:(cidx + 1) * LANES]
                    if masked and (cidx + 1) * LANES - 1 > r * FLASH_ROWS:
                        sc = jnp.where(diff >= (cidx * LANES - r * FLASH_ROWS), sc, -jnp.inf)
                    return sc

                mx = piece(live[0])
                for cidx in live[1:]:
                    mx = jnp.maximum(mx, piece(cidx))
                m_prev = m_ref[h, rs, :]
                m_new = jnp.maximum(m_prev, jnp.max(mx, axis=-1, keepdims=True))
                alpha = jnp.exp2(m_prev - m_new)
                lsum = jnp.zeros((FLASH_ROWS, LANES), F32)
                for cidx in range(ncol):
                    cs_ = slice(cidx * LANES, (cidx + 1) * LANES)
                    if cidx not in live:
                        p_buf[rs, cs_] = jnp.zeros((FLASH_ROWS, LANES), BF16)
                        continue
                    pc = jnp.exp2(piece(cidx) - m_new)
                    lsum = lsum + pc
                    p_buf[rs, cs_] = pc.astype(BF16)
                l_ref[h, rs, :] = alpha * l_ref[h, rs, :] + lsum
                acc_ref[h, rs, :] = alpha * acc_ref[h, rs, :]
                m_ref[h, rs, :] = m_new
            acc_ref[h] += _dot(p_buf[...], v_ref[:, ps])

    @pl.when(j < i)
    def _():
        step(False)

    @pl.when(j == i)
    def _():
        step(True)
        lane = lax.broadcasted_iota(jnp.int32, (1, LANES), 1)
        for p in range(FOX_HEADS // 2):
            lo = acc_ref[2 * p] / jnp.sum(l_ref[2 * p], axis=-1, keepdims=True)
            hi = acc_ref[2 * p + 1] / jnp.sum(l_ref[2 * p + 1], axis=-1, keepdims=True)
            o_ref[:, p * LANES:(p + 1) * LANES] = jnp.where(lane < FOX_HD, lo, hi)


def fox_flash(qx, kx, vb, nb, t, tile):
    assert tile % FLASH_ROWS == 0
    nt = t // tile
    pairs = [(i, j) for i in range(nt) for j in range(i + 1)]
    it = jnp.asarray([p[0] for p in pairs], jnp.int32)
    jt = jnp.asarray([p[1] for p in pairs], jnp.int32)
    return pl.pallas_call(
        _fox_flash_kernel,
        out_shape=jax.ShapeDtypeStruct((nb * t, FOX_WIDTH), F32),
        grid_spec=pltpu.PrefetchScalarGridSpec(
            num_scalar_prefetch=2,
            grid=(nb, len(pairs)),
            in_specs=[pl.BlockSpec((tile, FOX_HEADS * LANES), lambda b, p, it, jt: (b * nt + it[p], 0)),
                      pl.BlockSpec((tile, FOX_HEADS * LANES), lambda b, p, it, jt: (b * nt + jt[p], 0)),
                      pl.BlockSpec((tile, FOX_WIDTH), lambda b, p, it, jt: (b * nt + jt[p], 0))],
            out_specs=pl.BlockSpec((tile, FOX_WIDTH), lambda b, p, it, jt: (b * nt + it[p], 0)),
            scratch_shapes=[pltpu.VMEM((FOX_HEADS, tile, LANES), F32), pltpu.VMEM((FOX_HEADS, tile, LANES), F32),
                            pltpu.VMEM((FOX_HEADS, tile, LANES), F32),
                            pltpu.VMEM((2, tile, tile), F32), pltpu.VMEM((2, tile, tile), BF16)]),
        compiler_params=_cparams("parallel", "arbitrary"),
        name="fox_flash",
    )(it, jt, qx, kx, vb)


def _fox_decode_kernel(pt_ref, qb_ref, kn_ref, vn_ref, cs_ref, *rest, n_pages):
    k_refs = rest[0:n_pages]
    v_refs = rest[n_pages:2 * n_pages]
    f_refs = rest[2 * n_pages:3 * n_pages]
    triu_ref, o_ref, kpad_ref, vpad_ref = rest[3 * n_pages:]
    page = kpad_ref.shape[0]
    nrow = FOX_HEADS * SAMPLE_T
    qb = qb_ref[...].astype(BF16)

    def per_query(x):
        return jnp.concatenate([x] * SAMPLE_T, axis=0)

    fcat = jnp.concatenate([f_refs[i][0, 0] for i in range(n_pages)], axis=0)
    cum_in = _dot_exact_rhs(fcat, triu_ref[...])
    totals = jnp.broadcast_to(cum_in[:, page - 1:page], cum_in.shape)
    off = jnp.zeros((FOX_HEADS, page), F32)
    ss = []
    for i in range(n_pages):
        rows = slice(i * FOX_HEADS, (i + 1) * FOX_HEADS)
        kt = k_refs[i][0, 0].reshape(FOX_WIDTH, page).astype(BF16)
        ss.append(_dot(qb, kt) - per_query(cum_in[rows] + off))
        off = off + totals[rows]
    kpad_ref[...] = jnp.zeros_like(kpad_ref)
    vpad_ref[...] = jnp.zeros_like(vpad_ref)
    kpad_ref[0:SAMPLE_T, :] = kn_ref[...]
    vpad_ref[0:SAMPLE_T, :] = vn_ref[...]
    csrow = _pad_rows(cs_ref[...], page).T[LANE_FOXF:LANE_FOXF + FOX_HEADS]
    s_new = _dot_nt(qb, kpad_ref[...].astype(BF16)) - per_query(csrow + off)
    tq_idx = lax.broadcasted_iota(jnp.int32, (nrow, page), 0) // FOX_HEADS
    tk_idx = lax.broadcasted_iota(jnp.int32, (nrow, page), 1)
    ss.append(jnp.where(tk_idx <= tq_idx, s_new, -jnp.inf))

    m = ss[0]
    for s in ss[1:]:
        m = jnp.maximum(m, s)
    m = jnp.max(m, axis=-1, keepdims=True)
    lsum = jnp.zeros((nrow, page), F32)
    acc = jnp.zeros((nrow, FOX_WIDTH), F32)
    for i, s in enumerate(ss):
        p = jnp.exp(s - m)
        lsum = lsum + p
        if i < n_pages:
            acc = acc + _dot_nt(p.astype(BF16), v_refs[i][0, 0].reshape(FOX_WIDTH, page).astype(BF16))
        else:
            acc = acc + _dot(p.astype(BF16), vpad_ref[...].astype(BF16))
    acc = acc / jnp.sum(lsum, axis=-1, keepdims=True)
    lane = lax.broadcasted_iota(jnp.int32, (nrow, FOX_WIDTH), 1) // FOX_HD
    head = lax.broadcasted_iota(jnp.int32, (nrow, FOX_WIDTH), 0) % FOX_HEADS
    picked = jnp.where(lane == head, acc, 0.0)
    o_ref[...] = jnp.sum(picked.reshape(SAMPLE_T, FOX_HEADS, FOX_WIDTH), axis=1)


def fox_decode(page_table, qx, kn, z, cs, pool_kt, pool_vt, pool_ft, layer):
    nb, n_pages = page_table.shape
    page = pool_kt.shape[-1]
    assert page == LANES
    nrow = FOX_HEADS * SAMPLE_T
    pt = page_table.reshape(-1)
    triu = jnp.asarray(np.triu(np.ones((page, page))), BF16)
    pair_of_head = jnp.asarray(np.arange(FOX_HEADS)[:, None] // 2 == np.arange(FOX_HEADS // 2)[None, :], F32)
    qb = (qx.reshape(nb, SAMPLE_T, FOX_HEADS, 1, LANES) * pair_of_head[None, None, :, :, None]).reshape(
        nb * nrow, FOX_WIDTH)

    def pg5(i):
        return lambda b, pt: (layer, pt[b * n_pages + i], 0, 0, 0)

    def pg4(i):
        return lambda b, pt: (layer, pt[b * n_pages + i], 0, 0)

    row = lambda b, pt: (b, 0)
    in_specs = [pl.BlockSpec((nrow, FOX_WIDTH), row),
                pl.BlockSpec((SAMPLE_T, FOX_WIDTH), row),
                pl.BlockSpec((SAMPLE_T, FOX_WIDTH), lambda b, pt: (b, COL_FV)),
                pl.BlockSpec((SAMPLE_T, LANES), row)]
    in_specs += [pl.BlockSpec((1, 1, FOX_HEADS, FOX_HD, page), pg5(i)) for i in range(n_pages)]
    in_specs += [pl.BlockSpec((1, 1, FOX_HEADS, FOX_HD, page), pg5(i)) for i in range(n_pages)]
    in_specs += [pl.BlockSpec((1, 1, FOX_HEADS, page), pg4(i)) for i in range(n_pages)]
    in_specs += [pl.BlockSpec((page, page), lambda b, pt: (0, 0))]
    return pl.pallas_call(
        functools.partial(_fox_decode_kernel, n_pages=n_pages),
        out_shape=jax.ShapeDtypeStruct((nb * SAMPLE_T, FOX_WIDTH), F32),
        grid_spec=pltpu.PrefetchScalarGridSpec(
            num_scalar_prefetch=1,
            grid=(nb,),
            in_specs=in_specs,
            out_specs=pl.BlockSpec((SAMPLE_T, FOX_WIDTH), row),
            scratch_shapes=[pltpu.VMEM((page, FOX_WIDTH), F32), pltpu.VMEM((page, FOX_WIDTH), F32)]),
        compiler_params=_cparams("parallel"),
        name="fox_decode",
    )(pt, qb, kn, z, cs, *([pool_kt] * n_pages), *([pool_vt] * n_pages), *([pool_ft] * n_pages), triu)


def _mlstm_kernel(q_ref, k_ref, v_ref, o_ref, g_ref, cs_ref, *rest, short):
    if short:
        c0_ref, n0_ref, m0_ref, gn_ref, y_ref, cout_ref, nout_ref, mout_ref, c_ref, n_ref, m_ref = rest
    else:
        (grow_ref, csrow_ref, c0_ref, n0_ref, m0_ref, gn_ref, y_ref, cout_ref, nout_ref, mout_ref,
         c_ref, n_ref, m_ref) = rest
    c = pl.program_id(1)
    nc = pl.num_programs(1)
    L = q_ref.shape[0]

    @pl.when(c == 0)
    def _():
        c_ref[...] = c0_ref[0, 0]
        n_ref[...] = n0_ref[0]
        m_ref[...] = m0_ref[0]

    Lk = max(L, LANES)
    Lq = max(L, MIN_BF16_ROWS)
    g = g_ref[...]
    cs = cs_ref[...]
    if short:
        grow = _pad_rows(g, Lk).T
        csrow = _pad_rows(cs, Lk).T
    else:
        grow = grow_ref[...]
        csrow = csrow_ref[...]
    causal = (lax.broadcasted_iota(jnp.int32, (Lq, Lk), 0) >= lax.broadcasted_iota(jnp.int32, (Lq, Lk), 1))
    scale = ML_HD ** -0.5
    heads = range(ML_HEADS)
    hsl = [slice(h * ML_HD, (h + 1) * ML_HD) for h in heads]
    qb = [_pad_rows(q_ref[:, hsl[h]], Lq).astype(BF16) for h in heads]
    kb = [_pad_rows(k_ref[:, hsl[h]], Lk).astype(BF16) for h in heads]
    vb = [_pad_rows(v_ref[:, hsl[h]], Lk).astype(BF16) for h in heads]
    bcol_k = [_pad_rows(cs[:, LANE_MLF + h:LANE_MLF + h + 1], Lk) for h in heads]
    bcol = [b[0:Lq] for b in bcol_k]
    icol = [_pad_rows(g[:, LANE_MLI + h:LANE_MLI + h + 1], Lk, -jnp.inf) for h in heads]
    log_d = [jnp.where(causal, bcol[h] - csrow[LANE_MLF + h:LANE_MLF + h + 1, :]
                       + grow[LANE_MLI + h:LANE_MLI + h + 1, :], -jnp.inf) for h in heads]
    log_inter = [bcol[h] + m_ref[0:1, h:h + 1] for h in heads]
    m_t = [jnp.maximum(log_inter[h], jnp.max(log_d[h], axis=-1, keepdims=True)) for h in heads]
    inter_w = [jnp.exp(log_inter[h] - m_t[h]) for h in heads]
    qk = [_dot_nt(qb[h], kb[h]) for h in heads]
    ch = [c_ref[h] for h in heads]
    n_row = [n_ref[h:h + 1, :] for h in heads]
    qc = [_dot_nt(qb[h], ch[h].astype(BF16)) for h in heads]
    s = [qk[h] * (jnp.exp(log_d[h] - m_t[h]) * scale) for h in heads]
    sv = [_dot(s[h].astype(BF16), vb[h]) for h in heads]
    for h in heads:
        num = sv[h] + inter_w[h] * qc[h]
        den = (jnp.sum(s[h], axis=-1, keepdims=True)
               + inter_w[h] * jnp.sum(qb[h].astype(F32) * n_row[h], axis=-1, keepdims=True))
        hh = num / jnp.maximum(jnp.abs(den), jnp.exp(-m_t[h]))
        y_ref[:, hsl[h]] = _rms(hh[0:L], gn_ref[...]) * _sigmoid(o_ref[:, hsl[h]])
    m_end = [m_t[h][L - 1:L, :] for h in heads]
    a_end = [inter_w[h][L - 1:L, :] for h in heads]
    w_col = [jnp.exp(bcol[h][L - 1:L, :] - bcol_k[h] + icol[h] - m_end[h]) * scale for h in heads]
    upd = [_dot_tn((vb[h].astype(F32) * w_col[h]).astype(BF16), kb[h]) for h in heads]
    for h in heads:
        c_ref[h] = a_end[h] * ch[h] + upd[h]
        n_ref[h:h + 1, :] = a_end[h] * n_row[h] + jnp.sum(kb[h].astype(F32) * w_col[h], axis=0, keepdims=True)
        m_ref[0:1, h:h + 1] = m_end[h]

    @pl.when(c == nc - 1)
    def _():
        cout_ref[0] = c_ref[...]
        nout_ref[0] = n_ref[...]
        mout_ref[0] = m_ref[...]


def mlstm(za, zb, g, cs, grow, csrow, c0, n0, m0, gn, nb, t, L, c_layer):
    nc = t // L
    rows = lambda b, c: (b * nc + c, 0)
    rr = 2 * SUBLANES
    m0p = jnp.zeros((nb, 1, LANES), F32).at[:, 0, :ML_HEADS].set(m0)
    short = grow is None
    row_specs = [] if short else [pl.BlockSpec((rr, L), lambda b, c: (b * (LANES // rr), c))] * 2
    row_args = [] if short else [grow, csrow]
    outs = pl.pallas_call(
        functools.partial(_mlstm_kernel, short=short),
        out_shape=(jax.ShapeDtypeStruct((nb * t, ML_WIDTH), F32),
                   jax.ShapeDtypeStruct((nb, ML_HEADS, ML_HD, ML_HD), F32),
                   jax.ShapeDtypeStruct((nb, ML_HEADS, ML_HD), F32),
                   jax.ShapeDtypeStruct((nb, 1, LANES), F32)),
        grid=(nb, nc),
        in_specs=[pl.BlockSpec((L, 512), lambda b, c: (b * nc + c, COL_MQ)),
                  pl.BlockSpec((L, 512), lambda b, c: (b * nc + c, COL_MK)),
                  pl.BlockSpec((L, 512), lambda b, c: (b * nc + c, COL_MV)),
                  pl.BlockSpec((L, 512), lambda b, c: (b * nc + c, COL_MO)),
                  pl.BlockSpec((L, LANES), rows), pl.BlockSpec((L, LANES), rows), *row_specs,
                  pl.BlockSpec((1, 1, ML_HEADS, ML_HD, ML_HD), lambda b, c: (c_layer, b, 0, 0, 0)),
                  pl.BlockSpec((1, ML_HEADS, ML_HD), lambda b, c: (b, 0, 0)),
                  pl.BlockSpec((1, 1, LANES), lambda b, c: (b, 0, 0)),
                  pl.BlockSpec((1, ML_HD), lambda b, c: (0, 0))],
        out_specs=(pl.BlockSpec((L, ML_WIDTH), rows),
                   pl.BlockSpec((1, ML_HEADS, ML_HD, ML_HD), lambda b, c: (b, 0, 0, 0)),
                   pl.BlockSpec((1, ML_HEADS, ML_HD), lambda b, c: (b, 0, 0)),
                   pl.BlockSpec((1, 1, LANES), lambda b, c: (b, 0, 0))),
        scratch_shapes=[pltpu.VMEM((ML_HEADS, ML_HD, ML_HD), F32), pltpu.VMEM((ML_HEADS, ML_HD), F32),
                        pltpu.VMEM((1, LANES), F32)],
        compiler_params=_cparams("parallel", "arbitrary"),
        name="mlstm",
    )(zb, zb, zb, za, g, cs, *row_args, c0, n0, m0p, gn.reshape(1, ML_HD))
    y, c_new, n_new, m_new = outs
    return y, c_new, n_new, m_new[:, 0, :ML_HEADS]


def _merge_kernel(x_ref, g_ref, wg_ref, ys_ref, yf_ref, ym_ref, ws_ref, wf_ref, wm_ref, wo_ref, o_ref):
    x = x_ref[...]
    hn = _rms(x, g_ref[...]).astype(BF16)
    branches = (ys_ref, ws_ref), (yf_ref, wf_ref), (ym_ref, wm_ref)
    merged = None
    for b, (y_ref, w_ref) in enumerate(branches):
        gate = _sigmoid(_dot(hn, wg_ref[:, b * D_MODEL:(b + 1) * D_MODEL]))
        term = gate * _dot(y_ref[...].astype(BF16), w_ref[...])
        merged = term if merged is None else merged + term
    o_ref[...] = x + _dot(merged.astype(BF16), wo_ref[...])


def merge_out(x, g_mix, w_gates, ys, yf, ym, ws, wf, wm, wo, tm=512):
    m = x.shape[0]
    tm = min(tm, m)
    row = lambda i: (i, 0)
    const = lambda i: (0, 0)
    return pl.pallas_call(
        _merge_kernel,
        out_shape=jax.ShapeDtypeStruct((m, D_MODEL), F32),
        grid=(m // tm,),
        in_specs=[pl.BlockSpec((tm, D_MODEL), row), pl.BlockSpec((1, D_MODEL), const),
                  pl.BlockSpec((D_MODEL, 3 * D_MODEL), const),
                  pl.BlockSpec((tm, 512), row), pl.BlockSpec((tm, 512), row), pl.BlockSpec((tm, 512), row),
                  pl.BlockSpec((512, D_MODEL), const), pl.BlockSpec((512, D_MODEL), const),
                  pl.BlockSpec((512, D_MODEL), const), pl.BlockSpec((D_MODEL, D_MODEL), const)],
        out_specs=pl.BlockSpec((tm, D_MODEL), row),
        compiler_params=_cparams("parallel"),
        name="merge_out",
    )(x, g_mix.reshape(1, D_MODEL), w_gates, ys, yf, ym, ws, wf, wm, wo)


def _cross_block_kernel(x_ref, g_ref, wq_ref, gq_ref, k_ref, v_ref, wo_ref, o_ref):
    x = x_ref[...]
    q = _dot(_rms(x, g_ref[...]).astype(BF16), wq_ref[...])
    heads = range(MEM_HEADS)
    hsl = [slice(h * MEM_HD, (h + 1) * MEM_HD) for h in heads]
    qh = [_rms(q[:, hsl[h]], gq_ref[...]).astype(BF16) for h in heads]
    s = [_dot_nt(qh[h], k_ref[:, hsl[h]].astype(BF16)) * (MEM_HD ** -0.5) for h in heads]
    m = [jnp.max(s[h], axis=-1, keepdims=True) for h in heads]
    p = [jnp.exp(s[h] - m[h]) for h in heads]
    l = [jnp.sum(p[h], axis=-1, keepdims=True) for h in heads]
    pv = [_dot(p[h].astype(BF16), v_ref[:, hsl[h]].astype(BF16)) for h in heads]
    outs = [(pv[h] / l[h]).astype(BF16) for h in heads]
    o_ref[...] = x + _dot(jnp.concatenate(outs, axis=1), wo_ref[...])


def cross_block(x, g, wq, gq, mem_k, mem_v, wo, nb, t, tq):
    nq = t // tq
    n_mem = mem_k.shape[0] // nb
    const = lambda b, i: (0, 0)
    row = lambda b, i: (b * nq + i, 0)
    return pl.pallas_call(
        _cross_block_kernel,
        out_shape=jax.ShapeDtypeStruct((nb * t, D_MODEL), F32),
        grid=(nb, nq),
        in_specs=[pl.BlockSpec((tq, D_MODEL), row), pl.BlockSpec((1, D_MODEL), const),
                  pl.BlockSpec((D_MODEL, MEM_WIDTH), const), pl.BlockSpec((1, MEM_HD), const),
                  pl.BlockSpec((n_mem, MEM_WIDTH), lambda b, i: (b, 0)),
                  pl.BlockSpec((n_mem, MEM_WIDTH), lambda b, i: (b, 0)),
                  pl.BlockSpec((MEM_WIDTH, D_MODEL), const)],
        out_specs=pl.BlockSpec((tq, D_MODEL), row),
        compiler_params=_cparams("parallel", "parallel"),
        name="cross_block",
    )(x, g.reshape(1, D_MODEL), wq, gq.reshape(1, MEM_HD), mem_k, mem_v, wo)


def _cross_cached_kernel(q_ref, k_ref, v_ref, o_ref):
    nseq = k_ref.shape[1]
    tq = q_ref.shape[0] // nseq
    seqs = range(nseq)
    n_cols = k_ref.shape[2]
    same_head = (lax.broadcasted_iota(jnp.int32, (MEM_HEADS * tq, n_cols), 0) // tq ==
                 lax.broadcasted_iota(jnp.int32, (MEM_HEADS * tq, n_cols), 1) % MEM_HEADS)
    qr = [jnp.concatenate([q_ref[sq * tq:(sq + 1) * tq, h * MEM_HD:(h + 1) * MEM_HD] for h in range(MEM_HEADS)],
                          axis=0).astype(BF16) for sq in seqs]
    s = [jnp.where(same_head, _dot_nt(qr[sq], k_ref[0, sq].astype(BF16)) * (MEM_HD ** -0.5), -jnp.inf)
         for sq in seqs]
    m = [jnp.max(s[sq], axis=-1, keepdims=True) for sq in seqs]
    p = [jnp.exp(s[sq] - m[sq]) for sq in seqs]
    l = [jnp.sum(p[sq], axis=-1, keepdims=True) for sq in seqs]
    o = [_dot(p[sq].astype(BF16), v_ref[0, sq].astype(BF16)) / l[sq] for sq in seqs]
    for sq in seqs:
        for h in range(MEM_HEADS):
            o_ref[sq * tq:(sq + 1) * tq, h * MEM_HD:(h + 1) * MEM_HD] = o[sq][h * tq:(h + 1) * tq]


CROSS_CACHED_SEQS = 4


def cross_attend_cached(q, mem_k, mem_v, nb, t, layer):
    ns = CROSS_CACHED_SEQS
    kv_spec = pl.BlockSpec((1, ns, mem_k.shape[2], MEM_HD), lambda b: (layer, b, 0, 0))
    return pl.pallas_call(
        _cross_cached_kernel,
        out_shape=jax.ShapeDtypeStruct((nb * t, MEM_WIDTH), F32),
        grid=(nb // ns,),
        in_specs=[pl.BlockSpec((ns * t, MEM_WIDTH), lambda b: (b, 0)), kv_spec, kv_spec],
        out_specs=pl.BlockSpec((ns * t, MEM_WIDTH), lambda b: (b, 0)),
        compiler_params=_cparams("parallel"),
        name="cross_attend_cached",
    )(q, mem_k, mem_v)


def _proj_residual_kernel(x_ref, a_ref, w_ref, o_ref):
    o_ref[...] = x_ref[...] + _dot(a_ref[...].astype(BF16), w_ref[...])


def proj_residual(x, a, w, tm=512):
    m, n = x.shape
    k = a.shape[1]
    tm = min(tm, m)
    return pl.pallas_call(
        _proj_residual_kernel,
        out_shape=jax.ShapeDtypeStruct((m, n), F32),
        grid=(m // tm,),
        in_specs=[pl.BlockSpec((tm, n), lambda i: (i, 0)), pl.BlockSpec((tm, k), lambda i: (i, 0)),
                  pl.BlockSpec((k, n), lambda i: (0, 0))],
        out_specs=pl.BlockSpec((tm, n), lambda i: (i, 0)),
        compiler_params=_cparams("parallel"),
        name="proj_residual",
    )(x, a, w)


def _mlp_kernel(x_ref, g_ref, wu_ref, wd_ref, o_ref, hn_ref, acc_ref):
    f = pl.program_id(1)

    @pl.when(f == 0)
    def _():
        hn_ref[...] = _rms(x_ref[...], g_ref[...]).astype(BF16)
        acc_ref[...] = jnp.zeros_like(acc_ref)

    a = jnp.maximum(_dot(hn_ref[...], wu_ref[...]), 0.0)
    acc_ref[...] += _dot((a * a).astype(BF16), wd_ref[...])

    @pl.when(f == pl.num_programs(1) - 1)
    def _():
        o_ref[...] = x_ref[...] + acc_ref[...]


def mlp(x, g, wu, wd, tm=1024, tf=1024):
    m, d = x.shape
    dff = wu.shape[1]
    tm = min(tm, m)
    return pl.pallas_call(
        _mlp_kernel,
        out_shape=jax.ShapeDtypeStruct((m, d), F32),
        grid=(m // tm, dff // tf),
        in_specs=[pl.BlockSpec((tm, d), lambda i, f: (i, 0)), pl.BlockSpec((1, d), lambda i, f: (0, 0)),
                  pl.BlockSpec((d, tf), lambda i, f: (0, f)), pl.BlockSpec((tf, d), lambda i, f: (f, 0))],
        out_specs=pl.BlockSpec((tm, d), lambda i, f: (i, 0)),
        scratch_shapes=[pltpu.VMEM((tm, d), BF16), pltpu.VMEM((tm, d), F32)],
        compiler_params=_cparams("parallel", "arbitrary"),
        name="mlp",
    )(x, g.reshape(1, d), wu, wd)


def _pack_w_in(w_in):
    offs = np.concatenate([[0], np.cumsum(SPLITS)])
    col = lambda i: w_in[:, int(offs[i]):int(offs[i + 1])]
    s5, fq, fk, fv, ff, mq, mk, mv, mi, mf, mo, gates = [col(i) for i in range(12)]
    pad = jnp.zeros((w_in.shape[0], 512 - FOX_HEADS - 2 * ML_HEADS), w_in.dtype)
    w_a = jnp.concatenate([s5, fq, fk, fv, mo, ff, mi, mf, pad], axis=1).astype(BF16)
    w_b = jnp.concatenate([mq, mk, mv], axis=1).astype(BF16)
    return w_a, w_b, gates.astype(BF16)


def _layer_weights(l, g_mix, w_in, s5_a_re, s5_a_im, s5_log_step, s5_b_re, s5_b_im, s5_c_re, s5_c_im, s5_d,
                   s5_w_glu, s5_b_glu, fox_gq, fox_gk, fox_bf, ml_bi, ml_bf, ml_gn, w_br_s5, w_br_fox, w_br_ml,
                   w_out, g_cross, w_cq, cross_gq, g_mem, w_mk, w_mv, cross_gk, w_co, g_mlp, w_up, w_down):
    bias_row = jnp.zeros((1, LANES), F32)
    bias_row = bias_row.at[0, LANE_FOXF:LANE_FOXF + FOX_HEADS].set(fox_bf[l])
    bias_row = bias_row.at[0, LANE_MLI:LANE_MLI + ML_HEADS].set(ml_bi[l])
    bias_row = bias_row.at[0, LANE_MLF:LANE_MLF + ML_HEADS].set(ml_bf[l])
    w_a, w_b, w_gates = _pack_w_in(w_in[l])
    return dict(
        g_mix=g_mix[l], w_a=w_a, w_b=w_b, w_gates=w_gates,
        s5=s5_params(s5_a_re[l], s5_a_im[l], s5_log_step[l], s5_b_re[l], s5_b_im[l], s5_c_re[l], s5_c_im[l],
                     s5_d[l], s5_w_glu[l], s5_b_glu[l], (SEQ_TILE // SUBLANES, SAMPLE_T // SUBLANES)),
        gq=jnp.tile(fox_gq[l], FOX_HEADS).reshape(1, FOX_WIDTH),
        gk=jnp.tile(fox_gk[l], FOX_HEADS).reshape(1, FOX_WIDTH),
        bias_row=bias_row, ml_gn=ml_gn[l],
        w_br_s5=w_br_s5[l].astype(BF16), w_br_fox=w_br_fox[l].astype(BF16), w_br_ml=w_br_ml[l].astype(BF16),
        w_out=w_out[l].astype(BF16), g_cross=g_cross[l], w_cq=w_cq[l].astype(BF16), cross_gq=cross_gq[l],
        g_mem=g_mem[l], w_mk=w_mk[l].astype(BF16), w_mv=w_mv[l].astype(BF16), cross_gk=cross_gk[l],
        w_co=w_co[l].astype(BF16), g_mlp=g_mlp[l], w_up=w_up[l].astype(BF16), w_down=w_down[l].astype(BF16))


def _hybrid_layer(x, W, nb, t, seq_tile, t_valid, s5_state, ml_state, mem_k, mem_v, fox_attend, augment,
                  mem_layer=None):
    z = norm_matmul(x, W["g_mix"], W["w_a"], tm=min(1024, nb * t), tn=1024)
    zb = norm_matmul(x, W["g_mix"], W["w_b"], tm=min(1024, nb * t), tn=W["w_b"].shape[1],
                     out_dtype=BF16 if seq_tile % MIN_BF16_ROWS == 0 else F32)
    prep = gate_prep(z, W["gq"], W["gk"], W["bias_row"], nb, t, seq_tile, t_valid, augment)
    kn, g, cs = prep[1:4]
    grow, csrow = prep[4:6] if augment else (None, None)
    y_s5, s5_re, s5_im = s5_mixer(z, s5_state[0], s5_state[1], W["s5"], nb, t, seq_tile, t_valid - 1,
                                  seqs=S5_SHORT_SEQS if t == SUBLANES else 1)
    y_fox = fox_attend(prep, z)
    y_ml, c_new, n_new, m_new = mlstm(z, zb, g, cs, grow, csrow, ml_state[0], ml_state[1], ml_state[2], W["ml_gn"],
                                      nb, t, seq_tile, 0 if mem_layer is None else mem_layer)
    x = merge_out(x, W["g_mix"], W["w_gates"], y_s5, y_fox, y_ml, W["w_br_s5"], W["w_br_fox"], W["w_br_ml"],
                  W["w_out"])
    if mem_layer is None:
        x = cross_block(x, W["g_cross"], W["w_cq"], W["cross_gq"], mem_k, mem_v, W["w_co"], nb, t, CROSS_TILE)
    else:
        qc = norm_matmul(x, W["g_cross"], W["w_cq"], head_gain=W["cross_gq"])
        oc = cross_attend_cached(qc, mem_k, mem_v, nb, t, mem_layer)
        x = proj_residual(x, oc, W["w_co"])
    x = mlp(x, W["g_mlp"], W["w_up"], W["w_down"])
    if augment:
        fox_k = prep[1].reshape(nb, FOX_HEADS, FOX_HD, t).transpose(0, 3, 1, 2)
        fox_v = prep[8].reshape(nb, FOX_HEADS, FOX_HD, t).transpose(0, 3, 1, 2)
    else:
        fox_k = kn.reshape(nb, t, FOX_HEADS, FOX_HD)
        fox_v = z[:, COL_FV * 512:(COL_FV + 1) * 512].reshape(nb, t, FOX_HEADS, FOX_HD)
    return x, fox_k, fox_v, g, s5_re.reshape(nb, S5_GROUPS, S5_STATE), s5_im.reshape(nb, S5_GROUPS, S5_STATE), \
        c_new, n_new, m_new


def kernel(x_prompt, x_sample, mem_prompt, cache_fox_k, cache_fox_v, cache_fox_logf, page_table, state_s5_re, state_s5_im, state_mlstm_C, state_mlstm_n, state_mlstm_m, cache_mem_k, cache_mem_v, g_mix, w_in, s5_a_re, s5_a_im, s5_log_step, s5_b_re, s5_b_im, s5_c_re, s5_c_im, s5_d, s5_w_glu, s5_b_glu, fox_gq, fox_gk, fox_bf, ml_bi, ml_bf, ml_gn, w_br_s5, w_br_fox, w_br_ml, w_out, g_cross, w_cq, cross_gq, g_mem, w_mk, w_mv, cross_gk, w_co, g_mlp, w_up, w_down):
    depth = w_in.shape[0]
    bp, tp, _ = x_prompt.shape
    bs, ts, _ = x_sample.shape
    n_mem = mem_prompt.shape[1]

    xp = x_prompt.reshape(bp * tp, D_MODEL)
    xs = jnp.pad(x_sample, ((0, 0), (0, SAMPLE_T - ts), (0, 0))).reshape(bs * SAMPLE_T, D_MODEL)
    mem = mem_prompt.reshape(bp * n_mem, D_MODEL)
    zeros_p = (jnp.zeros((bp, S5_LANES), F32), jnp.zeros((bp, S5_LANES), F32))
    zeros_ml = (jnp.zeros((1, bp, ML_HEADS, ML_HD, ML_HD), F32), jnp.zeros((bp, ML_HEADS, ML_HD), F32),
                jnp.zeros((bp, ML_HEADS), F32))
    pool_kt = jnp.transpose(cache_fox_k, (0, 1, 3, 4, 2))
    pool_vt = jnp.transpose(cache_fox_v, (0, 1, 3, 4, 2))
    pool_ft = jnp.transpose(cache_fox_logf, (0, 1, 3, 2))
    st_p, st_s = [], []
    for l in range(depth):
        W = _layer_weights(l, g_mix, w_in, s5_a_re, s5_a_im, s5_log_step, s5_b_re, s5_b_im, s5_c_re, s5_c_im, s5_d,
                           s5_w_glu, s5_b_glu, fox_gq, fox_gk, fox_bf, ml_bi, ml_bf, ml_gn, w_br_s5, w_br_fox,
                           w_br_ml, w_out, g_cross, w_cq, cross_gq, g_mem, w_mk, w_mv, cross_gk, w_co, g_mlp, w_up,
                           w_down)
        mk_p = norm_matmul(mem, W["g_mem"], W["w_mk"], head_gain=W["cross_gk"])
        mv_p = norm_matmul(mem, W["g_mem"], W["w_mv"])

        def flash(prep, z):
            return fox_flash(prep[0], prep[6], prep[7], bp, tp, FLASH_TILE)

        xp, fox_k, fox_v, g, s5r, s5i, c_new, n_new, m_new = _hybrid_layer(
            xp, W, bp, tp, SEQ_TILE, tp, zeros_p, zeros_ml, mk_p, mv_p, flash, True)
        st_p.append((fox_k, fox_v,
                     g[:, LANE_FOXF:LANE_FOXF + FOX_HEADS].reshape(bp, tp, FOX_HEADS),
                     s5r, s5i, c_new, n_new, m_new,
                     mk_p.reshape(bp, n_mem, MEM_HEADS, MEM_HD), mv_p.reshape(bp, n_mem, MEM_HEADS, MEM_HD)))

        def decode(prep, z, layer=l):
            return fox_decode(page_table, prep[0], prep[1], z, prep[3], pool_kt, pool_vt, pool_ft, layer)

        xs, fox_k, fox_v, g, s5r, s5i, c_new, n_new, m_new = _hybrid_layer(
            xs, W, bs, SAMPLE_T, SAMPLE_T, ts,
            (state_s5_re[l].reshape(bs, S5_LANES), state_s5_im[l].reshape(bs, S5_LANES)),
            (state_mlstm_C, state_mlstm_n[l], state_mlstm_m[l]),
            cache_mem_k.reshape(depth, bs, n_mem * MEM_HEADS, MEM_HD),
            cache_mem_v.reshape(depth, bs, n_mem * MEM_HEADS, MEM_HD), decode, False, mem_layer=l)
        st_s.append((fox_k[:, :ts], fox_v[:, :ts],
                     g[:, LANE_FOXF:LANE_FOXF + FOX_HEADS].reshape(bs, SAMPLE_T, FOX_HEADS)[:, :ts],
                     s5r, s5i, c_new, n_new, m_new))
    outs_p = [jnp.stack(a) for a in zip(*st_p)]
    outs_s = [jnp.stack(a) for a in zip(*st_s)]
    yp = xp.reshape(bp, tp, D_MODEL)
    ys = xs.reshape(bs, SAMPLE_T, D_MODEL)[:, :ts]
    return (yp, ys, *outs_p, *outs_s)
```
